```python
import math
import jax, jax.numpy as jnp
from jax import lax
import numpy as np

D_MODEL = 1024
BATCH = 4
SEQ = 4096
DEPTH = 1
DEC_BATCH = 128
DEC_SEQ = 4
PAST_LEN = 2048
PAGE_SIZE = 128

MIX_WIDTH = D_MODEL
ATTN_WIDTH = MIX_WIDTH // 2
SSM_WIDTH = MIX_WIDTH - ATTN_WIDTH
DIFF_HEAD_DIM = 64
N_DIFF_HEADS = ATTN_WIDTH // (2 * DIFF_HEAD_DIM)
SSM_GROUP = 16
N_SSM_GROUPS = SSM_WIDTH // SSM_GROUP
SSM_STATE = 64
N_MEM = 256
N_MEM_HEADS = 4
MEM_HEAD_DIM = D_MODEL // N_MEM_HEADS
D_FF = -(-8 * D_MODEL // (3 * 256)) * 256
IN_COLS = 3 * ATTN_WIDTH + SSM_WIDTH
Q_BLOCK = 128
RMS_EPS = 1e-6
DT_MIN = 1e-3
DT_MAX = 1e-1

kernel_name = "hybrid_diffattn_s5_memory_decoder_step"


def rms_norm(x, g):
    xf = x.astype(jnp.float32)
    xf = xf * lax.rsqrt(jnp.mean(jnp.square(xf), axis=-1, keepdims=True) + RMS_EPS)
    return (xf * g.astype(jnp.float32)).astype(x.dtype)


def alibi_slopes():
    return 2.0 ** (-8.0 * jnp.arange(1, N_DIFF_HEADS + 1, dtype=jnp.float32) / N_DIFF_HEADS)


def diff_lambda(lq1, lk1, lq2, lk2, layer):
    lam_init = 0.8 - 0.6 * math.exp(-0.3 * layer)
    dot_exp = lambda a, b: jnp.exp(jnp.sum(a.astype(jnp.float32) * b.astype(jnp.float32)))
    return dot_exp(lq1, lk1) - dot_exp(lq2, lk2) + lam_init, lam_init


def project_mixer(hn, w_in):
    b, t, _ = hn.shape
    proj = hn @ w_in
    q, k, v, u = jnp.split(proj, [ATTN_WIDTH, 2 * ATTN_WIDTH, 3 * ATTN_WIDTH], axis=-1)
    q = q.reshape(b, t, N_DIFF_HEADS, 2, DIFF_HEAD_DIM)
    k = k.reshape(b, t, N_DIFF_HEADS, 2 * DIFF_HEAD_DIM)
    v = v.reshape(b, t, N_DIFF_HEADS, 2 * DIFF_HEAD_DIM)
    return q, k, v, u


def diff_attn_core(q, k, v, q_pos, k_pos, lam):
    s = jnp.einsum("bqhcd,bkhcd->bhcqk", q, k, preferred_element_type=jnp.float32)
    s = s * (DIFF_HEAD_DIM ** -0.5)
    dist = (q_pos[:, None] - k_pos[None, :]).astype(jnp.float32)
    bias = -alibi_slopes()[:, None, None] * dist
    s = jnp.where(dist >= 0, s + bias[:, None], -jnp.inf)
    p = jax.nn.softmax(s, axis=-1)
    p = p[:, :, 0] - lam * p[:, :, 1]
    return jnp.einsum("bhqk,bkhe->bqhe", p.astype(v.dtype), v)


def prompt_diff_attention(q, k, v, lam):
    b, t = q.shape[:2]
    nb = t // Q_BLOCK
    kk = k.reshape(b, t, N_DIFF_HEADS, 2, DIFF_HEAD_DIM)
    pos = jnp.arange(t, dtype=jnp.int32)
    qb = q.reshape(b, nb, Q_BLOCK, N_DIFF_HEADS, 2, DIFF_HEAD_DIM).swapaxes(0, 1)
    pb = pos.reshape(nb, Q_BLOCK)
    ob = lax.map(lambda a: diff_attn_core(a[0], kk, v, a[1], pos, lam), (qb, pb))
    return ob.swapaxes(0, 1).reshape(b, t, N_DIFF_HEADS, 2 * DIFF_HEAD_DIM)


def sample_diff_attention(q, k_new, v_new, ck, cv, page_table, lam):
    db, t = q.shape[:2]
    past = page_table.shape[1] * PAGE_SIZE
    k_past = ck[page_table].reshape(db, past, N_DIFF_HEADS, 2 * DIFF_HEAD_DIM)
    v_past = cv[page_table].reshape(db, past, N_DIFF_HEADS, 2 * DIFF_HEAD_DIM)
    k_all = jnp.concatenate([k_past, k_new.astype(k_past.dtype)], axis=1)
    k_all = k_all.reshape(db, past + t, N_DIFF_HEADS, 2, DIFF_HEAD_DIM)
    v_all = jnp.concatenate([v_past, v_new.astype(v_past.dtype)], axis=1)
    q_pos = past + jnp.arange(t, dtype=jnp.int32)
    k_pos = jnp.arange(past + t, dtype=jnp.int32)
    return diff_attn_core(q, k_all, v_all, q_pos, k_pos, lam)


def finish_diff_heads(o, g_subln, lam_init):
    o = rms_norm(o, g_subln) * (1.0 - lam_init)
    return o.reshape(o.shape[0], o.shape[1], ATTN_WIDTH)


def s5_mix(u, h0_re, h0_im, lam_re, lam_im, log_dt, b_re, b_im, c_re, c_im, d_skip):
    b, t, _ = u.shape
    f32 = jnp.float32
    uf = u.astype(f32).reshape(b, t, N_SSM_GROUPS, SSM_GROUP)
    lam = lax.complex(lam_re.astype(f32), lam_im.astype(f32))
    dt = jnp.exp(log_dt.astype(f32))[:, None]
    lam_bar = jnp.exp(lam * dt)
    b_bar = ((lam_bar - 1.0) / lam)[..., None] * lax.complex(b_re.astype(f32), b_im.astype(f32))
    bu = jnp.einsum("btgh,gph->btgp", uf.astype(jnp.complex64), b_bar)
    h0 = lax.complex(h0_re.astype(f32), h0_im.astype(f32))
    bu = bu.at[:, 0].add(lam_bar * h0)
    a = jnp.broadcast_to(lam_bar, bu.shape)

    def combine(e1, e2):
        a1, b1 = e1
        a2, b2 = e2
        return a1 * a2, a2 * b1 + b2

    _, h = lax.associative_scan(combine, (a, bu), axis=1)
    c = lax.complex(c_re.astype(f32), c_im.astype(f32))
    y = jnp.real(jnp.einsum("btgp,ghp->btgh", h, c)) + d_skip.astype(f32) * uf
    h_last = h[:, -1]
    return y.reshape(b, t, SSM_WIDTH).astype(u.dtype), jnp.real(h_last), jnp.imag(h_last)


def merge_groups(attn_o, ssm_y, w_glu, g_ssm_out, w_out):
    y = jax.nn.gelu(ssm_y)
    y = y * jax.nn.sigmoid(y @ w_glu)
    y = rms_norm(y, g_ssm_out)
    return jnp.concatenate([attn_o, y.astype(attn_o.dtype)], axis=-1) @ w_out


def mem_kv(mem, g_mem_kv, w_mk, w_mv):
    b = mem.shape[0]
    mn = rms_norm(mem, g_mem_kv)
    k = (mn @ w_mk).reshape(b, N_MEM, N_MEM_HEADS, MEM_HEAD_DIM)
    v = (mn @ w_mv).reshape(b, N_MEM, N_MEM_HEADS, MEM_HEAD_DIM)
    return k, v


def mem_attend(hn, mk, mv, w_mq, w_mo):
    b, t, _ = hn.shape
    q = (hn @ w_mq).reshape(b, t, N_MEM_HEADS, MEM_HEAD_DIM)
    s = jnp.einsum("bqhd,bkhd->bhqk", q, mk.astype(q.dtype), preferred_element_type=jnp.float32)
    p = jax.nn.softmax(s * (MEM_HEAD_DIM ** -0.5), axis=-1)
    o = jnp.einsum("bhqk,bkhd->bqhd", p.astype(q.dtype), mv.astype(q.dtype)).reshape(b, t, D_MODEL)
    return o @ w_mo


def swiglu(hn, w_gate, w_up, w_down):
    return (jax.nn.silu(hn @ w_gate) * (hn @ w_up)) @ w_down


def setup_inputs(seed: int = 0) -> dict:
    key = jax.random.key(seed)
    keys = iter(jax.random.split(key, 64))
    f32 = jnp.float32
    n_pages = PAST_LEN // PAGE_SIZE
    n_phys = (DEC_BATCH * n_pages * 5) // 4

    def normal(shape, scale):
        return jax.random.normal(next(keys), shape, f32) * scale

    def gain(shape):
        return 1.0 + normal(shape, 0.02)

    d = {}
    d["x_prompt"] = normal((BATCH, SEQ, D_MODEL), 1.0)
    d["x_sample"] = normal((DEC_BATCH, DEC_SEQ, D_MODEL), 1.0)
    d["mem_prompt"] = normal((BATCH, N_MEM, D_MODEL), 1.0)
    d["cache_k"] = normal((DEPTH, n_phys, PAGE_SIZE, N_DIFF_HEADS, 2 * DIFF_HEAD_DIM), 1.0)
    d["cache_v"] = normal((DEPTH, n_phys, PAGE_SIZE, N_DIFF_HEADS, 2 * DIFF_HEAD_DIM), 1.0)
    d["page_table"] = jax.random.permutation(next(keys), n_phys)[: DEC_BATCH * n_pages].reshape(DEC_BATCH, n_pages).astype(jnp.int32)
    d["state_ssm_re"] = normal((DEPTH, DEC_BATCH, N_SSM_GROUPS, SSM_STATE), 0.1)
    d["state_ssm_im"] = normal((DEPTH, DEC_BATCH, N_SSM_GROUPS, SSM_STATE), 0.1)
    d["cache_mem_k"] = normal((DEPTH, DEC_BATCH, N_MEM, N_MEM_HEADS, MEM_HEAD_DIM), 1.0)
    d["cache_mem_v"] = normal((DEPTH, DEC_BATCH, N_MEM, N_MEM_HEADS, MEM_HEAD_DIM), 1.0)
    d["g_mix_pre"] = gain((DEPTH, D_MODEL))
    d["g_mix_post"] = gain((DEPTH, D_MODEL))
    d["w_in"] = normal((DEPTH, D_MODEL, IN_COLS), D_MODEL ** -0.5)
    d["lam_q1"] = normal((DEPTH, DIFF_HEAD_DIM), 0.1)
    d["lam_k1"] = normal((DEPTH, DIFF_HEAD_DIM), 0.1)
    d["lam_q2"] = normal((DEPTH, DIFF_HEAD_DIM), 0.1)
    d["lam_k2"] = normal((DEPTH, DIFF_HEAD_DIM), 0.1)
    d["g_subln"] = gain((DEPTH, 2 * DIFF_HEAD_DIM))
    n_idx = jnp.arange(SSM_STATE, dtype=f32)
    d["ssm_lam_re"] = -0.5 + normal((DEPTH, N_SSM_GROUPS, SSM_STATE), 0.01)
    d["ssm_lam_im"] = jnp.pi * n_idx + normal((DEPTH, N_SSM_GROUPS, SSM_STATE), 0.01)
    d["ssm_log_dt"] = jax.random.uniform(next(keys), (DEPTH, N_SSM_GROUPS), f32, minval=math.log(DT_MIN), maxval=math.log(DT_MAX))
    d["ssm_b_re"] = normal((DEPTH, N_SSM_GROUPS, SSM_STATE, SSM_GROUP), (0.5 / SSM_GROUP) ** 0.5)
    d["ssm_b_im"] = normal((DEPTH, N_SSM_GROUPS, SSM_STATE, SSM_GROUP), (0.5 / SSM_GROUP) ** 0.5)
    d["ssm_c_re"] = normal((DEPTH, N_SSM_GROUPS, SSM_GROUP, SSM_STATE), (0.5 / SSM_STATE) ** 0.5)
    d["ssm_c_im"] = normal((DEPTH, N_SSM_GROUPS, SSM_GROUP, SSM_STATE), (0.5 / SSM_STATE) ** 0.5)
    d["ssm_d"] = normal((DEPTH, N_SSM_GROUPS, SSM_GROUP), 1.0)
    d["w_glu"] = normal((DEPTH, SSM_WIDTH, SSM_WIDTH), SSM_WIDTH ** -0.5)
    d["g_ssm_out"] = gain((DEPTH, SSM_WIDTH))
    d["w_out"] = normal((DEPTH, MIX_WIDTH, D_MODEL), MIX_WIDTH ** -0.5)
    d["g_mem_pre"] = gain((DEPTH, D_MODEL))
    d["g_mem_post"] = gain((DEPTH, D_MODEL))
    d["g_mem_kv"] = gain((DEPTH, D_MODEL))
    d["w_mq"] = normal((DEPTH, D_MODEL, D_MODEL), D_MODEL ** -0.5)
    d["w_mk"] = normal((DEPTH, D_MODEL, D_MODEL), D_MODEL ** -0.5)
    d["w_mv"] = normal((DEPTH, D_MODEL, D_MODEL), D_MODEL ** -0.5)
    d["w_mo"] = normal((DEPTH, D_MODEL, D_MODEL), D_MODEL ** -0.5)
    d["g_ffn_pre"] = gain((DEPTH, D_MODEL))
    d["g_ffn_post"] = gain((DEPTH, D_MODEL))
    d["w_gate"] = normal((DEPTH, D_MODEL, D_FF), D_MODEL ** -0.5)
    d["w_up"] = normal((DEPTH, D_MODEL, D_FF), D_MODEL ** -0.5)
    d["w_down"] = normal((DEPTH, D_FF, D_MODEL), D_FF ** -0.5)
    return d


def reference(x_prompt, x_sample, mem_prompt, cache_k, cache_v, page_table, state_ssm_re, state_ssm_im,
              cache_mem_k, cache_mem_v, g_mix_pre, g_mix_post, w_in, lam_q1, lam_k1, lam_q2, lam_k2, g_subln,
              ssm_lam_re, ssm_lam_im, ssm_log_dt, ssm_b_re, ssm_b_im, ssm_c_re, ssm_c_im, ssm_d, w_glu, g_ssm_out,
              w_out, g_mem_pre, g_mem_post, g_mem_kv, w_mq, w_mk, w_mv, w_mo, g_ffn_pre, g_ffn_post,
              w_gate, w_up, w_down):
    y_p, y_s = x_prompt, x_sample
    kp_l, vp_l, ks_l, vs_l = [], [], [], []
    rp_l, ip_l, rs_l, is_l = [], [], [], []
    mkp_l, mvp_l = [], []
    for l in range(DEPTH):
        lam, lam_init = diff_lambda(lam_q1[l], lam_k1[l], lam_q2[l], lam_k2[l], l)
        ssm_p = (ssm_lam_re[l], ssm_lam_im[l], ssm_log_dt[l], ssm_b_re[l], ssm_b_im[l],
                 ssm_c_re[l], ssm_c_im[l], ssm_d[l])
        hp = rms_norm(y_p, g_mix_pre[l])
        hs = rms_norm(y_s, g_mix_pre[l])
        qp, kp, vp, up = project_mixer(hp, w_in[l])
        qs, ks, vs, us = project_mixer(hs, w_in[l])
        att_p = finish_diff_heads(prompt_diff_attention(qp, kp, vp, lam), g_subln[l], lam_init)
        att_s = finish_diff_heads(sample_diff_attention(qs, ks, vs, cache_k[l], cache_v[l], page_table, lam),
                                  g_subln[l], lam_init)
        zeros = jnp.zeros((x_prompt.shape[0], N_SSM_GROUPS, SSM_STATE), jnp.float32)
        ssm_yp, rp, ip = s5_mix(up, zeros, zeros, *ssm_p)
        ssm_ys, rs, is_ = s5_mix(us, state_ssm_re[l], state_ssm_im[l], *ssm_p)
        y_p = y_p + rms_norm(merge_groups(att_p, ssm_yp, w_glu[l], g_ssm_out[l], w_out[l]), g_mix_post[l])
        y_s = y_s + rms_norm(merge_groups(att_s, ssm_ys, w_glu[l], g_ssm_out[l], w_out[l]), g_mix_post[l])
        mkp, mvp = mem_kv(mem_prompt, g_mem_kv[l], w_mk[l], w_mv[l])
        hp = rms_norm(y_p, g_mem_pre[l])
        hs = rms_norm(y_s, g_mem_pre[l])
        y_p = y_p + rms_norm(mem_attend(hp, mkp, mvp, w_mq[l], w_mo[l]), g_mem_post[l])
        y_s = y_s + rms_norm(mem_attend(hs, cache_mem_k[l], cache_mem_v[l], w_mq[l], w_mo[l]), g_mem_post[l])
        hp = rms_norm(y_p, g_ffn_pre[l])
        hs = rms_norm(y_s, g_ffn_pre[l])
        y_p = y_p + rms_norm(swiglu(hp, w_gate[l], w_up[l], w_down[l]), g_ffn_post[l])
        y_s = y_s + rms_norm(swiglu(hs, w_gate[l], w_up[l], w_down[l]), g_ffn_post[l])
        kp_l.append(kp); vp_l.append(vp); ks_l.append(ks); vs_l.append(vs)
        rp_l.append(rp); ip_l.append(ip); rs_l.append(rs); is_l.append(is_)
        mkp_l.append(mkp); mvp_l.append(mvp)
    new_k_prompt = jnp.stack(kp_l, 0)
    new_v_prompt = jnp.stack(vp_l, 0)
    new_k_sample = jnp.stack(ks_l, 0)
    new_v_sample = jnp.stack(vs_l, 0)
    ssm_re_prompt = jnp.stack(rp_l, 0)
    ssm_im_prompt = jnp.stack(ip_l, 0)
    ssm_re_sample = jnp.stack(rs_l, 0)
    ssm_im_sample = jnp.stack(is_l, 0)
    mem_k_prompt = jnp.stack(mkp_l, 0)
    mem_v_prompt = jnp.stack(mvp_l, 0)
    return (y_p, y_s, new_k_prompt, new_v_prompt, new_k_sample, new_v_sample,
            ssm_re_prompt, ssm_im_prompt, ssm_re_sample, ssm_im_sample, mem_k_prompt, mem_v_prompt)
```

```python
import functools
import math

import jax
import jax.numpy as jnp
from jax import lax
from jax.experimental import pallas as pl
from jax.experimental.pallas import tpu as pltpu

F32 = jnp.float32
BF16 = jnp.bfloat16

RMS_EPS = 1e-6
HEAD_DIM = 64
HEAD_W = 2 * HEAD_DIM
SSM_GROUP = 16
SSM_STATE = 64
S5_BLOCK = 4
S5_QUARTERS = 4
PAGE = 128
NEG = -1e30
VMEM_LIMIT_BYTES = 56 * 1024 * 1024


def _cparams(n_grid_dims):
    return pltpu.CompilerParams(
        dimension_semantics=("arbitrary",) * n_grid_dims,
        vmem_limit_bytes=VMEM_LIMIT_BYTES)


def _const_spec(shape):
    nd = len(shape)
    return pl.BlockSpec(shape, lambda *_: (0,) * nd, pipeline_mode=pl.Buffered(1))


def _rms(x, g):
    ms = jnp.mean(x * x, axis=-1, keepdims=True)
    return x * lax.rsqrt(ms + RMS_EPS) * g


def _dot(a, b):
    return jnp.dot(a, b, preferred_element_type=F32)


def _dot_nt(a, b, precision=None):
    return lax.dot_general(a, b, (((1,), (1,)), ((), ())),
                           preferred_element_type=F32, precision=precision)


def _proj_kernel(x_ref, g_ref, w_ref, q_ref, k_ref, v_ref, kb_ref, vb_ref, u_ref, *, aw):
    hn = _rms(x_ref[...], g_ref[...]).astype(BF16)
    q = _dot(hn, w_ref[:, 0:aw])
    q_ref[...] = (q * (HEAD_DIM ** -0.5)).astype(BF16)
    k = _dot(hn, w_ref[:, aw:2 * aw])
    k_ref[...] = k
    kb_ref[...] = k.astype(BF16)
    v = _dot(hn, w_ref[:, 2 * aw:3 * aw])
    v_ref[...] = v
    vb_ref[...] = v.astype(BF16)
    u_ref[...] = _dot(hn, w_ref[:, 3 * aw:]).astype(BF16)


def _proj(x, g, w_bf, aw, tm):
    n, d = x.shape
    sw = w_bf.shape[1] - 3 * aw
    row = lambda width: pl.BlockSpec((tm, width), lambda i: (i, 0))
    return pl.pallas_call(
        functools.partial(_proj_kernel, aw=aw),
        grid=(n // tm,),
        in_specs=[row(d), _const_spec((1, d)), _const_spec(w_bf.shape)],
        out_specs=[row(aw), row(aw), row(aw), row(aw), row(aw), row(sw)],
        out_shape=[
            jax.ShapeDtypeStruct((n, aw), BF16),
            jax.ShapeDtypeStruct((n, aw), F32),
            jax.ShapeDtypeStruct((n, aw), F32),
            jax.ShapeDtypeStruct((n, aw), BF16),
            jax.ShapeDtypeStruct((n, aw), BF16),
            jax.ShapeDtypeStruct((n, sw), BF16),
        ],
        compiler_params=_cparams(1),
        name="proj",
    )(x, g, w_bf)


def _diff_lambda(lamv, lam_init):
    d1 = jnp.sum(lamv[0:1] * lamv[1:2], axis=-1, keepdims=True)
    d2 = jnp.sum(lamv[2:3] * lamv[3:4], axis=-1, keepdims=True)
    return jnp.exp(d1) - jnp.exp(d2) + lam_init


def _attn_prompt_kernel(slope_ref, lamv_ref, g_ref, q_ref, k_ref, v_ref, o_ref,
                        qq_ref, m_ref, l_ref, acc_ref, *, tq, lam_init):
    h = pl.program_id(1)
    qi = pl.program_id(2)
    slope = slope_ref[h]
    tk = tq

    q = q_ref[...].astype(F32)
    lane = lax.broadcasted_iota(jnp.int32, q.shape, 1)
    qq_ref[0:tq, :] = jnp.where(lane < HEAD_DIM, q, 0.0).astype(BF16)
    qq_ref[tq:2 * tq, :] = jnp.where(lane >= HEAD_DIM, q, 0.0).astype(BF16)
    m_ref[...] = jnp.full(m_ref.shape, NEG, F32)
    l_ref[...] = jnp.zeros(l_ref.shape, F32)
    acc_ref[...] = jnp.zeros(acc_ref.shape, F32)

    col = lax.broadcasted_iota(jnp.int32, (1, tk), 1)

    def step(kj, masked):
        k0 = pl.multiple_of(kj * tk, tk)
        k = k_ref[pl.ds(k0, tk), :]
        v = v_ref[pl.ds(k0, tk), :]
        s = _dot_nt(qq_ref[...], k)
        rel = (col + (kj - qi) * tk).astype(F32)
        s = s + slope * rel
        if masked:
            r = lax.broadcasted_iota(jnp.int32, (2 * tq, tk), 0)
            r = jnp.where(r >= tq, r - tq, r)
            c = lax.broadcasted_iota(jnp.int32, (2 * tq, tk), 1)
            s = jnp.where(c <= r, s, NEG)
        m_old = m_ref[...]
        m_new = jnp.maximum(m_old, jnp.max(s, axis=-1, keepdims=True))
        alpha = jnp.exp(m_old - m_new)
        p = jnp.exp(s - m_new)
        l_ref[...] = alpha * l_ref[...] + jnp.sum(p, axis=-1, keepdims=True)
        acc_ref[...] = alpha * acc_ref[...] + _dot(p.astype(BF16), v)
        m_ref[...] = m_new

    def body(kj, carry):
        step(kj, False)
        return carry

    lax.fori_loop(0, qi, body, 0)
    step(qi, True)

    o = acc_ref[...] / l_ref[...]
    lam = _diff_lambda(lamv_ref[...], lam_init)
    d = o[0:tq] - lam * o[tq:2 * tq]
    o_ref[...] = (_rms(d, g_ref[...]) * (1.0 - lam_init)).astype(BF16)


def _attn_prompt(q, kb, vb, slopes, lamv, g_subln, lam_init, tq):
    b, t, aw = q.shape
    nh = aw // HEAD_W
    smem = pl.BlockSpec(memory_space=pltpu.SMEM)
    return pl.pallas_call(
        functools.partial(_attn_prompt_kernel, tq=tq, lam_init=lam_init),
        grid=(b, nh, t // tq),
        in_specs=[
            smem,
            _const_spec(lamv.shape),
            _const_spec(g_subln.shape),
            pl.BlockSpec((None, tq, HEAD_W), lambda bi, hi, qi: (bi, qi, hi)),
            pl.BlockSpec((None, t, HEAD_W), lambda bi, hi, qi: (bi, 0, hi)),
            pl.BlockSpec((None, t, HEAD_W), lambda bi, hi, qi: (bi, 0, hi)),
        ],
        out_specs=pl.BlockSpec((None, tq, HEAD_W), lambda bi, hi, qi: (bi, qi, hi)),
        out_shape=jax.ShapeDtypeStruct((b, t, aw), BF16),
        scratch_shapes=[
            pltpu.VMEM((2 * tq, HEAD_W), BF16),
            pltpu.VMEM((2 * tq, 1), F32),
            pltpu.VMEM((2 * tq, 1), F32),
            pltpu.VMEM((2 * tq, HEAD_W), F32),
        ],
        compiler_params=_cparams(3),
        name="attn_prompt",
    )(slopes, lamv, g_subln, q, kb, vb)


def _attn_sample_kernel(pt_ref, lamv_ref, gt_ref, slope_ref, hmask_ref, q_ref, kn_ref, vn_ref,
                        ck_hbm, cv_hbm, o_ref, kbuf, vbuf, sem, s_ref,
                        *, n_pages, n_new, lam_init, chunk):
    b = pl.program_id(0)
    nb = pl.num_programs(0)
    slot = b % 2
    past = n_pages * PAGE
    nrow = q_ref.shape[0]
    half = nrow // 2

    def page_copy(hbm, buf, kind, sl, p, page):
        return pltpu.make_async_copy(hbm.at[page], buf.at[sl, pl.ds(p * PAGE, PAGE)], sem.at[kind, sl])

    def start_fetch(seq, sl):
        for p in range(n_pages):
            page = pt_ref[seq * n_pages + p]
            page_copy(ck_hbm, kbuf, 0, sl, p, page).start()
            page_copy(cv_hbm, vbuf, 1, sl, p, page).start()

    @pl.when(b == 0)
    def _():
        start_fetch(0, 0)

    @pl.when(b + 1 < nb)
    def _():
        start_fetch(b + 1, 1 - slot)

    for p in range(n_pages):
        page_copy(ck_hbm, kbuf, 0, slot, p, 0).wait()
        page_copy(cv_hbm, vbuf, 1, slot, p, 0).wait()

    q = q_ref[...]
    for c in range(past // chunk):
        kc = kbuf[slot, c * chunk:(c + 1) * chunk, :].astype(BF16)
        s_ref[:, c * chunk:(c + 1) * chunk] = _dot_nt(q, kc)

    slope = slope_ref[...]
    kpos = lax.broadcasted_iota(jnp.int32, (1, past), 1).astype(F32) - float(past)
    s = s_ref[...] + slope * kpos

    qf = q.astype(F32)
    tok = lax.broadcasted_iota(jnp.int32, (nrow, 1), 0) & (n_new - 1)
    kn = kn_ref[...]
    vn = vn_ref[...]
    s_new = []
    for j in range(n_new):
        sj = jnp.sum(qf * kn[j:j + 1, :], axis=-1, keepdims=True) + slope * float(j)
        s_new.append(jnp.where(tok >= j, sj, NEG))

    m = jnp.max(s, axis=-1, keepdims=True)
    for sj in s_new:
        m = jnp.maximum(m, sj)
    p = jnp.exp(s - m)
    l = jnp.sum(p, axis=-1, keepdims=True)
    pb = p.astype(BF16)
    acc = jnp.zeros((nrow, vbuf.shape[-1]), F32)
    for c in range(past // chunk):
        vc = vbuf[slot, c * chunk:(c + 1) * chunk, :].astype(BF16)
        acc = acc + _dot(pb[:, c * chunk:(c + 1) * chunk], vc)
    for j, sj in enumerate(s_new):
        pj = jnp.exp(sj - m)
        l = l + pj
        acc = acc + pj * vn[j:j + 1, :]

    o = acc / l
    lam = _diff_lambda(lamv_ref[...], lam_init)
    d = (o[0:half] - lam * o[half:nrow]) * hmask_ref[...]
    ms = jnp.sum(d * d, axis=-1, keepdims=True) * (1.0 / HEAD_W)
    dn = d * lax.rsqrt(ms + RMS_EPS) * gt_ref[...] * (1.0 - lam_init)
    e = dn[0:8] + dn[8:16]
    o_ref[...] = e + pltpu.roll(e, 4, 0)


def _attn_sample(page_table, q32, kn, vn, ck, cv, lamv, g_tiled, slope_rows, hmask, lam_init):
    db, nrow, aw = q32.shape
    n_pages = page_table.shape[1]
    n_new = kn.shape[1]
    past = n_pages * PAGE
    assert n_new == 4 and nrow == 32, "row folding assumes 4 heads x 4 tokens x 2 branches"
    grid_spec = pltpu.PrefetchScalarGridSpec(
        num_scalar_prefetch=1,
        grid=(db,),
        in_specs=[
            _const_spec(lamv.shape),
            _const_spec(g_tiled.shape),
            _const_spec(slope_rows.shape),
            _const_spec(hmask.shape),
            pl.BlockSpec((None, nrow, aw), lambda i, pt: (i, 0, 0)),
            pl.BlockSpec((None, n_new, aw), lambda i, pt: (i, 0, 0)),
            pl.BlockSpec((None, n_new, aw), lambda i, pt: (i, 0, 0)),
            pl.BlockSpec(memory_space=pl.ANY),
            pl.BlockSpec(memory_space=pl.ANY),
        ],
        out_specs=pl.BlockSpec((None, 8, aw), lambda i, pt: (i, 0, 0)),
        scratch_shapes=[
            pltpu.VMEM((2, past, aw), F32),
            pltpu.VMEM((2, past, aw), F32),
            pltpu.SemaphoreType.DMA((2, 2)),
            pltpu.VMEM((nrow, past), F32),
        ],
    )
    return pl.pallas_call(
        functools.partial(_attn_sample_kernel, n_pages=n_pages, n_new=n_new,
                          lam_init=lam_init, chunk=256),
        grid_spec=grid_spec,
        out_shape=jax.ShapeDtypeStruct((db, 8, aw), F32),
        compiler_params=_cparams(1),
        name="attn_sample",
    )(page_table.reshape(-1), lamv, g_tiled, slope_rows, hmask, q32, kn, vn, ck, cv)


def _s5_prep_kernel(lre_ref, lim_ref, ldt_ref, bre_ref, bim_ref, cre_ref, cim_ref, dsk_ref,
                    wst_ref, cat_ref, kt_ref, sct_ref):
    lr = lre_ref[...]
    li = lim_ref[...]
    dt = jnp.exp(ldt_ref[...])

    def lam_bar_pow(n):
        mag = jnp.exp(n * (lr * dt))
        ang = n * (li * dt)
        return mag * jnp.cos(ang), mag * jnp.sin(ang)

    a_r, a_i = lam_bar_pow(1.0)
    den = lr * lr + li * li
    nr, ni = a_r - 1.0, a_i
    f_r = (nr * lr + ni * li) / den
    f_i = (ni * lr - nr * li) / den
    b_r, b_i = bre_ref[...], bim_ref[...]
    bb_r = f_r * b_r - f_i * b_i
    bb_i = f_r * b_i + f_i * b_r
    c_r, c_i = cre_ref[...], cim_ref[...]
    gw = b_r.shape[0]
    ns = b_r.shape[1]

    ktau = []
    for tau in range(S5_BLOCK):
        p_r, p_i = lam_bar_pow(float(tau))
        w_r = bb_r * p_r - bb_i * p_i
        w_i = bb_r * p_i + bb_i * p_r
        j = S5_BLOCK - 1 - tau
        wst_ref[j * gw:(j + 1) * gw, 0:ns] = w_r.astype(BF16)
        wst_ref[j * gw:(j + 1) * gw, ns:2 * ns] = w_i.astype(BF16)
        ktau.append(_dot_nt(w_r, c_r, lax.Precision.HIGHEST) - _dot_nt(w_i, c_i, lax.Precision.HIGHEST))
    rr = lax.broadcasted_iota(jnp.int32, (gw, gw), 0)
    cc = lax.broadcasted_iota(jnp.int32, (gw, gw), 1)
    ktau[0] = ktau[0] + jnp.where(rr == cc, dsk_ref[...], 0.0)

    for j in range(S5_BLOCK):
        p_r, p_i = lam_bar_pow(float(j + 1))
        cat_ref[j * gw:(j + 1) * gw, 0:ns] = (c_r * p_r - c_i * p_i).astype(BF16)
        cat_ref[j * gw:(j + 1) * gw, ns:2 * ns] = (-(c_r * p_i + c_i * p_r)).astype(BF16)

    zeros = jnp.zeros((gw, gw), BF16)
    for i in range(S5_BLOCK):
        for j in range(S5_BLOCK):
            blk = ktau[j - i].astype(BF16) if j >= i else zeros
            kt_ref[i * gw:(i + 1) * gw, j * gw:(j + 1) * gw] = blk

    row = lax.broadcasted_iota(jnp.int32, (8, ns), 0)
    for idx, k in enumerate((1, 2, 4)):
        p_r, p_i = lam_bar_pow(float(S5_BLOCK * k))
        sct_ref[2 * idx] = jnp.where(row >= k, p_r, 0.0)
        sct_ref[2 * idx + 1] = jnp.where(row >= k, p_i, 0.0)
    n = (S5_BLOCK * (row + 1)).astype(F32)
    mag = jnp.exp(n * (lr * dt))
    ang = n * (li * dt)
    sct_ref[6] = mag * jnp.cos(ang)
    sct_ref[7] = mag * jnp.sin(ang)


def _s5_prep(lre, lim, ldt, bre, bim, cre, cim, dsk):
    nq, gw, ns = bre.shape
    kw = S5_BLOCK * gw
    per_q = lambda *tail: pl.BlockSpec((None,) + tail, lambda q: (q,) + (0,) * len(tail))
    return pl.pallas_call(
        _s5_prep_kernel,
        grid=(nq,),
        in_specs=[per_q(1, ns), per_q(1, ns), per_q(1, ns), per_q(gw, ns), per_q(gw, ns),
                  per_q(gw, ns), per_q(gw, ns), per_q(1, gw)],
        out_specs=[per_q(kw, 2 * ns), per_q(kw, 2 * ns), per_q(kw, kw), per_q(8, 8, ns)],
        out_shape=[
            jax.ShapeDtypeStruct((nq, kw, 2 * ns), BF16),
            jax.ShapeDtypeStruct((nq, kw, 2 * ns), BF16),
            jax.ShapeDtypeStruct((nq, kw, kw), BF16),
            jax.ShapeDtypeStruct((nq, 8, 8, ns), F32),
        ],
        compiler_params=_cparams(1),
        name="s5_prep",
    )(lre, lim, ldt, bre, bim, cre, cim, dsk)


def _quarter_lanes(u_ref, q, gw, sw):
    return jnp.concatenate(
        [u_ref[:, j * sw + q * gw: j * sw + (q + 1) * gw] for j in range(S5_BLOCK)], axis=1)


def _store_quarter(y_ref, y, q, gw, sw):
    for j in range(S5_BLOCK):
        y_ref[:, j * sw + q * gw: j * sw + (q + 1) * gw] = y[:, j * gw:(j + 1) * gw].astype(y_ref.dtype)


def _s5_prompt_kernel(u_ref, wst_ref, cat_ref, kt_ref, sct_ref, y_ref, st_ref,
                      carry_ref, s_scr, hp_scr, *, rb, gw, sw, ns):
    blk = pl.program_id(1)

    @pl.when(blk == 0)
    def _():
        carry_ref[...] = jnp.zeros(carry_ref.shape, F32)

    row0 = lax.broadcasted_iota(jnp.int32, (8, ns), 0) == 0
    for q in range(S5_QUARTERS):
        uq = _quarter_lanes(u_ref, q, gw, sw)
        s_scr[...] = _dot(uq, wst_ref[q])
        tabs = [sct_ref[q, i] for i in range(8)]

        def tile(t, carry, tabs=tabs):
            cr, ci = carry
            r0 = pl.multiple_of(t * 8, 8)
            xr = s_scr[pl.ds(r0, 8), 0:ns]
            xi = s_scr[pl.ds(r0, 8), ns:2 * ns]
            for idx, k in enumerate((1, 2, 4)):
                mr, mi = tabs[2 * idx], tabs[2 * idx + 1]
                sr = pltpu.roll(xr, k, 0)
                si = pltpu.roll(xi, k, 0)
                xr, xi = xr + (mr * sr - mi * si), xi + (mr * si + mi * sr)
            crb = jnp.broadcast_to(cr, (8, ns))
            cib = jnp.broadcast_to(ci, (8, ns))
            hr = xr + (tabs[6] * crb - tabs[7] * cib)
            hi = xi + (tabs[6] * cib + tabs[7] * crb)
            hp_scr[pl.ds(r0, 8), 0:ns] = jnp.where(row0, crb, pltpu.roll(hr, 1, 0))
            hp_scr[pl.ds(r0, 8), ns:2 * ns] = jnp.where(row0, cib, pltpu.roll(hi, 1, 0))
            return hr[7:8, :], hi[7:8, :]

        cr, ci = lax.fori_loop(0, rb // 8, tile,
                               (carry_ref[q, :, 0:ns], carry_ref[q, :, ns:2 * ns]))
        carry_ref[q, :, 0:ns] = cr
        carry_ref[q, :, ns:2 * ns] = ci
        st_ref[q, :, 0:ns] = cr
        st_ref[q, :, ns:2 * ns] = ci
        y = _dot_nt(hp_scr[...].astype(BF16), cat_ref[q]) + _dot(uq, kt_ref[q])
        _store_quarter(y_ref, y, q, gw, sw)


def _s5_prompt(u4, wst, cat, kt, sct, rb):
    b, rows, lanes = u4.shape
    nq, kw, ns2 = wst.shape
    ns, gw, sw = ns2 // 2, kw // S5_BLOCK, lanes // S5_BLOCK
    return pl.pallas_call(
        functools.partial(_s5_prompt_kernel, rb=rb, gw=gw, sw=sw, ns=ns),
        grid=(b, rows // rb),
        in_specs=[
            pl.BlockSpec((None, rb, lanes), lambda bi, ri: (bi, ri, 0)),
            _const_spec(wst.shape), _const_spec(cat.shape), _const_spec(kt.shape), _const_spec(sct.shape),
        ],
        out_specs=[
            pl.BlockSpec((None, rb, lanes), lambda bi, ri: (bi, ri, 0)),
            pl.BlockSpec((None, nq, 1, ns2), lambda bi, ri: (bi, 0, 0, 0)),
        ],
        out_shape=[
            jax.ShapeDtypeStruct((b, rows, lanes), BF16),
            jax.ShapeDtypeStruct((b, nq, 1, ns2), F32),
        ],
        scratch_shapes=[
            pltpu.VMEM((nq, 1, ns2), F32),
            pltpu.VMEM((rb, ns2), F32),
            pltpu.VMEM((rb, ns2), F32),
        ],
        compiler_params=_cparams(2),
        name="s5_prompt",
    )(u4, wst, cat, kt, sct)


def _s5_sample_kernel(u_ref, h0_ref, wst_ref, cat_ref, kt_ref, sct_ref, y_ref, hout_ref, *, gw, sw, ns):
    for q in range(S5_QUARTERS):
        uq = _quarter_lanes(u_ref, q, gw, sw)
        s = _dot(uq, wst_ref[q])
        h0 = h0_ref[q]
        h0r, h0i = h0[:, 0:ns], h0[:, ns:2 * ns]
        a4r = sct_ref[q, 6][0:1, :]
        a4i = sct_ref[q, 7][0:1, :]
        hout_ref[q, :, 0:ns] = s[:, 0:ns] + (a4r * h0r - a4i * h0i)
        hout_ref[q, :, ns:2 * ns] = s[:, ns:2 * ns] + (a4r * h0i + a4i * h0r)
        y = _dot_nt(h0.astype(BF16), cat_ref[q]) + _dot(uq, kt_ref[q])
        _store_quarter(y_ref, y, q, gw, sw)


def _s5_sample(u4, h0q, wst, cat, kt, sct):
    rows, lanes = u4.shape
    nq, kw, ns2 = wst.shape
    ns, gw, sw = ns2 // 2, kw // S5_BLOCK, lanes // S5_BLOCK
    return pl.pallas_call(
        functools.partial(_s5_sample_kernel, gw=gw, sw=sw, ns=ns),
        grid=(1,),
        in_specs=[_const_spec(u4.shape), _const_spec(h0q.shape), _const_spec(wst.shape),
                  _const_spec(cat.shape), _const_spec(kt.shape), _const_spec(sct.shape)],
        out_specs=[pl.BlockSpec(u4.shape, lambda i: (0, 0)),
                   pl.BlockSpec(h0q.shape, lambda i: (0, 0, 0))],
        out_shape=[jax.ShapeDtypeStruct(u4.shape, BF16), jax.ShapeDtypeStruct(h0q.shape, F32)],
        compiler_params=_cparams(1),
        name="s5_sample",
    )(u4, h0q, wst, cat, kt, sct)


def _mem_kv_kernel(x_ref, g_ref, wk_ref, wv_ref, k_ref, v_ref, kb_ref, vb_ref):
    hn = _rms(x_ref[...], g_ref[...]).astype(BF16)
    k = _dot(hn, wk_ref[...])
    k_ref[...] = k
    kb_ref[...] = k.astype(BF16)
    v = _dot(hn, wv_ref[...])
    v_ref[...] = v
    vb_ref[...] = v.astype(BF16)


def _mem_kv(mem, g, wk_bf, wv_bf, tm):
    n, d = mem.shape
    row = pl.BlockSpec((tm, d), lambda i: (i, 0))
    return pl.pallas_call(
        _mem_kv_kernel,
        grid=(n // tm,),
        in_specs=[row, _const_spec((1, d)), _const_spec(wk_bf.shape), _const_spec(wv_bf.shape)],
        out_specs=[row, row, row, row],
        out_shape=[jax.ShapeDtypeStruct((n, d), F32), jax.ShapeDtypeStruct((n, d), F32),
                   jax.ShapeDtypeStruct((n, d), BF16), jax.ShapeDtypeStruct((n, d), BF16)],
        compiler_params=_cparams(1),
        name="mem_kv",
    )(mem, g, wk_bf, wv_bf)


def _merge_and_query(x, att, sy, wglu_ref, gso_ref, wout_ref, gpost_ref, gmpre_ref, wmq_ref, mem_scale):
    aw = att.shape[1]
    y = jax.nn.gelu(sy.astype(F32))
    y = y * jax.nn.sigmoid(_dot(y.astype(BF16), wglu_ref[...]))
    y = _rms(y, gso_ref[...])
    mix = _dot(att, wout_ref[0:aw, :]) + _dot(y.astype(BF16), wout_ref[aw:, :])
    x1 = x + _rms(mix, gpost_ref[...])
    hq = _rms(x1, gmpre_ref[...]).astype(BF16)
    qm = (_dot(hq, wmq_ref[...]) * mem_scale).astype(BF16)
    return x1, qm


def _mix_mem_kernel(x_ref, att_ref, sy_ref, mk_ref, mv_ref, wglu_ref, gso_ref, wout_ref, gpost_ref,
                    gmpre_ref, wmq_ref, wmo_ref, gmpost_ref, o_ref, oh_ref, *, n_heads, mem_scale):
    x1, qm = _merge_and_query(x_ref[...], att_ref[...], sy_ref[...], wglu_ref, gso_ref, wout_ref,
                              gpost_ref, gmpre_ref, wmq_ref, mem_scale)
    hd = qm.shape[1] // n_heads
    for h in range(n_heads):
        sl = slice(h * hd, (h + 1) * hd)
        s = _dot_nt(qm[:, sl], mk_ref[:, sl])
        p = jnp.exp(s - jnp.max(s, axis=-1, keepdims=True))
        l = jnp.sum(p, axis=-1, keepdims=True)
        oh_ref[:, sl] = (_dot(p.astype(BF16), mv_ref[:, sl]) / l).astype(BF16)
    mo = _dot(oh_ref[...], wmo_ref[...])
    o_ref[...] = x1 + _rms(mo, gmpost_ref[...])


def _mix_mem(x, att, sy, mkb, mvb, w, tm, n_heads):
    b, t, d = x.shape
    aw, sw = att.shape[2], sy.shape[2]
    nm = mkb.shape[1]
    tok = lambda width: pl.BlockSpec((None, tm, width), lambda bi, ti: (bi, ti, 0))
    per_b = pl.BlockSpec((None, nm, d), lambda bi, ti: (bi, 0, 0))
    weights = [w["w_glu"], w["g_ssm_out"], w["w_out"], w["g_mix_post"], w["g_mem_pre"], w["w_mq"],
               w["w_mo"], w["g_mem_post"]]
    return pl.pallas_call(
        functools.partial(_mix_mem_kernel, n_heads=n_heads, mem_scale=(d // n_heads) ** -0.5),
        grid=(b, t // tm),
        in_specs=[tok(d), tok(aw), tok(sw), per_b, per_b] + [_const_spec(a.shape) for a in weights],
        out_specs=tok(d),
        out_shape=jax.ShapeDtypeStruct((b, t, d), F32),
        scratch_shapes=[pltpu.VMEM((tm, d), BF16)],
        compiler_params=_cparams(2),
        name="mix_mem",
    )(x, att, sy, mkb, mvb, *weights)


def _mix_q_kernel(x_ref, att_ref, sy_ref, wglu_ref, gso_ref, wout_ref, gpost_ref, gmpre_ref, wmq_ref,
                  x1_ref, qm_ref, *, mem_scale):
    x1, qm = _merge_and_query(x_ref[...], att_ref[...], sy_ref[...], wglu_ref, gso_ref, wout_ref,
                              gpost_ref, gmpre_ref, wmq_ref, mem_scale)
    x1_ref[...] = x1
    qm_ref[...] = qm


def _mix_q(x, att, sy, w, n_heads):
    n, d = x.shape
    weights = [w["w_glu"], w["g_ssm_out"], w["w_out"], w["g_mix_post"], w["g_mem_pre"], w["w_mq"]]
    ins = [x, att, sy] + weights
    return pl.pallas_call(
        functools.partial(_mix_q_kernel, mem_scale=(d // n_heads) ** -0.5),
        grid=(1,),
        in_specs=[_const_spec(a.shape) for a in ins],
        out_specs=[pl.BlockSpec((n, d), lambda i: (0, 0)), pl.BlockSpec((n, d), lambda i: (0, 0))],
        out_shape=[jax.ShapeDtypeStruct((n, d), F32), jax.ShapeDtypeStruct((n, d), BF16)],
        compiler_params=_cparams(1),
        name="mix_q",
    )(*ins)


def _mem_sample_kernel(hmask_ref, q_ref, k_ref, v_ref, o_ref, *, bb):
    for i in range(bb):
        q = q_ref[i]
        s = _dot_nt(q, k_ref[i].astype(BF16))
        p = jnp.exp(s - jnp.max(s, axis=-1, keepdims=True))
        l = jnp.sum(p, axis=-1, keepdims=True)
        o = _dot(p.astype(BF16), v_ref[i].astype(BF16)) / l * hmask_ref[...]
        e = o[0:8] + o[8:16]
        o_ref[i] = e + pltpu.roll(e, 4, 0)


def _mem_sample(q16, ck, cv, hmask, bb):
    db, nrow, d = q16.shape
    nm = ck.shape[1]
    assert nrow == 16, "row folding assumes 4 heads x 4 tokens"
    return pl.pallas_call(
        functools.partial(_mem_sample_kernel, bb=bb),
        grid=(db // bb,),
        in_specs=[
            _const_spec(hmask.shape),
            pl.BlockSpec((bb, nrow, d), lambda i: (i, 0, 0)),
            pl.BlockSpec((bb, nm, d), lambda i: (i, 0, 0)),
            pl.BlockSpec((bb, nm, d), lambda i: (i, 0, 0)),
        ],
        out_specs=pl.BlockSpec((bb, 8, d), lambda i: (i, 0, 0)),
        out_shape=jax.ShapeDtypeStruct((db, 8, d), F32),
        compiler_params=_cparams(1),
        name="mem_sample",
    )(hmask, q16, ck, cv)


def _mem_out_kernel(x1_ref, o_ref, wmo_ref, g_ref, y_ref):
    mo = _dot(o_ref[...], wmo_ref[...])
    y_ref[...] = x1_ref[...] + _rms(mo, g_ref[...])


def _mem_out(x1, o, wmo, g):
    n, d = x1.shape
    ins = [x1, o, wmo, g]
    return pl.pallas_call(
        _mem_out_kernel,
        grid=(1,),
        in_specs=[_const_spec(a.shape) for a in ins],
        out_specs=pl.BlockSpec((n, d), lambda i: (0, 0)),
        out_shape=jax.ShapeDtypeStruct((n, d), F32),
        compiler_params=_cparams(1),
        name="mem_out",
    )(*ins)


def _ffn_kernel(x_ref, gpre_ref, wg_ref, wu_ref, wd_ref, gpost_ref, o_ref, *, n_chunks):
    x = x_ref[...]
    hn = _rms(x, gpre_ref[...]).astype(BF16)
    ff = wg_ref.shape[1]
    cw = ff // n_chunks
    acc = jnp.zeros(x.shape, F32)
    for c in range(n_chunks):
        sl = slice(c * cw, (c + 1) * cw)
        a = jax.nn.silu(_dot(hn, wg_ref[:, sl])) * _dot(hn, wu_ref[:, sl])
        acc = acc + _dot(a.astype(BF16), wd_ref[sl, :])
    o_ref[...] = x + _rms(acc, gpost_ref[...])


def _ffn(x, gpre, wg, wu, wd, gpost, tm, n_chunks):
    n, d = x.shape
    row = pl.BlockSpec((tm, d), lambda i: (i, 0))
    weights = [gpre, wg, wu, wd, gpost]
    return pl.pallas_call(
        functools.partial(_ffn_kernel, n_chunks=n_chunks),
        grid=(n // tm,),
        in_specs=[row] + [_const_spec(a.shape) for a in weights],
        out_specs=row,
        out_shape=jax.ShapeDtypeStruct((n, d), F32),
        compiler_params=_cparams(1),
        name="ffn",
    )(x, *weights)


def _s5_param_layout(lam_re, lam_im, log_dt, b_re, b_im, c_re, c_im, d_skip):
    g, p, hc = b_re.shape
    nq = S5_QUARTERS
    gq = g // nq
    rowv = lambda a: a.reshape(nq, 1, gq * p)
    ldt = jnp.broadcast_to(log_dt[:, None], (g, p))
    eye = jnp.eye(gq, dtype=F32)

    def embed_b(b):
        bq = b.reshape(nq, gq, p, hc)
        return jnp.einsum("qgph,gk->qghkp", bq, eye).reshape(nq, gq * hc, gq * p)

    def embed_c(c):
        cq = c.reshape(nq, gq, hc, p)
        return jnp.einsum("qghp,gk->qghkp", cq, eye).reshape(nq, gq * hc, gq * p)

    return (rowv(lam_re), rowv(lam_im), rowv(ldt), embed_b(b_re), embed_b(b_im),
            embed_c(c_re), embed_c(c_im), d_skip.reshape(nq, 1, gq * hc))


def _state_to_quarters(re, im):
    n, g, p = re.shape
    gq = g // S5_QUARTERS
    f = lambda a: a.reshape(n, S5_QUARTERS, gq * p).transpose(1, 0, 2)
    return jnp.concatenate([f(re), f(im)], axis=-1)


def _state_from_quarters(st, g, p):
    n = st.shape[0]
    ns = st.shape[-1] // 2
    return st[..., :ns].reshape(n, g, p), st[..., ns:].reshape(n, g, p)


def kernel(x_prompt, x_sample, mem_prompt, cache_k, cache_v, page_table, state_ssm_re, state_ssm_im,
           cache_mem_k, cache_mem_v, g_mix_pre, g_mix_post, w_in, lam_q1, lam_k1, lam_q2, lam_k2, g_subln,
           ssm_lam_re, ssm_lam_im, ssm_log_dt, ssm_b_re, ssm_b_im, ssm_c_re, ssm_c_im, ssm_d, w_glu, g_ssm_out,
           w_out, g_mem_pre, g_mem_post, g_mem_kv, w_mq, w_mk, w_mv, w_mo, g_ffn_pre, g_ffn_post,
           w_gate, w_up, w_down):
    depth = w_in.shape[0]
    b, t, d = x_prompt.shape
    db, dt_, _ = x_sample.shape
    n_mem = mem_prompt.shape[1]
    n_mem_heads = cache_mem_k.shape[3]
    n_heads = cache_k.shape[3]
    aw = n_heads * HEAD_W
    g, p = ssm_lam_re.shape[1], ssm_lam_re.shape[2]
    sw = g * SSM_GROUP
    n_phys = cache_k.shape[1]
    assert dt_ == S5_BLOCK and t % (S5_BLOCK * 256) == 0

    slopes = 2.0 ** (-8.0 * jnp.arange(1, n_heads + 1, dtype=F32) / n_heads)
    row = lambda a: a.reshape(1, -1).astype(F32)
    bf = lambda a: a.astype(BF16)

    lane_head = jnp.arange(aw) // HEAD_W
    lane_branch = (jnp.arange(aw) % HEAD_W) // HEAD_DIM
    qmask = ((lane_head[None, None, :] == jnp.arange(n_heads)[None, :, None])
             & (lane_branch[None, None, :] == jnp.arange(2)[:, None, None]))
    hmask_attn = jnp.repeat((lane_head[None, :] == jnp.arange(n_heads)[:, None]), dt_, axis=0).astype(F32)
    slope_rows = jnp.tile(jnp.repeat(slopes, dt_), 2).reshape(-1, 1)
    mlane_head = jnp.arange(d) // (d // n_mem_heads)
    mmask = (mlane_head[None, :] == jnp.arange(n_mem_heads)[:, None])
    hmask_mem = jnp.repeat(mmask, dt_, axis=0).astype(F32)

    y_p = x_prompt
    y_s = x_sample.reshape(db * dt_, d)
    outs = {k: [] for k in ("kp", "vp", "ks", "vs", "rp", "ip", "rs", "is", "mkp", "mvp")}
    for l in range(depth):
        lam_init = 0.8 - 0.6 * math.exp(-0.3 * l)
        lamv = jnp.stack([lam_q1[l], lam_k1[l], lam_q2[l], lam_k2[l]]).astype(F32)
        gsub = row(g_subln[l])
        w_in_bf = bf(w_in[l])
        w = {
            "w_glu": bf(w_glu[l]), "g_ssm_out": row(g_ssm_out[l]), "w_out": bf(w_out[l]),
            "g_mix_post": row(g_mix_post[l]), "g_mem_pre": row(g_mem_pre[l]), "w_mq": bf(w_mq[l]),
            "w_mo": bf(w_mo[l]), "g_mem_post": row(g_mem_post[l]),
        }

        qp, kp, vp, kpb, vpb, up = _proj(y_p.reshape(b * t, d), row(g_mix_pre[l]), w_in_bf, aw, 512)
        qs, ks, vs, _, _, us = _proj(y_s, row(g_mix_pre[l]), w_in_bf, aw, 512)

        att_p = _attn_prompt(qp.reshape(b, t, aw), kpb.reshape(b, t, aw), vpb.reshape(b, t, aw),
                             slopes, lamv, gsub, lam_init, 256)
        qs3 = qs.reshape(db, dt_, aw)
        q32 = jnp.where(qmask[None, :, :, None, :], qs3[:, None, None, :, :], jnp.zeros((), BF16))
        q32 = q32.reshape(db, 2 * n_heads * dt_, aw)
        att_s8 = _attn_sample(page_table, q32, ks.reshape(db, dt_, aw), vs.reshape(db, dt_, aw),
                              cache_k[l].reshape(n_phys, PAGE, aw), cache_v[l].reshape(n_phys, PAGE, aw),
                              lamv, jnp.tile(gsub, (1, n_heads)), slope_rows, hmask_attn, lam_init)
        att_s = bf(att_s8[:, :dt_, :].reshape(db * dt_, aw))

        tabs = _s5_prep(*_s5_param_layout(ssm_lam_re[l], ssm_lam_im[l], ssm_log_dt[l], ssm_b_re[l],
                                          ssm_b_im[l], ssm_c_re[l], ssm_c_im[l], ssm_d[l]))
        syp, stp = _s5_prompt(up.reshape(b, t // S5_BLOCK, S5_BLOCK * sw), *tabs, 256)
        rp, ip = _state_from_quarters(stp.reshape(b, S5_QUARTERS, -1), g, p)
        sys_, sts = _s5_sample(us.reshape(db, S5_BLOCK * sw),
                               _state_to_quarters(state_ssm_re[l], state_ssm_im[l]), *tabs)
        rs, is_ = _state_from_quarters(sts.transpose(1, 0, 2), g, p)

        mkp, mvp, mkb, mvb = _mem_kv(mem_prompt.reshape(b * n_mem, d), row(g_mem_kv[l]),
                                     bf(w_mk[l]), bf(w_mv[l]), 512)
        y_p = _mix_mem(y_p, att_p, syp.reshape(b, t, sw), mkb.reshape(b, n_mem, d), mvb.reshape(b, n_mem, d),
                       w, 512, n_mem_heads)
        x1s, qms = _mix_q(y_s, att_s, sys_.reshape(db * dt_, sw), w, n_mem_heads)
        q16 = jnp.where(mmask[None, :, None, :], qms.reshape(db, 1, dt_, d), jnp.zeros((), BF16))
        o8 = _mem_sample(q16.reshape(db, n_mem_heads * dt_, d), cache_mem_k[l].reshape(db, n_mem, d),
                         cache_mem_v[l].reshape(db, n_mem, d), hmask_mem, 4)
        y_s = _mem_out(x1s, bf(o8[:, :dt_, :].reshape(db * dt_, d)), w["w_mo"], w["g_mem_post"])

        ffw = (row(g_ffn_pre[l]), bf(w_gate[l]), bf(w_up[l]), bf(w_down[l]), row(g_ffn_post[l]))
        y_p = _ffn(y_p.reshape(b * t, d), *ffw, 512, 2).reshape(b, t, d)
        y_s = _ffn(y_s, *ffw, 512, 2)

        outs["kp"].append(kp.reshape(b, t, n_heads, HEAD_W))
        outs["vp"].append(vp.reshape(b, t, n_heads, HEAD_W))
        outs["ks"].append(ks.reshape(db, dt_, n_heads, HEAD_W))
        outs["vs"].append(vs.reshape(db, dt_, n_heads, HEAD_W))
        outs["rp"].append(rp); outs["ip"].append(ip); outs["rs"].append(rs); outs["is"].append(is_)
        outs["mkp"].append(mkp.reshape(b, n_mem, n_mem_heads, d // n_mem_heads))
        outs["mvp"].append(mvp.reshape(b, n_mem, n_mem_heads, d // n_mem_heads))

    st = lambda k: jnp.stack(outs[k], 0)
    return (y_p, y_s.reshape(db, dt_, d), st("kp"), st("vp"), st("ks"), st("vs"),
            st("rp"), st("ip"), st("rs"), st("is"), st("mkp"), st("mvp"))
```

```python
import functools
import math

import jax
import jax.numpy as jnp
from jax import lax
from jax.experimental import pallas as pl
from jax.experimental.pallas import tpu as pltpu

F32 = jnp.float32
BF16 = jnp.bfloat16

RMS_EPS = 1e-6
HEAD_DIM = 64
HEAD_W = 2 * HEAD_DIM
SSM_GROUP = 16
SSM_STATE = 64
S5_BLOCK = 4
S5_QUARTERS = 4
PAGE = 128
NEG = -1e30
VMEM_LIMIT_BYTES = 56 * 1024 * 1024


def _cparams(n_grid_dims):
    return pltpu.CompilerParams(
        dimension_semantics=("arbitrary",) * n_grid_dims,
        vmem_limit_bytes=VMEM_LIMIT_BYTES)


def _const_spec(shape):
    nd = len(shape)
    return pl.BlockSpec(shape, lambda *_: (0,) * nd, pipeline_mode=pl.Buffered(1))


def _rms(x, g):
    ms = jnp.mean(x * x, axis=-1, keepdims=True)
    return x * lax.rsqrt(ms + RMS_EPS) * g


def _dot(a, b):
    return jnp.dot(a, b, preferred_element_type=F32)


def _dot_nt(a, b, precision=None):
    return lax.dot_general(a, b, (((1,), (1,)), ((), ())),
                           preferred_element_type=F32, precision=precision)


def _proj_kernel(x_ref, g_ref, w_ref, q_ref, k_ref, v_ref, kb_ref, vb_ref, u_ref, *, aw):
    hn = _rms(x_ref[...], g_ref[...]).astype(BF16)
    q = _dot(hn, w_ref[:, 0:aw])
    q_ref[...] = (q * (HEAD_DIM ** -0.5)).astype(BF16)
    k = _dot(hn, w_ref[:, aw:2 * aw])
    k_ref[...] = k
    kb_ref[...] = k.astype(BF16)
    v = _dot(hn, w_ref[:, 2 * aw:3 * aw])
    v_ref[...] = v
    vb_ref[...] = v.astype(BF16)
    u_ref[...] = _dot(hn, w_ref[:, 3 * aw:]).astype(BF16)


def _proj(x, g, w_bf, aw, tm):
    n, d = x.shape
    sw = w_bf.shape[1] - 3 * aw
    row = lambda width: pl.BlockSpec((tm, width), lambda i: (i, 0))
    return pl.pallas_call(
        functools.partial(_proj_kernel, aw=aw),
        grid=(n // tm,),
        in_specs=[row(d), _const_spec((1, d)), _const_spec(w_bf.shape)],
        out_specs=[row(aw), row(aw), row(aw), row(aw), row(aw), row(sw)],
        out_shape=[
            jax.ShapeDtypeStruct((n, aw), BF16),
            jax.ShapeDtypeStruct((n, aw), F32),
            jax.ShapeDtypeStruct((n, aw), F32),
            jax.ShapeDtypeStruct((n, aw), BF16),
            jax.ShapeDtypeStruct((n, aw), BF16),
            jax.ShapeDtypeStruct((n, sw), BF16),
        ],
        compiler_params=_cparams(1),
        name="proj",
    )(x, g, w_bf)


def _diff_lambda(lamv, lam_init):
    d1 = jnp.sum(lamv[0:1] * lamv[1:2], axis=-1, keepdims=True)
    d2 = jnp.sum(lamv[2:3] * lamv[3:4], axis=-1, keepdims=True)
    return jnp.exp(d1) - jnp.exp(d2) + lam_init


def _attn_prompt_kernel(slope_ref, lamv_ref, g_ref, qt_ref, k_ref, vt_ref, o_ref,
                        qq_ref, base_ref, m_ref, l_ref, acc_ref, *, tq, lam_init):
    h = pl.program_id(1)
    qi = pl.program_id(2)
    slope = slope_ref[h]
    tk = tq

    qt = qt_ref[...].astype(F32)
    frow = lax.broadcasted_iota(jnp.int32, qt.shape, 0)
    qq_ref[:, 0:tq] = jnp.where(frow < HEAD_DIM, qt, 0.0).astype(BF16)
    qq_ref[:, tq:2 * tq] = jnp.where(frow >= HEAD_DIM, qt, 0.0).astype(BF16)
    krow = lax.broadcasted_iota(jnp.int32, (tk, 2 * tq), 0)
    base_ref[...] = slope * krow.astype(F32)
    m_ref[...] = jnp.full(m_ref.shape, NEG, F32)
    l_ref[...] = jnp.zeros(l_ref.shape, F32)
    acc_ref[...] = jnp.zeros(acc_ref.shape, F32)

    def step(kj, masked):
        k0 = pl.multiple_of(kj * tk, tk)
        k = k_ref[pl.ds(k0, tk), :]
        vt = vt_ref[:, pl.ds(k0, tk)]
        t = _dot(k, qq_ref[...]) + base_ref[...]
        if masked:
            qcol = lax.broadcasted_iota(jnp.int32, (tk, 2 * tq), 1)
            qcol = jnp.where(qcol >= tq, qcol - tq, qcol)
            t = jnp.where(krow <= qcol, t, NEG)
        off = slope * jnp.full((1, 2 * tq), (kj - qi) * tk, jnp.int32).astype(F32)
        m_old = m_ref[...]
        m_new = jnp.maximum(m_old, jnp.max(t, axis=0, keepdims=True) + off)
        alpha = jnp.exp(m_old - m_new)
        p = jnp.exp(t - (m_new - off))
        l_ref[...] = alpha * l_ref[...] + jnp.sum(p, axis=0, keepdims=True)
        acc_ref[...] = alpha * acc_ref[...] + _dot(vt, p.astype(BF16))
        m_ref[...] = m_new

    def body(kj, carry):
        step(kj, False)
        return carry

    lax.fori_loop(0, qi, body, 0)
    step(qi, True)

    ot = acc_ref[...] / l_ref[...]
    lam = _diff_lambda(lamv_ref[...], lam_init)
    d = ot[:, 0:tq] - lam * ot[:, tq:2 * tq]
    ms = jnp.mean(d * d, axis=0, keepdims=True)
    o_ref[...] = (d * lax.rsqrt(ms + RMS_EPS) * g_ref[...] * (1.0 - lam_init)).astype(BF16)


def _attn_prompt(qt, kb, vt, slopes, lamv, g_col, lam_init, tq):
    b, aw, t = qt.shape
    nh = aw // HEAD_W
    smem = pl.BlockSpec(memory_space=pltpu.SMEM)
    return pl.pallas_call(
        functools.partial(_attn_prompt_kernel, tq=tq, lam_init=lam_init),
        grid=(b, nh, t // tq),
        in_specs=[
            smem,
            _const_spec(lamv.shape),
            _const_spec(g_col.shape),
            pl.BlockSpec((None, HEAD_W, tq), lambda bi, hi, qi: (bi, hi, qi)),
            pl.BlockSpec((None, t, HEAD_W), lambda bi, hi, qi: (bi, 0, hi)),
            pl.BlockSpec((None, HEAD_W, t), lambda bi, hi, qi: (bi, hi, 0)),
        ],
        out_specs=pl.BlockSpec((None, HEAD_W, tq), lambda bi, hi, qi: (bi, hi, qi)),
        out_shape=jax.ShapeDtypeStruct((b, aw, t), BF16),
        scratch_shapes=[
            pltpu.VMEM((HEAD_W, 2 * tq), BF16),
            pltpu.VMEM((tq, 2 * tq), F32),
            pltpu.VMEM((1, 2 * tq), F32),
            pltpu.VMEM((1, 2 * tq), F32),
            pltpu.VMEM((HEAD_W, 2 * tq), F32),
        ],
        compiler_params=_cparams(3),
        name="attn_prompt",
    )(slopes, lamv, g_col, qt, kb, vt)


def _attn_sample_kernel(pt_ref, lamv_ref, g_ref, slope2_ref, base2_ref, newmask_ref, q2_ref, kn_ref, vn_ref,
                        ck_hbm, cv_hbm, o_ref, kbuf, vbuf, sem, s_ref,
                        *, n_pages, page_rows, n_heads, lam_init, chunk):
    b = pl.program_id(0)
    nb = pl.num_programs(0)
    slot = b % 2
    past_rows = n_pages * page_rows
    nrow = q2_ref.shape[0] // 2

    def page_copy(hbm, buf, kind, sl, p, page):
        src = hbm.at[pl.ds(pl.multiple_of(page * page_rows, page_rows), page_rows)]
        return pltpu.make_async_copy(src, buf.at[sl, pl.ds(p * page_rows, page_rows)], sem.at[kind, sl])

    def start_fetch(seq, sl):
        for p in range(n_pages):
            page = pt_ref[seq * n_pages + p]
            page_copy(ck_hbm, kbuf, 0, sl, p, page).start()
            page_copy(cv_hbm, vbuf, 1, sl, p, page).start()

    @pl.when(b == 0)
    def _():
        start_fetch(0, 0)

    @pl.when(b + 1 < nb)
    def _():
        start_fetch(b + 1, 1 - slot)

    for p in range(n_pages):
        page_copy(ck_hbm, kbuf, 0, slot, p, 0).wait()
        page_copy(cv_hbm, vbuf, 1, slot, p, 0).wait()

    pad = jnp.zeros((chunk - kn_ref.shape[0], HEAD_W), F32)
    kbuf[slot, past_rows:past_rows + chunk, :] = jnp.concatenate([kn_ref[...], pad], axis=0)
    vbuf[slot, past_rows:past_rows + chunk, :] = jnp.concatenate([vn_ref[...], pad], axis=0)

    q2 = q2_ref[...]
    slope2 = slope2_ref[...]
    n_pairs = past_rows // (2 * chunk)
    for a in range(n_pairs):
        r0 = 2 * a * chunk
        ke = kbuf[slot, r0:r0 + chunk, :].astype(BF16)
        ko = kbuf[slot, r0 + chunk:r0 + 2 * chunk, :].astype(BF16)
        s2 = _dot_nt(q2, jnp.concatenate([ke, ko], axis=1))
        tok0 = float(2 * a * (chunk // n_heads) - past_rows // n_heads)
        s_ref[:, a * chunk:(a + 1) * chunk] = s2 + base2_ref[...] + slope2 * tok0
    q = q2[0:nrow, 0:HEAD_W]
    kn = kbuf[slot, past_rows:past_rows + chunk, :].astype(BF16)
    s_new = _dot_nt(q, kn) + newmask_ref[...]

    s2 = s_ref[...]
    m2 = jnp.max(s2, axis=-1, keepdims=True)
    m = jnp.maximum(jnp.maximum(m2[0:nrow], m2[nrow:2 * nrow]), jnp.max(s_new, axis=-1, keepdims=True))
    p2 = jnp.exp(s2 - jnp.concatenate([m, m], axis=0))
    p_new = jnp.exp(s_new - m)
    l2 = jnp.sum(p2, axis=-1, keepdims=True)
    l = l2[0:nrow] + l2[nrow:2 * nrow] + jnp.sum(p_new, axis=-1, keepdims=True)
    p2b = p2.astype(BF16)
    acc2 = jnp.zeros((2 * nrow, 2 * HEAD_W), F32)
    for a in range(n_pairs):
        r0 = 2 * a * chunk
        ve = vbuf[slot, r0:r0 + chunk, :].astype(BF16)
        vo = vbuf[slot, r0 + chunk:r0 + 2 * chunk, :].astype(BF16)
        acc2 = acc2 + _dot(p2b[:, a * chunk:(a + 1) * chunk], jnp.concatenate([ve, vo], axis=1))
    vn = vbuf[slot, past_rows:past_rows + chunk, :].astype(BF16)
    acc = acc2[0:nrow, 0:HEAD_W] + acc2[nrow:2 * nrow, HEAD_W:2 * HEAD_W] + _dot(p_new.astype(BF16), vn)

    o = acc / l
    lam = _diff_lambda(lamv_ref[...], lam_init)
    half = nrow // 2
    d = o[0:half] - lam * o[half:nrow]
    o_ref[...] = _rms(d, g_ref[...]) * (1.0 - lam_init)


def _attn_sample(page_table, q2, knf, vnf, ckf, cvf, lamv, g_row, slope2, base2, newmask, n_heads, lam_init):
    db, nrow2, _ = q2.shape
    nrow = nrow2 // 2
    n_pages = page_table.shape[1]
    chunk = base2.shape[1]
    page_rows = PAGE * n_heads
    past_rows = n_pages * page_rows
    assert past_rows % (2 * chunk) == 0 and knf.shape[1] <= chunk
    grid_spec = pltpu.PrefetchScalarGridSpec(
        num_scalar_prefetch=1,
        grid=(db,),
        in_specs=[
            _const_spec(lamv.shape),
            _const_spec(g_row.shape),
            _const_spec(slope2.shape),
            _const_spec(base2.shape),
            _const_spec(newmask.shape),
            pl.BlockSpec((None, nrow2, 2 * HEAD_W), lambda i, pt: (i, 0, 0)),
            pl.BlockSpec((None,) + knf.shape[1:], lambda i, pt: (i, 0, 0)),
            pl.BlockSpec((None,) + vnf.shape[1:], lambda i, pt: (i, 0, 0)),
            pl.BlockSpec(memory_space=pl.ANY),
            pl.BlockSpec(memory_space=pl.ANY),
        ],
        out_specs=pl.BlockSpec((None, nrow // 2, HEAD_W), lambda i, pt: (i, 0, 0)),
        scratch_shapes=[
            pltpu.VMEM((2, past_rows + chunk, HEAD_W), F32),
            pltpu.VMEM((2, past_rows + chunk, HEAD_W), F32),
            pltpu.SemaphoreType.DMA((2, 2)),
            pltpu.VMEM((nrow2, past_rows // 2), F32),
        ],
    )
    return pl.pallas_call(
        functools.partial(_attn_sample_kernel, n_pages=n_pages, page_rows=page_rows, n_heads=n_heads,
                          lam_init=lam_init, chunk=chunk),
        grid_spec=grid_spec,
        out_shape=jax.ShapeDtypeStruct((db, nrow // 2, HEAD_W), F32),
        compiler_params=_cparams(1),
        name="attn_sample",
    )(page_table.reshape(-1), lamv, g_row, slope2, base2, newmask, q2, knf, vnf, ckf, cvf)


def _s5_prep_kernel(lre_ref, lim_ref, ldt_ref, bre_ref, bim_ref, cre_ref, cim_ref, dsk_ref,
                    wst_ref, cat_ref, kt_ref, sct_ref):
    lr = lre_ref[...]
    li = lim_ref[...]
    dt = jnp.exp(ldt_ref[...])

    def lam_bar_pow(n):
        mag = jnp.exp(n * (lr * dt))
        ang = n * (li * dt)
        return mag * jnp.cos(ang), mag * jnp.sin(ang)

    a_r, a_i = lam_bar_pow(1.0)
    den = lr * lr + li * li
    nr, ni = a_r - 1.0, a_i
    f_r = (nr * lr + ni * li) / den
    f_i = (ni * lr - nr * li) / den
    b_r, b_i = bre_ref[...], bim_ref[...]
    bb_r = f_r * b_r - f_i * b_i
    bb_i = f_r * b_i + f_i * b_r
    c_r, c_i = cre_ref[...], cim_ref[...]
    gw = b_r.shape[0]
    ns = b_r.shape[1]

    ktau = []
    for tau in range(S5_BLOCK):
        p_r, p_i = lam_bar_pow(float(tau))
        w_r = bb_r * p_r - bb_i * p_i
        w_i = bb_r * p_i + bb_i * p_r
        j = S5_BLOCK - 1 - tau
        wst_ref[j * gw:(j + 1) * gw, 0:ns] = w_r.astype(BF16)
        wst_ref[j * gw:(j + 1) * gw, ns:2 * ns] = w_i.astype(BF16)
        ktau.append(_dot_nt(w_r, c_r, lax.Precision.HIGHEST) - _dot_nt(w_i, c_i, lax.Precision.HIGHEST))
    rr = lax.broadcasted_iota(jnp.int32, (gw, gw), 0)
    cc = lax.broadcasted_iota(jnp.int32, (gw, gw), 1)
    ktau[0] = ktau[0] + jnp.where(rr == cc, dsk_ref[...], 0.0)

    for j in range(S5_BLOCK):
        p_r, p_i = lam_bar_pow(float(j + 1))
        cat_ref[j * gw:(j + 1) * gw, 0:ns] = (c_r * p_r - c_i * p_i).astype(BF16)
        cat_ref[j * gw:(j + 1) * gw, ns:2 * ns] = (-(c_r * p_i + c_i * p_r)).astype(BF16)

    zeros = jnp.zeros((gw, gw), BF16)
    for i in range(S5_BLOCK):
        for j in range(S5_BLOCK):
            blk = ktau[j - i].astype(BF16) if j >= i else zeros
            kt_ref[i * gw:(i + 1) * gw, j * gw:(j + 1) * gw] = blk

    row = lax.broadcasted_iota(jnp.int32, (8, ns), 0)
    for idx, k in enumerate((1, 2, 4)):
        p_r, p_i = lam_bar_pow(float(S5_BLOCK * k))
        sct_ref[2 * idx] = jnp.where(row >= k, p_r, 0.0)
        sct_ref[2 * idx + 1] = jnp.where(row >= k, p_i, 0.0)
    n = (S5_BLOCK * (row + 1)).astype(F32)
    mag = jnp.exp(n * (lr * dt))
    ang = n * (li * dt)
    sct_ref[6] = mag * jnp.cos(ang)
    sct_ref[7] = mag * jnp.sin(ang)


def _s5_prep(lre, lim, ldt, bre, bim, cre, cim, dsk):
    nq, gw, ns = bre.shape
    kw = S5_BLOCK * gw
    per_q = lambda *tail: pl.BlockSpec((None,) + tail, lambda q: (q,) + (0,) * len(tail))
    return pl.pallas_call(
        _s5_prep_kernel,
        grid=(nq,),
        in_specs=[per_q(1, ns), per_q(1, ns), per_q(1, ns), per_q(gw, ns), per_q(gw, ns),
                  per_q(gw, ns), per_q(gw, ns), per_q(1, gw)],
        out_specs=[per_q(kw, 2 * ns), per_q(kw, 2 * ns), per_q(kw, kw), per_q(8, 8, ns)],
        out_shape=[
            jax.ShapeDtypeStruct((nq, kw, 2 * ns), BF16),
            jax.ShapeDtypeStruct((nq, kw, 2 * ns), BF16),
            jax.ShapeDtypeStruct((nq, kw, kw), BF16),
            jax.ShapeDtypeStruct((nq, 8, 8, ns), F32),
        ],
        compiler_params=_cparams(1),
        name="s5_prep",
    )(lre, lim, ldt, bre, bim, cre, cim, dsk)


def _quarter_lanes(u_ref, q, gw, sw):
    return jnp.concatenate(
        [u_ref[:, j * sw + q * gw: j * sw + (q + 1) * gw] for j in range(S5_BLOCK)], axis=1)


def _store_quarter(y_ref, y, q, gw, sw):
    for j in range(S5_BLOCK):
        y_ref[:, j * sw + q * gw: j * sw + (q + 1) * gw] = y[:, j * gw:(j + 1) * gw].astype(y_ref.dtype)


def _s5_prompt_kernel(u_ref, wst_ref, cat_ref, kt_ref, sct_ref, y_ref, st_ref,
                      carry_ref, s_scr, hp_scr, *, rb, gw, sw, ns):
    blk = pl.program_id(1)

    @pl.when(blk == 0)
    def _():
        carry_ref[...] = jnp.zeros(carry_ref.shape, F32)

    row0 = lax.broadcasted_iota(jnp.int32, (8, ns), 0) == 0
    for q in range(S5_QUARTERS):
        uq = _quarter_lanes(u_ref, q, gw, sw)
        s_scr[...] = _dot(uq, wst_ref[q])
        tabs = [sct_ref[q, i] for i in range(8)]

        def tile(t, carry, tabs=tabs):
            cr, ci = carry
            r0 = pl.multiple_of(t * 8, 8)
            xr = s_scr[pl.ds(r0, 8), 0:ns]
            xi = s_scr[pl.ds(r0, 8), ns:2 * ns]
            for idx, k in enumerate((1, 2, 4)):
                mr, mi = tabs[2 * idx], tabs[2 * idx + 1]
                sr = pltpu.roll(xr, k, 0)
                si = pltpu.roll(xi, k, 0)
                xr, xi = xr + (mr * sr - mi * si), xi + (mr * si + mi * sr)
            crb = jnp.broadcast_to(cr, (8, ns))
            cib = jnp.broadcast_to(ci, (8, ns))
            hr = xr + (tabs[6] * crb - tabs[7] * cib)
            hi = xi + (tabs[6] * cib + tabs[7] * crb)
            hp_scr[pl.ds(r0, 8), 0:ns] = jnp.where(row0, crb, pltpu.roll(hr, 1, 0))
            hp_scr[pl.ds(r0, 8), ns:2 * ns] = jnp.where(row0, cib, pltpu.roll(hi, 1, 0))
            return hr[7:8, :], hi[7:8, :]

        cr, ci = lax.fori_loop(0, rb // 8, tile,
                               (carry_ref[q, :, 0:ns], carry_ref[q, :, ns:2 * ns]))
        carry_ref[q, :, 0:ns] = cr
        carry_ref[q, :, ns:2 * ns] = ci
        st_ref[q, :, 0:ns] = cr
        st_ref[q, :, ns:2 * ns] = ci
        y = _dot_nt(hp_scr[...].astype(BF16), cat_ref[q]) + _dot(uq, kt_ref[q])
        _store_quarter(y_ref, y, q, gw, sw)


def _s5_prompt(u4, wst, cat, kt, sct, rb):
    b, rows, lanes = u4.shape
    nq, kw, ns2 = wst.shape
    ns, gw, sw = ns2 // 2, kw // S5_BLOCK, lanes // S5_BLOCK
    return pl.pallas_call(
        functools.partial(_s5_prompt_kernel, rb=rb, gw=gw, sw=sw, ns=ns),
        grid=(b, rows // rb),
        in_specs=[
            pl.BlockSpec((None, rb, lanes), lambda bi, ri: (bi, ri, 0)),
            _const_spec(wst.shape), _const_spec(cat.shape), _const_spec(kt.shape), _const_spec(sct.shape),
        ],
        out_specs=[
            pl.BlockSpec((None, rb, lanes), lambda bi, ri: (bi, ri, 0)),
            pl.BlockSpec((None, nq, 1, ns2), lambda bi, ri: (bi, 0, 0, 0)),
        ],
        out_shape=[
            jax.ShapeDtypeStruct((b, rows, lanes), BF16),
            jax.ShapeDtypeStruct((b, nq, 1, ns2), F32),
        ],
        scratch_shapes=[
            pltpu.VMEM((nq, 1, ns2), F32),
            pltpu.VMEM((rb, ns2), F32),
            pltpu.VMEM((rb, ns2), F32),
        ],
        compiler_params=_cparams(2),
        name="s5_prompt",
    )(u4, wst, cat, kt, sct)


def _s5_sample_kernel(u_ref, h0_ref, wst_ref, cat_ref, kt_ref, sct_ref, y_ref, hout_ref, *, gw, sw, ns):
    for q in range(S5_QUARTERS):
        uq = _quarter_lanes(u_ref, q, gw, sw)
        s = _dot(uq, wst_ref[q])
        h0 = h0_ref[q]
        h0r, h0i = h0[:, 0:ns], h0[:, ns:2 * ns]
        a4r = sct_ref[q, 6][0:1, :]
        a4i = sct_ref[q, 7][0:1, :]
        hout_ref[q, :, 0:ns] = s[:, 0:ns] + (a4r * h0r - a4i * h0i)
        hout_ref[q, :, ns:2 * ns] = s[:, ns:2 * ns] + (a4r * h0i + a4i * h0r)
        y = _dot_nt(h0.astype(BF16), cat_ref[q]) + _dot(uq, kt_ref[q])
        _store_quarter(y_ref, y, q, gw, sw)


def _s5_sample(u4, h0q, wst, cat, kt, sct):
    rows, lanes = u4.shape
    nq, kw, ns2 = wst.shape
    ns, gw, sw = ns2 // 2, kw // S5_BLOCK, lanes // S5_BLOCK
    return pl.pallas_call(
        functools.partial(_s5_sample_kernel, gw=gw, sw=sw, ns=ns),
        grid=(1,),
        in_specs=[_const_spec(u4.shape), _const_spec(h0q.shape), _const_spec(wst.shape),
                  _const_spec(cat.shape), _const_spec(kt.shape), _const_spec(sct.shape)],
        out_specs=[pl.BlockSpec(u4.shape, lambda i: (0, 0)),
                   pl.BlockSpec(h0q.shape, lambda i: (0, 0, 0))],
        out_shape=[jax.ShapeDtypeStruct(u4.shape, BF16), jax.ShapeDtypeStruct(h0q.shape, F32)],
        compiler_params=_cparams(1),
        name="s5_sample",
    )(u4, h0q, wst, cat, kt, sct)


def _mem_kv_kernel(x_ref, g_ref, wk_ref, wv_ref, k_ref, v_ref, kb_ref, vb_ref):
    hn = _rms(x_ref[...], g_ref[...]).astype(BF16)
    k = _dot(hn, wk_ref[...])
    k_ref[...] = k
    kb_ref[...] = k.astype(BF16)
    v = _dot(hn, wv_ref[...])
    v_ref[...] = v
    vb_ref[...] = v.astype(BF16)


def _mem_kv(mem, g, wk_bf, wv_bf, tm):
    n, d = mem.shape
    row = pl.BlockSpec((tm, d), lambda i: (i, 0))
    return pl.pallas_call(
        _mem_kv_kernel,
        grid=(n // tm,),
        in_specs=[row, _const_spec((1, d)), _const_spec(wk_bf.shape), _const_spec(wv_bf.shape)],
        out_specs=[row, row, row, row],
        out_shape=[jax.ShapeDtypeStruct((n, d), F32), jax.ShapeDtypeStruct((n, d), F32),
                   jax.ShapeDtypeStruct((n, d), BF16), jax.ShapeDtypeStruct((n, d), BF16)],
        compiler_params=_cparams(1),
        name="mem_kv",
    )(mem, g, wk_bf, wv_bf)


def _merge_and_query(x, att, sy, wglu_ref, gso_ref, wout_ref, gpost_ref, gmpre_ref, wmq_ref, mem_scale):
    aw = att.shape[1]
    y = jax.nn.gelu(sy.astype(F32))
    y = y * jax.nn.sigmoid(_dot(y.astype(BF16), wglu_ref[...]))
    y = _rms(y, gso_ref[...])
    mix = _dot(att, wout_ref[0:aw, :]) + _dot(y.astype(BF16), wout_ref[aw:, :])
    x1 = x + _rms(mix, gpost_ref[...])
    hq = _rms(x1, gmpre_ref[...]).astype(BF16)
    qm = (_dot(hq, wmq_ref[...]) * mem_scale).astype(BF16)
    return x1, qm


def _mix_mem_kernel(x_ref, att_ref, sy_ref, mk_ref, mv_ref, wglu_ref, gso_ref, wout_ref, gpost_ref,
                    gmpre_ref, wmq_ref, wmo_ref, gmpost_ref, o_ref, oh_ref, *, n_heads, mem_scale):
    x1, qm = _merge_and_query(x_ref[...], att_ref[...], sy_ref[...], wglu_ref, gso_ref, wout_ref,
                              gpost_ref, gmpre_ref, wmq_ref, mem_scale)
    hd = qm.shape[1] // n_heads
    for h in range(n_heads):
        sl = slice(h * hd, (h + 1) * hd)
        s = _dot_nt(qm[:, sl], mk_ref[:, sl])
        p = jnp.exp(s - jnp.max(s, axis=-1, keepdims=True))
        l = jnp.sum(p, axis=-1, keepdims=True)
        oh_ref[:, sl] = (_dot(p.astype(BF16), mv_ref[:, sl]) / l).astype(BF16)
    mo = _dot(oh_ref[...], wmo_ref[...])
    o_ref[...] = x1 + _rms(mo, gmpost_ref[...])


def _mix_mem(x, att, sy, mkb, mvb, w, tm, n_heads):
    b, t, d = x.shape
    aw, sw = att.shape[2], sy.shape[2]
    nm = mkb.shape[1]
    tok = lambda width: pl.BlockSpec((None, tm, width), lambda bi, ti: (bi, ti, 0))
    per_b = pl.BlockSpec((None, nm, d), lambda bi, ti: (bi, 0, 0))
    weights = [w["w_glu"], w["g_ssm_out"], w["w_out"], w["g_mix_post"], w["g_mem_pre"], w["w_mq"],
               w["w_mo"], w["g_mem_post"]]
    return pl.pallas_call(
        functools.partial(_mix_mem_kernel, n_heads=n_heads, mem_scale=(d // n_heads) ** -0.5),
        grid=(b, t // tm),
        in_specs=[tok(d), tok(aw), tok(sw), per_b, per_b] + [_const_spec(a.shape) for a in weights],
        out_specs=tok(d),
        out_shape=jax.ShapeDtypeStruct((b, t, d), F32),
        scratch_shapes=[pltpu.VMEM((tm, d), BF16)],
        compiler_params=_cparams(2),
        name="mix_mem",
    )(x, att, sy, mkb, mvb, *weights)


def _mix_q_kernel(x_ref, att_ref, sy_ref, wglu_ref, gso_ref, wout_ref, gpost_ref, gmpre_ref, wmq_ref,
                  x1_ref, qm_ref, *, mem_scale):
    x1, qm = _merge_and_query(x_ref[...], att_ref[...], sy_ref[...], wglu_ref, gso_ref, wout_ref,
                              gpost_ref, gmpre_ref, wmq_ref, mem_scale)
    x1_ref[...] = x1
    qm_ref[...] = qm


def _mix_q(x, att, sy, w, n_heads):
    n, d = x.shape
    weights = [w["w_glu"], w["g_ssm_out"], w["w_out"], w["g_mix_post"], w["g_mem_pre"], w["w_mq"]]
    ins = [x, att, sy] + weights
    return pl.pallas_call(
        functools.partial(_mix_q_kernel, mem_scale=(d // n_heads) ** -0.5),
        grid=(1,),
        in_specs=[_const_spec(a.shape) for a in ins],
        out_specs=[pl.BlockSpec((n, d), lambda i: (0, 0)), pl.BlockSpec((n, d), lambda i: (0, 0))],
        out_shape=[jax.ShapeDtypeStruct((n, d), F32), jax.ShapeDtypeStruct((n, d), BF16)],
        compiler_params=_cparams(1),
        name="mix_q",
    )(*ins)


def _mem_sample_kernel(hmask_ref, q_ref, k_ref, v_ref, o_ref, *, bb):
    for i in range(bb):
        q = q_ref[i]
        s = _dot_nt(q, k_ref[i].astype(BF16))
        p = jnp.exp(s - jnp.max(s, axis=-1, keepdims=True))
        l = jnp.sum(p, axis=-1, keepdims=True)
        o = _dot(p.astype(BF16), v_ref[i].astype(BF16)) / l * hmask_ref[...]
        e = o[0:8] + o[8:16]
        o_ref[i] = e + pltpu.roll(e, 4, 0)


def _mem_sample(q16, ck, cv, hmask, bb):
    db, nrow, d = q16.shape
    nm = ck.shape[1]
    assert nrow == 16, "row folding assumes 4 heads x 4 tokens"
    return pl.pallas_call(
        functools.partial(_mem_sample_kernel, bb=bb),
        grid=(db // bb,),
        in_specs=[
            _const_spec(hmask.shape),
            pl.BlockSpec((bb, nrow, d), lambda i: (i, 0, 0)),
            pl.BlockSpec((bb, nm, d), lambda i: (i, 0, 0)),
            pl.BlockSpec((bb, nm, d), lambda i: (i, 0, 0)),
        ],
        out_specs=pl.BlockSpec((bb, 8, d), lambda i: (i, 0, 0)),
        out_shape=jax.ShapeDtypeStruct((db, 8, d), F32),
        compiler_params=_cparams(1),
        name="mem_sample",
    )(hmask, q16, ck, cv)


def _mem_out_kernel(x1_ref, o_ref, wmo_ref, g_ref, y_ref):
    mo = _dot(o_ref[...], wmo_ref[...])
    y_ref[...] = x1_ref[...] + _rms(mo, g_ref[...])


def _mem_out(x1, o, wmo, g):
    n, d = x1.shape
    ins = [x1, o, wmo, g]
    return pl.pallas_call(
        _mem_out_kernel,
        grid=(1,),
        in_specs=[_const_spec(a.shape) for a in ins],
        out_specs=pl.BlockSpec((n, d), lambda i: (0, 0)),
        out_shape=jax.ShapeDtypeStruct((n, d), F32),
        compiler_params=_cparams(1),
        name="mem_out",
    )(*ins)


def _ffn_kernel(x_ref, gpre_ref, wg_ref, wu_ref, wd_ref, gpost_ref, o_ref, *, n_chunks):
    x = x_ref[...]
    hn = _rms(x, gpre_ref[...]).astype(BF16)
    ff = wg_ref.shape[1]
    cw = ff // n_chunks
    acc = jnp.zeros(x.shape, F32)
    for c in range(n_chunks):
        sl = slice(c * cw, (c + 1) * cw)
        a = jax.nn.silu(_dot(hn, wg_ref[:, sl])) * _dot(hn, wu_ref[:, sl])
        acc = acc + _dot(a.astype(BF16), wd_ref[sl, :])
    o_ref[...] = x + _rms(acc, gpost_ref[...])


def _ffn(x, gpre, wg, wu, wd, gpost, tm, n_chunks):
    n, d = x.shape
    row = pl.BlockSpec((tm, d), lambda i: (i, 0))
    weights = [gpre, wg, wu, wd, gpost]
    return pl.pallas_call(
        functools.partial(_ffn_kernel, n_chunks=n_chunks),
        grid=(n // tm,),
        in_specs=[row] + [_const_spec(a.shape) for a in weights],
        out_specs=row,
        out_shape=jax.ShapeDtypeStruct((n, d), F32),
        compiler_params=_cparams(1),
        name="ffn",
    )(x, *weights)


def _s5_param_layout(lam_re, lam_im, log_dt, b_re, b_im, c_re, c_im, d_skip):
    g, p, hc = b_re.shape
    nq = S5_QUARTERS
    gq = g // nq
    rowv = lambda a: a.reshape(nq, 1, gq * p)
    ldt = jnp.broadcast_to(log_dt[:, None], (g, p))
    eye = jnp.eye(gq, dtype=F32)

    def embed_b(b):
        bq = b.reshape(nq, gq, p, hc)
        return jnp.einsum("qgph,gk->qghkp", bq, eye).reshape(nq, gq * hc, gq * p)

    def embed_c(c):
        cq = c.reshape(nq, gq, hc, p)
        return jnp.einsum("qghp,gk->qghkp", cq, eye).reshape(nq, gq * hc, gq * p)

    return (rowv(lam_re), rowv(lam_im), rowv(ldt), embed_b(b_re), embed_b(b_im),
            embed_c(c_re), embed_c(c_im), d_skip.reshape(nq, 1, gq * hc))


def _state_to_quarters(re, im):
    n, g, p = re.shape
    gq = g // S5_QUARTERS
    f = lambda a: a.reshape(n, S5_QUARTERS, gq * p).transpose(1, 0, 2)
    return jnp.concatenate([f(re), f(im)], axis=-1)


def _state_from_quarters(st, g, p):
    n = st.shape[0]
    ns = st.shape[-1] // 2
    return st[..., :ns].reshape(n, g, p), st[..., ns:].reshape(n, g, p)


def kernel(x_prompt, x_sample, mem_prompt, cache_k, cache_v, page_table, state_ssm_re, state_ssm_im,
           cache_mem_k, cache_mem_v, g_mix_pre, g_mix_post, w_in, lam_q1, lam_k1, lam_q2, lam_k2, g_subln,
           ssm_lam_re, ssm_lam_im, ssm_log_dt, ssm_b_re, ssm_b_im, ssm_c_re, ssm_c_im, ssm_d, w_glu, g_ssm_out,
           w_out, g_mem_pre, g_mem_post, g_mem_kv, w_mq, w_mk, w_mv, w_mo, g_ffn_pre, g_ffn_post,
           w_gate, w_up, w_down):
    depth = w_in.shape[0]
    b, t, d = x_prompt.shape
    db, dt_, _ = x_sample.shape
    n_mem = mem_prompt.shape[1]
    n_mem_heads = cache_mem_k.shape[3]
    n_heads = cache_k.shape[3]
    aw = n_heads * HEAD_W
    g, p = ssm_lam_re.shape[1], ssm_lam_re.shape[2]
    sw = g * SSM_GROUP
    n_phys = cache_k.shape[1]
    assert dt_ == S5_BLOCK and t % (S5_BLOCK * 256) == 0

    slopes = 2.0 ** (-8.0 * jnp.arange(1, n_heads + 1, dtype=F32) / n_heads)
    row = lambda a: a.reshape(1, -1).astype(F32)
    bf = lambda a: a.astype(BF16)

    branch_lane = jnp.arange(HEAD_W) // HEAD_DIM
    nrow = 2 * n_heads * dt_
    chunk = 256
    r = jnp.arange(nrow)
    r_head, r_tok = (r // dt_) % n_heads, r % dt_
    r_slope = slopes[r_head][:, None]
    c = jnp.arange(chunk)
    c_tok, c_head = c // n_heads, c % n_heads
    own_head = c_head[None, :] == r_head[:, None]
    base = r_slope * c_tok[None, :].astype(F32) + jnp.where(own_head, 0.0, NEG)
    base2 = jnp.concatenate([base, base + r_slope * float(chunk // n_heads)], axis=0)
    slope2 = jnp.concatenate([r_slope, r_slope], axis=0)
    new_ok = own_head & (c[None, :] < dt_ * n_heads) & (c_tok[None, :] <= r_tok[:, None])
    newmask = jnp.where(new_ok, r_slope * c_tok[None, :].astype(F32), NEG)
    mlane_head = jnp.arange(d) // (d // n_mem_heads)
    mmask = (mlane_head[None, :] == jnp.arange(n_mem_heads)[:, None])
    hmask_mem = jnp.repeat(mmask, dt_, axis=0).astype(F32)

    y_p = x_prompt
    y_s = x_sample.reshape(db * dt_, d)
    outs = {k: [] for k in ("kp", "vp", "ks", "vs", "rp", "ip", "rs", "is", "mkp", "mvp")}
    for l in range(depth):
        lam_init = 0.8 - 0.6 * math.exp(-0.3 * l)
        lamv = jnp.stack([lam_q1[l], lam_k1[l], lam_q2[l], lam_k2[l]]).astype(F32)
        gsub = row(g_subln[l])
        w_in_bf = bf(w_in[l])
        w = {
            "w_glu": bf(w_glu[l]), "g_ssm_out": row(g_ssm_out[l]), "w_out": bf(w_out[l]),
            "g_mix_post": row(g_mix_post[l]), "g_mem_pre": row(g_mem_pre[l]), "w_mq": bf(w_mq[l]),
            "w_mo": bf(w_mo[l]), "g_mem_post": row(g_mem_post[l]),
        }

        qp, kp, vp, kpb, vpb, up = _proj(y_p.reshape(b * t, d), row(g_mix_pre[l]), w_in_bf, aw, 512)
        qs, ks, vs, _, _, us = _proj(y_s, row(g_mix_pre[l]), w_in_bf, aw, 512)

        att_pt = _attn_prompt(jnp.swapaxes(qp.reshape(b, t, aw), 1, 2), kpb.reshape(b, t, aw),
                              jnp.swapaxes(vpb.reshape(b, t, aw), 1, 2),
                              slopes, lamv, g_subln[l].reshape(HEAD_W, 1).astype(F32), lam_init, 256)
        att_p = jnp.swapaxes(att_pt, 1, 2)
        qh = qs.reshape(db, dt_, n_heads, HEAD_W).transpose(0, 2, 1, 3)
        zq = jnp.zeros((), BF16)
        q32 = jnp.stack([jnp.where(branch_lane == 0, qh, zq), jnp.where(branch_lane == 1, qh, zq)], axis=1)
        q32 = q32.reshape(db, nrow, HEAD_W)
        zq32 = jnp.zeros_like(q32)
        q2 = jnp.concatenate([jnp.concatenate([q32, zq32], axis=2), jnp.concatenate([zq32, q32], axis=2)], axis=1)
        att_s16 = _attn_sample(page_table, q2, ks.reshape(db, dt_ * n_heads, HEAD_W),
                               vs.reshape(db, dt_ * n_heads, HEAD_W),
                               cache_k[l].reshape(n_phys * PAGE * n_heads, HEAD_W),
                               cache_v[l].reshape(n_phys * PAGE * n_heads, HEAD_W),
                               lamv, gsub, slope2, base2, newmask, n_heads, lam_init)
        att_s = bf(att_s16.reshape(db, n_heads, dt_, HEAD_W).transpose(0, 2, 1, 3).reshape(db * dt_, aw))

        tabs = _s5_prep(*_s5_param_layout(ssm_lam_re[l], ssm_lam_im[l], ssm_log_dt[l], ssm_b_re[l],
                                          ssm_b_im[l], ssm_c_re[l], ssm_c_im[l], ssm_d[l]))
        syp, stp = _s5_prompt(up.reshape(b, t // S5_BLOCK, S5_BLOCK * sw), *tabs, 256)
        rp, ip = _state_from_quarters(stp.reshape(b, S5_QUARTERS, -1), g, p)
        sys_, sts = _s5_sample(us.reshape(db, S5_BLOCK * sw),
                               _state_to_quarters(state_ssm_re[l], state_ssm_im[l]), *tabs)
        rs, is_ = _state_from_quarters(sts.transpose(1, 0, 2), g, p)

        mkp, mvp, mkb, mvb = _mem_kv(mem_prompt.reshape(b * n_mem, d), row(g_mem_kv[l]),
                                     bf(w_mk[l]), bf(w_mv[l]), 512)
        y_p = _mix_mem(y_p, att_p, syp.reshape(b, t, sw), mkb.reshape(b, n_mem, d), mvb.reshape(b, n_mem, d),
                       w, 512, n_mem_heads)
        x1s, qms = _mix_q(y_s, att_s, sys_.reshape(db * dt_, sw), w, n_mem_heads)
        q16 = jnp.where(mmask[None, :, None, :], qms.reshape(db, 1, dt_, d), jnp.zeros((), BF16))
        o8 = _mem_sample(q16.reshape(db, n_mem_heads * dt_, d), cache_mem_k[l].reshape(db, n_mem, d),
                         cache_mem_v[l].reshape(db, n_mem, d), hmask_mem, 4)
        y_s = _mem_out(x1s, bf(o8[:, :dt_, :].reshape(db * dt_, d)), w["w_mo"], w["g_mem_post"])

        ffw = (row(g_ffn_pre[l]), bf(w_gate[l]), bf(w_up[l]), bf(w_down[l]), row(g_ffn_post[l]))
        y_p = _ffn(y_p.reshape(b * t, d), *ffw, 512, 2).reshape(b, t, d)
        y_s = _ffn(y_s, *ffw, 512, 2)

        outs["kp"].append(kp.reshape(b, t, n_heads, HEAD_W))
        outs["vp"].append(vp.reshape(b, t, n_heads, HEAD_W))
        outs["ks"].append(ks.reshape(db, dt_, n_heads, HEAD_W))
        outs["vs"].append(vs.reshape(db, dt_, n_heads, HEAD_W))
        outs["rp"].append(rp); outs["ip"].append(ip); outs["rs"].append(rs); outs["is"].append(is_)
        outs["mkp"].append(mkp.reshape(b, n_mem, n_mem_heads, d // n_mem_heads))
        outs["mvp"].append(mvp.reshape(b, n_mem, n_mem_heads, d // n_mem_heads))

    st = lambda k: jnp.stack(outs[k], 0)
    return (y_p, y_s.reshape(db, dt_, d), st("kp"), st("vp"), st("ks"), st("vs"),
            st("rp"), st("ip"), st("rs"), st("is"), st("mkp"), st("mvp"))
```

```python
import functools
import math

import jax
import jax.numpy as jnp
from jax import lax
from jax.experimental import pallas as pl
from jax.experimental.pallas import tpu as pltpu

F32 = jnp.float32
BF16 = jnp.bfloat16

RMS_EPS = 1e-6
HEAD_DIM = 64
HEAD_W = 2 * HEAD_DIM
SSM_GROUP = 16
SSM_STATE = 64
S5_BLOCK = 4
S5_QUARTERS = 4
PAGE = 128
NEG = -1e30
LOG2E = 1.4426950408889634
VMEM_LIMIT_BYTES = 56 * 1024 * 1024


def _cparams(n_grid_dims):
    return pltpu.CompilerParams(
        dimension_semantics=("arbitrary",) * n_grid_dims,
        vmem_limit_bytes=VMEM_LIMIT_BYTES)


def _const_spec(shape):
    nd = len(shape)
    return pl.BlockSpec(shape, lambda *_: (0,) * nd, pipeline_mode=pl.Buffered(1))


def _rms(x, g):
    ms = jnp.mean(x * x, axis=-1, keepdims=True)
    return x * lax.rsqrt(ms + RMS_EPS) * g


def _dot(a, b):
    return jnp.dot(a, b, preferred_element_type=F32)


def _dot_nt(a, b, precision=None):
    return lax.dot_general(a, b, (((1,), (1,)), ((), ())),
                           preferred_element_type=F32, precision=precision)


def _proj_kernel(x_ref, g_ref, w_ref, q_ref, k_ref, v_ref, kb_ref, vb_ref, u_ref, *, aw):
    hn = _rms(x_ref[...], g_ref[...]).astype(BF16)
    q = _dot(hn, w_ref[:, 0:aw])
    q_ref[...] = (q * (HEAD_DIM ** -0.5 * LOG2E)).astype(BF16)
    k = _dot(hn, w_ref[:, aw:2 * aw])
    k_ref[...] = k
    kb_ref[...] = k.astype(BF16)
    v = _dot(hn, w_ref[:, 2 * aw:3 * aw])
    v_ref[...] = v
    vb_ref[...] = v.astype(BF16)
    u_ref[...] = _dot(hn, w_ref[:, 3 * aw:]).astype(BF16)


def _proj(x, g, w_bf, aw, tm):
    n, d = x.shape
    sw = w_bf.shape[1] - 3 * aw
    row = lambda width: pl.BlockSpec((tm, width), lambda i: (i, 0))
    return pl.pallas_call(
        functools.partial(_proj_kernel, aw=aw),
        grid=(n // tm,),
        in_specs=[row(d), _const_spec((1, d)), _const_spec(w_bf.shape)],
        out_specs=[row(aw), row(aw), row(aw), row(aw), row(aw), row(sw)],
        out_shape=[
            jax.ShapeDtypeStruct((n, aw), BF16),
            jax.ShapeDtypeStruct((n, aw), F32),
            jax.ShapeDtypeStruct((n, aw), F32),
            jax.ShapeDtypeStruct((n, aw), BF16),
            jax.ShapeDtypeStruct((n, aw), BF16),
            jax.ShapeDtypeStruct((n, sw), BF16),
        ],
        compiler_params=_cparams(1),
        name="proj",
    )(x, g, w_bf)


def _diff_lambda(lamv, lam_init):
    d1 = jnp.sum(lamv[0:1] * lamv[1:2], axis=-1, keepdims=True)
    d2 = jnp.sum(lamv[2:3] * lamv[3:4], axis=-1, keepdims=True)
    return jnp.exp(d1) - jnp.exp(d2) + lam_init


def _attn_prompt_kernel(sl_ref, lamv_ref, g_ref, pos_ref, qt_ref, k_ref, vt_ref, o_ref,
                        qq_ref, m_ref, acc_ref, t_ref, *, tq, lam_init):
    h = pl.program_id(1)
    qi = pl.program_id(2)
    tk = tq

    qt = qt_ref[...].astype(F32)
    frow = lax.broadcasted_iota(jnp.int32, qt.shape, 0)
    qq_ref[0:HEAD_W, 0:tq] = jnp.where(frow < HEAD_DIM, qt, 0.0).astype(BF16)
    qq_ref[0:HEAD_W, tq:2 * tq] = jnp.where(frow >= HEAD_DIM, qt, 0.0).astype(BF16)
    arow = lax.broadcasted_iota(jnp.int32, (HEAD_W, 2 * tq), 0)
    slope_parts = jnp.where((arow & 1) == 0, sl_ref[1, h], sl_ref[2, h])
    qq_ref[HEAD_W:2 * HEAD_W, :] = jnp.where(arow < 4, slope_parts, 0.0).astype(BF16)
    m_ref[...] = jnp.full(m_ref.shape, NEG, F32)
    acc_ref[...] = jnp.zeros(acc_ref.shape, F32)
    ones = jnp.ones((acc_ref.shape[0] - HEAD_W, tk), BF16)

    def scores(kj):
        k0 = pl.multiple_of(kj * tk, tk)
        ka = jnp.concatenate([k_ref[pl.ds(k0, tk), :], pos_ref[...]], axis=1)
        return _dot(ka, qq_ref[...])

    def consume(kj, t):
        k0 = pl.multiple_of(kj * tk, tk)
        vta = jnp.concatenate([vt_ref[:, pl.ds(k0, tk)], ones], axis=0)
        off = sl_ref[0, h] * jnp.full((1, 2 * tq), (kj - qi) * tk, jnp.int32).astype(F32)
        m_old = m_ref[...]
        m_new = jnp.maximum(m_old, jnp.max(t, axis=0, keepdims=True) + off)
        alpha = jnp.exp2(m_old - m_new)
        p = jnp.exp2(t - (m_new - off)).astype(BF16)
        acc_ref[...] = alpha * acc_ref[...] + _dot(vta, p)
        m_ref[...] = m_new

    t_ref[...] = scores(0)

    def body(kj, carry):
        t = t_ref[...]
        t_ref[...] = scores(kj + 1)
        consume(kj, t)
        return carry

    lax.fori_loop(0, qi, body, 0)
    krow = lax.broadcasted_iota(jnp.int32, (tk, 2 * tq), 0)
    qcol = lax.broadcasted_iota(jnp.int32, (tk, 2 * tq), 1)
    qcol = jnp.where(qcol >= tq, qcol - tq, qcol)
    consume(qi, jnp.where(krow <= qcol, t_ref[...], NEG))

    acc = acc_ref[...]
    ot = acc[0:HEAD_W] / acc[HEAD_W:HEAD_W + 1]
    lam = _diff_lambda(lamv_ref[...], lam_init)
    d = ot[:, 0:tq] - lam * ot[:, tq:2 * tq]
    ms = jnp.mean(d * d, axis=0, keepdims=True)
    o_ref[...] = (d * lax.rsqrt(ms + RMS_EPS) * g_ref[...] * (1.0 - lam_init)).astype(BF16)


def _attn_prompt(qt, kb, vt, slope_parts, lamv, g_col, lam_init, tq):
    b, aw, t = qt.shape
    nh = aw // HEAD_W
    smem = pl.BlockSpec(memory_space=pltpu.SMEM)
    i = jnp.arange(tq)
    lane = jnp.arange(HEAD_W)
    feat = jnp.where(lane[None, :] < 2, (i % 256)[:, None], jnp.where(lane[None, :] < 4, (i // 256 * 256)[:, None], 0))
    pos = feat.astype(BF16)
    return pl.pallas_call(
        functools.partial(_attn_prompt_kernel, tq=tq, lam_init=lam_init),
        grid=(b, nh, t // tq),
        in_specs=[
            smem,
            _const_spec(lamv.shape),
            _const_spec(g_col.shape),
            _const_spec(pos.shape),
            pl.BlockSpec((None, HEAD_W, tq), lambda bi, hi, qi: (bi, hi, qi)),
            pl.BlockSpec((None, t, HEAD_W), lambda bi, hi, qi: (bi, 0, hi)),
            pl.BlockSpec((None, HEAD_W, t), lambda bi, hi, qi: (bi, hi, 0)),
        ],
        out_specs=pl.BlockSpec((None, HEAD_W, tq), lambda bi, hi, qi: (bi, hi, qi)),
        out_shape=jax.ShapeDtypeStruct((b, aw, t), BF16),
        scratch_shapes=[
            pltpu.VMEM((2 * HEAD_W, 2 * tq), BF16),
            pltpu.VMEM((1, 2 * tq), F32),
            pltpu.VMEM((HEAD_W + 16, 2 * tq), F32),
            pltpu.VMEM((tq, 2 * tq), F32),
        ],
        compiler_params=_cparams(3),
        name="attn_prompt",
    )(slope_parts, lamv, g_col, pos, qt, kb, vt)


def _attn_sample_kernel(pt_ref, lamv_ref, g_ref, slope2_ref, base2_ref, newmask_ref, q2_ref, kn_ref, vn_ref,
                        ck_hbm, cv_hbm, o_ref, kbuf, vbuf, sem, s_ref,
                        *, n_pages, page_rows, n_heads, lam_init, chunk):
    b = pl.program_id(0)
    nb = pl.num_programs(0)
    slot = b % 2
    past_rows = n_pages * page_rows
    nrow = q2_ref.shape[0] // 2

    def page_copy(hbm, buf, kind, sl, p, page):
        src = hbm.at[pl.ds(pl.multiple_of(page * page_rows, page_rows), page_rows)]
        return pltpu.make_async_copy(src, buf.at[sl, pl.ds(p * page_rows, page_rows)], sem.at[kind, sl])

    def start_fetch(seq, sl):
        for p in range(n_pages):
            page = pt_ref[seq * n_pages + p]
            page_copy(ck_hbm, kbuf, 0, sl, p, page).start()
            page_copy(cv_hbm, vbuf, 1, sl, p, page).start()

    @pl.when(b == 0)
    def _():
        start_fetch(0, 0)

    @pl.when(b + 1 < nb)
    def _():
        start_fetch(b + 1, 1 - slot)

    for p in range(n_pages):
        page_copy(ck_hbm, kbuf, 0, slot, p, 0).wait()
        page_copy(cv_hbm, vbuf, 1, slot, p, 0).wait()

    pad = jnp.zeros((chunk - kn_ref.shape[0], HEAD_W), F32)
    kbuf[slot, past_rows:past_rows + chunk, :] = jnp.concatenate([kn_ref[...], pad], axis=0)
    vbuf[slot, past_rows:past_rows + chunk, :] = jnp.concatenate([vn_ref[...], pad], axis=0)

    q2 = q2_ref[...]
    slope2 = slope2_ref[...]
    n_pairs = past_rows // (2 * chunk)
    for a in range(n_pairs):
        r0 = 2 * a * chunk
        ke = kbuf[slot, r0:r0 + chunk, :].astype(BF16)
        ko = kbuf[slot, r0 + chunk:r0 + 2 * chunk, :].astype(BF16)
        s2 = _dot_nt(q2, jnp.concatenate([ke, ko], axis=1))
        tok0 = float(2 * a * (chunk // n_heads) - past_rows // n_heads)
        s_ref[:, a * chunk:(a + 1) * chunk] = s2 + base2_ref[...] + slope2 * tok0
    q = q2[0:nrow, 0:HEAD_W]
    kn = kbuf[slot, past_rows:past_rows + chunk, :].astype(BF16)
    s_new = _dot_nt(q, kn) + newmask_ref[...]

    s2 = s_ref[...]
    m2 = jnp.max(s2, axis=-1, keepdims=True)
    m = jnp.maximum(jnp.maximum(m2[0:nrow], m2[nrow:2 * nrow]), jnp.max(s_new, axis=-1, keepdims=True))
    p2 = jnp.exp2(s2 - jnp.concatenate([m, m], axis=0))
    p_new = jnp.exp2(s_new - m)
    l2 = jnp.sum(p2, axis=-1, keepdims=True)
    l = l2[0:nrow] + l2[nrow:2 * nrow] + jnp.sum(p_new, axis=-1, keepdims=True)
    p2b = p2.astype(BF16)
    acc2 = jnp.zeros((2 * nrow, 2 * HEAD_W), F32)
    for a in range(n_pairs):
        r0 = 2 * a * chunk
        ve = vbuf[slot, r0:r0 + chunk, :].astype(BF16)
        vo = vbuf[slot, r0 + chunk:r0 + 2 * chunk, :].astype(BF16)
        acc2 = acc2 + _dot(p2b[:, a * chunk:(a + 1) * chunk], jnp.concatenate([ve, vo], axis=1))
    vn = vbuf[slot, past_rows:past_rows + chunk, :].astype(BF16)
    acc = acc2[0:nrow, 0:HEAD_W] + acc2[nrow:2 * nrow, HEAD_W:2 * HEAD_W] + _dot(p_new.astype(BF16), vn)

    o = acc / l
    lam = _diff_lambda(lamv_ref[...], lam_init)
    half = nrow // 2
    d = o[0:half] - lam * o[half:nrow]
    o_ref[...] = _rms(d, g_ref[...]) * (1.0 - lam_init)


def _attn_sample(page_table, q2, knf, vnf, ckf, cvf, lamv, g_row, slope2, base2, newmask, n_heads, lam_init):
    db, nrow2, _ = q2.shape
    nrow = nrow2 // 2
    n_pages = page_table.shape[1]
    chunk = base2.shape[1]
    page_rows = PAGE * n_heads
    past_rows = n_pages * page_rows
    assert past_rows % (2 * chunk) == 0 and knf.shape[1] <= chunk
    grid_spec = pltpu.PrefetchScalarGridSpec(
        num_scalar_prefetch=1,
        grid=(db,),
        in_specs=[
            _const_spec(lamv.shape),
            _const_spec(g_row.shape),
            _const_spec(slope2.shape),
            _const_spec(base2.shape),
            _const_spec(newmask.shape),
            pl.BlockSpec((None, nrow2, 2 * HEAD_W), lambda i, pt: (i, 0, 0)),
            pl.BlockSpec((None,) + knf.shape[1:], lambda i, pt: (i, 0, 0)),
            pl.BlockSpec((None,) + vnf.shape[1:], lambda i, pt: (i, 0, 0)),
            pl.BlockSpec(memory_space=pl.ANY),
            pl.BlockSpec(memory_space=pl.ANY),
        ],
        out_specs=pl.BlockSpec((None, nrow // 2, HEAD_W), lambda i, pt: (i, 0, 0)),
        scratch_shapes=[
            pltpu.VMEM((2, past_rows + chunk, HEAD_W), F32),
            pltpu.VMEM((2, past_rows + chunk, HEAD_W), F32),
            pltpu.SemaphoreType.DMA((2, 2)),
            pltpu.VMEM((nrow2, past_rows // 2), F32),
        ],
    )
    return pl.pallas_call(
        functools.partial(_attn_sample_kernel, n_pages=n_pages, page_rows=page_rows, n_heads=n_heads,
                          lam_init=lam_init, chunk=chunk),
        grid_spec=grid_spec,
        out_shape=jax.ShapeDtypeStruct((db, nrow // 2, HEAD_W), F32),
        compiler_params=_cparams(1),
        name="attn_sample",
    )(page_table.reshape(-1), lamv, g_row, slope2, base2, newmask, q2, knf, vnf, ckf, cvf)


def _s5_prep_kernel(lre_ref, lim_ref, ldt_ref, bre_ref, bim_ref, cre_ref, cim_ref, dsk_ref,
                    wst_ref, cat_ref, kt_ref, sct_ref):
    lr = lre_ref[...]
    li = lim_ref[...]
    dt = jnp.exp(ldt_ref[...])

    def lam_bar_pow(n):
        mag = jnp.exp(n * (lr * dt))
        ang = n * (li * dt)
        return mag * jnp.cos(ang), mag * jnp.sin(ang)

    a_r, a_i = lam_bar_pow(1.0)
    den = lr * lr + li * li
    nr, ni = a_r - 1.0, a_i
    f_r = (nr * lr + ni * li) / den
    f_i = (ni * lr - nr * li) / den
    b_r, b_i = bre_ref[...], bim_ref[...]
    bb_r = f_r * b_r - f_i * b_i
    bb_i = f_r * b_i + f_i * b_r
    c_r, c_i = cre_ref[...], cim_ref[...]
    gw = b_r.shape[0]
    ns = b_r.shape[1]

    ktau = []
    for tau in range(S5_BLOCK):
        p_r, p_i = lam_bar_pow(float(tau))
        w_r = bb_r * p_r - bb_i * p_i
        w_i = bb_r * p_i + bb_i * p_r
        j = S5_BLOCK - 1 - tau
        wst_ref[j * gw:(j + 1) * gw, 0:ns] = w_r.astype(BF16)
        wst_ref[j * gw:(j + 1) * gw, ns:2 * ns] = w_i.astype(BF16)
        ktau.append(_dot_nt(w_r, c_r, lax.Precision.HIGHEST) - _dot_nt(w_i, c_i, lax.Precision.HIGHEST))
    rr = lax.broadcasted_iota(jnp.int32, (gw, gw), 0)
    cc = lax.broadcasted_iota(jnp.int32, (gw, gw), 1)
    ktau[0] = ktau[0] + jnp.where(rr == cc, dsk_ref[...], 0.0)

    for j in range(S5_BLOCK):
        p_r, p_i = lam_bar_pow(float(j + 1))
        cat_ref[j * gw:(j + 1) * gw, 0:ns] = (c_r * p_r - c_i * p_i).astype(BF16)
        cat_ref[j * gw:(j + 1) * gw, ns:2 * ns] = (-(c_r * p_i + c_i * p_r)).astype(BF16)

    zeros = jnp.zeros((gw, gw), BF16)
    for i in range(S5_BLOCK):
        for j in range(S5_BLOCK):
            blk = ktau[j - i].astype(BF16) if j >= i else zeros
            kt_ref[i * gw:(i + 1) * gw, j * gw:(j + 1) * gw] = blk

    row = lax.broadcasted_iota(jnp.int32, (8, ns), 0)
    for idx, k in enumerate((1, 2, 4)):
        p_r, p_i = lam_bar_pow(float(S5_BLOCK * k))
        sct_ref[2 * idx] = jnp.where(row >= k, p_r, 0.0)
        sct_ref[2 * idx + 1] = jnp.where(row >= k, p_i, 0.0)
    n = (S5_BLOCK * (row + 1)).astype(F32)
    mag = jnp.exp(n * (lr * dt))
    ang = n * (li * dt)
    sct_ref[6] = mag * jnp.cos(ang)
    sct_ref[7] = mag * jnp.sin(ang)


def _s5_prep(lre, lim, ldt, bre, bim, cre, cim, dsk):
    nq, gw, ns = bre.shape
    kw = S5_BLOCK * gw
    per_q = lambda *tail: pl.BlockSpec((None,) + tail, lambda q: (q,) + (0,) * len(tail))
    return pl.pallas_call(
        _s5_prep_kernel,
        grid=(nq,),
        in_specs=[per_q(1, ns), per_q(1, ns), per_q(1, ns), per_q(gw, ns), per_q(gw, ns),
                  per_q(gw, ns), per_q(gw, ns), per_q(1, gw)],
        out_specs=[per_q(kw, 2 * ns), per_q(kw, 2 * ns), per_q(kw, kw), per_q(8, 8, ns)],
        out_shape=[
            jax.ShapeDtypeStruct((nq, kw, 2 * ns), BF16),
            jax.ShapeDtypeStruct((nq, kw, 2 * ns), BF16),
            jax.ShapeDtypeStruct((nq, kw, kw), BF16),
            jax.ShapeDtypeStruct((nq, 8, 8, ns), F32),
        ],
        compiler_params=_cparams(1),
        name="s5_prep",
    )(lre, lim, ldt, bre, bim, cre, cim, dsk)


def _quarter_lanes(u_ref, q, gw, sw):
    return jnp.concatenate(
        [u_ref[:, j * sw + q * gw: j * sw + (q + 1) * gw] for j in range(S5_BLOCK)], axis=1)


def _store_quarter(y_ref, y, q, gw, sw):
    for j in range(S5_BLOCK):
        y_ref[:, j * sw + q * gw: j * sw + (q + 1) * gw] = y[:, j * gw:(j + 1) * gw].astype(y_ref.dtype)


def _s5_prompt_kernel(u_ref, wst_ref, cat_ref, kt_ref, sct_ref, y_ref, st_ref,
                      carry_ref, s_scr, hp_scr, *, rb, gw, sw, ns):
    blk = pl.program_id(1)

    @pl.when(blk == 0)
    def _():
        carry_ref[...] = jnp.zeros(carry_ref.shape, F32)

    row0 = lax.broadcasted_iota(jnp.int32, (8, ns), 0) == 0
    for q in range(S5_QUARTERS):
        uq = _quarter_lanes(u_ref, q, gw, sw)
        s_scr[...] = _dot(uq, wst_ref[q])
        tabs = [sct_ref[q, i] for i in range(8)]

        def tile(t, carry, tabs=tabs):
            cr, ci = carry
            r0 = pl.multiple_of(t * 8, 8)
            xr = s_scr[pl.ds(r0, 8), 0:ns]
            xi = s_scr[pl.ds(r0, 8), ns:2 * ns]
            for idx, k in enumerate((1, 2, 4)):
                mr, mi = tabs[2 * idx], tabs[2 * idx + 1]
                sr = pltpu.roll(xr, k, 0)
                si = pltpu.roll(xi, k, 0)
                xr, xi = xr + (mr * sr - mi * si), xi + (mr * si + mi * sr)
            crb = jnp.broadcast_to(cr, (8, ns))
            cib = jnp.broadcast_to(ci, (8, ns))
            hr = xr + (tabs[6] * crb - tabs[7] * cib)
            hi = xi + (tabs[6] * cib + tabs[7] * crb)
            hp_scr[pl.ds(r0, 8), 0:ns] = jnp.where(row0, crb, pltpu.roll(hr, 1, 0))
            hp_scr[pl.ds(r0, 8), ns:2 * ns] = jnp.where(row0, cib, pltpu.roll(hi, 1, 0))
            return hr[7:8, :], hi[7:8, :]

        cr, ci = lax.fori_loop(0, rb // 8, tile,
                               (carry_ref[q, :, 0:ns], carry_ref[q, :, ns:2 * ns]))
        carry_ref[q, :, 0:ns] = cr
        carry_ref[q, :, ns:2 * ns] = ci
        st_ref[q, :, 0:ns] = cr
        st_ref[q, :, ns:2 * ns] = ci
        y = _dot_nt(hp_scr[...].astype(BF16), cat_ref[q]) + _dot(uq, kt_ref[q])
        _store_quarter(y_ref, y, q, gw, sw)


def _s5_prompt(u4, wst, cat, kt, sct, rb):
    b, rows, lanes = u4.shape
    nq, kw, ns2 = wst.shape
    ns, gw, sw = ns2 // 2, kw // S5_BLOCK, lanes // S5_BLOCK
    return pl.pallas_call(
        functools.partial(_s5_prompt_kernel, rb=rb, gw=gw, sw=sw, ns=ns),
        grid=(b, rows // rb),
        in_specs=[
            pl.BlockSpec((None, rb, lanes), lambda bi, ri: (bi, ri, 0)),
            _const_spec(wst.shape), _const_spec(cat.shape), _const_spec(kt.shape), _const_spec(sct.shape),
        ],
        out_specs=[
            pl.BlockSpec((None, rb, lanes), lambda bi, ri: (bi, ri, 0)),
            pl.BlockSpec((None, nq, 1, ns2), lambda bi, ri: (bi, 0, 0, 0)),
        ],
        out_shape=[
            jax.ShapeDtypeStruct((b, rows, lanes), BF16),
            jax.ShapeDtypeStruct((b, nq, 1, ns2), F32),
        ],
        scratch_shapes=[
            pltpu.VMEM((nq, 1, ns2), F32),
            pltpu.VMEM((rb, ns2), F32),
            pltpu.VMEM((rb, ns2), F32),
        ],
        compiler_params=_cparams(2),
        name="s5_prompt",
    )(u4, wst, cat, kt, sct)


def _s5_sample_kernel(u_ref, h0_ref, wst_ref, cat_ref, kt_ref, sct_ref, y_ref, hout_ref, *, gw, sw, ns):
    for q in range(S5_QUARTERS):
        uq = _quarter_lanes(u_ref, q, gw, sw)
        s = _dot(uq, wst_ref[q])
        h0 = h0_ref[q]
        h0r, h0i = h0[:, 0:ns], h0[:, ns:2 * ns]
        a4r = sct_ref[q, 6][0:1, :]
        a4i = sct_ref[q, 7][0:1, :]
        hout_ref[q, :, 0:ns] = s[:, 0:ns] + (a4r * h0r - a4i * h0i)
        hout_ref[q, :, ns:2 * ns] = s[:, ns:2 * ns] + (a4r * h0i + a4i * h0r)
        y = _dot_nt(h0.astype(BF16), cat_ref[q]) + _dot(uq, kt_ref[q])
        _store_quarter(y_ref, y, q, gw, sw)


def _s5_sample(u4, h0q, wst, cat, kt, sct):
    rows, lanes = u4.shape
    nq, kw, ns2 = wst.shape
    ns, gw, sw = ns2 // 2, kw // S5_BLOCK, lanes // S5_BLOCK
    return pl.pallas_call(
        functools.partial(_s5_sample_kernel, gw=gw, sw=sw, ns=ns),
        grid=(1,),
        in_specs=[_const_spec(u4.shape), _const_spec(h0q.shape), _const_spec(wst.shape),
                  _const_spec(cat.shape), _const_spec(kt.shape), _const_spec(sct.shape)],
        out_specs=[pl.BlockSpec(u4.shape, lambda i: (0, 0)),
                   pl.BlockSpec(h0q.shape, lambda i: (0, 0, 0))],
        out_shape=[jax.ShapeDtypeStruct(u4.shape, BF16), jax.ShapeDtypeStruct(h0q.shape, F32)],
        compiler_params=_cparams(1),
        name="s5_sample",
    )(u4, h0q, wst, cat, kt, sct)


def _mem_kv_kernel(x_ref, g_ref, wk_ref, wv_ref, k_ref, v_ref, kb_ref, vb_ref):
    hn = _rms(x_ref[...], g_ref[...]).astype(BF16)
    k = _dot(hn, wk_ref[...])
    k_ref[...] = k
    kb_ref[...] = k.astype(BF16)
    v = _dot(hn, wv_ref[...])
    v_ref[...] = v
    vb_ref[...] = v.astype(BF16)


def _mem_kv(mem, g, wk_bf, wv_bf, tm):
    n, d = mem.shape
    row = pl.BlockSpec((tm, d), lambda i: (i, 0))
    return pl.pallas_call(
        _mem_kv_kernel,
        grid=(n // tm,),
        in_specs=[row, _const_spec((1, d)), _const_spec(wk_bf.shape), _const_spec(wv_bf.shape)],
        out_specs=[row, row, row, row],
        out_shape=[jax.ShapeDtypeStruct((n, d), F32), jax.ShapeDtypeStruct((n, d), F32),
                   jax.ShapeDtypeStruct((n, d), BF16), jax.ShapeDtypeStruct((n, d), BF16)],
        compiler_params=_cparams(1),
        name="mem_kv",
    )(mem, g, wk_bf, wv_bf)


def _merge_and_query(x, att, sy, wglu_ref, gso_ref, wout_ref, gpost_ref, gmpre_ref, wmq_ref, mem_scale):
    aw = att.shape[1]
    y = jax.nn.gelu(sy.astype(F32))
    y = y * jax.nn.sigmoid(_dot(y.astype(BF16), wglu_ref[...]))
    y = _rms(y, gso_ref[...])
    mix = _dot(att, wout_ref[0:aw, :]) + _dot(y.astype(BF16), wout_ref[aw:, :])
    x1 = x + _rms(mix, gpost_ref[...])
    hq = _rms(x1, gmpre_ref[...]).astype(BF16)
    qm = (_dot(hq, wmq_ref[...]) * mem_scale).astype(BF16)
    return x1, qm


def _mix_mem_kernel(x_ref, att_ref, sy_ref, mk_ref, mv_ref, wglu_ref, gso_ref, wout_ref, gpost_ref,
                    gmpre_ref, wmq_ref, wmo_ref, gmpost_ref, o_ref, oh_ref, *, n_heads, mem_scale):
    x1, qm = _merge_and_query(x_ref[...], att_ref[...], sy_ref[...], wglu_ref, gso_ref, wout_ref,
                              gpost_ref, gmpre_ref, wmq_ref, mem_scale)
    hd = qm.shape[1] // n_heads
    for h in range(n_heads):
        sl = slice(h * hd, (h + 1) * hd)
        s = _dot_nt(qm[:, sl], mk_ref[:, sl])
        p = jnp.exp(s - jnp.max(s, axis=-1, keepdims=True))
        l = jnp.sum(p, axis=-1, keepdims=True)
        oh_ref[:, sl] = (_dot(p.astype(BF16), mv_ref[:, sl]) / l).astype(BF16)
    mo = _dot(oh_ref[...], wmo_ref[...])
    o_ref[...] = x1 + _rms(mo, gmpost_ref[...])


def _mix_mem(x, att, sy, mkb, mvb, w, tm, n_heads):
    b, t, d = x.shape
    aw, sw = att.shape[2], sy.shape[2]
    nm = mkb.shape[1]
    tok = lambda width: pl.BlockSpec((None, tm, width), lambda bi, ti: (bi, ti, 0))
    per_b = pl.BlockSpec((None, nm, d), lambda bi, ti: (bi, 0, 0))
    weights = [w["w_glu"], w["g_ssm_out"], w["w_out"], w["g_mix_post"], w["g_mem_pre"], w["w_mq"],
               w["w_mo"], w["g_mem_post"]]
    return pl.pallas_call(
        functools.partial(_mix_mem_kernel, n_heads=n_heads, mem_scale=(d // n_heads) ** -0.5),
        grid=(b, t // tm),
        in_specs=[tok(d), tok(aw), tok(sw), per_b, per_b] + [_const_spec(a.shape) for a in weights],
        out_specs=tok(d),
        out_shape=jax.ShapeDtypeStruct((b, t, d), F32),
        scratch_shapes=[pltpu.VMEM((tm, d), BF16)],
        compiler_params=_cparams(2),
        name="mix_mem",
    )(x, att, sy, mkb, mvb, *weights)


def _mix_q_kernel(x_ref, att_ref, sy_ref, wglu_ref, gso_ref, wout_ref, gpost_ref, gmpre_ref, wmq_ref,
                  x1_ref, qm_ref, *, mem_scale):
    x1, qm = _merge_and_query(x_ref[...], att_ref[...], sy_ref[...], wglu_ref, gso_ref, wout_ref,
                              gpost_ref, gmpre_ref, wmq_ref, mem_scale)
    x1_ref[...] = x1
    qm_ref[...] = qm


def _mix_q(x, att, sy, w, n_heads):
    n, d = x.shape
    weights = [w["w_glu"], w["g_ssm_out"], w["w_out"], w["g_mix_post"], w["g_mem_pre"], w["w_mq"]]
    ins = [x, att, sy] + weights
    return pl.pallas_call(
        functools.partial(_mix_q_kernel, mem_scale=(d // n_heads) ** -0.5),
        grid=(1,),
        in_specs=[_const_spec(a.shape) for a in ins],
        out_specs=[pl.BlockSpec((n, d), lambda i: (0, 0)), pl.BlockSpec((n, d), lambda i: (0, 0))],
        out_shape=[jax.ShapeDtypeStruct((n, d), F32), jax.ShapeDtypeStruct((n, d), BF16)],
        compiler_params=_cparams(1),
        name="mix_q",
    )(*ins)


def _mem_sample_kernel(hmask_ref, q_ref, k_ref, v_ref, o_ref, *, bb):
    for i in range(bb):
        q = q_ref[i]
        s = _dot_nt(q, k_ref[i].astype(BF16))
        p = jnp.exp(s - jnp.max(s, axis=-1, keepdims=True))
        l = jnp.sum(p, axis=-1, keepdims=True)
        o = _dot(p.astype(BF16), v_ref[i].astype(BF16)) / l * hmask_ref[...]
        e = o[0:8] + o[8:16]
        o_ref[i] = e + pltpu.roll(e, 4, 0)


def _mem_sample(q16, ck, cv, hmask, bb):
    db, nrow, d = q16.shape
    nm = ck.shape[1]
    assert nrow == 16, "row folding assumes 4 heads x 4 tokens"
    return pl.pallas_call(
        functools.partial(_mem_sample_kernel, bb=bb),
        grid=(db // bb,),
        in_specs=[
            _const_spec(hmask.shape),
            pl.BlockSpec((bb, nrow, d), lambda i: (i, 0, 0)),
            pl.BlockSpec((bb, nm, d), lambda i: (i, 0, 0)),
            pl.BlockSpec((bb, nm, d), lambda i: (i, 0, 0)),
        ],
        out_specs=pl.BlockSpec((bb, 8, d), lambda i: (i, 0, 0)),
        out_shape=jax.ShapeDtypeStruct((db, 8, d), F32),
        compiler_params=_cparams(1),
        name="mem_sample",
    )(hmask, q16, ck, cv)


def _mem_out_kernel(x1_ref, o_ref, wmo_ref, g_ref, y_ref):
    mo = _dot(o_ref[...], wmo_ref[...])
    y_ref[...] = x1_ref[...] + _rms(mo, g_ref[...])


def _mem_out(x1, o, wmo, g):
    n, d = x1.shape
    ins = [x1, o, wmo, g]
    return pl.pallas_call(
        _mem_out_kernel,
        grid=(1,),
        in_specs=[_const_spec(a.shape) for a in ins],
        out_specs=pl.BlockSpec((n, d), lambda i: (0, 0)),
        out_shape=jax.ShapeDtypeStruct((n, d), F32),
        compiler_params=_cparams(1),
        name="mem_out",
    )(*ins)


def _ffn_kernel(x_ref, gpre_ref, wg_ref, wu_ref, wd_ref, gpost_ref, o_ref, *, n_chunks):
    x = x_ref[...]
    hn = _rms(x, gpre_ref[...]).astype(BF16)
    ff = wg_ref.shape[1]
    cw = ff // n_chunks
    acc = jnp.zeros(x.shape, F32)
    for c in range(n_chunks):
        sl = slice(c * cw, (c + 1) * cw)
        a = jax.nn.silu(_dot(hn, wg_ref[:, sl])) * _dot(hn, wu_ref[:, sl])
        acc = acc + _dot(a.astype(BF16), wd_ref[sl, :])
    o_ref[...] = x + _rms(acc, gpost_ref[...])


def _ffn(x, gpre, wg, wu, wd, gpost, tm, n_chunks):
    n, d = x.shape
    row = pl.BlockSpec((tm, d), lambda i: (i, 0))
    weights = [gpre, wg, wu, wd, gpost]
    return pl.pallas_call(
        functools.partial(_ffn_kernel, n_chunks=n_chunks),
        grid=(n // tm,),
        in_specs=[row] + [_const_spec(a.shape) for a in weights],
        out_specs=row,
        out_shape=jax.ShapeDtypeStruct((n, d), F32),
        compiler_params=_cparams(1),
        name="ffn",
    )(x, *weights)


def _s5_param_layout(lam_re, lam_im, log_dt, b_re, b_im, c_re, c_im, d_skip):
    g, p, hc = b_re.shape
    nq = S5_QUARTERS
    gq = g // nq
    rowv = lambda a: a.reshape(nq, 1, gq * p)
    ldt = jnp.broadcast_to(log_dt[:, None], (g, p))
    eye = jnp.eye(gq, dtype=F32)

    def embed_b(b):
        bq = b.reshape(nq, gq, p, hc)
        return jnp.einsum("qgph,gk->qghkp", bq, eye).reshape(nq, gq * hc, gq * p)

    def embed_c(c):
        cq = c.reshape(nq, gq, hc, p)
        return jnp.einsum("qghp,gk->qghkp", cq, eye).reshape(nq, gq * hc, gq * p)

    return (rowv(lam_re), rowv(lam_im), rowv(ldt), embed_b(b_re), embed_b(b_im),
            embed_c(c_re), embed_c(c_im), d_skip.reshape(nq, 1, gq * hc))


def _state_to_quarters(re, im):
    n, g, p = re.shape
    gq = g // S5_QUARTERS
    f = lambda a: a.reshape(n, S5_QUARTERS, gq * p).transpose(1, 0, 2)
    return jnp.concatenate([f(re), f(im)], axis=-1)


def _state_from_quarters(st, g, p):
    n = st.shape[0]
    ns = st.shape[-1] // 2
    return st[..., :ns].reshape(n, g, p), st[..., ns:].reshape(n, g, p)


def kernel(x_prompt, x_sample, mem_prompt, cache_k, cache_v, page_table, state_ssm_re, state_ssm_im,
           cache_mem_k, cache_mem_v, g_mix_pre, g_mix_post, w_in, lam_q1, lam_k1, lam_q2, lam_k2, g_subln,
           ssm_lam_re, ssm_lam_im, ssm_log_dt, ssm_b_re, ssm_b_im, ssm_c_re, ssm_c_im, ssm_d, w_glu, g_ssm_out,
           w_out, g_mem_pre, g_mem_post, g_mem_kv, w_mq, w_mk, w_mv, w_mo, g_ffn_pre, g_ffn_post,
           w_gate, w_up, w_down):
    depth = w_in.shape[0]
    b, t, d = x_prompt.shape
    db, dt_, _ = x_sample.shape
    n_mem = mem_prompt.shape[1]
    n_mem_heads = cache_mem_k.shape[3]
    n_heads = cache_k.shape[3]
    aw = n_heads * HEAD_W
    g, p = ssm_lam_re.shape[1], ssm_lam_re.shape[2]
    sw = g * SSM_GROUP
    n_phys = cache_k.shape[1]
    assert dt_ == S5_BLOCK and t % (S5_BLOCK * 256) == 0

    slopes = LOG2E * 2.0 ** (-8.0 * jnp.arange(1, n_heads + 1, dtype=F32) / n_heads)
    slope_hi = slopes.astype(BF16).astype(F32)
    slope_parts = jnp.stack([slopes, slope_hi, (slopes - slope_hi).astype(BF16).astype(F32)])
    row = lambda a: a.reshape(1, -1).astype(F32)
    bf = lambda a: a.astype(BF16)

    branch_lane = jnp.arange(HEAD_W) // HEAD_DIM
    nrow = 2 * n_heads * dt_
    chunk = 256
    r = jnp.arange(nrow)
    r_head, r_tok = (r // dt_) % n_heads, r % dt_
    r_slope = slopes[r_head][:, None]
    c = jnp.arange(chunk)
    c_tok, c_head = c // n_heads, c % n_heads
    own_head = c_head[None, :] == r_head[:, None]
    base = r_slope * c_tok[None, :].astype(F32) + jnp.where(own_head, 0.0, NEG)
    base2 = jnp.concatenate([base, base + r_slope * float(chunk // n_heads)], axis=0)
    slope2 = jnp.concatenate([r_slope, r_slope], axis=0)
    new_ok = own_head & (c[None, :] < dt_ * n_heads) & (c_tok[None, :] <= r_tok[:, None])
    newmask = jnp.where(new_ok, r_slope * c_tok[None, :].astype(F32), NEG)
    mlane_head = jnp.arange(d) // (d // n_mem_heads)
    mmask = (mlane_head[None, :] == jnp.arange(n_mem_heads)[:, None])
    hmask_mem = jnp.repeat(mmask, dt_, axis=0).astype(F32)

    y_p = x_prompt
    y_s = x_sample.reshape(db * dt_, d)
    outs = {k: [] for k in ("kp", "vp", "ks", "vs", "rp", "ip", "rs", "is", "mkp", "mvp")}
    for l in range(depth):
        lam_init = 0.8 - 0.6 * math.exp(-0.3 * l)
        lamv = jnp.stack([lam_q1[l], lam_k1[l], lam_q2[l], lam_k2[l]]).astype(F32)
        gsub = row(g_subln[l])
        w_in_bf = bf(w_in[l])
        w = {
            "w_glu": bf(w_glu[l]), "g_ssm_out": row(g_ssm_out[l]), "w_out": bf(w_out[l]),
            "g_mix_post": row(g_mix_post[l]), "g_mem_pre": row(g_mem_pre[l]), "w_mq": bf(w_mq[l]),
            "w_mo": bf(w_mo[l]), "g_mem_post": row(g_mem_post[l]),
        }

        qp, kp, vp, kpb, vpb, up = _proj(y_p.reshape(b * t, d), row(g_mix_pre[l]), w_in_bf, aw, 512)
        qs, ks, vs, _, _, us = _proj(y_s, row(g_mix_pre[l]), w_in_bf, aw, 512)

        att_pt = _attn_prompt(jnp.swapaxes(qp.reshape(b, t, aw), 1, 2), kpb.reshape(b, t, aw),
                              jnp.swapaxes(vpb.reshape(b, t, aw), 1, 2),
                              slope_parts, lamv, g_subln[l].reshape(HEAD_W, 1).astype(F32), lam_init, 512)
        att_p = jnp.swapaxes(att_pt, 1, 2)
        qh = qs.reshape(db, dt_, n_heads, HEAD_W).transpose(0, 2, 1, 3)
        zq = jnp.zeros((), BF16)
        q32 = jnp.stack([jnp.where(branch_lane == 0, qh, zq), jnp.where(branch_lane == 1, qh, zq)], axis=1)
        q32 = q32.reshape(db, nrow, HEAD_W)
        zq32 = jnp.zeros_like(q32)
        q2 = jnp.concatenate([jnp.concatenate([q32, zq32], axis=2), jnp.concatenate([zq32, q32], axis=2)], axis=1)
        att_s16 = _attn_sample(page_table, q2, ks.reshape(db, dt_ * n_heads, HEAD_W),
                               vs.reshape(db, dt_ * n_heads, HEAD_W),
                               cache_k[l].reshape(n_phys * PAGE * n_heads, HEAD_W),
                               cache_v[l].reshape(n_phys * PAGE * n_heads, HEAD_W),
                               lamv, gsub, slope2, base2, newmask, n_heads, lam_init)
        att_s = bf(att_s16.reshape(db, n_heads, dt_, HEAD_W).transpose(0, 2, 1, 3).reshape(db * dt_, aw))

        tabs = _s5_prep(*_s5_param_layout(ssm_lam_re[l], ssm_lam_im[l], ssm_log_dt[l], ssm_b_re[l],
                                          ssm_b_im[l], ssm_c_re[l], ssm_c_im[l], ssm_d[l]))
        syp, stp = _s5_prompt(up.reshape(b, t // S5_BLOCK, S5_BLOCK * sw), *tabs, 256)
        rp, ip = _state_from_quarters(stp.reshape(b, S5_QUARTERS, -1), g, p)
        sys_, sts = _s5_sample(us.reshape(db, S5_BLOCK * sw),
                               _state_to_quarters(state_ssm_re[l], state_ssm_im[l]), *tabs)
        rs, is_ = _state_from_quarters(sts.transpose(1, 0, 2), g, p)

        mkp, mvp, mkb, mvb = _mem_kv(mem_prompt.reshape(b * n_mem, d), row(g_mem_kv[l]),
                                     bf(w_mk[l]), bf(w_mv[l]), 512)
        y_p = _mix_mem(y_p, att_p, syp.reshape(b, t, sw), mkb.reshape(b, n_mem, d), mvb.reshape(b, n_mem, d),
                       w, 512, n_mem_heads)
        x1s, qms = _mix_q(y_s, att_s, sys_.reshape(db * dt_, sw), w, n_mem_heads)
        q16 = jnp.where(mmask[None, :, None, :], qms.reshape(db, 1, dt_, d), jnp.zeros((), BF16))
        o8 = _mem_sample(q16.reshape(db, n_mem_heads * dt_, d), cache_mem_k[l].reshape(db, n_mem, d),
                         cache_mem_v[l].reshape(db, n_mem, d), hmask_mem, 4)
        y_s = _mem_out(x1s, bf(o8[:, :dt_, :].reshape(db * dt_, d)), w["w_mo"], w["g_mem_post"])

        ffw = (row(g_ffn_pre[l]), bf(w_gate[l]), bf(w_up[l]), bf(w_down[l]), row(g_ffn_post[l]))
        y_p = _ffn(y_p.reshape(b * t, d), *ffw, 512, 2).reshape(b, t, d)
        y_s = _ffn(y_s, *ffw, 512, 2)

        outs["kp"].append(kp.reshape(b, t, n_heads, HEAD_W))
        outs["vp"].append(vp.reshape(b, t, n_heads, HEAD_W))
        outs["ks"].append(ks.reshape(db, dt_, n_heads, HEAD_W))
        outs["vs"].append(vs.reshape(db, dt_, n_heads, HEAD_W))
        outs["rp"].append(rp); outs["ip"].append(ip); outs["rs"].append(rs); outs["is"].append(is_)
        outs["mkp"].append(mkp.reshape(b, n_mem, n_mem_heads, d // n_mem_heads))
        outs["mvp"].append(mvp.reshape(b, n_mem, n_mem_heads, d // n_mem_heads))

    st = lambda k: jnp.stack(outs[k], 0)
    return (y_p, y_s.reshape(db, dt_, d), st("kp"), st("vp"), st("ks"), st("vs"),
            st("rp"), st("ip"), st("rs"), st("is"), st("mkp"), st("mvp"))
```

```python
import functools
import math

import jax
import jax.numpy as jnp
from jax import lax
from jax.experimental import pallas as pl
from jax.experimental.pallas import tpu as pltpu

F32 = jnp.float32
BF16 = jnp.bfloat16

RMS_EPS = 1e-6
HEAD_DIM = 64
HEAD_W = 2 * HEAD_DIM
SSM_GROUP = 16
SSM_STATE = 64
S5_BLOCK = 4
S5_QUARTERS = 4
PAGE = 128
LANES = 128
NEG = -1e30
LOG2E = 1.4426950408889634
VMEM_LIMIT_BYTES = 56 * 1024 * 1024


def _cparams(n_grid_dims):
    return pltpu.CompilerParams(
        dimension_semantics=("arbitrary",) * n_grid_dims,
        vmem_limit_bytes=VMEM_LIMIT_BYTES)


def _const_spec(shape):
    nd = len(shape)
    return pl.BlockSpec(shape, lambda *_: (0,) * nd, pipeline_mode=pl.Buffered(1))


def _rms(x, g):
    ms = jnp.mean(x * x, axis=-1, keepdims=True)
    return x * lax.rsqrt(ms + RMS_EPS) * g


def _dot(a, b):
    return jnp.dot(a, b, preferred_element_type=F32)


def _dot_nt(a, b, precision=None):
    return lax.dot_general(a, b, (((1,), (1,)), ((), ())),
                           preferred_element_type=F32, precision=precision)


def _proj_kernel(x_ref, g_ref, w_ref, q_ref, k_ref, v_ref, kb_ref, vb_ref, u_ref, u_scr, *, aw, tm):
    hn = _rms(x_ref[...], g_ref[...]).astype(BF16)
    q = _dot(hn, w_ref[:, 0:aw])
    q_ref[...] = (q * (HEAD_DIM ** -0.5 * LOG2E)).astype(BF16)
    nh = aw // HEAD_W
    k = _dot(hn, w_ref[:, aw:2 * aw])
    kb_ref[...] = k.astype(BF16)
    v = _dot(hn, w_ref[:, 2 * aw:3 * aw])
    vb_ref[...] = v.astype(BF16)
    for h in range(nh):
        k_ref[pl.ds(h, tm, stride=nh), :] = k[:, h * HEAD_W:(h + 1) * HEAD_W]
        v_ref[pl.ds(h, tm, stride=nh), :] = v[:, h * HEAD_W:(h + 1) * HEAD_W]
    u = _dot(hn, w_ref[:, 3 * aw:])
    n_lb = u_scr.shape[0]
    sw = n_lb * LANES
    for c in range(n_lb):
        u_scr[c] = u[:, c * LANES:(c + 1) * LANES]
    for j in range(S5_BLOCK):
        for c in range(n_lb):
            rows = u_scr[c, pl.ds(j, tm // S5_BLOCK, stride=S5_BLOCK), :]
            u_ref[:, j * sw + c * LANES:j * sw + (c + 1) * LANES] = rows.astype(BF16)


def _proj(x, g, w_bf, aw, tm):
    n, d = x.shape
    sw = w_bf.shape[1] - 3 * aw
    nh = aw // HEAD_W
    row = lambda width: pl.BlockSpec((tm, width), lambda i: (i, 0))
    flat = pl.BlockSpec((tm * nh, HEAD_W), lambda i: (i, 0))
    return pl.pallas_call(
        functools.partial(_proj_kernel, aw=aw, tm=tm),
        grid=(n // tm,),
        in_specs=[row(d), _const_spec((1, d)), _const_spec(w_bf.shape)],
        out_specs=[row(aw), flat, flat, row(aw), row(aw),
                   pl.BlockSpec((tm // S5_BLOCK, S5_BLOCK * sw), lambda i: (i, 0))],
        out_shape=[
            jax.ShapeDtypeStruct((n, aw), BF16),
            jax.ShapeDtypeStruct((n * nh, HEAD_W), F32),
            jax.ShapeDtypeStruct((n * nh, HEAD_W), F32),
            jax.ShapeDtypeStruct((n, aw), BF16),
            jax.ShapeDtypeStruct((n, aw), BF16),
            jax.ShapeDtypeStruct((n // S5_BLOCK, S5_BLOCK * sw), BF16),
        ],
        scratch_shapes=[pltpu.VMEM((sw // LANES, tm, LANES), F32)],
        compiler_params=_cparams(1),
        name="proj",
    )(x, g, w_bf)


def _diff_lambda(lamv, lam_init):
    d1 = jnp.sum(lamv[0:1] * lamv[1:2], axis=-1, keepdims=True)
    d2 = jnp.sum(lamv[2:3] * lamv[3:4], axis=-1, keepdims=True)
    return jnp.exp(d1) - jnp.exp(d2) + lam_init


def _attn_prompt_kernel(sl_ref, lamv_ref, g_ref, pos_ref, qt_ref, k_ref, vt_ref, o_ref,
                        qq_ref, m_ref, acc_ref, t_ref, *, tq, lam_init):
    h = pl.program_id(1)
    qi = pl.program_id(2)
    tk = tq

    qt = qt_ref[...].astype(F32)
    frow = lax.broadcasted_iota(jnp.int32, qt.shape, 0)
    qq_ref[0:HEAD_W, 0:tq] = jnp.where(frow < HEAD_DIM, qt, 0.0).astype(BF16)
    qq_ref[0:HEAD_W, tq:2 * tq] = jnp.where(frow >= HEAD_DIM, qt, 0.0).astype(BF16)
    arow = lax.broadcasted_iota(jnp.int32, (HEAD_W, 2 * tq), 0)
    slope_parts = jnp.where((arow & 1) == 0, sl_ref[1, h], sl_ref[2, h])
    qq_ref[HEAD_W:2 * HEAD_W, :] = jnp.where(arow < 4, slope_parts, 0.0).astype(BF16)
    m_ref[...] = jnp.full(m_ref.shape, NEG, F32)
    acc_ref[...] = jnp.zeros(acc_ref.shape, F32)
    ones = jnp.ones((acc_ref.shape[0] - HEAD_W, tk), BF16)

    def scores(kj):
        k0 = pl.multiple_of(kj * tk, tk)
        ka = jnp.concatenate([k_ref[pl.ds(k0, tk), :], pos_ref[...]], axis=1)
        return _dot(ka, qq_ref[...])

    def consume(kj, t):
        k0 = pl.multiple_of(kj * tk, tk)
        vta = jnp.concatenate([vt_ref[:, pl.ds(k0, tk)], ones], axis=0)
        off = sl_ref[0, h] * jnp.full((1, 2 * tq), (kj - qi) * tk, jnp.int32).astype(F32)
        m_old = m_ref[...]
        m_new = jnp.maximum(m_old, jnp.max(t, axis=0, keepdims=True) + off)
        alpha = jnp.exp2(m_old - m_new)
        p = jnp.exp2(t - (m_new - off)).astype(BF16)
        acc_ref[...] = alpha * acc_ref[...] + _dot(vta, p)
        m_ref[...] = m_new

    t_ref[...] = scores(0)

    def body(kj, carry):
        t = t_ref[...]
        t_ref[...] = scores(kj + 1)
        consume(kj, t)
        return carry

    lax.fori_loop(0, qi, body, 0)
    krow = lax.broadcasted_iota(jnp.int32, (tk, 2 * tq), 0)
    qcol = lax.broadcasted_iota(jnp.int32, (tk, 2 * tq), 1)
    qcol = jnp.where(qcol >= tq, qcol - tq, qcol)
    consume(qi, jnp.where(krow <= qcol, t_ref[...], NEG))

    acc = acc_ref[...]
    ot = acc[0:HEAD_W] / acc[HEAD_W:HEAD_W + 1]
    lam = _diff_lambda(lamv_ref[...], lam_init)
    d = ot[:, 0:tq] - lam * ot[:, tq:2 * tq]
    ms = jnp.mean(d * d, axis=0, keepdims=True)
    o_ref[...] = (d * lax.rsqrt(ms + RMS_EPS) * g_ref[...] * (1.0 - lam_init)).astype(BF16)


def _attn_prompt(qt, kb, vt, slope_parts, lamv, g_col, lam_init, tq):
    b, aw, t = qt.shape
    nh = aw // HEAD_W
    smem = pl.BlockSpec(memory_space=pltpu.SMEM)
    i = jnp.arange(tq)
    lane = jnp.arange(HEAD_W)
    feat = jnp.where(lane[None, :] < 2, (i % 256)[:, None], jnp.where(lane[None, :] < 4, (i // 256 * 256)[:, None], 0))
    pos = feat.astype(BF16)
    return pl.pallas_call(
        functools.partial(_attn_prompt_kernel, tq=tq, lam_init=lam_init),
        grid=(b, nh, t // tq),
        in_specs=[
            smem,
            _const_spec(lamv.shape),
            _const_spec(g_col.shape),
            _const_spec(pos.shape),
            pl.BlockSpec((None, HEAD_W, tq), lambda bi, hi, qi: (bi, hi, qi)),
            pl.BlockSpec((None, t, HEAD_W), lambda bi, hi, qi: (bi, 0, hi)),
            pl.BlockSpec((None, HEAD_W, t), lambda bi, hi, qi: (bi, hi, 0)),
        ],
        out_specs=pl.BlockSpec((None, HEAD_W, tq), lambda bi, hi, qi: (bi, hi, qi)),
        out_shape=jax.ShapeDtypeStruct((b, aw, t), BF16),
        scratch_shapes=[
            pltpu.VMEM((2 * HEAD_W, 2 * tq), BF16),
            pltpu.VMEM((1, 2 * tq), F32),
            pltpu.VMEM((HEAD_W + 16, 2 * tq), F32),
            pltpu.VMEM((tq, 2 * tq), F32),
        ],
        compiler_params=_cparams(3),
        name="attn_prompt",
    )(slope_parts, lamv, g_col, pos, qt, kb, vt)


def _attn_sample_kernel(pt_ref, lamv_ref, g_ref, slope2_ref, base2_ref, newmask_ref, q2_ref, kn_ref, vn_ref,
                        ck_hbm, cv_hbm, o_ref, kbuf, vbuf, sem, s_ref,
                        *, n_pages, page_rows, n_heads, lam_init, chunk):
    b = pl.program_id(0)
    nb = pl.num_programs(0)
    slot = b % 2
    past_rows = n_pages * page_rows
    nrow = q2_ref.shape[0] // 2

    def page_copy(hbm, buf, kind, sl, p, page):
        src = hbm.at[pl.ds(pl.multiple_of(page * page_rows, page_rows), page_rows)]
        return pltpu.make_async_copy(src, buf.at[sl, pl.ds(p * page_rows, page_rows)], sem.at[kind, sl])

    def start_fetch(seq, sl):
        for p in range(n_pages):
            page = pt_ref[seq * n_pages + p]
            page_copy(ck_hbm, kbuf, 0, sl, p, page).start()
            page_copy(cv_hbm, vbuf, 1, sl, p, page).start()

    @pl.when(b == 0)
    def _():
        start_fetch(0, 0)

    @pl.when(b + 1 < nb)
    def _():
        start_fetch(b + 1, 1 - slot)

    for p in range(n_pages):
        page_copy(ck_hbm, kbuf, 0, slot, p, 0).wait()
        page_copy(cv_hbm, vbuf, 1, slot, p, 0).wait()

    pad = jnp.zeros((chunk - kn_ref.shape[0], HEAD_W), F32)
    kbuf[slot, past_rows:past_rows + chunk, :] = jnp.concatenate([kn_ref[...], pad], axis=0)
    vbuf[slot, past_rows:past_rows + chunk, :] = jnp.concatenate([vn_ref[...], pad], axis=0)

    q2 = q2_ref[...]
    slope2 = slope2_ref[...]
    n_pairs = past_rows // (2 * chunk)
    for a in range(n_pairs):
        r0 = 2 * a * chunk
        ke = kbuf[slot, r0:r0 + chunk, :].astype(BF16)
        ko = kbuf[slot, r0 + chunk:r0 + 2 * chunk, :].astype(BF16)
        s2 = _dot_nt(q2, jnp.concatenate([ke, ko], axis=1))
        tok0 = float(2 * a * (chunk // n_heads) - past_rows // n_heads)
        s_ref[:, a * chunk:(a + 1) * chunk] = s2 + base2_ref[...] + slope2 * tok0
    q = q2[0:nrow, 0:HEAD_W]
    kn = kbuf[slot, past_rows:past_rows + chunk, :].astype(BF16)
    s_new = _dot_nt(q, kn) + newmask_ref[...]

    s2 = s_ref[...]
    m2 = jnp.max(s2, axis=-1, keepdims=True)
    m = jnp.maximum(jnp.maximum(m2[0:nrow], m2[nrow:2 * nrow]), jnp.max(s_new, axis=-1, keepdims=True))
    p2 = jnp.exp2(s2 - jnp.concatenate([m, m], axis=0))
    p_new = jnp.exp2(s_new - m)
    l2 = jnp.sum(p2, axis=-1, keepdims=True)
    l = l2[0:nrow] + l2[nrow:2 * nrow] + jnp.sum(p_new, axis=-1, keepdims=True)
    p2b = p2.astype(BF16)
    acc2 = jnp.zeros((2 * nrow, 2 * HEAD_W), F32)
    for a in range(n_pairs):
        r0 = 2 * a * chunk
        ve = vbuf[slot, r0:r0 + chunk, :].astype(BF16)
        vo = vbuf[slot, r0 + chunk:r0 + 2 * chunk, :].astype(BF16)
        acc2 = acc2 + _dot(p2b[:, a * chunk:(a + 1) * chunk], jnp.concatenate([ve, vo], axis=1))
    vn = vbuf[slot, past_rows:past_rows + chunk, :].astype(BF16)
    acc = acc2[0:nrow, 0:HEAD_W] + acc2[nrow:2 * nrow, HEAD_W:2 * HEAD_W] + _dot(p_new.astype(BF16), vn)

    o = acc / l
    lam = _diff_lambda(lamv_ref[...], lam_init)
    half = nrow // 2
    d = o[0:half] - lam * o[half:nrow]
    o_ref[...] = _rms(d, g_ref[...]) * (1.0 - lam_init)


def _attn_sample(page_table, q2, knf, vnf, ckf, cvf, lamv, g_row, slope2, base2, newmask, n_heads, lam_init):
    db, nrow2, _ = q2.shape
    nrow = nrow2 // 2
    n_pages = page_table.shape[1]
    chunk = base2.shape[1]
    page_rows = PAGE * n_heads
    past_rows = n_pages * page_rows
    assert past_rows % (2 * chunk) == 0 and knf.shape[1] <= chunk
    grid_spec = pltpu.PrefetchScalarGridSpec(
        num_scalar_prefetch=1,
        grid=(db,),
        in_specs=[
            _const_spec(lamv.shape),
            _const_spec(g_row.shape),
            _const_spec(slope2.shape),
            _const_spec(base2.shape),
            _const_spec(newmask.shape),
            pl.BlockSpec((None, nrow2, 2 * HEAD_W), lambda i, pt: (i, 0, 0)),
            pl.BlockSpec((None,) + knf.shape[1:], lambda i, pt: (i, 0, 0)),
            pl.BlockSpec((None,) + vnf.shape[1:], lambda i, pt: (i, 0, 0)),
            pl.BlockSpec(memory_space=pl.ANY),
            pl.BlockSpec(memory_space=pl.ANY),
        ],
        out_specs=pl.BlockSpec((None, nrow // 2, HEAD_W), lambda i, pt: (i, 0, 0)),
        scratch_shapes=[
            pltpu.VMEM((2, past_rows + chunk, HEAD_W), F32),
            pltpu.VMEM((2, past_rows + chunk, HEAD_W), F32),
            pltpu.SemaphoreType.DMA((2, 2)),
            pltpu.VMEM((nrow2, past_rows // 2), F32),
        ],
    )
    return pl.pallas_call(
        functools.partial(_attn_sample_kernel, n_pages=n_pages, page_rows=page_rows, n_heads=n_heads,
                          lam_init=lam_init, chunk=chunk),
        grid_spec=grid_spec,
        out_shape=jax.ShapeDtypeStruct((db, nrow // 2, HEAD_W), F32),
        compiler_params=_cparams(1),
        name="attn_sample",
    )(page_table.reshape(-1), lamv, g_row, slope2, base2, newmask, q2, knf, vnf, ckf, cvf)


def _s5_prep_kernel(lre_ref, lim_ref, ldt_ref, bre_ref, bim_ref, cre_ref, cim_ref, dsk_ref,
                    wst_ref, cat_ref, kt_ref, sct_ref):
    lr = lre_ref[...]
    li = lim_ref[...]
    dt = jnp.exp(ldt_ref[...])

    def lam_bar_pow(n):
        mag = jnp.exp(n * (lr * dt))
        ang = n * (li * dt)
        return mag * jnp.cos(ang), mag * jnp.sin(ang)

    a_r, a_i = lam_bar_pow(1.0)
    den = lr * lr + li * li
    nr, ni = a_r - 1.0, a_i
    f_r = (nr * lr + ni * li) / den
    f_i = (ni * lr - nr * li) / den
    b_r, b_i = bre_ref[...], bim_ref[...]
    bb_r = f_r * b_r - f_i * b_i
    bb_i = f_r * b_i + f_i * b_r
    c_r, c_i = cre_ref[...], cim_ref[...]
    gw = b_r.shape[0]
    ns = b_r.shape[1]

    ktau = []
    for tau in range(S5_BLOCK):
        p_r, p_i = lam_bar_pow(float(tau))
        w_r = bb_r * p_r - bb_i * p_i
        w_i = bb_r * p_i + bb_i * p_r
        j = S5_BLOCK - 1 - tau
        wst_ref[j * gw:(j + 1) * gw, 0:ns] = w_r.astype(BF16)
        wst_ref[j * gw:(j + 1) * gw, ns:2 * ns] = w_i.astype(BF16)
        ktau.append(_dot_nt(w_r, c_r, lax.Precision.HIGHEST) - _dot_nt(w_i, c_i, lax.Precision.HIGHEST))
    rr = lax.broadcasted_iota(jnp.int32, (gw, gw), 0)
    cc = lax.broadcasted_iota(jnp.int32, (gw, gw), 1)
    ktau[0] = ktau[0] + jnp.where(rr == cc, dsk_ref[...], 0.0)

    for j in range(S5_BLOCK):
        p_r, p_i = lam_bar_pow(float(j + 1))
        cat_ref[j * gw:(j + 1) * gw, 0:ns] = (c_r * p_r - c_i * p_i).astype(BF16)
        cat_ref[j * gw:(j + 1) * gw, ns:2 * ns] = (-(c_r * p_i + c_i * p_r)).astype(BF16)

    zeros = jnp.zeros((gw, gw), BF16)
    for i in range(S5_BLOCK):
        for j in range(S5_BLOCK):
            blk = ktau[j - i].astype(BF16) if j >= i else zeros
            kt_ref[i * gw:(i + 1) * gw, j * gw:(j + 1) * gw] = blk

    row = lax.broadcasted_iota(jnp.int32, (8, ns), 0)
    for idx, k in enumerate((1, 2, 4)):
        p_r, p_i = lam_bar_pow(float(S5_BLOCK * k))
        sct_ref[2 * idx] = jnp.where(row >= k, p_r, 0.0)
        sct_ref[2 * idx + 1] = jnp.where(row >= k, p_i, 0.0)
    n = (S5_BLOCK * (row + 1)).astype(F32)
    mag = jnp.exp(n * (lr * dt))
    ang = n * (li * dt)
    sct_ref[6] = mag * jnp.cos(ang)
    sct_ref[7] = mag * jnp.sin(ang)


def _s5_prep(lre, lim, ldt, bre, bim, cre, cim, dsk):
    nq, gw, ns = bre.shape
    kw = S5_BLOCK * gw
    per_q = lambda *tail: pl.BlockSpec((None,) + tail, lambda q: (q,) + (0,) * len(tail))
    return pl.pallas_call(
        _s5_prep_kernel,
        grid=(nq,),
        in_specs=[per_q(1, ns), per_q(1, ns), per_q(1, ns), per_q(gw, ns), per_q(gw, ns),
                  per_q(gw, ns), per_q(gw, ns), per_q(1, gw)],
        out_specs=[per_q(kw, 2 * ns), per_q(kw, 2 * ns), per_q(kw, kw), per_q(8, 8, ns)],
        out_shape=[
            jax.ShapeDtypeStruct((nq, kw, 2 * ns), BF16),
            jax.ShapeDtypeStruct((nq, kw, 2 * ns), BF16),
            jax.ShapeDtypeStruct((nq, kw, kw), BF16),
            jax.ShapeDtypeStruct((nq, 8, 8, ns), F32),
        ],
        compiler_params=_cparams(1),
        name="s5_prep",
    )(lre, lim, ldt, bre, bim, cre, cim, dsk)


def _quarter_lanes(u_ref, q, gw, sw):
    return jnp.concatenate(
        [u_ref[:, j * sw + q * gw: j * sw + (q + 1) * gw] for j in range(S5_BLOCK)], axis=1)


def _store_quarter(y_ref, y, q, gw, sw):
    for j in range(S5_BLOCK):
        y_ref[:, j * sw + q * gw: j * sw + (q + 1) * gw] = y[:, j * gw:(j + 1) * gw].astype(y_ref.dtype)


def _s5_prompt_kernel(u_ref, wst_ref, cat_ref, kt_ref, sct_ref, y_ref, st_ref,
                      carry_ref, s_scr, hp_scr, *, rb, gw, sw, ns):
    blk = pl.program_id(1)

    @pl.when(blk == 0)
    def _():
        carry_ref[...] = jnp.zeros(carry_ref.shape, F32)

    row0 = lax.broadcasted_iota(jnp.int32, (8, ns), 0) == 0
    for q in range(S5_QUARTERS):
        uq = _quarter_lanes(u_ref, q, gw, sw)
        s_scr[...] = _dot(uq, wst_ref[q])
        tabs = [sct_ref[q, i] for i in range(8)]

        def tile(t, carry, tabs=tabs):
            cr, ci = carry
            r0 = pl.multiple_of(t * 8, 8)
            xr = s_scr[pl.ds(r0, 8), 0:ns]
            xi = s_scr[pl.ds(r0, 8), ns:2 * ns]
            for idx, k in enumerate((1, 2, 4)):
                mr, mi = tabs[2 * idx], tabs[2 * idx + 1]
                sr = pltpu.roll(xr, k, 0)
                si = pltpu.roll(xi, k, 0)
                xr, xi = xr + (mr * sr - mi * si), xi + (mr * si + mi * sr)
            crb = jnp.broadcast_to(cr, (8, ns))
            cib = jnp.broadcast_to(ci, (8, ns))
            hr = xr + (tabs[6] * crb - tabs[7] * cib)
            hi = xi + (tabs[6] * cib + tabs[7] * crb)
            hp_scr[pl.ds(r0, 8), 0:ns] = jnp.where(row0, crb, pltpu.roll(hr, 1, 0))
            hp_scr[pl.ds(r0, 8), ns:2 * ns] = jnp.where(row0, cib, pltpu.roll(hi, 1, 0))
            return hr[7:8, :], hi[7:8, :]

        cr, ci = lax.fori_loop(0, rb // 8, tile,
                               (carry_ref[q, :, 0:ns], carry_ref[q, :, ns:2 * ns]))
        carry_ref[q, :, 0:ns] = cr
        carry_ref[q, :, ns:2 * ns] = ci
        st_ref[q, :, 0:ns] = cr
        st_ref[q, :, ns:2 * ns] = ci
        y = _dot_nt(hp_scr[...].astype(BF16), cat_ref[q]) + _dot(uq, kt_ref[q])
        _store_quarter(y_ref, y, q, gw, sw)


def _s5_prompt(u4, wst, cat, kt, sct, rb):
    b, rows, lanes = u4.shape
    nq, kw, ns2 = wst.shape
    ns, gw, sw = ns2 // 2, kw // S5_BLOCK, lanes // S5_BLOCK
    return pl.pallas_call(
        functools.partial(_s5_prompt_kernel, rb=rb, gw=gw, sw=sw, ns=ns),
        grid=(b, rows // rb),
        in_specs=[
            pl.BlockSpec((None, rb, lanes), lambda bi, ri: (bi, ri, 0)),
            _const_spec(wst.shape), _const_spec(cat.shape), _const_spec(kt.shape), _const_spec(sct.shape),
        ],
        out_specs=[
            pl.BlockSpec((None, rb, lanes), lambda bi, ri: (bi, ri, 0)),
            pl.BlockSpec((None, nq, 1, ns2), lambda bi, ri: (bi, 0, 0, 0)),
        ],
        out_shape=[
            jax.ShapeDtypeStruct((b, rows, lanes), BF16),
            jax.ShapeDtypeStruct((b, nq, 1, ns2), F32),
        ],
        scratch_shapes=[
            pltpu.VMEM((nq, 1, ns2), F32),
            pltpu.VMEM((rb, ns2), F32),
            pltpu.VMEM((rb, ns2), F32),
        ],
        compiler_params=_cparams(2),
        name="s5_prompt",
    )(u4, wst, cat, kt, sct)


def _s5_sample_kernel(u_ref, h0_ref, wst_ref, cat_ref, kt_ref, sct_ref, y_ref, hout_ref, *, gw, sw, ns):
    for q in range(S5_QUARTERS):
        uq = _quarter_lanes(u_ref, q, gw, sw)
        s = _dot(uq, wst_ref[q])
        h0 = h0_ref[q]
        h0r, h0i = h0[:, 0:ns], h0[:, ns:2 * ns]
        a4r = sct_ref[q, 6][0:1, :]
        a4i = sct_ref[q, 7][0:1, :]
        hout_ref[q, :, 0:ns] = s[:, 0:ns] + (a4r * h0r - a4i * h0i)
        hout_ref[q, :, ns:2 * ns] = s[:, ns:2 * ns] + (a4r * h0i + a4i * h0r)
        y = _dot_nt(h0.astype(BF16), cat_ref[q]) + _dot(uq, kt_ref[q])
        _store_quarter(y_ref, y, q, gw, sw)


def _s5_sample(u4, h0q, wst, cat, kt, sct):
    rows, lanes = u4.shape
    nq, kw, ns2 = wst.shape
    ns, gw, sw = ns2 // 2, kw // S5_BLOCK, lanes // S5_BLOCK
    return pl.pallas_call(
        functools.partial(_s5_sample_kernel, gw=gw, sw=sw, ns=ns),
        grid=(1,),
        in_specs=[_const_spec(u4.shape), _const_spec(h0q.shape), _const_spec(wst.shape),
                  _const_spec(cat.shape), _const_spec(kt.shape), _const_spec(sct.shape)],
        out_specs=[pl.BlockSpec(u4.shape, lambda i: (0, 0)),
                   pl.BlockSpec(h0q.shape, lambda i: (0, 0, 0))],
        out_shape=[jax.ShapeDtypeStruct(u4.shape, BF16), jax.ShapeDtypeStruct(h0q.shape, F32)],
        compiler_params=_cparams(1),
        name="s5_sample",
    )(u4, h0q, wst, cat, kt, sct)


def _mem_kv_kernel(x_ref, g_ref, wk_ref, wv_ref, k_ref, v_ref, kb_ref, vb_ref):
    hn = _rms(x_ref[...], g_ref[...]).astype(BF16)
    k = _dot(hn, wk_ref[...])
    k_ref[...] = k
    kb_ref[...] = k.astype(BF16)
    v = _dot(hn, wv_ref[...])
    v_ref[...] = v
    vb_ref[...] = v.astype(BF16)


def _mem_kv(mem, g, wk_bf, wv_bf, tm):
    n, d = mem.shape
    row = pl.BlockSpec((tm, d), lambda i: (i, 0))
    return pl.pallas_call(
        _mem_kv_kernel,
        grid=(n // tm,),
        in_specs=[row, _const_spec((1, d)), _const_spec(wk_bf.shape), _const_spec(wv_bf.shape)],
        out_specs=[row, row, row, row],
        out_shape=[jax.ShapeDtypeStruct((n, d), F32), jax.ShapeDtypeStruct((n, d), F32),
                   jax.ShapeDtypeStruct((n, d), BF16), jax.ShapeDtypeStruct((n, d), BF16)],
        compiler_params=_cparams(1),
        name="mem_kv",
    )(mem, g, wk_bf, wv_bf)


def _merge_and_query(x, att, sy4_ref, sy_scr, wglu_ref, gso_ref, wout_ref, gpost_ref, gmpre_ref, wmq_ref,
                     mem_scale):
    aw = att.shape[1]
    rows, n_lb = sy4_ref.shape[0], sy_scr.shape[0]
    sw = n_lb * LANES
    for j in range(S5_BLOCK):
        for c in range(n_lb):
            blk = sy4_ref[:, j * sw + c * LANES:j * sw + (c + 1) * LANES].astype(F32)
            sy_scr[c, pl.ds(j, rows, stride=S5_BLOCK), :] = blk
    y = jax.nn.gelu(jnp.concatenate([sy_scr[c] for c in range(n_lb)], axis=1))
    y = y * jax.nn.sigmoid(_dot(y.astype(BF16), wglu_ref[...]))
    y = _rms(y, gso_ref[...])
    mix = _dot(att, wout_ref[0:aw, :]) + _dot(y.astype(BF16), wout_ref[aw:, :])
    x1 = x + _rms(mix, gpost_ref[...])
    hq = _rms(x1, gmpre_ref[...]).astype(BF16)
    qm = (_dot(hq, wmq_ref[...]) * mem_scale).astype(BF16)
    return x1, qm


def _mix_mem_kernel(x_ref, att_ref, sy_ref, mk_ref, mv_ref, wglu_ref, gso_ref, wout_ref, gpost_ref,
                    gmpre_ref, wmq_ref, wmo_ref, gmpost_ref, o_ref, oh_ref, sy_scr, *, n_heads, mem_scale):
    x1, qm = _merge_and_query(x_ref[...], att_ref[...], sy_ref, sy_scr, wglu_ref, gso_ref, wout_ref,
                              gpost_ref, gmpre_ref, wmq_ref, mem_scale)
    hd = qm.shape[1] // n_heads
    for h in range(n_heads):
        sl = slice(h * hd, (h + 1) * hd)
        s = _dot_nt(qm[:, sl], mk_ref[:, sl])
        p = jnp.exp(s - jnp.max(s, axis=-1, keepdims=True))
        l = jnp.sum(p, axis=-1, keepdims=True)
        oh_ref[:, sl] = (_dot(p.astype(BF16), mv_ref[:, sl]) / l).astype(BF16)
    mo = _dot(oh_ref[...], wmo_ref[...])
    o_ref[...] = x1 + _rms(mo, gmpost_ref[...])


def _mix_mem(x, att, sy4, mkb, mvb, w, tm, n_heads):
    b, t, d = x.shape
    aw, sw = att.shape[2], sy4.shape[2] // S5_BLOCK
    nm = mkb.shape[1]
    tok = lambda width: pl.BlockSpec((None, tm, width), lambda bi, ti: (bi, ti, 0))
    tok4 = pl.BlockSpec((None, tm // S5_BLOCK, S5_BLOCK * sw), lambda bi, ti: (bi, ti, 0))
    per_b = pl.BlockSpec((None, nm, d), lambda bi, ti: (bi, 0, 0))
    weights = [w["w_glu"], w["g_ssm_out"], w["w_out"], w["g_mix_post"], w["g_mem_pre"], w["w_mq"],
               w["w_mo"], w["g_mem_post"]]
    return pl.pallas_call(
        functools.partial(_mix_mem_kernel, n_heads=n_heads, mem_scale=(d // n_heads) ** -0.5),
        grid=(b, t // tm),
        in_specs=[tok(d), tok(aw), tok4, per_b, per_b] + [_const_spec(a.shape) for a in weights],
        out_specs=tok(d),
        out_shape=jax.ShapeDtypeStruct((b, t, d), F32),
        scratch_shapes=[pltpu.VMEM((tm, d), BF16), pltpu.VMEM((sw // LANES, tm, LANES), F32)],
        compiler_params=_cparams(2),
        name="mix_mem",
    )(x, att, sy4, mkb, mvb, *weights)


def _mix_q_kernel(x_ref, att_ref, sy_ref, wglu_ref, gso_ref, wout_ref, gpost_ref, gmpre_ref, wmq_ref,
                  x1_ref, qm_ref, sy_scr, *, mem_scale):
    x1, qm = _merge_and_query(x_ref[...], att_ref[...], sy_ref, sy_scr, wglu_ref, gso_ref, wout_ref,
                              gpost_ref, gmpre_ref, wmq_ref, mem_scale)
    x1_ref[...] = x1
    qm_ref[...] = qm


def _mix_q(x, att, sy4, w, n_heads):
    n, d = x.shape
    weights = [w["w_glu"], w["g_ssm_out"], w["w_out"], w["g_mix_post"], w["g_mem_pre"], w["w_mq"]]
    ins = [x, att, sy4] + weights
    return pl.pallas_call(
        functools.partial(_mix_q_kernel, mem_scale=(d // n_heads) ** -0.5),
        grid=(1,),
        in_specs=[_const_spec(a.shape) for a in ins],
        out_specs=[pl.BlockSpec((n, d), lambda i: (0, 0)), pl.BlockSpec((n, d), lambda i: (0, 0))],
        out_shape=[jax.ShapeDtypeStruct((n, d), F32), jax.ShapeDtypeStruct((n, d), BF16)],
        scratch_shapes=[pltpu.VMEM((sy4.shape[1] // S5_BLOCK // LANES, n, LANES), F32)],
        compiler_params=_cparams(1),
        name="mix_q",
    )(*ins)


def _mem_sample_kernel(mask_ref, q_ref, ck_hbm, cv_hbm, o_ref, kbuf, vbuf, sem, *, n_mem, n_heads):
    b = pl.program_id(0)
    nb = pl.num_programs(0)
    slot = b % 2
    n_halves = kbuf.shape[1]

    def copies(seq, sl):
        out = []
        for kind, (hbm, buf) in enumerate(((ck_hbm, kbuf), (cv_hbm, vbuf))):
            for lb in range(n_halves):
                src = hbm.at[seq, :, :, pl.ds(lb * LANES, LANES)]
                dst = buf.at[sl, lb].reshape(n_mem, n_heads, LANES)
                out.append(pltpu.make_async_copy(src, dst, sem.at[kind, sl]))
        return out

    @pl.when(b == 0)
    def _():
        for cp in copies(0, 0):
            cp.start()

    @pl.when(b + 1 < nb)
    def _():
        for cp in copies(b + 1, 1 - slot):
            cp.start()

    for cp in copies(b, slot):
        cp.wait()

    kcat = jnp.concatenate([kbuf[slot, lb] for lb in range(n_halves)], axis=1).astype(BF16)
    vcat = jnp.concatenate([vbuf[slot, lb] for lb in range(n_halves)], axis=1).astype(BF16)
    s = _dot_nt(q_ref[...], kcat) + mask_ref[...]
    p = jnp.exp(s - jnp.max(s, axis=-1, keepdims=True))
    l = jnp.sum(p, axis=-1, keepdims=True)
    o_ref[...] = _dot(p.astype(BF16), vcat) / l


def _mem_sample(qh, ck, cv, mask):
    db, nrow, hd = qh.shape
    _, n_mem, n_heads, _ = ck.shape
    grid_spec = pltpu.PrefetchScalarGridSpec(
        num_scalar_prefetch=0,
        grid=(db,),
        in_specs=[
            _const_spec(mask.shape),
            pl.BlockSpec((None, nrow, hd), lambda i: (i, 0, 0)),
            pl.BlockSpec(memory_space=pl.ANY),
            pl.BlockSpec(memory_space=pl.ANY),
        ],
        out_specs=pl.BlockSpec((None, nrow, hd), lambda i: (i, 0, 0)),
        scratch_shapes=[
            pltpu.VMEM((2, hd // LANES, n_mem * n_heads, LANES), F32),
            pltpu.VMEM((2, hd // LANES, n_mem * n_heads, LANES), F32),
            pltpu.SemaphoreType.DMA((2, 2)),
        ],
    )
    return pl.pallas_call(
        functools.partial(_mem_sample_kernel, n_mem=n_mem, n_heads=n_heads),
        grid_spec=grid_spec,
        out_shape=jax.ShapeDtypeStruct((db, nrow, hd), F32),
        compiler_params=_cparams(1),
        name="mem_sample",
    )(mask, qh, ck, cv)


def _mem_out_kernel(x1_ref, o_ref, wmo_ref, g_ref, y_ref):
    mo = _dot(o_ref[...], wmo_ref[...])
    y_ref[...] = x1_ref[...] + _rms(mo, g_ref[...])


def _mem_out(x1, o, wmo, g):
    n, d = x1.shape
    ins = [x1, o, wmo, g]
    return pl.pallas_call(
        _mem_out_kernel,
        grid=(1,),
        in_specs=[_const_spec(a.shape) for a in ins],
        out_specs=pl.BlockSpec((n, d), lambda i: (0, 0)),
        out_shape=jax.ShapeDtypeStruct((n, d), F32),
        compiler_params=_cparams(1),
        name="mem_out",
    )(*ins)


def _ffn_kernel(x_ref, gpre_ref, wg_ref, wu_ref, wd_ref, gpost_ref, o_ref, *, n_chunks):
    x = x_ref[...]
    hn = _rms(x, gpre_ref[...]).astype(BF16)
    ff = wg_ref.shape[1]
    cw = ff // n_chunks
    acc = jnp.zeros(x.shape, F32)
    for c in range(n_chunks):
        sl = slice(c * cw, (c + 1) * cw)
        a = jax.nn.silu(_dot(hn, wg_ref[:, sl])) * _dot(hn, wu_ref[:, sl])
        acc = acc + _dot(a.astype(BF16), wd_ref[sl, :])
    o_ref[...] = x + _rms(acc, gpost_ref[...])


def _ffn(x, gpre, wg, wu, wd, gpost, tm, n_chunks):
    n, d = x.shape
    row = pl.BlockSpec((tm, d), lambda i: (i, 0))
    weights = [gpre, wg, wu, wd, gpost]
    return pl.pallas_call(
        functools.partial(_ffn_kernel, n_chunks=n_chunks),
        grid=(n // tm,),
        in_specs=[row] + [_const_spec(a.shape) for a in weights],
        out_specs=row,
        out_shape=jax.ShapeDtypeStruct((n, d), F32),
        compiler_params=_cparams(1),
        name="ffn",
    )(x, *weights)


def _s5_param_layout(lam_re, lam_im, log_dt, b_re, b_im, c_re, c_im, d_skip):
    g, p, hc = b_re.shape
    nq = S5_QUARTERS
    gq = g // nq
    rowv = lambda a: a.reshape(nq, 1, gq * p)
    ldt = jnp.broadcast_to(log_dt[:, None], (g, p))
    eye = jnp.eye(gq, dtype=F32)

    def embed_b(b):
        bq = b.reshape(nq, gq, p, hc)
        return jnp.einsum("qgph,gk->qghkp", bq, eye).reshape(nq, gq * hc, gq * p)

    def embed_c(c):
        cq = c.reshape(nq, gq, hc, p)
        return jnp.einsum("qghp,gk->qghkp", cq, eye).reshape(nq, gq * hc, gq * p)

    return (rowv(lam_re), rowv(lam_im), rowv(ldt), embed_b(b_re), embed_b(b_im),
            embed_c(c_re), embed_c(c_im), d_skip.reshape(nq, 1, gq * hc))


def _state_to_quarters(re, im):
    n, g, p = re.shape
    gq = g // S5_QUARTERS
    f = lambda a: a.reshape(n, S5_QUARTERS, gq * p).transpose(1, 0, 2)
    return jnp.concatenate([f(re), f(im)], axis=-1)


def _state_from_quarters(st, g, p):
    n = st.shape[0]
    ns = st.shape[-1] // 2
    return st[..., :ns].reshape(n, g, p), st[..., ns:].reshape(n, g, p)


def kernel(x_prompt, x_sample, mem_prompt, cache_k, cache_v, page_table, state_ssm_re, state_ssm_im,
           cache_mem_k, cache_mem_v, g_mix_pre, g_mix_post, w_in, lam_q1, lam_k1, lam_q2, lam_k2, g_subln,
           ssm_lam_re, ssm_lam_im, ssm_log_dt, ssm_b_re, ssm_b_im, ssm_c_re, ssm_c_im, ssm_d, w_glu, g_ssm_out,
           w_out, g_mem_pre, g_mem_post, g_mem_kv, w_mq, w_mk, w_mv, w_mo, g_ffn_pre, g_ffn_post,
           w_gate, w_up, w_down):
    depth = w_in.shape[0]
    b, t, d = x_prompt.shape
    db, dt_, _ = x_sample.shape
    n_mem = mem_prompt.shape[1]
    n_mem_heads = cache_mem_k.shape[3]
    n_heads = cache_k.shape[3]
    aw = n_heads * HEAD_W
    g, p = ssm_lam_re.shape[1], ssm_lam_re.shape[2]
    sw = g * SSM_GROUP
    n_phys = cache_k.shape[1]
    assert dt_ == S5_BLOCK and t % (S5_BLOCK * 256) == 0

    slopes = LOG2E * 2.0 ** (-8.0 * jnp.arange(1, n_heads + 1, dtype=F32) / n_heads)
    slope_hi = slopes.astype(BF16).astype(F32)
    slope_parts = jnp.stack([slopes, slope_hi, (slopes - slope_hi).astype(BF16).astype(F32)])
    row = lambda a: a.reshape(1, -1).astype(F32)
    bf = lambda a: a.astype(BF16)

    branch_lane = jnp.arange(HEAD_W) // HEAD_DIM
    nrow = 2 * n_heads * dt_
    chunk = 256
    r = jnp.arange(nrow)
    r_head, r_tok = (r // dt_) % n_heads, r % dt_
    r_slope = slopes[r_head][:, None]
    c = jnp.arange(chunk)
    c_tok, c_head = c // n_heads, c % n_heads
    own_head = c_head[None, :] == r_head[:, None]
    base = r_slope * c_tok[None, :].astype(F32) + jnp.where(own_head, 0.0, NEG)
    base2 = jnp.concatenate([base, base + r_slope * float(chunk // n_heads)], axis=0)
    slope2 = jnp.concatenate([r_slope, r_slope], axis=0)
    new_ok = own_head & (c[None, :] < dt_ * n_heads) & (c_tok[None, :] <= r_tok[:, None])
    newmask = jnp.where(new_ok, r_slope * c_tok[None, :].astype(F32), NEG)
    mem_row_head = jnp.arange(n_mem_heads * dt_) // dt_
    mem_col_head = jnp.arange(n_mem * n_mem_heads) % n_mem_heads
    mem_mask = jnp.where(mem_col_head[None, :] == mem_row_head[:, None], 0.0, NEG).astype(F32)

    y_p = x_prompt
    y_s = x_sample.reshape(db * dt_, d)
    outs = {k: [] for k in ("kp", "vp", "ks", "vs", "rp", "ip", "rs", "is", "mkp", "mvp")}
    for l in range(depth):
        lam_init = 0.8 - 0.6 * math.exp(-0.3 * l)
        lamv = jnp.stack([lam_q1[l], lam_k1[l], lam_q2[l], lam_k2[l]]).astype(F32)
        gsub = row(g_subln[l])
        w_in_bf = bf(w_in[l])
        w = {
            "w_glu": bf(w_glu[l]), "g_ssm_out": row(g_ssm_out[l]), "w_out": bf(w_out[l]),
            "g_mix_post": row(g_mix_post[l]), "g_mem_pre": row(g_mem_pre[l]), "w_mq": bf(w_mq[l]),
            "w_mo": bf(w_mo[l]), "g_mem_post": row(g_mem_post[l]),
        }

        qp, kp, vp, kpb, vpb, up = _proj(y_p.reshape(b * t, d), row(g_mix_pre[l]), w_in_bf, aw, 512)
        qs, ks, vs, _, _, us = _proj(y_s, row(g_mix_pre[l]), w_in_bf, aw, 512)

        att_pt = _attn_prompt(jnp.swapaxes(qp.reshape(b, t, aw), 1, 2), kpb.reshape(b, t, aw),
                              jnp.swapaxes(vpb.reshape(b, t, aw), 1, 2),
                              slope_parts, lamv, g_subln[l].reshape(HEAD_W, 1).astype(F32), lam_init, 512)
        att_p = jnp.swapaxes(att_pt, 1, 2)
        qh = qs.reshape(db, dt_, n_heads, HEAD_W).transpose(0, 2, 1, 3)
        zq = jnp.zeros((), BF16)
        q32 = jnp.stack([jnp.where(branch_lane == 0, qh, zq), jnp.where(branch_lane == 1, qh, zq)], axis=1)
        q32 = q32.reshape(db, nrow, HEAD_W)
        zq32 = jnp.zeros_like(q32)
        q2 = jnp.concatenate([jnp.concatenate([q32, zq32], axis=2), jnp.concatenate([zq32, q32], axis=2)], axis=1)
        att_s16 = _attn_sample(page_table, q2, ks.reshape(db, dt_ * n_heads, HEAD_W),
                               vs.reshape(db, dt_ * n_heads, HEAD_W),
                               cache_k[l].reshape(n_phys * PAGE * n_heads, HEAD_W),
                               cache_v[l].reshape(n_phys * PAGE * n_heads, HEAD_W),
                               lamv, gsub, slope2, base2, newmask, n_heads, lam_init)
        att_s = bf(att_s16.reshape(db, n_heads, dt_, HEAD_W).transpose(0, 2, 1, 3).reshape(db * dt_, aw))

        tabs = _s5_prep(*_s5_param_layout(ssm_lam_re[l], ssm_lam_im[l], ssm_log_dt[l], ssm_b_re[l],
                                          ssm_b_im[l], ssm_c_re[l], ssm_c_im[l], ssm_d[l]))
        syp, stp = _s5_prompt(up.reshape(b, t // S5_BLOCK, S5_BLOCK * sw), *tabs, 256)
        rp, ip = _state_from_quarters(stp.reshape(b, S5_QUARTERS, -1), g, p)
        sys_, sts = _s5_sample(us, _state_to_quarters(state_ssm_re[l], state_ssm_im[l]), *tabs)
        rs, is_ = _state_from_quarters(sts.transpose(1, 0, 2), g, p)

        mkp, mvp, mkb, mvb = _mem_kv(mem_prompt.reshape(b * n_mem, d), row(g_mem_kv[l]),
                                     bf(w_mk[l]), bf(w_mv[l]), 512)
        y_p = _mix_mem(y_p, att_p, syp, mkb.reshape(b, n_mem, d), mvb.reshape(b, n_mem, d), w, 512, n_mem_heads)
        x1s, qms = _mix_q(y_s, att_s, sys_, w, n_mem_heads)
        mhd = d // n_mem_heads
        qmh = qms.reshape(db, dt_, n_mem_heads, mhd).transpose(0, 2, 1, 3).reshape(db, n_mem_heads * dt_, mhd)
        omh = _mem_sample(qmh, cache_mem_k[l], cache_mem_v[l], mem_mask)
        om = bf(omh.reshape(db, n_mem_heads, dt_, mhd).transpose(0, 2, 1, 3).reshape(db * dt_, d))
        y_s = _mem_out(x1s, om, w["w_mo"], w["g_mem_post"])

        ffw = (row(g_ffn_pre[l]), bf(w_gate[l]), bf(w_up[l]), bf(w_down[l]), row(g_ffn_post[l]))
        y_p = _ffn(y_p.reshape(b * t, d), *ffw, 512, 2).reshape(b, t, d)
        y_s = _ffn(y_s, *ffw, 512, 2)

        outs["kp"].append(kp.reshape(b, t, n_heads, HEAD_W))
        outs["vp"].append(vp.reshape(b, t, n_heads, HEAD_W))
        outs["ks"].append(ks.reshape(db, dt_, n_heads, HEAD_W))
        outs["vs"].append(vs.reshape(db, dt_, n_heads, HEAD_W))
        outs["rp"].append(rp); outs["ip"].append(ip); outs["rs"].append(rs); outs["is"].append(is_)
        outs["mkp"].append(mkp.reshape(b, n_mem, n_mem_heads, d // n_mem_heads))
        outs["mvp"].append(mvp.reshape(b, n_mem, n_mem_heads, d // n_mem_heads))

    st = lambda k: jnp.stack(outs[k], 0)
    return (y_p, y_s.reshape(db, dt_, d), st("kp"), st("vp"), st("ks"), st("vs"),
            st("rp"), st("ip"), st("rs"), st("is"), st("mkp"), st("mvp"))
```

```python
import functools
import math

import jax
import jax.numpy as jnp
from jax import lax
from jax.experimental import pallas as pl
from jax.experimental.pallas import tpu as pltpu

F32 = jnp.float32
BF16 = jnp.bfloat16

RMS_EPS = 1e-6
HEAD_DIM = 64
HEAD_W = 2 * HEAD_DIM
SSM_GROUP = 16
SSM_STATE = 64
S5_BLOCK = 4
S5_QUARTERS = 4
PAGE = 128
LANES = 128
NEG = -1e30
LOG2E = 1.4426950408889634
VMEM_LIMIT_BYTES = 56 * 1024 * 1024


def _cparams(n_grid_dims):
    return pltpu.CompilerParams(
        dimension_semantics=("arbitrary",) * n_grid_dims,
        vmem_limit_bytes=VMEM_LIMIT_BYTES)


def _const_spec(shape):
    nd = len(shape)
    return pl.BlockSpec(shape, lambda *_: (0,) * nd, pipeline_mode=pl.Buffered(1))


def _rms(x, g):
    ms = jnp.mean(x * x, axis=-1, keepdims=True)
    return x * lax.rsqrt(ms + RMS_EPS) * g


def _dot(a, b):
    return jnp.dot(a, b, preferred_element_type=F32)


def _dot_nt(a, b, precision=None):
    return lax.dot_general(a, b, (((1,), (1,)), ((), ())),
                           preferred_element_type=F32, precision=precision)


def _proj_kernel(x_ref, g_ref, w_ref, q_ref, k_ref, v_ref, kb_ref, vb_ref, u_ref, u_scr, *, aw, tm):
    hn = _rms(x_ref[...], g_ref[...]).astype(BF16)
    q = _dot(hn, w_ref[:, 0:aw])
    q_ref[...] = (q * (HEAD_DIM ** -0.5 * LOG2E)).astype(BF16)
    nh = aw // HEAD_W
    k = _dot(hn, w_ref[:, aw:2 * aw])
    kb_ref[...] = k.astype(BF16)
    v = _dot(hn, w_ref[:, 2 * aw:3 * aw])
    vb_ref[...] = v.astype(BF16)
    for h in range(nh):
        k_ref[pl.ds(h, tm, stride=nh), :] = k[:, h * HEAD_W:(h + 1) * HEAD_W]
        v_ref[pl.ds(h, tm, stride=nh), :] = v[:, h * HEAD_W:(h + 1) * HEAD_W]
    u = _dot(hn, w_ref[:, 3 * aw:])
    n_lb = u_scr.shape[0]
    sw = n_lb * LANES
    for c in range(n_lb):
        u_scr[c] = u[:, c * LANES:(c + 1) * LANES]
    for j in range(S5_BLOCK):
        for c in range(n_lb):
            rows = u_scr[c, pl.ds(j, tm // S5_BLOCK, stride=S5_BLOCK), :]
            u_ref[:, j * sw + c * LANES:j * sw + (c + 1) * LANES] = rows.astype(BF16)


def _proj(x, g, w_bf, aw, tm):
    n, d = x.shape
    sw = w_bf.shape[1] - 3 * aw
    nh = aw // HEAD_W
    row = lambda width: pl.BlockSpec((tm, width), lambda i: (i, 0))
    flat = pl.BlockSpec((tm * nh, HEAD_W), lambda i: (i, 0))
    return pl.pallas_call(
        functools.partial(_proj_kernel, aw=aw, tm=tm),
        grid=(n // tm,),
        in_specs=[row(d), _const_spec((1, d)), _const_spec(w_bf.shape)],
        out_specs=[row(aw), flat, flat, row(aw), row(aw),
                   pl.BlockSpec((tm // S5_BLOCK, S5_BLOCK * sw), lambda i: (i, 0))],
        out_shape=[
            jax.ShapeDtypeStruct((n, aw), BF16),
            jax.ShapeDtypeStruct((n * nh, HEAD_W), F32),
            jax.ShapeDtypeStruct((n * nh, HEAD_W), F32),
            jax.ShapeDtypeStruct((n, aw), BF16),
            jax.ShapeDtypeStruct((n, aw), BF16),
            jax.ShapeDtypeStruct((n // S5_BLOCK, S5_BLOCK * sw), BF16),
        ],
        scratch_shapes=[pltpu.VMEM((sw // LANES, tm, LANES), F32)],
        compiler_params=_cparams(1),
        name="proj",
    )(x, g, w_bf)


def _diff_lambda(lamv, lam_init):
    d1 = jnp.sum(lamv[0:1] * lamv[1:2], axis=-1, keepdims=True)
    d2 = jnp.sum(lamv[2:3] * lamv[3:4], axis=-1, keepdims=True)
    return jnp.exp(d1) - jnp.exp(d2) + lam_init


def _attn_prompt_kernel(sl_ref, lamv_ref, g_ref, pos_ref, qt_ref, k_ref, vt_ref, o_ref,
                        qq_ref, m_ref, acc_ref, t0_ref, t1_ref, *, tq, lam_init):
    hp = pl.program_id(1)
    qi = pl.program_id(2)
    tk = tq
    n_hh = qq_ref.shape[0]
    heads = [hp * n_hh + hh for hh in range(n_hh)]
    hrows = [slice(hh * HEAD_W, (hh + 1) * HEAD_W) for hh in range(n_hh)]

    arow = lax.broadcasted_iota(jnp.int32, (HEAD_W, 2 * tq), 0)
    for hh in range(n_hh):
        qt = qt_ref[hrows[hh], :].astype(F32)
        frow = lax.broadcasted_iota(jnp.int32, qt.shape, 0)
        qq_ref[hh, 0:HEAD_W, 0:tq] = jnp.where(frow < HEAD_DIM, qt, 0.0).astype(BF16)
        qq_ref[hh, 0:HEAD_W, tq:2 * tq] = jnp.where(frow >= HEAD_DIM, qt, 0.0).astype(BF16)
        slope_parts = jnp.where((arow & 1) == 0, sl_ref[1, heads[hh]], sl_ref[2, heads[hh]])
        qq_ref[hh, HEAD_W:2 * HEAD_W, :] = jnp.where(arow < 4, slope_parts, 0.0).astype(BF16)
    m_ref[...] = jnp.full(m_ref.shape, NEG, F32)
    acc_ref[...] = jnp.zeros(acc_ref.shape, F32)
    ones = jnp.ones((acc_ref.shape[1] - HEAD_W, tk), BF16)

    def scores(hh, kj):
        k0 = pl.multiple_of(kj * tk, tk)
        ka = jnp.concatenate([k_ref[pl.ds(k0, tk), hrows[hh]], pos_ref[...]], axis=1)
        return _dot(ka, qq_ref[hh])

    def consume(hh, kj, t):
        k0 = pl.multiple_of(kj * tk, tk)
        vta = jnp.concatenate([vt_ref[hrows[hh], pl.ds(k0, tk)], ones], axis=0)
        off = sl_ref[0, heads[hh]] * jnp.full((1, 2 * tq), (kj - qi) * tk, jnp.int32).astype(F32)
        m_old = m_ref[hh]
        m_new = jnp.maximum(m_old, jnp.max(t, axis=0, keepdims=True) + off)
        alpha = jnp.exp2(m_old - m_new)
        p = jnp.exp2(t - (m_new - off)).astype(BF16)
        acc_ref[hh] = alpha * acc_ref[hh] + _dot(vta, p)
        m_ref[hh] = m_new

    tbuf = (t0_ref, t1_ref)

    def step(kj, cur):
        for hh in range(n_hh):
            tbuf[1 - cur][hh] = scores(hh, kj + 1)
        for hh in range(n_hh):
            consume(hh, kj, tbuf[cur][hh])

    for hh in range(n_hh):
        t0_ref[hh] = scores(hh, 0)

    def body(i, carry):
        step(2 * i, 0)
        step(2 * i + 1, 1)
        return carry

    lax.fori_loop(0, qi // 2, body, 0)
    odd = qi % 2 == 1

    @pl.when(odd)
    def _():
        step(qi - 1, 0)

    def finish(cur):
        krow = lax.broadcasted_iota(jnp.int32, (tk, 2 * tq), 0)
        qcol = lax.broadcasted_iota(jnp.int32, (tk, 2 * tq), 1)
        qcol = jnp.where(qcol >= tq, qcol - tq, qcol)
        lam = _diff_lambda(lamv_ref[...], lam_init)
        for hh in range(n_hh):
            consume(hh, qi, jnp.where(krow <= qcol, tbuf[cur][hh], NEG))
            acc = acc_ref[hh]
            ot = acc[0:HEAD_W] / acc[HEAD_W:HEAD_W + 1]
            d = ot[:, 0:tq] - lam * ot[:, tq:2 * tq]
            ms = jnp.mean(d * d, axis=0, keepdims=True)
            dn = d * lax.rsqrt(ms + RMS_EPS) * g_ref[...] * (1.0 - lam_init)
            o_ref[:, hrows[hh]] = dn.T.astype(BF16)

    @pl.when(odd)
    def _():
        finish(1)

    @pl.when(jnp.logical_not(odd))
    def _():
        finish(0)


def _attn_prompt(qt, kb, vt, slope_parts, lamv, g_col, lam_init, tq, n_hh):
    b, aw, t = qt.shape
    nh = aw // HEAD_W
    gw = n_hh * HEAD_W
    smem = pl.BlockSpec(memory_space=pltpu.SMEM)
    i = jnp.arange(tq)
    lane = jnp.arange(HEAD_W)
    feat = jnp.where(lane[None, :] < 2, (i % 256)[:, None], jnp.where(lane[None, :] < 4, (i // 256 * 256)[:, None], 0))
    pos = feat.astype(BF16)
    return pl.pallas_call(
        functools.partial(_attn_prompt_kernel, tq=tq, lam_init=lam_init),
        grid=(b, nh // n_hh, t // tq),
        in_specs=[
            smem,
            _const_spec(lamv.shape),
            _const_spec(g_col.shape),
            _const_spec(pos.shape),
            pl.BlockSpec((None, gw, tq), lambda bi, hi, qi: (bi, hi, qi)),
            pl.BlockSpec((None, t, gw), lambda bi, hi, qi: (bi, 0, hi)),
            pl.BlockSpec((None, gw, t), lambda bi, hi, qi: (bi, hi, 0)),
        ],
        out_specs=pl.BlockSpec((None, tq, gw), lambda bi, hi, qi: (bi, qi, hi)),
        out_shape=jax.ShapeDtypeStruct((b, t, aw), BF16),
        scratch_shapes=[
            pltpu.VMEM((n_hh, 2 * HEAD_W, 2 * tq), BF16),
            pltpu.VMEM((n_hh, 1, 2 * tq), F32),
            pltpu.VMEM((n_hh, HEAD_W + 16, 2 * tq), F32),
            pltpu.VMEM((n_hh, tq, 2 * tq), F32),
            pltpu.VMEM((n_hh, tq, 2 * tq), F32),
        ],
        compiler_params=_cparams(3),
        name="attn_prompt",
    )(slope_parts, lamv, g_col, pos, qt, kb, vt)


def _attn_sample_kernel(pt_ref, lamv_ref, g_ref, slope2_ref, base2_ref, newmask_ref, q2_ref, kn_ref, vn_ref,
                        ck_hbm, cv_hbm, o_ref, kbuf, vbuf, sem, s_ref,
                        *, n_pages, page_rows, n_heads, lam_init, chunk):
    b = pl.program_id(0)
    nb = pl.num_programs(0)
    slot = b % 2
    past_rows = n_pages * page_rows
    nrow = q2_ref.shape[0] // 2

    def page_copy(hbm, buf, kind, sl, p, page):
        src = hbm.at[pl.ds(pl.multiple_of(page * page_rows, page_rows), page_rows)]
        return pltpu.make_async_copy(src, buf.at[sl, pl.ds(p * page_rows, page_rows)], sem.at[kind, sl])

    def start_fetch(seq, sl):
        for p in range(n_pages):
            page = pt_ref[seq * n_pages + p]
            page_copy(ck_hbm, kbuf, 0, sl, p, page).start()
            page_copy(cv_hbm, vbuf, 1, sl, p, page).start()

    @pl.when(b == 0)
    def _():
        start_fetch(0, 0)

    @pl.when(b + 1 < nb)
    def _():
        start_fetch(b + 1, 1 - slot)

    for p in range(n_pages):
        page_copy(ck_hbm, kbuf, 0, slot, p, 0).wait()
        page_copy(cv_hbm, vbuf, 1, slot, p, 0).wait()

    pad = jnp.zeros((chunk - kn_ref.shape[0], HEAD_W), F32)
    kbuf[slot, past_rows:past_rows + chunk, :] = jnp.concatenate([kn_ref[...], pad], axis=0)
    vbuf[slot, past_rows:past_rows + chunk, :] = jnp.concatenate([vn_ref[...], pad], axis=0)

    q2 = q2_ref[...]
    slope2 = slope2_ref[...]
    n_pairs = past_rows // (2 * chunk)
    for a in range(n_pairs):
        r0 = 2 * a * chunk
        ke = kbuf[slot, r0:r0 + chunk, :].astype(BF16)
        ko = kbuf[slot, r0 + chunk:r0 + 2 * chunk, :].astype(BF16)
        s2 = _dot_nt(q2, jnp.concatenate([ke, ko], axis=1))
        tok0 = float(2 * a * (chunk // n_heads) - past_rows // n_heads)
        s_ref[:, a * chunk:(a + 1) * chunk] = s2 + base2_ref[...] + slope2 * tok0
    q = q2[0:nrow, 0:HEAD_W]
    kn = kbuf[slot, past_rows:past_rows + chunk, :].astype(BF16)
    s_new = _dot_nt(q, kn) + newmask_ref[...]

    s2 = s_ref[...]
    m2 = jnp.max(s2, axis=-1, keepdims=True)
    m = jnp.maximum(jnp.maximum(m2[0:nrow], m2[nrow:2 * nrow]), jnp.max(s_new, axis=-1, keepdims=True))
    p2 = jnp.exp2(s2 - jnp.concatenate([m, m], axis=0))
    p_new = jnp.exp2(s_new - m)
    l2 = jnp.sum(p2, axis=-1, keepdims=True)
    l = l2[0:nrow] + l2[nrow:2 * nrow] + jnp.sum(p_new, axis=-1, keepdims=True)
    p2b = p2.astype(BF16)
    acc2 = jnp.zeros((2 * nrow, 2 * HEAD_W), F32)
    for a in range(n_pairs):
        r0 = 2 * a * chunk
        ve = vbuf[slot, r0:r0 + chunk, :].astype(BF16)
        vo = vbuf[slot, r0 + chunk:r0 + 2 * chunk, :].astype(BF16)
        acc2 = acc2 + _dot(p2b[:, a * chunk:(a + 1) * chunk], jnp.concatenate([ve, vo], axis=1))
    vn = vbuf[slot, past_rows:past_rows + chunk, :].astype(BF16)
    acc = acc2[0:nrow, 0:HEAD_W] + acc2[nrow:2 * nrow, HEAD_W:2 * HEAD_W] + _dot(p_new.astype(BF16), vn)

    o = acc / l
    lam = _diff_lambda(lamv_ref[...], lam_init)
    half = nrow // 2
    d = o[0:half] - lam * o[half:nrow]
    o_ref[...] = _rms(d, g_ref[...]) * (1.0 - lam_init)


def _attn_sample(page_table, q2, knf, vnf, ckf, cvf, lamv, g_row, slope2, base2, newmask, n_heads, lam_init):
    db, nrow2, _ = q2.shape
    nrow = nrow2 // 2
    n_pages = page_table.shape[1]
    chunk = base2.shape[1]
    page_rows = PAGE * n_heads
    past_rows = n_pages * page_rows
    assert past_rows % (2 * chunk) == 0 and knf.shape[1] <= chunk
    grid_spec = pltpu.PrefetchScalarGridSpec(
        num_scalar_prefetch=1,
        grid=(db,),
        in_specs=[
            _const_spec(lamv.shape),
            _const_spec(g_row.shape),
            _const_spec(slope2.shape),
            _const_spec(base2.shape),
            _const_spec(newmask.shape),
            pl.BlockSpec((None, nrow2, 2 * HEAD_W), lambda i, pt: (i, 0, 0)),
            pl.BlockSpec((None,) + knf.shape[1:], lambda i, pt: (i, 0, 0)),
            pl.BlockSpec((None,) + vnf.shape[1:], lambda i, pt: (i, 0, 0)),
            pl.BlockSpec(memory_space=pl.ANY),
            pl.BlockSpec(memory_space=pl.ANY),
        ],
        out_specs=pl.BlockSpec((None, nrow // 2, HEAD_W), lambda i, pt: (i, 0, 0)),
        scratch_shapes=[
            pltpu.VMEM((2, past_rows + chunk, HEAD_W), F32),
            pltpu.VMEM((2, past_rows + chunk, HEAD_W), F32),
            pltpu.SemaphoreType.DMA((2, 2)),
            pltpu.VMEM((nrow2, past_rows // 2), F32),
        ],
    )
    return pl.pallas_call(
        functools.partial(_attn_sample_kernel, n_pages=n_pages, page_rows=page_rows, n_heads=n_heads,
                          lam_init=lam_init, chunk=chunk),
        grid_spec=grid_spec,
        out_shape=jax.ShapeDtypeStruct((db, nrow // 2, HEAD_W), F32),
        compiler_params=_cparams(1),
        name="attn_sample",
    )(page_table.reshape(-1), lamv, g_row, slope2, base2, newmask, q2, knf, vnf, ckf, cvf)


def _s5_prep_kernel(lre_ref, lim_ref, ldt_ref, bre_ref, bim_ref, cre_ref, cim_ref, dsk_ref,
                    wst_ref, cat_ref, kt_ref, sct_ref):
    lr = lre_ref[...]
    li = lim_ref[...]
    dt = jnp.exp(ldt_ref[...])

    def lam_bar_pow(n):
        mag = jnp.exp(n * (lr * dt))
        ang = n * (li * dt)
        return mag * jnp.cos(ang), mag * jnp.sin(ang)

    a_r, a_i = lam_bar_pow(1.0)
    den = lr * lr + li * li
    nr, ni = a_r - 1.0, a_i
    f_r = (nr * lr + ni * li) / den
    f_i = (ni * lr - nr * li) / den
    b_r, b_i = bre_ref[...], bim_ref[...]
    bb_r = f_r * b_r - f_i * b_i
    bb_i = f_r * b_i + f_i * b_r
    c_r, c_i = cre_ref[...], cim_ref[...]
    gw = b_r.shape[0]
    ns = b_r.shape[1]

    ktau = []
    for tau in range(S5_BLOCK):
        p_r, p_i = lam_bar_pow(float(tau))
        w_r = bb_r * p_r - bb_i * p_i
        w_i = bb_r * p_i + bb_i * p_r
        j = S5_BLOCK - 1 - tau
        wst_ref[j * gw:(j + 1) * gw, 0:ns] = w_r.astype(BF16)
        wst_ref[j * gw:(j + 1) * gw, ns:2 * ns] = w_i.astype(BF16)
        ktau.append(_dot_nt(w_r, c_r, lax.Precision.HIGHEST) - _dot_nt(w_i, c_i, lax.Precision.HIGHEST))
    rr = lax.broadcasted_iota(jnp.int32, (gw, gw), 0)
    cc = lax.broadcasted_iota(jnp.int32, (gw, gw), 1)
    ktau[0] = ktau[0] + jnp.where(rr == cc, dsk_ref[...], 0.0)

    for j in range(S5_BLOCK):
        p_r, p_i = lam_bar_pow(float(j + 1))
        cat_ref[j * gw:(j + 1) * gw, 0:ns] = (c_r * p_r - c_i * p_i).astype(BF16)
        cat_ref[j * gw:(j + 1) * gw, ns:2 * ns] = (-(c_r * p_i + c_i * p_r)).astype(BF16)

    zeros = jnp.zeros((gw, gw), BF16)
    for i in range(S5_BLOCK):
        for j in range(S5_BLOCK):
            blk = ktau[j - i].astype(BF16) if j >= i else zeros
            kt_ref[i * gw:(i + 1) * gw, j * gw:(j + 1) * gw] = blk

    row = lax.broadcasted_iota(jnp.int32, (8, ns), 0)
    for idx, k in enumerate((1, 2, 4)):
        p_r, p_i = lam_bar_pow(float(S5_BLOCK * k))
        sct_ref[2 * idx] = jnp.where(row >= k, p_r, 0.0)
        sct_ref[2 * idx + 1] = jnp.where(row >= k, p_i, 0.0)
    n = (S5_BLOCK * (row + 1)).astype(F32)
    mag = jnp.exp(n * (lr * dt))
    ang = n * (li * dt)
    sct_ref[6] = mag * jnp.cos(ang)
    sct_ref[7] = mag * jnp.sin(ang)


def _s5_prep(lre, lim, ldt, bre, bim, cre, cim, dsk):
    nq, gw, ns = bre.shape
    kw = S5_BLOCK * gw
    per_q = lambda *tail: pl.BlockSpec((None,) + tail, lambda q: (q,) + (0,) * len(tail))
    return pl.pallas_call(
        _s5_prep_kernel,
        grid=(nq,),
        in_specs=[per_q(1, ns), per_q(1, ns), per_q(1, ns), per_q(gw, ns), per_q(gw, ns),
                  per_q(gw, ns), per_q(gw, ns), per_q(1, gw)],
        out_specs=[per_q(kw, 2 * ns), per_q(kw, 2 * ns), per_q(kw, kw), per_q(8, 8, ns)],
        out_shape=[
            jax.ShapeDtypeStruct((nq, kw, 2 * ns), BF16),
            jax.ShapeDtypeStruct((nq, kw, 2 * ns), BF16),
            jax.ShapeDtypeStruct((nq, kw, kw), BF16),
            jax.ShapeDtypeStruct((nq, 8, 8, ns), F32),
        ],
        compiler_params=_cparams(1),
        name="s5_prep",
    )(lre, lim, ldt, bre, bim, cre, cim, dsk)


def _quarter_lanes(u_ref, q, gw, sw):
    return jnp.concatenate(
        [u_ref[:, j * sw + q * gw: j * sw + (q + 1) * gw] for j in range(S5_BLOCK)], axis=1)


def _store_quarter(y_ref, y, q, gw, sw):
    for j in range(S5_BLOCK):
        y_ref[:, j * sw + q * gw: j * sw + (q + 1) * gw] = y[:, j * gw:(j + 1) * gw].astype(y_ref.dtype)


def _s5_prompt_kernel(u_ref, wst_ref, cat_ref, kt_ref, sct_ref, y_ref, st_ref,
                      carry_ref, s_scr, hp_scr, *, rb, gw, sw, ns):
    blk = pl.program_id(1)

    @pl.when(blk == 0)
    def _():
        carry_ref[...] = jnp.zeros(carry_ref.shape, F32)

    row0 = lax.broadcasted_iota(jnp.int32, (8, ns), 0) == 0
    for q in range(S5_QUARTERS):
        uq = _quarter_lanes(u_ref, q, gw, sw)
        s_scr[...] = _dot(uq, wst_ref[q])
        tabs = [sct_ref[q, i] for i in range(8)]

        def tile(t, carry, tabs=tabs):
            cr, ci = carry
            r0 = pl.multiple_of(t * 8, 8)
            xr = s_scr[pl.ds(r0, 8), 0:ns]
            xi = s_scr[pl.ds(r0, 8), ns:2 * ns]
            for idx, k in enumerate((1, 2, 4)):
                mr, mi = tabs[2 * idx], tabs[2 * idx + 1]
                sr = pltpu.roll(xr, k, 0)
                si = pltpu.roll(xi, k, 0)
                xr, xi = xr + (mr * sr - mi * si), xi + (mr * si + mi * sr)
            crb = jnp.broadcast_to(cr, (8, ns))
            cib = jnp.broadcast_to(ci, (8, ns))
            hr = xr + (tabs[6] * crb - tabs[7] * cib)
            hi = xi + (tabs[6] * cib + tabs[7] * crb)
            hp_scr[pl.ds(r0, 8), 0:ns] = jnp.where(row0, crb, pltpu.roll(hr, 1, 0))
            hp_scr[pl.ds(r0, 8), ns:2 * ns] = jnp.where(row0, cib, pltpu.roll(hi, 1, 0))
            return hr[7:8, :], hi[7:8, :]

        cr, ci = lax.fori_loop(0, rb // 8, tile,
                               (carry_ref[q, :, 0:ns], carry_ref[q, :, ns:2 * ns]))
        carry_ref[q, :, 0:ns] = cr
        carry_ref[q, :, ns:2 * ns] = ci
        st_ref[q, :, 0:ns] = cr
        st_ref[q, :, ns:2 * ns] = ci
        y = _dot_nt(hp_scr[...].astype(BF16), cat_ref[q]) + _dot(uq, kt_ref[q])
        _store_quarter(y_ref, y, q, gw, sw)


def _s5_prompt(u4, wst, cat, kt, sct, rb):
    b, rows, lanes = u4.shape
    nq, kw, ns2 = wst.shape
    ns, gw, sw = ns2 // 2, kw // S5_BLOCK, lanes // S5_BLOCK
    return pl.pallas_call(
        functools.partial(_s5_prompt_kernel, rb=rb, gw=gw, sw=sw, ns=ns),
        grid=(b, rows // rb),
        in_specs=[
            pl.BlockSpec((None, rb, lanes), lambda bi, ri: (bi, ri, 0)),
            _const_spec(wst.shape), _const_spec(cat.shape), _const_spec(kt.shape), _const_spec(sct.shape),
        ],
        out_specs=[
            pl.BlockSpec((None, rb, lanes), lambda bi, ri: (bi, ri, 0)),
            pl.BlockSpec((None, nq, 1, ns2), lambda bi, ri: (bi, 0, 0, 0)),
        ],
        out_shape=[
            jax.ShapeDtypeStruct((b, rows, lanes), BF16),
            jax.ShapeDtypeStruct((b, nq, 1, ns2), F32),
        ],
        scratch_shapes=[
            pltpu.VMEM((nq, 1, ns2), F32),
            pltpu.VMEM((rb, ns2), F32),
            pltpu.VMEM((rb, ns2), F32),
        ],
        compiler_params=_cparams(2),
        name="s5_prompt",
    )(u4, wst, cat, kt, sct)


def _s5_sample_kernel(u_ref, h0_ref, wst_ref, cat_ref, kt_ref, sct_ref, y_ref, hout_ref, *, gw, sw, ns):
    for q in range(S5_QUARTERS):
        uq = _quarter_lanes(u_ref, q, gw, sw)
        s = _dot(uq, wst_ref[q])
        h0 = h0_ref[q]
        h0r, h0i = h0[:, 0:ns], h0[:, ns:2 * ns]
        a4r = sct_ref[q, 6][0:1, :]
        a4i = sct_ref[q, 7][0:1, :]
        hout_ref[q, :, 0:ns] = s[:, 0:ns] + (a4r * h0r - a4i * h0i)
        hout_ref[q, :, ns:2 * ns] = s[:, ns:2 * ns] + (a4r * h0i + a4i * h0r)
        y = _dot_nt(h0.astype(BF16), cat_ref[q]) + _dot(uq, kt_ref[q])
        _store_quarter(y_ref, y, q, gw, sw)


def _s5_sample(u4, h0q, wst, cat, kt, sct):
    rows, lanes = u4.shape
    nq, kw, ns2 = wst.shape
    ns, gw, sw = ns2 // 2, kw // S5_BLOCK, lanes // S5_BLOCK
    return pl.pallas_call(
        functools.partial(_s5_sample_kernel, gw=gw, sw=sw, ns=ns),
        grid=(1,),
        in_specs=[_const_spec(u4.shape), _const_spec(h0q.shape), _const_spec(wst.shape),
                  _const_spec(cat.shape), _const_spec(kt.shape), _const_spec(sct.shape)],
        out_specs=[pl.BlockSpec(u4.shape, lambda i: (0, 0)),
                   pl.BlockSpec(h0q.shape, lambda i: (0, 0, 0))],
        out_shape=[jax.ShapeDtypeStruct(u4.shape, BF16), jax.ShapeDtypeStruct(h0q.shape, F32)],
        compiler_params=_cparams(1),
        name="s5_sample",
    )(u4, h0q, wst, cat, kt, sct)


def _mem_kv_kernel(x_ref, g_ref, wk_ref, wv_ref, k_ref, v_ref, kb_ref, vb_ref):
    hn = _rms(x_ref[...], g_ref[...]).astype(BF16)
    k = _dot(hn, wk_ref[...])
    k_ref[...] = k
    kb_ref[...] = k.astype(BF16)
    v = _dot(hn, wv_ref[...])
    v_ref[...] = v
    vb_ref[...] = v.astype(BF16)


def _mem_kv(mem, g, wk_bf, wv_bf, tm):
    n, d = mem.shape
    row = pl.BlockSpec((tm, d), lambda i: (i, 0))
    return pl.pallas_call(
        _mem_kv_kernel,
        grid=(n // tm,),
        in_specs=[row, _const_spec((1, d)), _const_spec(wk_bf.shape), _const_spec(wv_bf.shape)],
        out_specs=[row, row, row, row],
        out_shape=[jax.ShapeDtypeStruct((n, d), F32), jax.ShapeDtypeStruct((n, d), F32),
                   jax.ShapeDtypeStruct((n, d), BF16), jax.ShapeDtypeStruct((n, d), BF16)],
        compiler_params=_cparams(1),
        name="mem_kv",
    )(mem, g, wk_bf, wv_bf)


def _merge_and_query(x, att, sy4_ref, sy_scr, wglu_ref, gso_ref, wout_ref, gpost_ref, gmpre_ref, wmq_ref,
                     mem_scale):
    aw = att.shape[1]
    rows, n_lb = sy4_ref.shape[0], sy_scr.shape[0]
    sw = n_lb * LANES
    for j in range(S5_BLOCK):
        for c in range(n_lb):
            blk = sy4_ref[:, j * sw + c * LANES:j * sw + (c + 1) * LANES].astype(F32)
            sy_scr[c, pl.ds(j, rows, stride=S5_BLOCK), :] = blk
    y = jax.nn.gelu(jnp.concatenate([sy_scr[c] for c in range(n_lb)], axis=1))
    y = y * jax.nn.sigmoid(_dot(y.astype(BF16), wglu_ref[...]))
    y = _rms(y, gso_ref[...])
    mix = _dot(att, wout_ref[0:aw, :]) + _dot(y.astype(BF16), wout_ref[aw:, :])
    x1 = x + _rms(mix, gpost_ref[...])
    hq = _rms(x1, gmpre_ref[...]).astype(BF16)
    qm = (_dot(hq, wmq_ref[...]) * mem_scale).astype(BF16)
    return x1, qm


def _mix_mem_kernel(x_ref, att_ref, sy_ref, mk_ref, mv_ref, wglu_ref, gso_ref, wout_ref, gpost_ref,
                    gmpre_ref, wmq_ref, wmo_ref, gmpost_ref, o_ref, oh_ref, sy_scr, *, n_heads, mem_scale):
    x1, qm = _merge_and_query(x_ref[...], att_ref[...], sy_ref, sy_scr, wglu_ref, gso_ref, wout_ref,
                              gpost_ref, gmpre_ref, wmq_ref, mem_scale)
    hd = qm.shape[1] // n_heads
    for h in range(n_heads):
        sl = slice(h * hd, (h + 1) * hd)
        s = _dot_nt(qm[:, sl], mk_ref[:, sl])
        p = jnp.exp(s - jnp.max(s, axis=-1, keepdims=True))
        l = jnp.sum(p, axis=-1, keepdims=True)
        oh_ref[:, sl] = (_dot(p.astype(BF16), mv_ref[:, sl]) / l).astype(BF16)
    mo = _dot(oh_ref[...], wmo_ref[...])
    o_ref[...] = x1 + _rms(mo, gmpost_ref[...])


def _mix_mem(x, att, sy4, mkb, mvb, w, tm, n_heads):
    b, t, d = x.shape
    aw, sw = att.shape[2], sy4.shape[2] // S5_BLOCK
    nm = mkb.shape[1]
    tok = lambda width: pl.BlockSpec((None, tm, width), lambda bi, ti: (bi, ti, 0))
    tok4 = pl.BlockSpec((None, tm // S5_BLOCK, S5_BLOCK * sw), lambda bi, ti: (bi, ti, 0))
    per_b = pl.BlockSpec((None, nm, d), lambda bi, ti: (bi, 0, 0))
    weights = [w["w_glu"], w["g_ssm_out"], w["w_out"], w["g_mix_post"], w["g_mem_pre"], w["w_mq"],
               w["w_mo"], w["g_mem_post"]]
    return pl.pallas_call(
        functools.partial(_mix_mem_kernel, n_heads=n_heads, mem_scale=(d // n_heads) ** -0.5),
        grid=(b, t // tm),
        in_specs=[tok(d), tok(aw), tok4, per_b, per_b] + [_const_spec(a.shape) for a in weights],
        out_specs=tok(d),
        out_shape=jax.ShapeDtypeStruct((b, t, d), F32),
        scratch_shapes=[pltpu.VMEM((tm, d), BF16), pltpu.VMEM((sw // LANES, tm, LANES), F32)],
        compiler_params=_cparams(2),
        name="mix_mem",
    )(x, att, sy4, mkb, mvb, *weights)


def _mix_q_kernel(x_ref, att_ref, sy_ref, wglu_ref, gso_ref, wout_ref, gpost_ref, gmpre_ref, wmq_ref,
                  x1_ref, qm_ref, sy_scr, *, mem_scale):
    x1, qm = _merge_and_query(x_ref[...], att_ref[...], sy_ref, sy_scr, wglu_ref, gso_ref, wout_ref,
                              gpost_ref, gmpre_ref, wmq_ref, mem_scale)
    x1_ref[...] = x1
    qm_ref[...] = qm


def _mix_q(x, att, sy4, w, n_heads):
    n, d = x.shape
    weights = [w["w_glu"], w["g_ssm_out"], w["w_out"], w["g_mix_post"], w["g_mem_pre"], w["w_mq"]]
    ins = [x, att, sy4] + weights
    return pl.pallas_call(
        functools.partial(_mix_q_kernel, mem_scale=(d // n_heads) ** -0.5),
        grid=(1,),
        in_specs=[_const_spec(a.shape) for a in ins],
        out_specs=[pl.BlockSpec((n, d), lambda i: (0, 0)), pl.BlockSpec((n, d), lambda i: (0, 0))],
        out_shape=[jax.ShapeDtypeStruct((n, d), F32), jax.ShapeDtypeStruct((n, d), BF16)],
        scratch_shapes=[pltpu.VMEM((sy4.shape[1] // S5_BLOCK // LANES, n, LANES), F32)],
        compiler_params=_cparams(1),
        name="mix_q",
    )(*ins)


def _mem_sample_kernel(mask_ref, q_ref, ck_hbm, cv_hbm, o_ref, kbuf, vbuf, sem, *, n_mem, n_heads):
    b = pl.program_id(0)
    nb = pl.num_programs(0)
    slot = b % 2
    n_halves = kbuf.shape[1]

    def copies(seq, sl):
        out = []
        for kind, (hbm, buf) in enumerate(((ck_hbm, kbuf), (cv_hbm, vbuf))):
            for lb in range(n_halves):
                src = hbm.at[seq, :, :, pl.ds(lb * LANES, LANES)]
                dst = buf.at[sl, lb].reshape(n_mem, n_heads, LANES)
                out.append(pltpu.make_async_copy(src, dst, sem.at[kind, sl]))
        return out

    @pl.when(b == 0)
    def _():
        for cp in copies(0, 0):
            cp.start()

    @pl.when(b + 1 < nb)
    def _():
        for cp in copies(b + 1, 1 - slot):
            cp.start()

    for cp in copies(b, slot):
        cp.wait()

    kcat = jnp.concatenate([kbuf[slot, lb] for lb in range(n_halves)], axis=1).astype(BF16)
    vcat = jnp.concatenate([vbuf[slot, lb] for lb in range(n_halves)], axis=1).astype(BF16)
    s = _dot_nt(q_ref[...], kcat) + mask_ref[...]
    p = jnp.exp(s - jnp.max(s, axis=-1, keepdims=True))
    l = jnp.sum(p, axis=-1, keepdims=True)
    o_ref[...] = _dot(p.astype(BF16), vcat) / l


def _mem_sample(qh, ck, cv, mask):
    db, nrow, hd = qh.shape
    _, n_mem, n_heads, _ = ck.shape
    grid_spec = pltpu.PrefetchScalarGridSpec(
        num_scalar_prefetch=0,
        grid=(db,),
        in_specs=[
            _const_spec(mask.shape),
            pl.BlockSpec((None, nrow, hd), lambda i: (i, 0, 0)),
            pl.BlockSpec(memory_space=pl.ANY),
            pl.BlockSpec(memory_space=pl.ANY),
        ],
        out_specs=pl.BlockSpec((None, nrow, hd), lambda i: (i, 0, 0)),
        scratch_shapes=[
            pltpu.VMEM((2, hd // LANES, n_mem * n_heads, LANES), F32),
            pltpu.VMEM((2, hd // LANES, n_mem * n_heads, LANES), F32),
            pltpu.SemaphoreType.DMA((2, 2)),
        ],
    )
    return pl.pallas_call(
        functools.partial(_mem_sample_kernel, n_mem=n_mem, n_heads=n_heads),
        grid_spec=grid_spec,
        out_shape=jax.ShapeDtypeStruct((db, nrow, hd), F32),
        compiler_params=_cparams(1),
        name="mem_sample",
    )(mask, qh, ck, cv)


def _mem_out_kernel(x1_ref, o_ref, wmo_ref, g_ref, y_ref):
    mo = _dot(o_ref[...], wmo_ref[...])
    y_ref[...] = x1_ref[...] + _rms(mo, g_ref[...])


def _mem_out(x1, o, wmo, g):
    n, d = x1.shape
    ins = [x1, o, wmo, g]
    return pl.pallas_call(
        _mem_out_kernel,
        grid=(1,),
        in_specs=[_const_spec(a.shape) for a in ins],
        out_specs=pl.BlockSpec((n, d), lambda i: (0, 0)),
        out_shape=jax.ShapeDtypeStruct((n, d), F32),
        compiler_params=_cparams(1),
        name="mem_out",
    )(*ins)


def _ffn_kernel(x_ref, gpre_ref, wg_ref, wu_ref, wd_ref, gpost_ref, o_ref, *, n_chunks):
    x = x_ref[...]
    hn = _rms(x, gpre_ref[...]).astype(BF16)
    ff = wg_ref.shape[1]
    cw = ff // n_chunks
    acc = jnp.zeros(x.shape, F32)
    for c in range(n_chunks):
        sl = slice(c * cw, (c + 1) * cw)
        a = jax.nn.silu(_dot(hn, wg_ref[:, sl])) * _dot(hn, wu_ref[:, sl])
        acc = acc + _dot(a.astype(BF16), wd_ref[sl, :])
    o_ref[...] = x + _rms(acc, gpost_ref[...])


def _ffn(x, gpre, wg, wu, wd, gpost, tm, n_chunks):
    n, d = x.shape
    row = pl.BlockSpec((tm, d), lambda i: (i, 0))
    weights = [gpre, wg, wu, wd, gpost]
    return pl.pallas_call(
        functools.partial(_ffn_kernel, n_chunks=n_chunks),
        grid=(n // tm,),
        in_specs=[row] + [_const_spec(a.shape) for a in weights],
        out_specs=row,
        out_shape=jax.ShapeDtypeStruct((n, d), F32),
        compiler_params=_cparams(1),
        name="ffn",
    )(x, *weights)


def _s5_param_layout(lam_re, lam_im, log_dt, b_re, b_im, c_re, c_im, d_skip):
    g, p, hc = b_re.shape
    nq = S5_QUARTERS
    gq = g // nq
    rowv = lambda a: a.reshape(nq, 1, gq * p)
    ldt = jnp.broadcast_to(log_dt[:, None], (g, p))
    eye = jnp.eye(gq, dtype=F32)

    def embed_b(b):
        bq = b.reshape(nq, gq, p, hc)
        return jnp.einsum("qgph,gk->qghkp", bq, eye).reshape(nq, gq * hc, gq * p)

    def embed_c(c):
        cq = c.reshape(nq, gq, hc, p)
        return jnp.einsum("qghp,gk->qghkp", cq, eye).reshape(nq, gq * hc, gq * p)

    return (rowv(lam_re), rowv(lam_im), rowv(ldt), embed_b(b_re), embed_b(b_im),
            embed_c(c_re), embed_c(c_im), d_skip.reshape(nq, 1, gq * hc))


def _state_to_quarters(re, im):
    n, g, p = re.shape
    gq = g // S5_QUARTERS
    f = lambda a: a.reshape(n, S5_QUARTERS, gq * p).transpose(1, 0, 2)
    return jnp.concatenate([f(re), f(im)], axis=-1)


def _state_from_quarters(st, g, p):
    n = st.shape[0]
    ns = st.shape[-1] // 2
    return st[..., :ns].reshape(n, g, p), st[..., ns:].reshape(n, g, p)


def kernel(x_prompt, x_sample, mem_prompt, cache_k, cache_v, page_table, state_ssm_re, state_ssm_im,
           cache_mem_k, cache_mem_v, g_mix_pre, g_mix_post, w_in, lam_q1, lam_k1, lam_q2, lam_k2, g_subln,
           ssm_lam_re, ssm_lam_im, ssm_log_dt, ssm_b_re, ssm_b_im, ssm_c_re, ssm_c_im, ssm_d, w_glu, g_ssm_out,
           w_out, g_mem_pre, g_mem_post, g_mem_kv, w_mq, w_mk, w_mv, w_mo, g_ffn_pre, g_ffn_post,
           w_gate, w_up, w_down):
    depth = w_in.shape[0]
    b, t, d = x_prompt.shape
    db, dt_, _ = x_sample.shape
    n_mem = mem_prompt.shape[1]
    n_mem_heads = cache_mem_k.shape[3]
    n_heads = cache_k.shape[3]
    aw = n_heads * HEAD_W
    g, p = ssm_lam_re.shape[1], ssm_lam_re.shape[2]
    sw = g * SSM_GROUP
    n_phys = cache_k.shape[1]
    assert dt_ == S5_BLOCK and t % (S5_BLOCK * 256) == 0

    slopes = LOG2E * 2.0 ** (-8.0 * jnp.arange(1, n_heads + 1, dtype=F32) / n_heads)
    slope_hi = slopes.astype(BF16).astype(F32)
    slope_parts = jnp.stack([slopes, slope_hi, (slopes - slope_hi).astype(BF16).astype(F32)])
    row = lambda a: a.reshape(1, -1).astype(F32)
    bf = lambda a: a.astype(BF16)

    branch_lane = jnp.arange(HEAD_W) // HEAD_DIM
    nrow = 2 * n_heads * dt_
    chunk = 256
    r = jnp.arange(nrow)
    r_head, r_tok = (r // dt_) % n_heads, r % dt_
    r_slope = slopes[r_head][:, None]
    c = jnp.arange(chunk)
    c_tok, c_head = c // n_heads, c % n_heads
    own_head = c_head[None, :] == r_head[:, None]
    base = r_slope * c_tok[None, :].astype(F32) + jnp.where(own_head, 0.0, NEG)
    base2 = jnp.concatenate([base, base + r_slope * float(chunk // n_heads)], axis=0)
    slope2 = jnp.concatenate([r_slope, r_slope], axis=0)
    new_ok = own_head & (c[None, :] < dt_ * n_heads) & (c_tok[None, :] <= r_tok[:, None])
    newmask = jnp.where(new_ok, r_slope * c_tok[None, :].astype(F32), NEG)
    mem_row_head = jnp.arange(n_mem_heads * dt_) // dt_
    mem_col_head = jnp.arange(n_mem * n_mem_heads) % n_mem_heads
    mem_mask = jnp.where(mem_col_head[None, :] == mem_row_head[:, None], 0.0, NEG).astype(F32)

    y_p = x_prompt
    y_s = x_sample.reshape(db * dt_, d)
    outs = {k: [] for k in ("kp", "vp", "ks", "vs", "rp", "ip", "rs", "is", "mkp", "mvp")}
    for l in range(depth):
        lam_init = 0.8 - 0.6 * math.exp(-0.3 * l)
        lamv = jnp.stack([lam_q1[l], lam_k1[l], lam_q2[l], lam_k2[l]]).astype(F32)
        gsub = row(g_subln[l])
        w_in_bf = bf(w_in[l])
        w = {
            "w_glu": bf(w_glu[l]), "g_ssm_out": row(g_ssm_out[l]), "w_out": bf(w_out[l]),
            "g_mix_post": row(g_mix_post[l]), "g_mem_pre": row(g_mem_pre[l]), "w_mq": bf(w_mq[l]),
            "w_mo": bf(w_mo[l]), "g_mem_post": row(g_mem_post[l]),
        }

        qp, kp, vp, kpb, vpb, up = _proj(y_p.reshape(b * t, d), row(g_mix_pre[l]), w_in_bf, aw, 512)
        qs, ks, vs, _, _, us = _proj(y_s, row(g_mix_pre[l]), w_in_bf, aw, 512)

        att_p = _attn_prompt(jnp.swapaxes(qp.reshape(b, t, aw), 1, 2), kpb.reshape(b, t, aw),
                             jnp.swapaxes(vpb.reshape(b, t, aw), 1, 2),
                             slope_parts, lamv, g_subln[l].reshape(HEAD_W, 1).astype(F32), lam_init, 512, 1)
        qh = qs.reshape(db, dt_, n_heads, HEAD_W).transpose(0, 2, 1, 3)
        zq = jnp.zeros((), BF16)
        q32 = jnp.stack([jnp.where(branch_lane == 0, qh, zq), jnp.where(branch_lane == 1, qh, zq)], axis=1)
        q32 = q32.reshape(db, nrow, HEAD_W)
        zq32 = jnp.zeros_like(q32)
        q2 = jnp.concatenate([jnp.concatenate([q32, zq32], axis=2), jnp.concatenate([zq32, q32], axis=2)], axis=1)
        att_s16 = _attn_sample(page_table, q2, ks.reshape(db, dt_ * n_heads, HEAD_W),
                               vs.reshape(db, dt_ * n_heads, HEAD_W),
                               cache_k[l].reshape(n_phys * PAGE * n_heads, HEAD_W),
                               cache_v[l].reshape(n_phys * PAGE * n_heads, HEAD_W),
                               lamv, gsub, slope2, base2, newmask, n_heads, lam_init)
        att_s = bf(att_s16.reshape(db, n_heads, dt_, HEAD_W).transpose(0, 2, 1, 3).reshape(db * dt_, aw))

        tabs = _s5_prep(*_s5_param_layout(ssm_lam_re[l], ssm_lam_im[l], ssm_log_dt[l], ssm_b_re[l],
                                          ssm_b_im[l], ssm_c_re[l], ssm_c_im[l], ssm_d[l]))
        syp, stp = _s5_prompt(up.reshape(b, t // S5_BLOCK, S5_BLOCK * sw), *tabs, 256)
        rp, ip = _state_from_quarters(stp.reshape(b, S5_QUARTERS, -1), g, p)
        sys_, sts = _s5_sample(us, _state_to_quarters(state_ssm_re[l], state_ssm_im[l]), *tabs)
        rs, is_ = _state_from_quarters(sts.transpose(1, 0, 2), g, p)

        mkp, mvp, mkb, mvb = _mem_kv(mem_prompt.reshape(b * n_mem, d), row(g_mem_kv[l]),
                                     bf(w_mk[l]), bf(w_mv[l]), 512)
        y_p = _mix_mem(y_p, att_p, syp, mkb.reshape(b, n_mem, d), mvb.reshape(b, n_mem, d), w, 512, n_mem_heads)
        x1s, qms = _mix_q(y_s, att_s, sys_, w, n_mem_heads)
        mhd = d // n_mem_heads
        qmh = qms.reshape(db, dt_, n_mem_heads, mhd).transpose(0, 2, 1, 3).reshape(db, n_mem_heads * dt_, mhd)
        omh = _mem_sample(qmh, cache_mem_k[l], cache_mem_v[l], mem_mask)
        om = bf(omh.reshape(db, n_mem_heads, dt_, mhd).transpose(0, 2, 1, 3).reshape(db * dt_, d))
        y_s = _mem_out(x1s, om, w["w_mo"], w["g_mem_post"])

        ffw = (row(g_ffn_pre[l]), bf(w_gate[l]), bf(w_up[l]), bf(w_down[l]), row(g_ffn_post[l]))
        y_p = _ffn(y_p.reshape(b * t, d), *ffw, 512, 2).reshape(b, t, d)
        y_s = _ffn(y_s, *ffw, 512, 2)

        outs["kp"].append(kp.reshape(b, t, n_heads, HEAD_W))
        outs["vp"].append(vp.reshape(b, t, n_heads, HEAD_W))
        outs["ks"].append(ks.reshape(db, dt_, n_heads, HEAD_W))
        outs["vs"].append(vs.reshape(db, dt_, n_heads, HEAD_W))
        outs["rp"].append(rp); outs["ip"].append(ip); outs["rs"].append(rs); outs["is"].append(is_)
        outs["mkp"].append(mkp.reshape(b, n_mem, n_mem_heads, d // n_mem_heads))
        outs["mvp"].append(mvp.reshape(b, n_mem, n_mem_heads, d // n_mem_heads))

    st = lambda k: jnp.stack(outs[k], 0)
    return (y_p, y_s.reshape(db, dt_, d), st("kp"), st("vp"), st("ks"), st("vs"),
            st("rp"), st("ip"), st("rs"), st("is"), st("mkp"), st("mvp"))
```

```python
import functools
import math

import jax
import jax.numpy as jnp
from jax import lax
from jax.experimental import pallas as pl
from jax.experimental.pallas import tpu as pltpu

F32 = jnp.float32
BF16 = jnp.bfloat16

RMS_EPS = 1e-6
HEAD_DIM = 64
HEAD_W = 2 * HEAD_DIM
SSM_GROUP = 16
SSM_STATE = 64
S5_BLOCK = 4
S5_QUARTERS = 4
PAGE = 128
LANES = 128
NEG = -1e30
LOG2E = 1.4426950408889634
VMEM_LIMIT_BYTES = 56 * 1024 * 1024


def _cparams(n_grid_dims):
    return pltpu.CompilerParams(
        dimension_semantics=("arbitrary",) * n_grid_dims,
        vmem_limit_bytes=VMEM_LIMIT_BYTES)


def _const_spec(shape):
    nd = len(shape)
    return pl.BlockSpec(shape, lambda *_: (0,) * nd, pipeline_mode=pl.Buffered(1))


def _rms(x, g):
    ms = jnp.mean(x * x, axis=-1, keepdims=True)
    return x * lax.rsqrt(ms + RMS_EPS) * g


def _dot(a, b):
    return jnp.dot(a, b, preferred_element_type=F32)


def _dot_nt(a, b, precision=None):
    return lax.dot_general(a, b, (((1,), (1,)), ((), ())),
                           preferred_element_type=F32, precision=precision)


def _proj_kernel(x_ref, g_ref, w_ref, q_ref, k_ref, v_ref, kb_ref, vb_ref, u_ref, u_scr, *, aw, tm):
    hn = _rms(x_ref[...], g_ref[...]).astype(BF16)
    q = _dot(hn, w_ref[:, 0:aw])
    q_ref[...] = (q * (HEAD_DIM ** -0.5 * LOG2E)).astype(BF16)
    nh = aw // HEAD_W
    k = _dot(hn, w_ref[:, aw:2 * aw])
    kb_ref[...] = k.astype(BF16)
    v = _dot(hn, w_ref[:, 2 * aw:3 * aw])
    vb_ref[...] = v.astype(BF16)
    for h in range(nh):
        k_ref[pl.ds(h, tm, stride=nh), :] = k[:, h * HEAD_W:(h + 1) * HEAD_W]
        v_ref[pl.ds(h, tm, stride=nh), :] = v[:, h * HEAD_W:(h + 1) * HEAD_W]
    u = _dot(hn, w_ref[:, 3 * aw:])
    n_lb = u_scr.shape[0]
    sw = n_lb * LANES
    for c in range(n_lb):
        u_scr[c] = u[:, c * LANES:(c + 1) * LANES]
    for j in range(S5_BLOCK):
        for c in range(n_lb):
            rows = u_scr[c, pl.ds(j, tm // S5_BLOCK, stride=S5_BLOCK), :]
            u_ref[:, j * sw + c * LANES:j * sw + (c + 1) * LANES] = rows.astype(BF16)


def _proj(x, g, w_bf, aw, tm):
    n, d = x.shape
    sw = w_bf.shape[1] - 3 * aw
    nh = aw // HEAD_W
    row = lambda width: pl.BlockSpec((tm, width), lambda i: (i, 0))
    flat = pl.BlockSpec((tm * nh, HEAD_W), lambda i: (i, 0))
    return pl.pallas_call(
        functools.partial(_proj_kernel, aw=aw, tm=tm),
        grid=(n // tm,),
        in_specs=[row(d), _const_spec((1, d)), _const_spec(w_bf.shape)],
        out_specs=[row(aw), flat, flat, row(aw), row(aw),
                   pl.BlockSpec((tm // S5_BLOCK, S5_BLOCK * sw), lambda i: (i, 0))],
        out_shape=[
            jax.ShapeDtypeStruct((n, aw), BF16),
            jax.ShapeDtypeStruct((n * nh, HEAD_W), F32),
            jax.ShapeDtypeStruct((n * nh, HEAD_W), F32),
            jax.ShapeDtypeStruct((n, aw), BF16),
            jax.ShapeDtypeStruct((n, aw), BF16),
            jax.ShapeDtypeStruct((n // S5_BLOCK, S5_BLOCK * sw), BF16),
        ],
        scratch_shapes=[pltpu.VMEM((sw // LANES, tm, LANES), F32)],
        compiler_params=_cparams(1),
        name="proj",
    )(x, g, w_bf)


def _diff_lambda(lamv, lam_init):
    d1 = jnp.sum(lamv[0:1] * lamv[1:2], axis=-1, keepdims=True)
    d2 = jnp.sum(lamv[2:3] * lamv[3:4], axis=-1, keepdims=True)
    return jnp.exp(d1) - jnp.exp(d2) + lam_init


def _attn_prompt_body(sl_ref, lamv_ref, g_ref, pos_ref, qt_ref, k_ref, vt_ref, o_ref,
                        qq_ref, m_ref, acc_ref, t0_ref, t1_ref, *, tq, lam_init):
    hp = pl.program_id(1)
    qi = pl.program_id(2)
    tk = tq
    n_hh = qq_ref.shape[0]
    heads = [hp * n_hh + hh for hh in range(n_hh)]
    hrows = [slice(hh * HEAD_W, (hh + 1) * HEAD_W) for hh in range(n_hh)]

    arow = lax.broadcasted_iota(jnp.int32, (HEAD_W, 2 * tq), 0)
    for hh in range(n_hh):
        qt = qt_ref[hrows[hh], :].astype(F32)
        frow = lax.broadcasted_iota(jnp.int32, qt.shape, 0)
        qq_ref[hh, 0:HEAD_W, 0:tq] = jnp.where(frow < HEAD_DIM, qt, 0.0).astype(BF16)
        qq_ref[hh, 0:HEAD_W, tq:2 * tq] = jnp.where(frow >= HEAD_DIM, qt, 0.0).astype(BF16)
        slope_parts = jnp.where((arow & 1) == 0, sl_ref[1, heads[hh]], sl_ref[2, heads[hh]])
        qq_ref[hh, HEAD_W:2 * HEAD_W, :] = jnp.where(arow < 4, slope_parts, 0.0).astype(BF16)
    m_ref[...] = jnp.full(m_ref.shape, NEG, F32)
    acc_ref[...] = jnp.zeros(acc_ref.shape, F32)
    ones = jnp.ones((acc_ref.shape[1] - HEAD_W, tk), BF16)

    def scores(hh, kj):
        k0 = pl.multiple_of(kj * tk, tk)
        ka = jnp.concatenate([k_ref[pl.ds(k0, tk), hrows[hh]], pos_ref[...]], axis=1)
        return _dot(ka, qq_ref[hh])

    def consume(hh, kj, t):
        k0 = pl.multiple_of(kj * tk, tk)
        vta = jnp.concatenate([vt_ref[hrows[hh], pl.ds(k0, tk)], ones], axis=0)
        off = sl_ref[0, heads[hh]] * jnp.full((1, 2 * tq), (kj - qi) * tk, jnp.int32).astype(F32)
        m_old = m_ref[hh]
        m_new = jnp.maximum(m_old, jnp.max(t, axis=0, keepdims=True) + off)
        alpha = jnp.exp2(m_old - m_new)
        p = jnp.exp2(t - (m_new - off)).astype(BF16)
        acc_ref[hh] = alpha * acc_ref[hh] + _dot(vta, p)
        m_ref[hh] = m_new

    tbuf = (t0_ref, t1_ref)

    def step(kj, cur):
        for hh in range(n_hh):
            tbuf[1 - cur][hh] = scores(hh, kj + 1)
        for hh in range(n_hh):
            consume(hh, kj, tbuf[cur][hh])

    for hh in range(n_hh):
        t0_ref[hh] = scores(hh, 0)

    def body(i, carry):
        step(2 * i, 0)
        step(2 * i + 1, 1)
        return carry

    lax.fori_loop(0, qi // 2, body, 0)
    odd = qi % 2 == 1

    @pl.when(odd)
    def _():
        step(qi - 1, 0)

    def finish(cur):
        krow = lax.broadcasted_iota(jnp.int32, (tk, 2 * tq), 0)
        qcol = lax.broadcasted_iota(jnp.int32, (tk, 2 * tq), 1)
        qcol = jnp.where(qcol >= tq, qcol - tq, qcol)
        lam = _diff_lambda(lamv_ref[...], lam_init)
        for hh in range(n_hh):
            consume(hh, qi, jnp.where(krow <= qcol, tbuf[cur][hh], NEG))
            acc = acc_ref[hh]
            ot = acc[0:HEAD_W] / acc[HEAD_W:HEAD_W + 1]
            d = ot[:, 0:tq] - lam * ot[:, tq:2 * tq]
            ms = jnp.mean(d * d, axis=0, keepdims=True)
            dn = d * lax.rsqrt(ms + RMS_EPS) * g_ref[...] * (1.0 - lam_init)
            o_ref[:, hrows[hh]] = dn.T.astype(BF16)

    @pl.when(odd)
    def _():
        finish(1)

    @pl.when(jnp.logical_not(odd))
    def _():
        finish(0)


def _page_copy(hbm, buf, sem, kind, sl, p, page, page_rows):
    src = hbm.at[pl.ds(pl.multiple_of(page * page_rows, page_rows), page_rows)]
    return pltpu.make_async_copy(src, buf.at[sl, pl.ds(p * page_rows, page_rows)], sem.at[kind, sl])


def _attn_sample_prefetch(pt_ref, ck_hbm, cv_hbm, kbuf, vbuf, sem, seq, n_seq, *, n_pages, page_rows):
    def start_fetch(s, sl):
        for p in range(n_pages):
            page = pt_ref[s * n_pages + p]
            _page_copy(ck_hbm, kbuf, sem, 0, sl, p, page, page_rows).start()
            _page_copy(cv_hbm, vbuf, sem, 1, sl, p, page, page_rows).start()

    @pl.when(seq == 0)
    def _():
        start_fetch(0, 0)

    @pl.when(seq + 1 < n_seq)
    def _():
        start_fetch(seq + 1, 1 - seq % 2)


def _attn_sample_compute(lamv_ref, g_ref, slope2_ref, base2_ref, newmask_ref, q2_ref, kn_ref, vn_ref,
                         ck_hbm, cv_hbm, o_ref, kbuf, vbuf, sem, s_ref, seq,
                         *, n_pages, page_rows, n_heads, lam_init, chunk):
    slot = seq % 2
    past_rows = n_pages * page_rows
    nrow = q2_ref.shape[0] // 2

    for p in range(n_pages):
        _page_copy(ck_hbm, kbuf, sem, 0, slot, p, 0, page_rows).wait()
        _page_copy(cv_hbm, vbuf, sem, 1, slot, p, 0, page_rows).wait()

    pad = jnp.zeros((chunk - kn_ref.shape[0], HEAD_W), F32)
    kbuf[slot, past_rows:past_rows + chunk, :] = jnp.concatenate([kn_ref[...], pad], axis=0)
    vbuf[slot, past_rows:past_rows + chunk, :] = jnp.concatenate([vn_ref[...], pad], axis=0)

    q2 = q2_ref[...]
    slope2 = slope2_ref[...]
    n_pairs = past_rows // (2 * chunk)
    for a in range(n_pairs):
        r0 = 2 * a * chunk
        ke = kbuf[slot, r0:r0 + chunk, :].astype(BF16)
        ko = kbuf[slot, r0 + chunk:r0 + 2 * chunk, :].astype(BF16)
        s2 = _dot_nt(q2, jnp.concatenate([ke, ko], axis=1))
        tok0 = float(2 * a * (chunk // n_heads) - past_rows // n_heads)
        s_ref[:, a * chunk:(a + 1) * chunk] = s2 + base2_ref[...] + slope2 * tok0
    q = q2[0:nrow, 0:HEAD_W]
    kn = kbuf[slot, past_rows:past_rows + chunk, :].astype(BF16)
    s_new = _dot_nt(q, kn) + newmask_ref[...]

    s2 = s_ref[...]
    m2 = jnp.max(s2, axis=-1, keepdims=True)
    m = jnp.maximum(jnp.maximum(m2[0:nrow], m2[nrow:2 * nrow]), jnp.max(s_new, axis=-1, keepdims=True))
    p2 = jnp.exp2(s2 - jnp.concatenate([m, m], axis=0))
    p_new = jnp.exp2(s_new - m)
    l2 = jnp.sum(p2, axis=-1, keepdims=True)
    l = l2[0:nrow] + l2[nrow:2 * nrow] + jnp.sum(p_new, axis=-1, keepdims=True)
    p2b = p2.astype(BF16)
    acc2 = jnp.zeros((2 * nrow, 2 * HEAD_W), F32)
    for a in range(n_pairs):
        r0 = 2 * a * chunk
        ve = vbuf[slot, r0:r0 + chunk, :].astype(BF16)
        vo = vbuf[slot, r0 + chunk:r0 + 2 * chunk, :].astype(BF16)
        acc2 = acc2 + _dot(p2b[:, a * chunk:(a + 1) * chunk], jnp.concatenate([ve, vo], axis=1))
    vn = vbuf[slot, past_rows:past_rows + chunk, :].astype(BF16)
    acc = acc2[0:nrow, 0:HEAD_W] + acc2[nrow:2 * nrow, HEAD_W:2 * HEAD_W] + _dot(p_new.astype(BF16), vn)

    o = acc / l
    lam = _diff_lambda(lamv_ref[...], lam_init)
    half = nrow // 2
    d = o[0:half] - lam * o[half:nrow]
    o_ref[...] = _rms(d, g_ref[...]) * (1.0 - lam_init)


def _attn_kernel(pt_ref, sl_ref, lamv_ref, gcol_ref, pos_ref, qt_ref, k_ref, vt_ref,
                 grow_ref, slope2_ref, base2_ref, newmask_ref, q2_ref, kn_ref, vn_ref, ck_hbm, cv_hbm,
                 op_ref, os_ref,
                 qq_ref, m_ref, acc_ref, t0_ref, t1_ref, kbuf, vbuf, sem, s_ref,
                 *, tq, lam_init, n_pages, page_rows, n_heads, chunk):
    seq = (pl.program_id(0) * pl.num_programs(1) + pl.program_id(1)) * pl.num_programs(2) + pl.program_id(2)
    n_seq = pl.num_programs(0) * pl.num_programs(1) * pl.num_programs(2)
    _attn_sample_prefetch(pt_ref, ck_hbm, cv_hbm, kbuf, vbuf, sem, seq, n_seq,
                          n_pages=n_pages, page_rows=page_rows)
    _attn_prompt_body(sl_ref, lamv_ref, gcol_ref, pos_ref, qt_ref, k_ref, vt_ref, op_ref,
                      qq_ref, m_ref, acc_ref, t0_ref, t1_ref, tq=tq, lam_init=lam_init)
    _attn_sample_compute(lamv_ref, grow_ref, slope2_ref, base2_ref, newmask_ref, q2_ref, kn_ref, vn_ref,
                         ck_hbm, cv_hbm, os_ref, kbuf, vbuf, sem, s_ref, seq,
                         n_pages=n_pages, page_rows=page_rows, n_heads=n_heads, lam_init=lam_init, chunk=chunk)


def _attn(qt, kb, vt, slope_parts, lamv, g_col, page_table, q2, knf, vnf, ckf, cvf, g_row, slope2, base2,
          newmask, lam_init, tq):
    b, aw, t = qt.shape
    nh = aw // HEAD_W
    nq = t // tq
    db, nrow2, _ = q2.shape
    nrow = nrow2 // 2
    n_pages = page_table.shape[1]
    chunk = base2.shape[1]
    page_rows = PAGE * nh
    past_rows = n_pages * page_rows
    assert past_rows % (2 * chunk) == 0 and knf.shape[1] <= chunk
    assert db == b * nh * nq, "one decode sequence per prompt grid step"
    i = jnp.arange(tq)
    lane = jnp.arange(HEAD_W)
    feat = jnp.where(lane[None, :] < 2, (i % 256)[:, None], jnp.where(lane[None, :] < 4, (i // 256 * 256)[:, None], 0))
    pos = feat.astype(BF16)
    seq_of = lambda bi, hi, qi: (bi * nh + hi) * nq + qi
    per_seq = lambda a: pl.BlockSpec((None,) + a.shape[1:], lambda bi, hi, qi, pt: (seq_of(bi, hi, qi), 0, 0))
    const = lambda a: pl.BlockSpec(a.shape, lambda bi, hi, qi, pt: (0,) * a.ndim, pipeline_mode=pl.Buffered(1))
    grid_spec = pltpu.PrefetchScalarGridSpec(
        num_scalar_prefetch=1,
        grid=(b, nh, nq),
        in_specs=[
            pl.BlockSpec(memory_space=pltpu.SMEM),
            const(lamv), const(g_col), const(pos),
            pl.BlockSpec((None, HEAD_W, tq), lambda bi, hi, qi, pt: (bi, hi, qi)),
            pl.BlockSpec((None, t, HEAD_W), lambda bi, hi, qi, pt: (bi, 0, hi)),
            pl.BlockSpec((None, HEAD_W, t), lambda bi, hi, qi, pt: (bi, hi, 0)),
            const(g_row), const(slope2), const(base2), const(newmask),
            per_seq(q2), per_seq(knf), per_seq(vnf),
            pl.BlockSpec(memory_space=pl.ANY),
            pl.BlockSpec(memory_space=pl.ANY),
        ],
        out_specs=[
            pl.BlockSpec((None, tq, HEAD_W), lambda bi, hi, qi, pt: (bi, qi, hi)),
            pl.BlockSpec((None, nrow // 2, HEAD_W), lambda bi, hi, qi, pt: (seq_of(bi, hi, qi), 0, 0)),
        ],
        scratch_shapes=[
            pltpu.VMEM((1, 2 * HEAD_W, 2 * tq), BF16),
            pltpu.VMEM((1, 1, 2 * tq), F32),
            pltpu.VMEM((1, HEAD_W + 16, 2 * tq), F32),
            pltpu.VMEM((1, tq, 2 * tq), F32),
            pltpu.VMEM((1, tq, 2 * tq), F32),
            pltpu.VMEM((2, past_rows + chunk, HEAD_W), F32),
            pltpu.VMEM((2, past_rows + chunk, HEAD_W), F32),
            pltpu.SemaphoreType.DMA((2, 2)),
            pltpu.VMEM((nrow2, past_rows // 2), F32),
        ],
    )
    return pl.pallas_call(
        functools.partial(_attn_kernel, tq=tq, lam_init=lam_init, n_pages=n_pages, page_rows=page_rows,
                          n_heads=nh, chunk=chunk),
        grid_spec=grid_spec,
        out_shape=[jax.ShapeDtypeStruct((b, t, aw), BF16),
                   jax.ShapeDtypeStruct((db, nrow // 2, HEAD_W), F32)],
        compiler_params=_cparams(3),
        name="attn",
    )(page_table.reshape(-1), slope_parts, lamv, g_col, pos, qt, kb, vt,
      g_row, slope2, base2, newmask, q2, knf, vnf, ckf, cvf)


def _s5_prep_kernel(lre_ref, lim_ref, ldt_ref, bre_ref, bim_ref, cre_ref, cim_ref, dsk_ref,
                    wst_ref, cat_ref, kt_ref, sct_ref):
    lr = lre_ref[...]
    li = lim_ref[...]
    dt = jnp.exp(ldt_ref[...])

    def lam_bar_pow(n):
        mag = jnp.exp(n * (lr * dt))
        ang = n * (li * dt)
        return mag * jnp.cos(ang), mag * jnp.sin(ang)

    a_r, a_i = lam_bar_pow(1.0)
    den = lr * lr + li * li
    nr, ni = a_r - 1.0, a_i
    f_r = (nr * lr + ni * li) / den
    f_i = (ni * lr - nr * li) / den
    b_r, b_i = bre_ref[...], bim_ref[...]
    bb_r = f_r * b_r - f_i * b_i
    bb_i = f_r * b_i + f_i * b_r
    c_r, c_i = cre_ref[...], cim_ref[...]
    gw = b_r.shape[0]
    ns = b_r.shape[1]

    ktau = []
    for tau in range(S5_BLOCK):
        p_r, p_i = lam_bar_pow(float(tau))
        w_r = bb_r * p_r - bb_i * p_i
        w_i = bb_r * p_i + bb_i * p_r
        j = S5_BLOCK - 1 - tau
        wst_ref[j * gw:(j + 1) * gw, 0:ns] = w_r.astype(BF16)
        wst_ref[j * gw:(j + 1) * gw, ns:2 * ns] = w_i.astype(BF16)
        ktau.append(_dot_nt(w_r, c_r, lax.Precision.HIGHEST) - _dot_nt(w_i, c_i, lax.Precision.HIGHEST))
    rr = lax.broadcasted_iota(jnp.int32, (gw, gw), 0)
    cc = lax.broadcasted_iota(jnp.int32, (gw, gw), 1)
    ktau[0] = ktau[0] + jnp.where(rr == cc, dsk_ref[...], 0.0)

    for j in range(S5_BLOCK):
        p_r, p_i = lam_bar_pow(float(j + 1))
        cat_ref[j * gw:(j + 1) * gw, 0:ns] = (c_r * p_r - c_i * p_i).astype(BF16)
        cat_ref[j * gw:(j + 1) * gw, ns:2 * ns] = (-(c_r * p_i + c_i * p_r)).astype(BF16)

    zeros = jnp.zeros((gw, gw), BF16)
    for i in range(S5_BLOCK):
        for j in range(S5_BLOCK):
            blk = ktau[j - i].astype(BF16) if j >= i else zeros
            kt_ref[i * gw:(i + 1) * gw, j * gw:(j + 1) * gw] = blk

    row = lax.broadcasted_iota(jnp.int32, (8, ns), 0)
    for idx, k in enumerate((1, 2, 4)):
        p_r, p_i = lam_bar_pow(float(S5_BLOCK * k))
        sct_ref[2 * idx] = jnp.where(row >= k, p_r, 0.0)
        sct_ref[2 * idx + 1] = jnp.where(row >= k, p_i, 0.0)
    n = (S5_BLOCK * (row + 1)).astype(F32)
    mag = jnp.exp(n * (lr * dt))
    ang = n * (li * dt)
    sct_ref[6] = mag * jnp.cos(ang)
    sct_ref[7] = mag * jnp.sin(ang)


def _s5_prep(lre, lim, ldt, bre, bim, cre, cim, dsk):
    nq, gw, ns = bre.shape
    kw = S5_BLOCK * gw
    per_q = lambda *tail: pl.BlockSpec((None,) + tail, lambda q: (q,) + (0,) * len(tail))
    return pl.pallas_call(
        _s5_prep_kernel,
        grid=(nq,),
        in_specs=[per_q(1, ns), per_q(1, ns), per_q(1, ns), per_q(gw, ns), per_q(gw, ns),
                  per_q(gw, ns), per_q(gw, ns), per_q(1, gw)],
        out_specs=[per_q(kw, 2 * ns), per_q(kw, 2 * ns), per_q(kw, kw), per_q(8, 8, ns)],
        out_shape=[
            jax.ShapeDtypeStruct((nq, kw, 2 * ns), BF16),
            jax.ShapeDtypeStruct((nq, kw, 2 * ns), BF16),
            jax.ShapeDtypeStruct((nq, kw, kw), BF16),
            jax.ShapeDtypeStruct((nq, 8, 8, ns), F32),
        ],
        compiler_params=_cparams(1),
        name="s5_prep",
    )(lre, lim, ldt, bre, bim, cre, cim, dsk)


def _quarter_lanes(u_ref, q, gw, sw):
    return jnp.concatenate(
        [u_ref[:, j * sw + q * gw: j * sw + (q + 1) * gw] for j in range(S5_BLOCK)], axis=1)


def _store_quarter(y_ref, y, q, gw, sw):
    for j in range(S5_BLOCK):
        y_ref[:, j * sw + q * gw: j * sw + (q + 1) * gw] = y[:, j * gw:(j + 1) * gw].astype(y_ref.dtype)


def _s5_prompt_kernel(u_ref, wst_ref, cat_ref, kt_ref, sct_ref, y_ref, st_ref,
                      carry_ref, s_scr, hp_scr, *, rb, gw, sw, ns):
    blk = pl.program_id(1)

    @pl.when(blk == 0)
    def _():
        carry_ref[...] = jnp.zeros(carry_ref.shape, F32)

    row0 = lax.broadcasted_iota(jnp.int32, (8, ns), 0) == 0
    for q in range(S5_QUARTERS):
        uq = _quarter_lanes(u_ref, q, gw, sw)
        s_scr[...] = _dot(uq, wst_ref[q])
        tabs = [sct_ref[q, i] for i in range(8)]

        def tile(t, carry, tabs=tabs):
            cr, ci = carry
            r0 = pl.multiple_of(t * 8, 8)
            xr = s_scr[pl.ds(r0, 8), 0:ns]
            xi = s_scr[pl.ds(r0, 8), ns:2 * ns]
            for idx, k in enumerate((1, 2, 4)):
                mr, mi = tabs[2 * idx], tabs[2 * idx + 1]
                sr = pltpu.roll(xr, k, 0)
                si = pltpu.roll(xi, k, 0)
                xr, xi = xr + (mr * sr - mi * si), xi + (mr * si + mi * sr)
            crb = jnp.broadcast_to(cr, (8, ns))
            cib = jnp.broadcast_to(ci, (8, ns))
            hr = xr + (tabs[6] * crb - tabs[7] * cib)
            hi = xi + (tabs[6] * cib + tabs[7] * crb)
            hp_scr[pl.ds(r0, 8), 0:ns] = jnp.where(row0, crb, pltpu.roll(hr, 1, 0))
            hp_scr[pl.ds(r0, 8), ns:2 * ns] = jnp.where(row0, cib, pltpu.roll(hi, 1, 0))
            return hr[7:8, :], hi[7:8, :]

        cr, ci = lax.fori_loop(0, rb // 8, tile,
                               (carry_ref[q, :, 0:ns], carry_ref[q, :, ns:2 * ns]))
        carry_ref[q, :, 0:ns] = cr
        carry_ref[q, :, ns:2 * ns] = ci
        st_ref[q, :, 0:ns] = cr
        st_ref[q, :, ns:2 * ns] = ci
        y = _dot_nt(hp_scr[...].astype(BF16), cat_ref[q]) + _dot(uq, kt_ref[q])
        _store_quarter(y_ref, y, q, gw, sw)


def _s5_prompt(u4, wst, cat, kt, sct, rb):
    b, rows, lanes = u4.shape
    nq, kw, ns2 = wst.shape
    ns, gw, sw = ns2 // 2, kw // S5_BLOCK, lanes // S5_BLOCK
    return pl.pallas_call(
        functools.partial(_s5_prompt_kernel, rb=rb, gw=gw, sw=sw, ns=ns),
        grid=(b, rows // rb),
        in_specs=[
            pl.BlockSpec((None, rb, lanes), lambda bi, ri: (bi, ri, 0)),
            _const_spec(wst.shape), _const_spec(cat.shape), _const_spec(kt.shape), _const_spec(sct.shape),
        ],
        out_specs=[
            pl.BlockSpec((None, rb, lanes), lambda bi, ri: (bi, ri, 0)),
            pl.BlockSpec((None, nq, 1, ns2), lambda bi, ri: (bi, 0, 0, 0)),
        ],
        out_shape=[
            jax.ShapeDtypeStruct((b, rows, lanes), BF16),
            jax.ShapeDtypeStruct((b, nq, 1, ns2), F32),
        ],
        scratch_shapes=[
            pltpu.VMEM((nq, 1, ns2), F32),
            pltpu.VMEM((rb, ns2), F32),
            pltpu.VMEM((rb, ns2), F32),
        ],
        compiler_params=_cparams(2),
        name="s5_prompt",
    )(u4, wst, cat, kt, sct)


def _s5_sample_kernel(u_ref, h0_ref, wst_ref, cat_ref, kt_ref, sct_ref, y_ref, hout_ref, *, gw, sw, ns):
    for q in range(S5_QUARTERS):
        uq = _quarter_lanes(u_ref, q, gw, sw)
        s = _dot(uq, wst_ref[q])
        h0 = h0_ref[q]
        h0r, h0i = h0[:, 0:ns], h0[:, ns:2 * ns]
        a4r = sct_ref[q, 6][0:1, :]
        a4i = sct_ref[q, 7][0:1, :]
        hout_ref[q, :, 0:ns] = s[:, 0:ns] + (a4r * h0r - a4i * h0i)
        hout_ref[q, :, ns:2 * ns] = s[:, ns:2 * ns] + (a4r * h0i + a4i * h0r)
        y = _dot_nt(h0.astype(BF16), cat_ref[q]) + _dot(uq, kt_ref[q])
        _store_quarter(y_ref, y, q, gw, sw)


def _s5_sample(u4, h0q, wst, cat, kt, sct):
    rows, lanes = u4.shape
    nq, kw, ns2 = wst.shape
    ns, gw, sw = ns2 // 2, kw // S5_BLOCK, lanes // S5_BLOCK
    return pl.pallas_call(
        functools.partial(_s5_sample_kernel, gw=gw, sw=sw, ns=ns),
        grid=(1,),
        in_specs=[_const_spec(u4.shape), _const_spec(h0q.shape), _const_spec(wst.shape),
                  _const_spec(cat.shape), _const_spec(kt.shape), _const_spec(sct.shape)],
        out_specs=[pl.BlockSpec(u4.shape, lambda i: (0, 0)),
                   pl.BlockSpec(h0q.shape, lambda i: (0, 0, 0))],
        out_shape=[jax.ShapeDtypeStruct(u4.shape, BF16), jax.ShapeDtypeStruct(h0q.shape, F32)],
        compiler_params=_cparams(1),
        name="s5_sample",
    )(u4, h0q, wst, cat, kt, sct)


def _mem_kv_kernel(x_ref, g_ref, wk_ref, wv_ref, k_ref, v_ref, kb_ref, vb_ref):
    hn = _rms(x_ref[...], g_ref[...]).astype(BF16)
    k = _dot(hn, wk_ref[...])
    k_ref[...] = k
    kb_ref[...] = k.astype(BF16)
    v = _dot(hn, wv_ref[...])
    v_ref[...] = v
    vb_ref[...] = v.astype(BF16)


def _mem_kv(mem, g, wk_bf, wv_bf, tm):
    n, d = mem.shape
    row = pl.BlockSpec((tm, d), lambda i: (i, 0))
    return pl.pallas_call(
        _mem_kv_kernel,
        grid=(n // tm,),
        in_specs=[row, _const_spec((1, d)), _const_spec(wk_bf.shape), _const_spec(wv_bf.shape)],
        out_specs=[row, row, row, row],
        out_shape=[jax.ShapeDtypeStruct((n, d), F32), jax.ShapeDtypeStruct((n, d), F32),
                   jax.ShapeDtypeStruct((n, d), BF16), jax.ShapeDtypeStruct((n, d), BF16)],
        compiler_params=_cparams(1),
        name="mem_kv",
    )(mem, g, wk_bf, wv_bf)


def _merge_and_query(x, att, sy4_ref, sy_scr, wglu_ref, gso_ref, wout_ref, gpost_ref, gmpre_ref, wmq_ref,
                     mem_scale):
    aw = att.shape[1]
    rows, n_lb = sy4_ref.shape[0], sy_scr.shape[0]
    sw = n_lb * LANES
    for j in range(S5_BLOCK):
        for c in range(n_lb):
            blk = sy4_ref[:, j * sw + c * LANES:j * sw + (c + 1) * LANES].astype(F32)
            sy_scr[c, pl.ds(j, rows, stride=S5_BLOCK), :] = blk
    y = jax.nn.gelu(jnp.concatenate([sy_scr[c] for c in range(n_lb)], axis=1))
    y = y * jax.nn.sigmoid(_dot(y.astype(BF16), wglu_ref[...]))
    y = _rms(y, gso_ref[...])
    mix = _dot(att, wout_ref[0:aw, :]) + _dot(y.astype(BF16), wout_ref[aw:, :])
    x1 = x + _rms(mix, gpost_ref[...])
    hq = _rms(x1, gmpre_ref[...]).astype(BF16)
    qm = (_dot(hq, wmq_ref[...]) * mem_scale).astype(BF16)
    return x1, qm


def _mix_mem_kernel(x_ref, att_ref, sy_ref, mk_ref, mv_ref, wglu_ref, gso_ref, wout_ref, gpost_ref,
                    gmpre_ref, wmq_ref, wmo_ref, gmpost_ref, o_ref, oh_ref, sy_scr, *, n_heads, mem_scale):
    x1, qm = _merge_and_query(x_ref[...], att_ref[...], sy_ref, sy_scr, wglu_ref, gso_ref, wout_ref,
                              gpost_ref, gmpre_ref, wmq_ref, mem_scale)
    hd = qm.shape[1] // n_heads
    for h in range(n_heads):
        sl = slice(h * hd, (h + 1) * hd)
        s = _dot_nt(qm[:, sl], mk_ref[:, sl])
        p = jnp.exp(s - jnp.max(s, axis=-1, keepdims=True))
        l = jnp.sum(p, axis=-1, keepdims=True)
        oh_ref[:, sl] = (_dot(p.astype(BF16), mv_ref[:, sl]) / l).astype(BF16)
    mo = _dot(oh_ref[...], wmo_ref[...])
    o_ref[...] = x1 + _rms(mo, gmpost_ref[...])


def _mix_mem(x, att, sy4, mkb, mvb, w, tm, n_heads):
    b, t, d = x.shape
    aw, sw = att.shape[2], sy4.shape[2] // S5_BLOCK
    nm = mkb.shape[1]
    tok = lambda width: pl.BlockSpec((None, tm, width), lambda bi, ti: (bi, ti, 0))
    tok4 = pl.BlockSpec((None, tm // S5_BLOCK, S5_BLOCK * sw), lambda bi, ti: (bi, ti, 0))
    per_b = pl.BlockSpec((None, nm, d), lambda bi, ti: (bi, 0, 0))
    weights = [w["w_glu"], w["g_ssm_out"], w["w_out"], w["g_mix_post"], w["g_mem_pre"], w["w_mq"],
               w["w_mo"], w["g_mem_post"]]
    return pl.pallas_call(
        functools.partial(_mix_mem_kernel, n_heads=n_heads, mem_scale=(d // n_heads) ** -0.5),
        grid=(b, t // tm),
        in_specs=[tok(d), tok(aw), tok4, per_b, per_b] + [_const_spec(a.shape) for a in weights],
        out_specs=tok(d),
        out_shape=jax.ShapeDtypeStruct((b, t, d), F32),
        scratch_shapes=[pltpu.VMEM((tm, d), BF16), pltpu.VMEM((sw // LANES, tm, LANES), F32)],
        compiler_params=_cparams(2),
        name="mix_mem",
    )(x, att, sy4, mkb, mvb, *weights)


def _mix_q_kernel(x_ref, att_ref, sy_ref, wglu_ref, gso_ref, wout_ref, gpost_ref, gmpre_ref, wmq_ref,
                  x1_ref, qm_ref, sy_scr, *, mem_scale):
    x1, qm = _merge_and_query(x_ref[...], att_ref[...], sy_ref, sy_scr, wglu_ref, gso_ref, wout_ref,
                              gpost_ref, gmpre_ref, wmq_ref, mem_scale)
    x1_ref[...] = x1
    qm_ref[...] = qm


def _mix_q(x, att, sy4, w, n_heads):
    n, d = x.shape
    weights = [w["w_glu"], w["g_ssm_out"], w["w_out"], w["g_mix_post"], w["g_mem_pre"], w["w_mq"]]
    ins = [x, att, sy4] + weights
    return pl.pallas_call(
        functools.partial(_mix_q_kernel, mem_scale=(d // n_heads) ** -0.5),
        grid=(1,),
        in_specs=[_const_spec(a.shape) for a in ins],
        out_specs=[pl.BlockSpec((n, d), lambda i: (0, 0)), pl.BlockSpec((n, d), lambda i: (0, 0))],
        out_shape=[jax.ShapeDtypeStruct((n, d), F32), jax.ShapeDtypeStruct((n, d), BF16)],
        scratch_shapes=[pltpu.VMEM((sy4.shape[1] // S5_BLOCK // LANES, n, LANES), F32)],
        compiler_params=_cparams(1),
        name="mix_q",
    )(*ins)


def _mem_sample_kernel(mask_ref, q_ref, ck_hbm, cv_hbm, o_ref, kbuf, vbuf, sem, *, n_mem, n_heads):
    b = pl.program_id(0)
    nb = pl.num_programs(0)
    slot = b % 2
    n_halves = kbuf.shape[1]

    def copies(seq, sl):
        out = []
        for kind, (hbm, buf) in enumerate(((ck_hbm, kbuf), (cv_hbm, vbuf))):
            for lb in range(n_halves):
                src = hbm.at[seq, :, :, pl.ds(lb * LANES, LANES)]
                dst = buf.at[sl, lb].reshape(n_mem, n_heads, LANES)
                out.append(pltpu.make_async_copy(src, dst, sem.at[kind, sl]))
        return out

    @pl.when(b == 0)
    def _():
        for cp in copies(0, 0):
            cp.start()

    @pl.when(b + 1 < nb)
    def _():
        for cp in copies(b + 1, 1 - slot):
            cp.start()

    for cp in copies(b, slot):
        cp.wait()

    kcat = jnp.concatenate([kbuf[slot, lb] for lb in range(n_halves)], axis=1).astype(BF16)
    vcat = jnp.concatenate([vbuf[slot, lb] for lb in range(n_halves)], axis=1).astype(BF16)
    s = _dot_nt(q_ref[...], kcat) + mask_ref[...]
    p = jnp.exp(s - jnp.max(s, axis=-1, keepdims=True))
    l = jnp.sum(p, axis=-1, keepdims=True)
    o_ref[...] = _dot(p.astype(BF16), vcat) / l


def _mem_sample(qh, ck, cv, mask):
    db, nrow, hd = qh.shape
    _, n_mem, n_heads, _ = ck.shape
    grid_spec = pltpu.PrefetchScalarGridSpec(
        num_scalar_prefetch=0,
        grid=(db,),
        in_specs=[
            _const_spec(mask.shape),
            pl.BlockSpec((None, nrow, hd), lambda i: (i, 0, 0)),
            pl.BlockSpec(memory_space=pl.ANY),
            pl.BlockSpec(memory_space=pl.ANY),
        ],
        out_specs=pl.BlockSpec((None, nrow, hd), lambda i: (i, 0, 0)),
        scratch_shapes=[
            pltpu.VMEM((2, hd // LANES, n_mem * n_heads, LANES), F32),
            pltpu.VMEM((2, hd // LANES, n_mem * n_heads, LANES), F32),
            pltpu.SemaphoreType.DMA((2, 2)),
        ],
    )
    return pl.pallas_call(
        functools.partial(_mem_sample_kernel, n_mem=n_mem, n_heads=n_heads),
        grid_spec=grid_spec,
        out_shape=jax.ShapeDtypeStruct((db, nrow, hd), F32),
        compiler_params=_cparams(1),
        name="mem_sample",
    )(mask, qh, ck, cv)


def _mem_out_kernel(x1_ref, o_ref, wmo_ref, g_ref, y_ref):
    mo = _dot(o_ref[...], wmo_ref[...])
    y_ref[...] = x1_ref[...] + _rms(mo, g_ref[...])


def _mem_out(x1, o, wmo, g):
    n, d = x1.shape
    ins = [x1, o, wmo, g]
    return pl.pallas_call(
        _mem_out_kernel,
        grid=(1,),
        in_specs=[_const_spec(a.shape) for a in ins],
        out_specs=pl.BlockSpec((n, d), lambda i: (0, 0)),
        out_shape=jax.ShapeDtypeStruct((n, d), F32),
        compiler_params=_cparams(1),
        name="mem_out",
    )(*ins)


def _ffn_kernel(x_ref, gpre_ref, wg_ref, wu_ref, wd_ref, gpost_ref, o_ref, *, n_chunks):
    x = x_ref[...]
    hn = _rms(x, gpre_ref[...]).astype(BF16)
    ff = wg_ref.shape[1]
    cw = ff // n_chunks
    acc = jnp.zeros(x.shape, F32)
    for c in range(n_chunks):
        sl = slice(c * cw, (c + 1) * cw)
        a = jax.nn.silu(_dot(hn, wg_ref[:, sl])) * _dot(hn, wu_ref[:, sl])
        acc = acc + _dot(a.astype(BF16), wd_ref[sl, :])
    o_ref[...] = x + _rms(acc, gpost_ref[...])


def _ffn(x, gpre, wg, wu, wd, gpost, tm, n_chunks):
    n, d = x.shape
    row = pl.BlockSpec((tm, d), lambda i: (i, 0))
    weights = [gpre, wg, wu, wd, gpost]
    return pl.pallas_call(
        functools.partial(_ffn_kernel, n_chunks=n_chunks),
        grid=(n // tm,),
        in_specs=[row] + [_const_spec(a.shape) for a in weights],
        out_specs=row,
        out_shape=jax.ShapeDtypeStruct((n, d), F32),
        compiler_params=_cparams(1),
        name="ffn",
    )(x, *weights)


def _s5_param_layout(lam_re, lam_im, log_dt, b_re, b_im, c_re, c_im, d_skip):
    g, p, hc = b_re.shape
    nq = S5_QUARTERS
    gq = g // nq
    rowv = lambda a: a.reshape(nq, 1, gq * p)
    ldt = jnp.broadcast_to(log_dt[:, None], (g, p))
    eye = jnp.eye(gq, dtype=F32)

    def embed_b(b):
        bq = b.reshape(nq, gq, p, hc)
        return jnp.einsum("qgph,gk->qghkp", bq, eye).reshape(nq, gq * hc, gq * p)

    def embed_c(c):
        cq = c.reshape(nq, gq, hc, p)
        return jnp.einsum("qghp,gk->qghkp", cq, eye).reshape(nq, gq * hc, gq * p)

    return (rowv(lam_re), rowv(lam_im), rowv(ldt), embed_b(b_re), embed_b(b_im),
            embed_c(c_re), embed_c(c_im), d_skip.reshape(nq, 1, gq * hc))


def _state_to_quarters(re, im):
    n, g, p = re.shape
    gq = g // S5_QUARTERS
    f = lambda a: a.reshape(n, S5_QUARTERS, gq * p).transpose(1, 0, 2)
    return jnp.concatenate([f(re), f(im)], axis=-1)


def _state_from_quarters(st, g, p):
    n = st.shape[0]
    ns = st.shape[-1] // 2
    return st[..., :ns].reshape(n, g, p), st[..., ns:].reshape(n, g, p)


def kernel(x_prompt, x_sample, mem_prompt, cache_k, cache_v, page_table, state_ssm_re, state_ssm_im,
           cache_mem_k, cache_mem_v, g_mix_pre, g_mix_post, w_in, lam_q1, lam_k1, lam_q2, lam_k2, g_subln,
           ssm_lam_re, ssm_lam_im, ssm_log_dt, ssm_b_re, ssm_b_im, ssm_c_re, ssm_c_im, ssm_d, w_glu, g_ssm_out,
           w_out, g_mem_pre, g_mem_post, g_mem_kv, w_mq, w_mk, w_mv, w_mo, g_ffn_pre, g_ffn_post,
           w_gate, w_up, w_down):
    depth = w_in.shape[0]
    b, t, d = x_prompt.shape
    db, dt_, _ = x_sample.shape
    n_mem = mem_prompt.shape[1]
    n_mem_heads = cache_mem_k.shape[3]
    n_heads = cache_k.shape[3]
    aw = n_heads * HEAD_W
    g, p = ssm_lam_re.shape[1], ssm_lam_re.shape[2]
    sw = g * SSM_GROUP
    n_phys = cache_k.shape[1]
    assert dt_ == S5_BLOCK and t % (S5_BLOCK * 256) == 0

    slopes = LOG2E * 2.0 ** (-8.0 * jnp.arange(1, n_heads + 1, dtype=F32) / n_heads)
    slope_hi = slopes.astype(BF16).astype(F32)
    slope_parts = jnp.stack([slopes, slope_hi, (slopes - slope_hi).astype(BF16).astype(F32)])
    row = lambda a: a.reshape(1, -1).astype(F32)
    bf = lambda a: a.astype(BF16)

    branch_lane = jnp.arange(HEAD_W) // HEAD_DIM
    nrow = 2 * n_heads * dt_
    chunk = 256
    r = jnp.arange(nrow)
    r_head, r_tok = (r // dt_) % n_heads, r % dt_
    r_slope = slopes[r_head][:, None]
    c = jnp.arange(chunk)
    c_tok, c_head = c // n_heads, c % n_heads
    own_head = c_head[None, :] == r_head[:, None]
    base = r_slope * c_tok[None, :].astype(F32) + jnp.where(own_head, 0.0, NEG)
    base2 = jnp.concatenate([base, base + r_slope * float(chunk // n_heads)], axis=0)
    slope2 = jnp.concatenate([r_slope, r_slope], axis=0)
    new_ok = own_head & (c[None, :] < dt_ * n_heads) & (c_tok[None, :] <= r_tok[:, None])
    newmask = jnp.where(new_ok, r_slope * c_tok[None, :].astype(F32), NEG)
    mem_row_head = jnp.arange(n_mem_heads * dt_) // dt_
    mem_col_head = jnp.arange(n_mem * n_mem_heads) % n_mem_heads
    mem_mask = jnp.where(mem_col_head[None, :] == mem_row_head[:, None], 0.0, NEG).astype(F32)

    y_p = x_prompt
    y_s = x_sample.reshape(db * dt_, d)
    outs = {k: [] for k in ("kp", "vp", "ks", "vs", "rp", "ip", "rs", "is", "mkp", "mvp")}
    for l in range(depth):
        lam_init = 0.8 - 0.6 * math.exp(-0.3 * l)
        lamv = jnp.stack([lam_q1[l], lam_k1[l], lam_q2[l], lam_k2[l]]).astype(F32)
        gsub = row(g_subln[l])
        w_in_bf = bf(w_in[l])
        w = {
            "w_glu": bf(w_glu[l]), "g_ssm_out": row(g_ssm_out[l]), "w_out": bf(w_out[l]),
            "g_mix_post": row(g_mix_post[l]), "g_mem_pre": row(g_mem_pre[l]), "w_mq": bf(w_mq[l]),
            "w_mo": bf(w_mo[l]), "g_mem_post": row(g_mem_post[l]),
        }

        qp, kp, vp, kpb, vpb, up = _proj(y_p.reshape(b * t, d), row(g_mix_pre[l]), w_in_bf, aw, 512)
        qs, ks, vs, _, _, us = _proj(y_s, row(g_mix_pre[l]), w_in_bf, aw, 512)

        qh = qs.reshape(db, dt_, n_heads, HEAD_W).transpose(0, 2, 1, 3)
        zq = jnp.zeros((), BF16)
        q32 = jnp.stack([jnp.where(branch_lane == 0, qh, zq), jnp.where(branch_lane == 1, qh, zq)], axis=1)
        q32 = q32.reshape(db, nrow, HEAD_W)
        zq32 = jnp.zeros_like(q32)
        q2 = jnp.concatenate([jnp.concatenate([q32, zq32], axis=2), jnp.concatenate([zq32, q32], axis=2)], axis=1)
        att_p, att_s16 = _attn(jnp.swapaxes(qp.reshape(b, t, aw), 1, 2), kpb.reshape(b, t, aw),
                               jnp.swapaxes(vpb.reshape(b, t, aw), 1, 2),
                               slope_parts, lamv, g_subln[l].reshape(HEAD_W, 1).astype(F32),
                               page_table, q2, ks.reshape(db, dt_ * n_heads, HEAD_W),
                               vs.reshape(db, dt_ * n_heads, HEAD_W),
                               cache_k[l].reshape(n_phys * PAGE * n_heads, HEAD_W),
                               cache_v[l].reshape(n_phys * PAGE * n_heads, HEAD_W),
                               gsub, slope2, base2, newmask, lam_init, 512)
        att_s = bf(att_s16.reshape(db, n_heads, dt_, HEAD_W).transpose(0, 2, 1, 3).reshape(db * dt_, aw))

        tabs = _s5_prep(*_s5_param_layout(ssm_lam_re[l], ssm_lam_im[l], ssm_log_dt[l], ssm_b_re[l],
                                          ssm_b_im[l], ssm_c_re[l], ssm_c_im[l], ssm_d[l]))
        syp, stp = _s5_prompt(up.reshape(b, t // S5_BLOCK, S5_BLOCK * sw), *tabs, 256)
        rp, ip = _state_from_quarters(stp.reshape(b, S5_QUARTERS, -1), g, p)
        sys_, sts = _s5_sample(us, _state_to_quarters(state_ssm_re[l], state_ssm_im[l]), *tabs)
        rs, is_ = _state_from_quarters(sts.transpose(1, 0, 2), g, p)

        mkp, mvp, mkb, mvb = _mem_kv(mem_prompt.reshape(b * n_mem, d), row(g_mem_kv[l]),
                                     bf(w_mk[l]), bf(w_mv[l]), 512)
        y_p = _mix_mem(y_p, att_p, syp, mkb.reshape(b, n_mem, d), mvb.reshape(b, n_mem, d), w, 512, n_mem_heads)
        x1s, qms = _mix_q(y_s, att_s, sys_, w, n_mem_heads)
        mhd = d // n_mem_heads
        qmh = qms.reshape(db, dt_, n_mem_heads, mhd).transpose(0, 2, 1, 3).reshape(db, n_mem_heads * dt_, mhd)
        omh = _mem_sample(qmh, cache_mem_k[l], cache_mem_v[l], mem_mask)
        om = bf(omh.reshape(db, n_mem_heads, dt_, mhd).transpose(0, 2, 1, 3).reshape(db * dt_, d))
        y_s = _mem_out(x1s, om, w["w_mo"], w["g_mem_post"])

        ffw = (row(g_ffn_pre[l]), bf(w_gate[l]), bf(w_up[l]), bf(w_down[l]), row(g_ffn_post[l]))
        y_p = _ffn(y_p.reshape(b * t, d), *ffw, 512, 2).reshape(b, t, d)
        y_s = _ffn(y_s, *ffw, 512, 2)

        outs["kp"].append(kp.reshape(b, t, n_heads, HEAD_W))
        outs["vp"].append(vp.reshape(b, t, n_heads, HEAD_W))
        outs["ks"].append(ks.reshape(db, dt_, n_heads, HEAD_W))
        outs["vs"].append(vs.reshape(db, dt_, n_heads, HEAD_W))
        outs["rp"].append(rp); outs["ip"].append(ip); outs["rs"].append(rs); outs["is"].append(is_)
        outs["mkp"].append(mkp.reshape(b, n_mem, n_mem_heads, d // n_mem_heads))
        outs["mvp"].append(mvp.reshape(b, n_mem, n_mem_heads, d // n_mem_heads))

    st = lambda k: jnp.stack(outs[k], 0)
    return (y_p, y_s.reshape(db, dt_, d), st("kp"), st("vp"), st("ks"), st("vs"),
            st("rp"), st("ip"), st("rs"), st("is"), st("mkp"), st("mvp"))
```

```python
import functools
import math

import jax
import jax.numpy as jnp
from jax import lax
from jax.experimental import pallas as pl
from jax.experimental.pallas import tpu as pltpu

F32 = jnp.float32
BF16 = jnp.bfloat16

RMS_EPS = 1e-6
HEAD_DIM = 64
HEAD_W = 2 * HEAD_DIM
SSM_GROUP = 16
SSM_STATE = 64
S5_BLOCK = 4
S5_QUARTERS = 4
PAGE = 128
LANES = 128
NEG = -1e30
LOG2E = 1.4426950408889634
VMEM_LIMIT_BYTES = 56 * 1024 * 1024


def _cparams(n_grid_dims):
    return pltpu.CompilerParams(
        dimension_semantics=("arbitrary",) * n_grid_dims,
        vmem_limit_bytes=VMEM_LIMIT_BYTES)


def _const_spec(shape):
    nd = len(shape)
    return pl.BlockSpec(shape, lambda *_: (0,) * nd, pipeline_mode=pl.Buffered(1))


def _rms(x, g):
    ms = jnp.mean(x * x, axis=-1, keepdims=True)
    return x * lax.rsqrt(ms + RMS_EPS) * g


def _dot(a, b):
    return jnp.dot(a, b, preferred_element_type=F32)


def _dot_nt(a, b, precision=None):
    return lax.dot_general(a, b, (((1,), (1,)), ((), ())),
                           preferred_element_type=F32, precision=precision)


def _proj_kernel(x_ref, g_ref, w_ref, q_ref, k_ref, v_ref, kb_ref, vb_ref, u_ref, u_scr, *, aw, tm):
    hn = _rms(x_ref[...], g_ref[...]).astype(BF16)
    q = _dot(hn, w_ref[:, 0:aw])
    q_ref[...] = (q * (HEAD_DIM ** -0.5 * LOG2E)).astype(BF16)
    nh = aw // HEAD_W
    k = _dot(hn, w_ref[:, aw:2 * aw])
    kb_ref[...] = k.astype(BF16)
    v = _dot(hn, w_ref[:, 2 * aw:3 * aw])
    vb_ref[...] = v.astype(BF16)
    for h in range(nh):
        k_ref[pl.ds(h, tm, stride=nh), :] = k[:, h * HEAD_W:(h + 1) * HEAD_W]
        v_ref[pl.ds(h, tm, stride=nh), :] = v[:, h * HEAD_W:(h + 1) * HEAD_W]
    u = _dot(hn, w_ref[:, 3 * aw:])
    n_lb = u_scr.shape[0]
    sw = n_lb * LANES
    for c in range(n_lb):
        u_scr[c] = u[:, c * LANES:(c + 1) * LANES]
    for j in range(S5_BLOCK):
        for c in range(n_lb):
            rows = u_scr[c, pl.ds(j, tm // S5_BLOCK, stride=S5_BLOCK), :]
            u_ref[:, j * sw + c * LANES:j * sw + (c + 1) * LANES] = rows.astype(BF16)


def _proj(x, g, w_bf, aw, tm):
    n, d = x.shape
    sw = w_bf.shape[1] - 3 * aw
    nh = aw // HEAD_W
    row = lambda width: pl.BlockSpec((tm, width), lambda i: (i, 0))
    flat = pl.BlockSpec((tm * nh, HEAD_W), lambda i: (i, 0))
    return pl.pallas_call(
        functools.partial(_proj_kernel, aw=aw, tm=tm),
        grid=(n // tm,),
        in_specs=[row(d), _const_spec((1, d)), _const_spec(w_bf.shape)],
        out_specs=[row(aw), flat, flat, row(aw), row(aw),
                   pl.BlockSpec((tm // S5_BLOCK, S5_BLOCK * sw), lambda i: (i, 0))],
        out_shape=[
            jax.ShapeDtypeStruct((n, aw), BF16),
            jax.ShapeDtypeStruct((n * nh, HEAD_W), F32),
            jax.ShapeDtypeStruct((n * nh, HEAD_W), F32),
            jax.ShapeDtypeStruct((n, aw), BF16),
            jax.ShapeDtypeStruct((n, aw), BF16),
            jax.ShapeDtypeStruct((n // S5_BLOCK, S5_BLOCK * sw), BF16),
        ],
        scratch_shapes=[pltpu.VMEM((sw // LANES, tm, LANES), F32)],
        compiler_params=_cparams(1),
        name="proj",
    )(x, g, w_bf)


def _diff_lambda(lamv, lam_init):
    d1 = jnp.sum(lamv[0:1] * lamv[1:2], axis=-1, keepdims=True)
    d2 = jnp.sum(lamv[2:3] * lamv[3:4], axis=-1, keepdims=True)
    return jnp.exp(d1) - jnp.exp(d2) + lam_init


def _attn_prompt_body(sl_ref, lamv_ref, g_ref, pos_ref, qt_ref, k_ref, vt_ref, o_ref,
                        qq_ref, m_ref, acc_ref, t0_ref, t1_ref, *, tq, lam_init):
    hp = pl.program_id(1)
    qi = pl.program_id(2)
    tk = tq
    n_hh = qq_ref.shape[0]
    heads = [hp * n_hh + hh for hh in range(n_hh)]
    hrows = [slice(hh * HEAD_W, (hh + 1) * HEAD_W) for hh in range(n_hh)]

    arow = lax.broadcasted_iota(jnp.int32, (HEAD_W, 2 * tq), 0)
    for hh in range(n_hh):
        qt = qt_ref[hrows[hh], :].astype(F32)
        frow = lax.broadcasted_iota(jnp.int32, qt.shape, 0)
        qq_ref[hh, 0:HEAD_W, 0:tq] = jnp.where(frow < HEAD_DIM, qt, 0.0).astype(BF16)
        qq_ref[hh, 0:HEAD_W, tq:2 * tq] = jnp.where(frow >= HEAD_DIM, qt, 0.0).astype(BF16)
        slope_parts = jnp.where((arow & 1) == 0, sl_ref[1, heads[hh]], sl_ref[2, heads[hh]])
        qq_ref[hh, HEAD_W:2 * HEAD_W, :] = jnp.where(arow < 4, slope_parts, 0.0).astype(BF16)
    m_ref[...] = jnp.full(m_ref.shape, NEG, F32)
    acc_ref[...] = jnp.zeros(acc_ref.shape, F32)
    ones = jnp.ones((acc_ref.shape[1] - HEAD_W, tk), BF16)

    def scores(hh, kj):
        k0 = pl.multiple_of(kj * tk, tk)
        ka = jnp.concatenate([k_ref[pl.ds(k0, tk), hrows[hh]], pos_ref[...]], axis=1)
        return _dot(ka, qq_ref[hh])

    def consume(hh, kj, t):
        k0 = pl.multiple_of(kj * tk, tk)
        vta = jnp.concatenate([vt_ref[hrows[hh], pl.ds(k0, tk)], ones], axis=0)
        off = sl_ref[0, heads[hh]] * jnp.full((1, 2 * tq), (kj - qi) * tk, jnp.int32).astype(F32)
        m_old = m_ref[hh]
        m_new = jnp.maximum(m_old, jnp.max(t, axis=0, keepdims=True) + off)
        alpha = jnp.exp2(m_old - m_new)
        p = jnp.exp2(t - (m_new - off)).astype(BF16)
        acc_ref[hh] = alpha * acc_ref[hh] + _dot(vta, p)
        m_ref[hh] = m_new

    tbuf = (t0_ref, t1_ref)

    def step(kj, cur):
        for hh in range(n_hh):
            tbuf[1 - cur][hh] = scores(hh, kj + 1)
        for hh in range(n_hh):
            consume(hh, kj, tbuf[cur][hh])

    for hh in range(n_hh):
        t0_ref[hh] = scores(hh, 0)

    def body(i, carry):
        step(2 * i, 0)
        step(2 * i + 1, 1)
        return carry

    lax.fori_loop(0, qi // 2, body, 0)
    odd = qi % 2 == 1

    @pl.when(odd)
    def _():
        step(qi - 1, 0)

    def finish(cur):
        krow = lax.broadcasted_iota(jnp.int32, (tk, 2 * tq), 0)
        qcol = lax.broadcasted_iota(jnp.int32, (tk, 2 * tq), 1)
        qcol = jnp.where(qcol >= tq, qcol - tq, qcol)
        lam = _diff_lambda(lamv_ref[...], lam_init)
        for hh in range(n_hh):
            consume(hh, qi, jnp.where(krow <= qcol, tbuf[cur][hh], NEG))
            acc = acc_ref[hh]
            ot = acc[0:HEAD_W] / acc[HEAD_W:HEAD_W + 1]
            d = ot[:, 0:tq] - lam * ot[:, tq:2 * tq]
            ms = jnp.mean(d * d, axis=0, keepdims=True)
            dn = d * lax.rsqrt(ms + RMS_EPS) * g_ref[...] * (1.0 - lam_init)
            o_ref[:, hrows[hh]] = dn.T.astype(BF16)

    @pl.when(odd)
    def _():
        finish(1)

    @pl.when(jnp.logical_not(odd))
    def _():
        finish(0)


def _page_copy(hbm, buf, sem, kind, sl, p, page, page_rows):
    src = hbm.at[pl.ds(pl.multiple_of(page * page_rows, page_rows), page_rows)]
    return pltpu.make_async_copy(src, buf.at[sl, pl.ds(p * page_rows, page_rows)], sem.at[kind, sl])


def _attn_sample_prefetch(pt_ref, ck_hbm, cv_hbm, kbuf, vbuf, sem, seq, n_seq, *, n_pages, page_rows):
    def start_fetch(s, sl):
        for p in range(n_pages):
            page = pt_ref[s * n_pages + p]
            _page_copy(ck_hbm, kbuf, sem, 0, sl, p, page, page_rows).start()
            _page_copy(cv_hbm, vbuf, sem, 1, sl, p, page, page_rows).start()

    @pl.when(seq == 0)
    def _():
        start_fetch(0, 0)

    @pl.when(seq + 1 < n_seq)
    def _():
        start_fetch(seq + 1, 1 - seq % 2)


def _attn_sample_compute(lamv_ref, g_ref, slope2_ref, base2_ref, newmask_ref, q2_ref, kn_ref, vn_ref,
                         ck_hbm, cv_hbm, o_ref, kbuf, vbuf, sem, s_ref, seq,
                         *, n_pages, page_rows, n_heads, lam_init, chunk):
    slot = seq % 2
    past_rows = n_pages * page_rows
    nrow = q2_ref.shape[0] // 2

    for p in range(n_pages):
        _page_copy(ck_hbm, kbuf, sem, 0, slot, p, 0, page_rows).wait()
        _page_copy(cv_hbm, vbuf, sem, 1, slot, p, 0, page_rows).wait()

    pad = jnp.zeros((chunk - kn_ref.shape[0], HEAD_W), F32)
    kbuf[slot, past_rows:past_rows + chunk, :] = jnp.concatenate([kn_ref[...], pad], axis=0)
    vbuf[slot, past_rows:past_rows + chunk, :] = jnp.concatenate([vn_ref[...], pad], axis=0)

    q2 = q2_ref[...]
    slope2 = slope2_ref[...]
    n_pairs = past_rows // (2 * chunk)
    for a in range(n_pairs):
        r0 = 2 * a * chunk
        ke = kbuf[slot, r0:r0 + chunk, :].astype(BF16)
        ko = kbuf[slot, r0 + chunk:r0 + 2 * chunk, :].astype(BF16)
        s2 = _dot_nt(q2, jnp.concatenate([ke, ko], axis=1))
        tok0 = float(2 * a * (chunk // n_heads) - past_rows // n_heads)
        s_ref[:, a * chunk:(a + 1) * chunk] = s2 + base2_ref[...] + slope2 * tok0
    q = q2[0:nrow, 0:HEAD_W]
    kn = kbuf[slot, past_rows:past_rows + chunk, :].astype(BF16)
    s_new = _dot_nt(q, kn) + newmask_ref[...]

    s2 = s_ref[...]
    m2 = jnp.max(s2, axis=-1, keepdims=True)
    m = jnp.maximum(jnp.maximum(m2[0:nrow], m2[nrow:2 * nrow]), jnp.max(s_new, axis=-1, keepdims=True))
    p2 = jnp.exp2(s2 - jnp.concatenate([m, m], axis=0))
    p_new = jnp.exp2(s_new - m)
    l2 = jnp.sum(p2, axis=-1, keepdims=True)
    l = l2[0:nrow] + l2[nrow:2 * nrow] + jnp.sum(p_new, axis=-1, keepdims=True)
    p2b = p2.astype(BF16)
    acc2 = jnp.zeros((2 * nrow, 2 * HEAD_W), F32)
    for a in range(n_pairs):
        r0 = 2 * a * chunk
        ve = vbuf[slot, r0:r0 + chunk, :].astype(BF16)
        vo = vbuf[slot, r0 + chunk:r0 + 2 * chunk, :].astype(BF16)
        acc2 = acc2 + _dot(p2b[:, a * chunk:(a + 1) * chunk], jnp.concatenate([ve, vo], axis=1))
    vn = vbuf[slot, past_rows:past_rows + chunk, :].astype(BF16)
    acc = acc2[0:nrow, 0:HEAD_W] + acc2[nrow:2 * nrow, HEAD_W:2 * HEAD_W] + _dot(p_new.astype(BF16), vn)

    o = acc / l
    lam = _diff_lambda(lamv_ref[...], lam_init)
    half = nrow // 2
    d = o[0:half] - lam * o[half:nrow]
    o_ref[...] = _rms(d, g_ref[...]) * (1.0 - lam_init)


def _attn_kernel(pt_ref, sl_ref, lamv_ref, gcol_ref, pos_ref, qt_ref, k_ref, vt_ref,
                 grow_ref, slope2_ref, base2_ref, newmask_ref, q2_ref, kn_ref, vn_ref, ck_hbm, cv_hbm,
                 op_ref, os_ref,
                 qq_ref, m_ref, acc_ref, t0_ref, t1_ref, kbuf, vbuf, sem, s_ref,
                 *, tq, lam_init, n_pages, page_rows, n_heads, chunk):
    seq = (pl.program_id(0) * pl.num_programs(1) + pl.program_id(1)) * pl.num_programs(2) + pl.program_id(2)
    n_seq = pl.num_programs(0) * pl.num_programs(1) * pl.num_programs(2)
    _attn_sample_prefetch(pt_ref, ck_hbm, cv_hbm, kbuf, vbuf, sem, seq, n_seq,
                          n_pages=n_pages, page_rows=page_rows)
    _attn_prompt_body(sl_ref, lamv_ref, gcol_ref, pos_ref, qt_ref, k_ref, vt_ref, op_ref,
                      qq_ref, m_ref, acc_ref, t0_ref, t1_ref, tq=tq, lam_init=lam_init)
    _attn_sample_compute(lamv_ref, grow_ref, slope2_ref, base2_ref, newmask_ref, q2_ref, kn_ref, vn_ref,
                         ck_hbm, cv_hbm, os_ref, kbuf, vbuf, sem, s_ref, seq,
                         n_pages=n_pages, page_rows=page_rows, n_heads=n_heads, lam_init=lam_init, chunk=chunk)


def _attn(qt, kb, vt, slope_parts, lamv, g_col, page_table, q2, knf, vnf, ckf, cvf, g_row, slope2, base2,
          newmask, lam_init, tq):
    b, aw, t = qt.shape
    nh = aw // HEAD_W
    nq = t // tq
    db, nrow2, _ = q2.shape
    nrow = nrow2 // 2
    n_pages = page_table.shape[1]
    chunk = base2.shape[1]
    page_rows = PAGE * nh
    past_rows = n_pages * page_rows
    assert past_rows % (2 * chunk) == 0 and knf.shape[1] <= chunk
    assert db == b * nh * nq, "one decode sequence per prompt grid step"
    i = jnp.arange(tq)
    lane = jnp.arange(HEAD_W)
    feat = jnp.where(lane[None, :] < 2, (i % 256)[:, None], jnp.where(lane[None, :] < 4, (i // 256 * 256)[:, None], 0))
    pos = feat.astype(BF16)
    seq_of = lambda bi, hi, qi: (bi * nh + hi) * nq + qi
    per_seq = lambda a: pl.BlockSpec((None,) + a.shape[1:], lambda bi, hi, qi, pt: (seq_of(bi, hi, qi), 0, 0))
    const = lambda a: pl.BlockSpec(a.shape, lambda bi, hi, qi, pt: (0,) * a.ndim, pipeline_mode=pl.Buffered(1))
    grid_spec = pltpu.PrefetchScalarGridSpec(
        num_scalar_prefetch=1,
        grid=(b, nh, nq),
        in_specs=[
            pl.BlockSpec(memory_space=pltpu.SMEM),
            const(lamv), const(g_col), const(pos),
            pl.BlockSpec((None, HEAD_W, tq), lambda bi, hi, qi, pt: (bi, hi, qi)),
            pl.BlockSpec((None, t, HEAD_W), lambda bi, hi, qi, pt: (bi, 0, hi)),
            pl.BlockSpec((None, HEAD_W, t), lambda bi, hi, qi, pt: (bi, hi, 0)),
            const(g_row), const(slope2), const(base2), const(newmask),
            per_seq(q2), per_seq(knf), per_seq(vnf),
            pl.BlockSpec(memory_space=pl.ANY),
            pl.BlockSpec(memory_space=pl.ANY),
        ],
        out_specs=[
            pl.BlockSpec((None, tq, HEAD_W), lambda bi, hi, qi, pt: (bi, qi, hi)),
            pl.BlockSpec((None, nrow // 2, HEAD_W), lambda bi, hi, qi, pt: (seq_of(bi, hi, qi), 0, 0)),
        ],
        scratch_shapes=[
            pltpu.VMEM((1, 2 * HEAD_W, 2 * tq), BF16),
            pltpu.VMEM((1, 1, 2 * tq), F32),
            pltpu.VMEM((1, HEAD_W + 16, 2 * tq), F32),
            pltpu.VMEM((1, tq, 2 * tq), F32),
            pltpu.VMEM((1, tq, 2 * tq), F32),
            pltpu.VMEM((2, past_rows + chunk, HEAD_W), F32),
            pltpu.VMEM((2, past_rows + chunk, HEAD_W), F32),
            pltpu.SemaphoreType.DMA((2, 2)),
            pltpu.VMEM((nrow2, past_rows // 2), F32),
        ],
    )
    return pl.pallas_call(
        functools.partial(_attn_kernel, tq=tq, lam_init=lam_init, n_pages=n_pages, page_rows=page_rows,
                          n_heads=nh, chunk=chunk),
        grid_spec=grid_spec,
        out_shape=[jax.ShapeDtypeStruct((b, t, aw), BF16),
                   jax.ShapeDtypeStruct((db, nrow // 2, HEAD_W), F32)],
        compiler_params=_cparams(3),
        name="attn",
    )(page_table.reshape(-1), slope_parts, lamv, g_col, pos, qt, kb, vt,
      g_row, slope2, base2, newmask, q2, knf, vnf, ckf, cvf)


def _s5_prep_kernel(lre_ref, lim_ref, ldt_ref, bre_ref, bim_ref, cre_ref, cim_ref, dsk_ref,
                    wst_ref, cat_ref, kt_ref, sct_ref):
    lr = lre_ref[...]
    li = lim_ref[...]
    dt = jnp.exp(ldt_ref[...])

    def lam_bar_pow(n):
        mag = jnp.exp(n * (lr * dt))
        ang = n * (li * dt)
        return mag * jnp.cos(ang), mag * jnp.sin(ang)

    a_r, a_i = lam_bar_pow(1.0)
    den = lr * lr + li * li
    nr, ni = a_r - 1.0, a_i
    f_r = (nr * lr + ni * li) / den
    f_i = (ni * lr - nr * li) / den
    b_r, b_i = bre_ref[...], bim_ref[...]
    bb_r = f_r * b_r - f_i * b_i
    bb_i = f_r * b_i + f_i * b_r
    c_r, c_i = cre_ref[...], cim_ref[...]
    gw = b_r.shape[0]
    ns = b_r.shape[1]

    ktau = []
    for tau in range(S5_BLOCK):
        p_r, p_i = lam_bar_pow(float(tau))
        w_r = bb_r * p_r - bb_i * p_i
        w_i = bb_r * p_i + bb_i * p_r
        j = S5_BLOCK - 1 - tau
        wst_ref[j * gw:(j + 1) * gw, 0:ns] = w_r.astype(BF16)
        wst_ref[j * gw:(j + 1) * gw, ns:2 * ns] = w_i.astype(BF16)
        ktau.append(_dot_nt(w_r, c_r, lax.Precision.HIGHEST) - _dot_nt(w_i, c_i, lax.Precision.HIGHEST))
    rr = lax.broadcasted_iota(jnp.int32, (gw, gw), 0)
    cc = lax.broadcasted_iota(jnp.int32, (gw, gw), 1)
    ktau[0] = ktau[0] + jnp.where(rr == cc, dsk_ref[...], 0.0)

    for j in range(S5_BLOCK):
        p_r, p_i = lam_bar_pow(float(j + 1))
        cat_ref[j * gw:(j + 1) * gw, 0:ns] = (c_r * p_r - c_i * p_i).astype(BF16)
        cat_ref[j * gw:(j + 1) * gw, ns:2 * ns] = (-(c_r * p_i + c_i * p_r)).astype(BF16)

    zeros = jnp.zeros((gw, gw), BF16)
    for i in range(S5_BLOCK):
        for j in range(S5_BLOCK):
            blk = ktau[j - i].astype(BF16) if j >= i else zeros
            kt_ref[i * gw:(i + 1) * gw, j * gw:(j + 1) * gw] = blk

    row = lax.broadcasted_iota(jnp.int32, (8, ns), 0)
    for idx, k in enumerate((1, 2, 4)):
        p_r, p_i = lam_bar_pow(float(S5_BLOCK * k))
        sct_ref[2 * idx] = jnp.where(row >= k, p_r, 0.0)
        sct_ref[2 * idx + 1] = jnp.where(row >= k, p_i, 0.0)
    n = (S5_BLOCK * (row + 1)).astype(F32)
    mag = jnp.exp(n * (lr * dt))
    ang = n * (li * dt)
    sct_ref[6] = mag * jnp.cos(ang)
    sct_ref[7] = mag * jnp.sin(ang)


def _s5_prep(lre, lim, ldt, bre, bim, cre, cim, dsk):
    nq, gw, ns = bre.shape
    kw = S5_BLOCK * gw
    per_q = lambda *tail: pl.BlockSpec((None,) + tail, lambda q: (q,) + (0,) * len(tail))
    return pl.pallas_call(
        _s5_prep_kernel,
        grid=(nq,),
        in_specs=[per_q(1, ns), per_q(1, ns), per_q(1, ns), per_q(gw, ns), per_q(gw, ns),
                  per_q(gw, ns), per_q(gw, ns), per_q(1, gw)],
        out_specs=[per_q(kw, 2 * ns), per_q(kw, 2 * ns), per_q(kw, kw), per_q(8, 8, ns)],
        out_shape=[
            jax.ShapeDtypeStruct((nq, kw, 2 * ns), BF16),
            jax.ShapeDtypeStruct((nq, kw, 2 * ns), BF16),
            jax.ShapeDtypeStruct((nq, kw, kw), BF16),
            jax.ShapeDtypeStruct((nq, 8, 8, ns), F32),
        ],
        compiler_params=_cparams(1),
        name="s5_prep",
    )(lre, lim, ldt, bre, bim, cre, cim, dsk)


def _quarter_lanes(u_ref, q, gw, sw):
    return jnp.concatenate(
        [u_ref[:, j * sw + q * gw: j * sw + (q + 1) * gw] for j in range(S5_BLOCK)], axis=1)


def _store_quarter(y_ref, y, q, gw, sw):
    for j in range(S5_BLOCK):
        y_ref[:, j * sw + q * gw: j * sw + (q + 1) * gw] = y[:, j * gw:(j + 1) * gw].astype(y_ref.dtype)


def _s5_prompt_kernel(u_ref, wst_ref, cat_ref, kt_ref, sct_ref, y_ref, st_ref,
                      carry_ref, s_scr, hp_scr, *, rb, gw, sw, ns):
    blk = pl.program_id(1)

    @pl.when(blk == 0)
    def _():
        carry_ref[...] = jnp.zeros(carry_ref.shape, F32)

    row0 = lax.broadcasted_iota(jnp.int32, (8, ns), 0) == 0
    for q in range(S5_QUARTERS):
        uq = _quarter_lanes(u_ref, q, gw, sw)
        s_scr[...] = _dot(uq, wst_ref[q])
        tabs = [sct_ref[q, i] for i in range(8)]

        def tile(t, carry, tabs=tabs):
            cr, ci = carry
            r0 = pl.multiple_of(t * 8, 8)
            xr = s_scr[pl.ds(r0, 8), 0:ns]
            xi = s_scr[pl.ds(r0, 8), ns:2 * ns]
            for idx, k in enumerate((1, 2, 4)):
                mr, mi = tabs[2 * idx], tabs[2 * idx + 1]
                sr = pltpu.roll(xr, k, 0)
                si = pltpu.roll(xi, k, 0)
                xr, xi = xr + (mr * sr - mi * si), xi + (mr * si + mi * sr)
            crb = jnp.broadcast_to(cr, (8, ns))
            cib = jnp.broadcast_to(ci, (8, ns))
            hr = xr + (tabs[6] * crb - tabs[7] * cib)
            hi = xi + (tabs[6] * cib + tabs[7] * crb)
            hp_scr[pl.ds(r0, 8), 0:ns] = jnp.where(row0, crb, pltpu.roll(hr, 1, 0))
            hp_scr[pl.ds(r0, 8), ns:2 * ns] = jnp.where(row0, cib, pltpu.roll(hi, 1, 0))
            return hr[7:8, :], hi[7:8, :]

        cr, ci = lax.fori_loop(0, rb // 8, tile,
                               (carry_ref[q, :, 0:ns], carry_ref[q, :, ns:2 * ns]))
        carry_ref[q, :, 0:ns] = cr
        carry_ref[q, :, ns:2 * ns] = ci
        st_ref[q, :, 0:ns] = cr
        st_ref[q, :, ns:2 * ns] = ci
        y = _dot_nt(hp_scr[...].astype(BF16), cat_ref[q]) + _dot(uq, kt_ref[q])
        _store_quarter(y_ref, y, q, gw, sw)


def _s5_prompt(u4, wst, cat, kt, sct, rb):
    b, rows, lanes = u4.shape
    nq, kw, ns2 = wst.shape
    ns, gw, sw = ns2 // 2, kw // S5_BLOCK, lanes // S5_BLOCK
    return pl.pallas_call(
        functools.partial(_s5_prompt_kernel, rb=rb, gw=gw, sw=sw, ns=ns),
        grid=(b, rows // rb),
        in_specs=[
            pl.BlockSpec((None, rb, lanes), lambda bi, ri: (bi, ri, 0)),
            _const_spec(wst.shape), _const_spec(cat.shape), _const_spec(kt.shape), _const_spec(sct.shape),
        ],
        out_specs=[
            pl.BlockSpec((None, rb, lanes), lambda bi, ri: (bi, ri, 0)),
            pl.BlockSpec((None, nq, 1, ns2), lambda bi, ri: (bi, 0, 0, 0)),
        ],
        out_shape=[
            jax.ShapeDtypeStruct((b, rows, lanes), BF16),
            jax.ShapeDtypeStruct((b, nq, 1, ns2), F32),
        ],
        scratch_shapes=[
            pltpu.VMEM((nq, 1, ns2), F32),
            pltpu.VMEM((rb, ns2), F32),
            pltpu.VMEM((rb, ns2), F32),
        ],
        compiler_params=_cparams(2),
        name="s5_prompt",
    )(u4, wst, cat, kt, sct)


def _s5_sample_kernel(u_ref, h0_ref, wst_ref, cat_ref, kt_ref, sct_ref, y_ref, hout_ref, *, gw, sw, ns):
    for q in range(S5_QUARTERS):
        uq = _quarter_lanes(u_ref, q, gw, sw)
        s = _dot(uq, wst_ref[q])
        h0 = h0_ref[q]
        h0r, h0i = h0[:, 0:ns], h0[:, ns:2 * ns]
        a4r = sct_ref[q, 6][0:1, :]
        a4i = sct_ref[q, 7][0:1, :]
        hout_ref[q, :, 0:ns] = s[:, 0:ns] + (a4r * h0r - a4i * h0i)
        hout_ref[q, :, ns:2 * ns] = s[:, ns:2 * ns] + (a4r * h0i + a4i * h0r)
        y = _dot_nt(h0.astype(BF16), cat_ref[q]) + _dot(uq, kt_ref[q])
        _store_quarter(y_ref, y, q, gw, sw)


def _s5_sample(u4, h0q, wst, cat, kt, sct):
    rows, lanes = u4.shape
    nq, kw, ns2 = wst.shape
    ns, gw, sw = ns2 // 2, kw // S5_BLOCK, lanes // S5_BLOCK
    return pl.pallas_call(
        functools.partial(_s5_sample_kernel, gw=gw, sw=sw, ns=ns),
        grid=(1,),
        in_specs=[_const_spec(u4.shape), _const_spec(h0q.shape), _const_spec(wst.shape),
                  _const_spec(cat.shape), _const_spec(kt.shape), _const_spec(sct.shape)],
        out_specs=[pl.BlockSpec(u4.shape, lambda i: (0, 0)),
                   pl.BlockSpec(h0q.shape, lambda i: (0, 0, 0))],
        out_shape=[jax.ShapeDtypeStruct(u4.shape, BF16), jax.ShapeDtypeStruct(h0q.shape, F32)],
        compiler_params=_cparams(1),
        name="s5_sample",
    )(u4, h0q, wst, cat, kt, sct)


def _mem_kv_kernel(x_ref, g_ref, wk_ref, wv_ref, k_ref, v_ref, kb_ref, vb_ref):
    hn = _rms(x_ref[...], g_ref[...]).astype(BF16)
    k = _dot(hn, wk_ref[...])
    k_ref[...] = k
    kb_ref[...] = k.astype(BF16)
    v = _dot(hn, wv_ref[...])
    v_ref[...] = v
    vb_ref[...] = v.astype(BF16)


def _mem_kv(mem, g, wk_bf, wv_bf, tm):
    n, d = mem.shape
    row = pl.BlockSpec((tm, d), lambda i: (i, 0))
    return pl.pallas_call(
        _mem_kv_kernel,
        grid=(n // tm,),
        in_specs=[row, _const_spec((1, d)), _const_spec(wk_bf.shape), _const_spec(wv_bf.shape)],
        out_specs=[row, row, row, row],
        out_shape=[jax.ShapeDtypeStruct((n, d), F32), jax.ShapeDtypeStruct((n, d), F32),
                   jax.ShapeDtypeStruct((n, d), BF16), jax.ShapeDtypeStruct((n, d), BF16)],
        compiler_params=_cparams(1),
        name="mem_kv",
    )(mem, g, wk_bf, wv_bf)


def _merge_and_query(x, att, sy4_ref, sy_scr, wglu_ref, gso_ref, wout_ref, gpost_ref, gmpre_ref, wmq_ref,
                     mem_scale):
    aw = att.shape[1]
    rows, n_lb = sy4_ref.shape[0], sy_scr.shape[0]
    sw = n_lb * LANES
    for j in range(S5_BLOCK):
        for c in range(n_lb):
            blk = sy4_ref[:, j * sw + c * LANES:j * sw + (c + 1) * LANES].astype(F32)
            sy_scr[c, pl.ds(j, rows, stride=S5_BLOCK), :] = blk
    y = jax.nn.gelu(jnp.concatenate([sy_scr[c] for c in range(n_lb)], axis=1))
    y = y * jax.nn.sigmoid(_dot(y.astype(BF16), wglu_ref[...]))
    y = _rms(y, gso_ref[...])
    mix = _dot(att, wout_ref[0:aw, :]) + _dot(y.astype(BF16), wout_ref[aw:, :])
    x1 = x + _rms(mix, gpost_ref[...])
    hq = _rms(x1, gmpre_ref[...]).astype(BF16)
    qm = (_dot(hq, wmq_ref[...]) * mem_scale).astype(BF16)
    return x1, qm


def _mix_mem_body(x_ref, att_ref, sy_ref, mk_ref, mv_ref, wglu_ref, gso_ref, wout_ref, gpost_ref,
                  gmpre_ref, wmq_ref, wmo_ref, gmpost_ref, o_ref, oh_ref, sy_scr, *, n_heads, mem_scale):
    x1, qm = _merge_and_query(x_ref[...], att_ref[...], sy_ref, sy_scr, wglu_ref, gso_ref, wout_ref,
                              gpost_ref, gmpre_ref, wmq_ref, mem_scale)
    hd = qm.shape[1] // n_heads
    for h in range(n_heads):
        sl = slice(h * hd, (h + 1) * hd)
        s = _dot_nt(qm[:, sl], mk_ref[:, sl])
        p = jnp.exp(s - jnp.max(s, axis=-1, keepdims=True))
        l = jnp.sum(p, axis=-1, keepdims=True)
        oh_ref[:, sl] = (_dot(p.astype(BF16), mv_ref[:, sl]) / l).astype(BF16)
    mo = _dot(oh_ref[...], wmo_ref[...])
    o_ref[...] = x1 + _rms(mo, gmpost_ref[...])


def _mix_q_kernel(x_ref, att_ref, sy_ref, wglu_ref, gso_ref, wout_ref, gpost_ref, gmpre_ref, wmq_ref,
                  x1_ref, qm_ref, sy_scr, *, mem_scale):
    x1, qm = _merge_and_query(x_ref[...], att_ref[...], sy_ref, sy_scr, wglu_ref, gso_ref, wout_ref,
                              gpost_ref, gmpre_ref, wmq_ref, mem_scale)
    x1_ref[...] = x1
    qm_ref[...] = qm


def _mix_q(x, att, sy4, w, n_heads):
    n, d = x.shape
    weights = [w["w_glu"], w["g_ssm_out"], w["w_out"], w["g_mix_post"], w["g_mem_pre"], w["w_mq"]]
    ins = [x, att, sy4] + weights
    return pl.pallas_call(
        functools.partial(_mix_q_kernel, mem_scale=(d // n_heads) ** -0.5),
        grid=(1,),
        in_specs=[_const_spec(a.shape) for a in ins],
        out_specs=[pl.BlockSpec((n, d), lambda i: (0, 0)), pl.BlockSpec((n, d), lambda i: (0, 0))],
        out_shape=[jax.ShapeDtypeStruct((n, d), F32), jax.ShapeDtypeStruct((n, d), BF16)],
        scratch_shapes=[pltpu.VMEM((sy4.shape[1] // S5_BLOCK // LANES, n, LANES), F32)],
        compiler_params=_cparams(1),
        name="mix_q",
    )(*ins)


def _mem_sample_copies(ck_hbm, cv_hbm, kbuf, vbuf, sem, step, n_per, n_mem, n_heads):
    sl = step % 2
    out = []
    for j in range(n_per):
        for kind, (hbm, buf) in enumerate(((ck_hbm, kbuf), (cv_hbm, vbuf))):
            for lb in range(kbuf.shape[2]):
                src = hbm.at[step * n_per + j, :, :, pl.ds(lb * LANES, LANES)]
                dst = buf.at[sl, j, lb].reshape(n_mem, n_heads, LANES)
                out.append(pltpu.make_async_copy(src, dst, sem.at[kind, sl]))
    return out


def _mem_sample_compute(mask_ref, q_ref, o_ref, kbuf, vbuf, slot, j):
    n_halves = kbuf.shape[2]
    kcat = jnp.concatenate([kbuf[slot, j, lb] for lb in range(n_halves)], axis=1).astype(BF16)
    vcat = jnp.concatenate([vbuf[slot, j, lb] for lb in range(n_halves)], axis=1).astype(BF16)
    s = _dot_nt(q_ref[j], kcat) + mask_ref[...]
    p = jnp.exp(s - jnp.max(s, axis=-1, keepdims=True))
    l = jnp.sum(p, axis=-1, keepdims=True)
    o_ref[j] = _dot(p.astype(BF16), vcat) / l


def _mix_mem_kernel(x_ref, att_ref, sy_ref, mk_ref, mv_ref, wglu_ref, gso_ref, wout_ref, gpost_ref,
                    gmpre_ref, wmq_ref, wmo_ref, gmpost_ref, mask_ref, qs_ref, ck_hbm, cv_hbm,
                    o_ref, os_ref, oh_ref, sy_scr, kbuf, vbuf, sem, *, n_heads, mem_scale, n_mem):
    step = pl.program_id(0) * pl.num_programs(1) + pl.program_id(1)
    n_steps = pl.num_programs(0) * pl.num_programs(1)
    n_per = qs_ref.shape[0]
    copies = lambda st: _mem_sample_copies(ck_hbm, cv_hbm, kbuf, vbuf, sem, st, n_per, n_mem, n_heads)

    @pl.when(step == 0)
    def _():
        for cp in copies(0):
            cp.start()

    @pl.when(step + 1 < n_steps)
    def _():
        for cp in copies(step + 1):
            cp.start()

    _mix_mem_body(x_ref, att_ref, sy_ref, mk_ref, mv_ref, wglu_ref, gso_ref, wout_ref, gpost_ref,
                  gmpre_ref, wmq_ref, wmo_ref, gmpost_ref, o_ref, oh_ref, sy_scr,
                  n_heads=n_heads, mem_scale=mem_scale)

    for cp in copies(step):
        cp.wait()
    for j in range(n_per):
        _mem_sample_compute(mask_ref, qs_ref, os_ref, kbuf, vbuf, step % 2, j)


def _mix_mem(x, att, sy4, mkb, mvb, w, qh, ck, cv, mask, tm, n_heads):
    b, t, d = x.shape
    aw, sw = att.shape[2], sy4.shape[2] // S5_BLOCK
    nm = mkb.shape[1]
    nt = t // tm
    db, nrow, hd = qh.shape
    n_ckeys = ck.shape[1]
    n_per = db // (b * nt)
    assert db == n_per * b * nt, "decode sequences must divide evenly over the prompt grid steps"
    tok = lambda width: pl.BlockSpec((None, tm, width), lambda bi, ti: (bi, ti, 0))
    tok4 = pl.BlockSpec((None, tm // S5_BLOCK, S5_BLOCK * sw), lambda bi, ti: (bi, ti, 0))
    per_b = pl.BlockSpec((None, nm, d), lambda bi, ti: (bi, 0, 0))
    per_step = pl.BlockSpec((n_per, nrow, hd), lambda bi, ti: (bi * nt + ti, 0, 0))
    any_spec = pl.BlockSpec(memory_space=pl.ANY)
    weights = [w["w_glu"], w["g_ssm_out"], w["w_out"], w["g_mix_post"], w["g_mem_pre"], w["w_mq"],
               w["w_mo"], w["g_mem_post"], mask]
    return pl.pallas_call(
        functools.partial(_mix_mem_kernel, n_heads=n_heads, mem_scale=(d // n_heads) ** -0.5, n_mem=n_ckeys),
        grid=(b, nt),
        in_specs=[tok(d), tok(aw), tok4, per_b, per_b] + [_const_spec(a.shape) for a in weights]
                 + [per_step, any_spec, any_spec],
        out_specs=[tok(d), per_step],
        out_shape=[jax.ShapeDtypeStruct((b, t, d), F32), jax.ShapeDtypeStruct((db, nrow, hd), F32)],
        scratch_shapes=[
            pltpu.VMEM((tm, d), BF16),
            pltpu.VMEM((sw // LANES, tm, LANES), F32),
            pltpu.VMEM((2, n_per, hd // LANES, n_ckeys * n_heads, LANES), F32),
            pltpu.VMEM((2, n_per, hd // LANES, n_ckeys * n_heads, LANES), F32),
            pltpu.SemaphoreType.DMA((2, 2)),
        ],
        compiler_params=_cparams(2),
        name="mix_mem",
    )(x, att, sy4, mkb, mvb, *weights, qh, ck, cv)


def _mem_out_kernel(x1_ref, o_ref, wmo_ref, g_ref, y_ref):
    mo = _dot(o_ref[...], wmo_ref[...])
    y_ref[...] = x1_ref[...] + _rms(mo, g_ref[...])


def _mem_out(x1, o, wmo, g):
    n, d = x1.shape
    ins = [x1, o, wmo, g]
    return pl.pallas_call(
        _mem_out_kernel,
        grid=(1,),
        in_specs=[_const_spec(a.shape) for a in ins],
        out_specs=pl.BlockSpec((n, d), lambda i: (0, 0)),
        out_shape=jax.ShapeDtypeStruct((n, d), F32),
        compiler_params=_cparams(1),
        name="mem_out",
    )(*ins)


def _ffn_kernel(x_ref, gpre_ref, wg_ref, wu_ref, wd_ref, gpost_ref, o_ref, *, n_chunks):
    x = x_ref[...]
    hn = _rms(x, gpre_ref[...]).astype(BF16)
    ff = wg_ref.shape[1]
    cw = ff // n_chunks
    acc = jnp.zeros(x.shape, F32)
    for c in range(n_chunks):
        sl = slice(c * cw, (c + 1) * cw)
        a = jax.nn.silu(_dot(hn, wg_ref[:, sl])) * _dot(hn, wu_ref[:, sl])
        acc = acc + _dot(a.astype(BF16), wd_ref[sl, :])
    o_ref[...] = x + _rms(acc, gpost_ref[...])


def _ffn(x, gpre, wg, wu, wd, gpost, tm, n_chunks):
    n, d = x.shape
    row = pl.BlockSpec((tm, d), lambda i: (i, 0))
    weights = [gpre, wg, wu, wd, gpost]
    return pl.pallas_call(
        functools.partial(_ffn_kernel, n_chunks=n_chunks),
        grid=(n // tm,),
        in_specs=[row] + [_const_spec(a.shape) for a in weights],
        out_specs=row,
        out_shape=jax.ShapeDtypeStruct((n, d), F32),
        compiler_params=_cparams(1),
        name="ffn",
    )(x, *weights)


def _s5_param_layout(lam_re, lam_im, log_dt, b_re, b_im, c_re, c_im, d_skip):
    g, p, hc = b_re.shape
    nq = S5_QUARTERS
    gq = g // nq
    rowv = lambda a: a.reshape(nq, 1, gq * p)
    ldt = jnp.broadcast_to(log_dt[:, None], (g, p))
    eye = jnp.eye(gq, dtype=F32)

    def embed_b(b):
        bq = b.reshape(nq, gq, p, hc)
        return jnp.einsum("qgph,gk->qghkp", bq, eye).reshape(nq, gq * hc, gq * p)

    def embed_c(c):
        cq = c.reshape(nq, gq, hc, p)
        return jnp.einsum("qghp,gk->qghkp", cq, eye).reshape(nq, gq * hc, gq * p)

    return (rowv(lam_re), rowv(lam_im), rowv(ldt), embed_b(b_re), embed_b(b_im),
            embed_c(c_re), embed_c(c_im), d_skip.reshape(nq, 1, gq * hc))


def _state_to_quarters(re, im):
    n, g, p = re.shape
    gq = g // S5_QUARTERS
    f = lambda a: a.reshape(n, S5_QUARTERS, gq * p).transpose(1, 0, 2)
    return jnp.concatenate([f(re), f(im)], axis=-1)


def _state_from_quarters(st, g, p):
    n = st.shape[0]
    ns = st.shape[-1] // 2
    return st[..., :ns].reshape(n, g, p), st[..., ns:].reshape(n, g, p)


def kernel(x_prompt, x_sample, mem_prompt, cache_k, cache_v, page_table, state_ssm_re, state_ssm_im,
           cache_mem_k, cache_mem_v, g_mix_pre, g_mix_post, w_in, lam_q1, lam_k1, lam_q2, lam_k2, g_subln,
           ssm_lam_re, ssm_lam_im, ssm_log_dt, ssm_b_re, ssm_b_im, ssm_c_re, ssm_c_im, ssm_d, w_glu, g_ssm_out,
           w_out, g_mem_pre, g_mem_post, g_mem_kv, w_mq, w_mk, w_mv, w_mo, g_ffn_pre, g_ffn_post,
           w_gate, w_up, w_down):
    depth = w_in.shape[0]
    b, t, d = x_prompt.shape
    db, dt_, _ = x_sample.shape
    n_mem = mem_prompt.shape[1]
    n_mem_heads = cache_mem_k.shape[3]
    n_heads = cache_k.shape[3]
    aw = n_heads * HEAD_W
    g, p = ssm_lam_re.shape[1], ssm_lam_re.shape[2]
    sw = g * SSM_GROUP
    n_phys = cache_k.shape[1]
    assert dt_ == S5_BLOCK and t % (S5_BLOCK * 256) == 0

    slopes = LOG2E * 2.0 ** (-8.0 * jnp.arange(1, n_heads + 1, dtype=F32) / n_heads)
    slope_hi = slopes.astype(BF16).astype(F32)
    slope_parts = jnp.stack([slopes, slope_hi, (slopes - slope_hi).astype(BF16).astype(F32)])
    row = lambda a: a.reshape(1, -1).astype(F32)
    bf = lambda a: a.astype(BF16)

    branch_lane = jnp.arange(HEAD_W) // HEAD_DIM
    nrow = 2 * n_heads * dt_
    chunk = 256
    r = jnp.arange(nrow)
    r_head, r_tok = (r // dt_) % n_heads, r % dt_
    r_slope = slopes[r_head][:, None]
    c = jnp.arange(chunk)
    c_tok, c_head = c // n_heads, c % n_heads
    own_head = c_head[None, :] == r_head[:, None]
    base = r_slope * c_tok[None, :].astype(F32) + jnp.where(own_head, 0.0, NEG)
    base2 = jnp.concatenate([base, base + r_slope * float(chunk // n_heads)], axis=0)
    slope2 = jnp.concatenate([r_slope, r_slope], axis=0)
    new_ok = own_head & (c[None, :] < dt_ * n_heads) & (c_tok[None, :] <= r_tok[:, None])
    newmask = jnp.where(new_ok, r_slope * c_tok[None, :].astype(F32), NEG)
    mem_row_head = jnp.arange(n_mem_heads * dt_) // dt_
    mem_col_head = jnp.arange(n_mem * n_mem_heads) % n_mem_heads
    mem_mask = jnp.where(mem_col_head[None, :] == mem_row_head[:, None], 0.0, NEG).astype(F32)

    y_p = x_prompt
    y_s = x_sample.reshape(db * dt_, d)
    outs = {k: [] for k in ("kp", "vp", "ks", "vs", "rp", "ip", "rs", "is", "mkp", "mvp")}
    for l in range(depth):
        lam_init = 0.8 - 0.6 * math.exp(-0.3 * l)
        lamv = jnp.stack([lam_q1[l], lam_k1[l], lam_q2[l], lam_k2[l]]).astype(F32)
        gsub = row(g_subln[l])
        w_in_bf = bf(w_in[l])
        w = {
            "w_glu": bf(w_glu[l]), "g_ssm_out": row(g_ssm_out[l]), "w_out": bf(w_out[l]),
            "g_mix_post": row(g_mix_post[l]), "g_mem_pre": row(g_mem_pre[l]), "w_mq": bf(w_mq[l]),
            "w_mo": bf(w_mo[l]), "g_mem_post": row(g_mem_post[l]),
        }

        qp, kp, vp, kpb, vpb, up = _proj(y_p.reshape(b * t, d), row(g_mix_pre[l]), w_in_bf, aw, 512)
        qs, ks, vs, _, _, us = _proj(y_s, row(g_mix_pre[l]), w_in_bf, aw, 512)

        qh = qs.reshape(db, dt_, n_heads, HEAD_W).transpose(0, 2, 1, 3)
        zq = jnp.zeros((), BF16)
        q32 = jnp.stack([jnp.where(branch_lane == 0, qh, zq), jnp.where(branch_lane == 1, qh, zq)], axis=1)
        q32 = q32.reshape(db, nrow, HEAD_W)
        zq32 = jnp.zeros_like(q32)
        q2 = jnp.concatenate([jnp.concatenate([q32, zq32], axis=2), jnp.concatenate([zq32, q32], axis=2)], axis=1)
        att_p, att_s16 = _attn(jnp.swapaxes(qp.reshape(b, t, aw), 1, 2), kpb.reshape(b, t, aw),
                               jnp.swapaxes(vpb.reshape(b, t, aw), 1, 2),
                               slope_parts, lamv, g_subln[l].reshape(HEAD_W, 1).astype(F32),
                               page_table, q2, ks.reshape(db, dt_ * n_heads, HEAD_W),
                               vs.reshape(db, dt_ * n_heads, HEAD_W),
                               cache_k[l].reshape(n_phys * PAGE * n_heads, HEAD_W),
                               cache_v[l].reshape(n_phys * PAGE * n_heads, HEAD_W),
                               gsub, slope2, base2, newmask, lam_init, 512)
        att_s = bf(att_s16.reshape(db, n_heads, dt_, HEAD_W).transpose(0, 2, 1, 3).reshape(db * dt_, aw))

        tabs = _s5_prep(*_s5_param_layout(ssm_lam_re[l], ssm_lam_im[l], ssm_log_dt[l], ssm_b_re[l],
                                          ssm_b_im[l], ssm_c_re[l], ssm_c_im[l], ssm_d[l]))
        syp, stp = _s5_prompt(up.reshape(b, t // S5_BLOCK, S5_BLOCK * sw), *tabs, 256)
        rp, ip = _state_from_quarters(stp.reshape(b, S5_QUARTERS, -1), g, p)
        sys_, sts = _s5_sample(us, _state_to_quarters(state_ssm_re[l], state_ssm_im[l]), *tabs)
        rs, is_ = _state_from_quarters(sts.transpose(1, 0, 2), g, p)

        mkp, mvp, mkb, mvb = _mem_kv(mem_prompt.reshape(b * n_mem, d), row(g_mem_kv[l]),
                                     bf(w_mk[l]), bf(w_mv[l]), 512)
        x1s, qms = _mix_q(y_s, att_s, sys_, w, n_mem_heads)
        mhd = d // n_mem_heads
        qmh = qms.reshape(db, dt_, n_mem_heads, mhd).transpose(0, 2, 1, 3).reshape(db, n_mem_heads * dt_, mhd)
        y_p, omh = _mix_mem(y_p, att_p, syp, mkb.reshape(b, n_mem, d), mvb.reshape(b, n_mem, d), w,
                            qmh, cache_mem_k[l], cache_mem_v[l], mem_mask, 512, n_mem_heads)
        om = bf(omh.reshape(db, n_mem_heads, dt_, mhd).transpose(0, 2, 1, 3).reshape(db * dt_, d))
        y_s = _mem_out(x1s, om, w["w_mo"], w["g_mem_post"])

        ffw = (row(g_ffn_pre[l]), bf(w_gate[l]), bf(w_up[l]), bf(w_down[l]), row(g_ffn_post[l]))
        y_p = _ffn(y_p.reshape(b * t, d), *ffw, 512, 2).reshape(b, t, d)
        y_s = _ffn(y_s, *ffw, 512, 2)

        outs["kp"].append(kp.reshape(b, t, n_heads, HEAD_W))
        outs["vp"].append(vp.reshape(b, t, n_heads, HEAD_W))
        outs["ks"].append(ks.reshape(db, dt_, n_heads, HEAD_W))
        outs["vs"].append(vs.reshape(db, dt_, n_heads, HEAD_W))
        outs["rp"].append(rp); outs["ip"].append(ip); outs["rs"].append(rs); outs["is"].append(is_)
        outs["mkp"].append(mkp.reshape(b, n_mem, n_mem_heads, d // n_mem_heads))
        outs["mvp"].append(mvp.reshape(b, n_mem, n_mem_heads, d // n_mem_heads))

    st = lambda k: jnp.stack(outs[k], 0)
    return (y_p, y_s.reshape(db, dt_, d), st("kp"), st("vp"), st("ks"), st("vs"),
            st("rp"), st("ip"), st("rs"), st("is"), st("mkp"), st("mvp"))
```

```python
import functools
import math

import jax
import jax.numpy as jnp
from jax import lax
from jax.experimental import pallas as pl
from jax.experimental.pallas import tpu as pltpu

F32 = jnp.float32
BF16 = jnp.bfloat16

RMS_EPS = 1e-6
HEAD_DIM = 64
HEAD_W = 2 * HEAD_DIM
SSM_GROUP = 16
SSM_STATE = 64
S5_BLOCK = 4
S5_QUARTERS = 4
PAGE = 128
LANES = 128
NEG = -1e30
LOG2E = 1.4426950408889634
VMEM_LIMIT_BYTES = 56 * 1024 * 1024


def _cparams(n_grid_dims):
    return pltpu.CompilerParams(
        dimension_semantics=("arbitrary",) * n_grid_dims,
        vmem_limit_bytes=VMEM_LIMIT_BYTES)


def _const_spec(shape):
    nd = len(shape)
    return pl.BlockSpec(shape, lambda *_: (0,) * nd, pipeline_mode=pl.Buffered(1))


def _rms(x, g):
    ms = jnp.mean(x * x, axis=-1, keepdims=True)
    return x * lax.rsqrt(ms + RMS_EPS) * g


def _dot(a, b):
    return jnp.dot(a, b, preferred_element_type=F32)


def _dot_nt(a, b, precision=None):
    return lax.dot_general(a, b, (((1,), (1,)), ((), ())),
                           preferred_element_type=F32, precision=precision)


def _proj_kernel(x_ref, g_ref, w_ref, q_ref, k_ref, v_ref, kb_ref, vb_ref, u_ref, u_scr, *, aw, tm):
    hn = _rms(x_ref[...], g_ref[...]).astype(BF16)
    q = _dot(hn, w_ref[:, 0:aw])
    q_ref[...] = (q * (HEAD_DIM ** -0.5 * LOG2E)).astype(BF16)
    nh = aw // HEAD_W
    k = _dot(hn, w_ref[:, aw:2 * aw])
    kb_ref[...] = k.astype(BF16)
    v = _dot(hn, w_ref[:, 2 * aw:3 * aw])
    vb_ref[...] = v.astype(BF16)
    for h in range(nh):
        k_ref[pl.ds(h, tm, stride=nh), :] = k[:, h * HEAD_W:(h + 1) * HEAD_W]
        v_ref[pl.ds(h, tm, stride=nh), :] = v[:, h * HEAD_W:(h + 1) * HEAD_W]
    u = _dot(hn, w_ref[:, 3 * aw:])
    n_lb = u_scr.shape[0]
    sw = n_lb * LANES
    for c in range(n_lb):
        u_scr[c] = u[:, c * LANES:(c + 1) * LANES]
    for j in range(S5_BLOCK):
        for c in range(n_lb):
            rows = u_scr[c, pl.ds(j, tm // S5_BLOCK, stride=S5_BLOCK), :]
            u_ref[:, j * sw + c * LANES:j * sw + (c + 1) * LANES] = rows.astype(BF16)


def _proj(x, g, w_bf, aw, tm):
    n, d = x.shape
    sw = w_bf.shape[1] - 3 * aw
    nh = aw // HEAD_W
    row = lambda width: pl.BlockSpec((tm, width), lambda i: (i, 0))
    flat = pl.BlockSpec((tm * nh, HEAD_W), lambda i: (i, 0))
    return pl.pallas_call(
        functools.partial(_proj_kernel, aw=aw, tm=tm),
        grid=(n // tm,),
        in_specs=[row(d), _const_spec((1, d)), _const_spec(w_bf.shape)],
        out_specs=[row(aw), flat, flat, row(aw), row(aw),
                   pl.BlockSpec((tm // S5_BLOCK, S5_BLOCK * sw), lambda i: (i, 0))],
        out_shape=[
            jax.ShapeDtypeStruct((n, aw), BF16),
            jax.ShapeDtypeStruct((n * nh, HEAD_W), F32),
            jax.ShapeDtypeStruct((n * nh, HEAD_W), F32),
            jax.ShapeDtypeStruct((n, aw), BF16),
            jax.ShapeDtypeStruct((n, aw), BF16),
            jax.ShapeDtypeStruct((n // S5_BLOCK, S5_BLOCK * sw), BF16),
        ],
        scratch_shapes=[pltpu.VMEM((sw // LANES, tm, LANES), F32)],
        compiler_params=_cparams(1),
        name="proj",
    )(x, g, w_bf)


def _diff_lambda(lamv, lam_init):
    d1 = jnp.sum(lamv[0:1] * lamv[1:2], axis=-1, keepdims=True)
    d2 = jnp.sum(lamv[2:3] * lamv[3:4], axis=-1, keepdims=True)
    return jnp.exp(d1) - jnp.exp(d2) + lam_init


def _attn_prompt_body(sl_ref, lamv_ref, g_ref, pos_ref, qt_ref, k_ref, vt_ref, o_ref,
                        qq_ref, m_ref, acc_ref, t0_ref, t1_ref, *, tq, lam_init):
    hp = pl.program_id(1)
    qi = pl.program_id(2)
    tk = tq
    n_hh = qq_ref.shape[0]
    heads = [hp * n_hh + hh for hh in range(n_hh)]
    hrows = [slice(hh * HEAD_W, (hh + 1) * HEAD_W) for hh in range(n_hh)]

    def prep_queries(qblk):
        q0 = pl.multiple_of(qblk * tq, tq)
        arow = lax.broadcasted_iota(jnp.int32, (HEAD_W, 2 * tq), 0)
        for hh in range(n_hh):
            qt = qt_ref[hrows[hh], pl.ds(q0, tq)].astype(F32)
            frow = lax.broadcasted_iota(jnp.int32, qt.shape, 0)
            qq_ref[hh, 0:HEAD_W, 0:tq] = jnp.where(frow < HEAD_DIM, qt, 0.0).astype(BF16)
            qq_ref[hh, 0:HEAD_W, tq:2 * tq] = jnp.where(frow >= HEAD_DIM, qt, 0.0).astype(BF16)
            slope_parts = jnp.where((arow & 1) == 0, sl_ref[1, heads[hh]], sl_ref[2, heads[hh]])
            qq_ref[hh, HEAD_W:2 * HEAD_W, :] = jnp.where(arow < 4, slope_parts, 0.0).astype(BF16)

    m_ref[...] = jnp.full(m_ref.shape, NEG, F32)
    acc_ref[...] = jnp.zeros(acc_ref.shape, F32)
    ones = jnp.ones((acc_ref.shape[1] - HEAD_W, tk), BF16)

    def scores(hh, kj):
        k0 = pl.multiple_of(kj * tk, tk)
        ka = jnp.concatenate([k_ref[pl.ds(k0, tk), hrows[hh]], pos_ref[...]], axis=1)
        return _dot(ka, qq_ref[hh])

    def consume(hh, kj, t):
        k0 = pl.multiple_of(kj * tk, tk)
        vta = jnp.concatenate([vt_ref[hrows[hh], pl.ds(k0, tk)], ones], axis=0)
        off = sl_ref[0, heads[hh]] * jnp.full((1, 2 * tq), (kj - qi) * tk, jnp.int32).astype(F32)
        m_old = m_ref[hh]
        m_new = jnp.maximum(m_old, jnp.max(t, axis=0, keepdims=True) + off)
        alpha = jnp.exp2(m_old - m_new)
        p = jnp.exp2(t - (m_new - off)).astype(BF16)
        acc_ref[hh] = alpha * acc_ref[hh] + _dot(vta, p)
        m_ref[hh] = m_new

    tbuf = (t0_ref, t1_ref)

    def step(kj, cur):
        for hh in range(n_hh):
            tbuf[1 - cur][hh] = scores(hh, kj + 1)
        for hh in range(n_hh):
            consume(hh, kj, tbuf[cur][hh])

    @pl.when(qi == 0)
    def _():
        prep_queries(0)
        for hh in range(n_hh):
            t0_ref[hh] = scores(hh, 0)

    def body(i, carry):
        step(2 * i, 0)
        step(2 * i + 1, 1)
        return carry

    lax.fori_loop(0, qi // 2, body, 0)
    odd = qi % 2 == 1

    @pl.when(odd)
    def _():
        step(qi - 1, 0)

    def finish(cur):
        krow = lax.broadcasted_iota(jnp.int32, (tk, 2 * tq), 0)
        qcol = lax.broadcasted_iota(jnp.int32, (tk, 2 * tq), 1)
        qcol = jnp.where(qcol >= tq, qcol - tq, qcol)
        lam = _diff_lambda(lamv_ref[...], lam_init)
        for hh in range(n_hh):
            consume(hh, qi, jnp.where(krow <= qcol, tbuf[cur][hh], NEG))
        prep_queries(jnp.minimum(qi + 1, pl.num_programs(2) - 1))
        for hh in range(n_hh):
            t0_ref[hh] = scores(hh, 0)
        for hh in range(n_hh):
            acc = acc_ref[hh]
            ot = acc[0:HEAD_W] / acc[HEAD_W:HEAD_W + 1]
            d = ot[:, 0:tq] - lam * ot[:, tq:2 * tq]
            ms = jnp.mean(d * d, axis=0, keepdims=True)
            dn = d * lax.rsqrt(ms + RMS_EPS) * g_ref[...] * (1.0 - lam_init)
            o_ref[:, hrows[hh]] = dn.T.astype(BF16)

    @pl.when(odd)
    def _():
        finish(1)

    @pl.when(jnp.logical_not(odd))
    def _():
        finish(0)


def _page_copy(hbm, buf, sem, kind, sl, p, page, page_rows):
    src = hbm.at[pl.ds(pl.multiple_of(page * page_rows, page_rows), page_rows)]
    return pltpu.make_async_copy(src, buf.at[sl, pl.ds(p * page_rows, page_rows)], sem.at[kind, sl])


def _attn_sample_prefetch(pt_ref, ck_hbm, cv_hbm, kbuf, vbuf, sem, seq, n_seq, *, n_pages, page_rows):
    def start_fetch(s, sl):
        for p in range(n_pages):
            page = pt_ref[s * n_pages + p]
            _page_copy(ck_hbm, kbuf, sem, 0, sl, p, page, page_rows).start()
            _page_copy(cv_hbm, vbuf, sem, 1, sl, p, page, page_rows).start()

    @pl.when(seq == 0)
    def _():
        start_fetch(0, 0)

    @pl.when(seq + 1 < n_seq)
    def _():
        start_fetch(seq + 1, 1 - seq % 2)


def _attn_sample_compute(lamv_ref, g_ref, slope2_ref, base2_ref, newmask_ref, q2_ref, kn_ref, vn_ref,
                         ck_hbm, cv_hbm, o_ref, kbuf, vbuf, sem, s_ref, seq,
                         *, n_pages, page_rows, n_heads, lam_init, chunk):
    slot = seq % 2
    past_rows = n_pages * page_rows
    nrow = q2_ref.shape[0] // 2

    for p in range(n_pages):
        _page_copy(ck_hbm, kbuf, sem, 0, slot, p, 0, page_rows).wait()
        _page_copy(cv_hbm, vbuf, sem, 1, slot, p, 0, page_rows).wait()

    pad = jnp.zeros((chunk - kn_ref.shape[0], HEAD_W), F32)
    kbuf[slot, past_rows:past_rows + chunk, :] = jnp.concatenate([kn_ref[...], pad], axis=0)
    vbuf[slot, past_rows:past_rows + chunk, :] = jnp.concatenate([vn_ref[...], pad], axis=0)

    q2 = q2_ref[...]
    slope2 = slope2_ref[...]
    n_pairs = past_rows // (2 * chunk)

    for a in range(n_pairs):
        r0 = 2 * a * chunk
        ke = kbuf[slot, r0:r0 + chunk, :].astype(BF16)
        ko = kbuf[slot, r0 + chunk:r0 + 2 * chunk, :].astype(BF16)
        s2 = _dot_nt(q2, jnp.concatenate([ke, ko], axis=1))
        tok0 = float(2 * a * (chunk // n_heads) - past_rows // n_heads)
        s_ref[:, a * chunk:(a + 1) * chunk] = s2 + base2_ref[...] + slope2 * tok0
    q = q2[0:nrow, 0:HEAD_W]
    kn = kbuf[slot, past_rows:past_rows + chunk, :].astype(BF16)
    s_new = _dot_nt(q, kn) + newmask_ref[...]

    s2 = s_ref[...]
    m2 = jnp.max(s2, axis=-1, keepdims=True)
    m = jnp.maximum(jnp.maximum(m2[0:nrow], m2[nrow:2 * nrow]), jnp.max(s_new, axis=-1, keepdims=True))
    p2 = jnp.exp2(s2 - jnp.concatenate([m, m], axis=0))
    p_new = jnp.exp2(s_new - m)
    l2 = jnp.sum(p2, axis=-1, keepdims=True)
    l = l2[0:nrow] + l2[nrow:2 * nrow] + jnp.sum(p_new, axis=-1, keepdims=True)
    p2b = p2.astype(BF16)
    acc2 = jnp.zeros((2 * nrow, 2 * HEAD_W), F32)
    for a in range(n_pairs):
        r0 = 2 * a * chunk
        ve = vbuf[slot, r0:r0 + chunk, :].astype(BF16)
        vo = vbuf[slot, r0 + chunk:r0 + 2 * chunk, :].astype(BF16)
        acc2 = acc2 + _dot(p2b[:, a * chunk:(a + 1) * chunk], jnp.concatenate([ve, vo], axis=1))
    vn = vbuf[slot, past_rows:past_rows + chunk, :].astype(BF16)
    acc = acc2[0:nrow, 0:HEAD_W] + acc2[nrow:2 * nrow, HEAD_W:2 * HEAD_W] + _dot(p_new.astype(BF16), vn)

    o = acc / l
    lam = _diff_lambda(lamv_ref[...], lam_init)
    half = nrow // 2
    d = o[0:half] - lam * o[half:nrow]
    o_ref[...] = _rms(d, g_ref[...]) * (1.0 - lam_init)


def _attn_kernel(pt_ref, sl_ref, lamv_ref, gcol_ref, pos_ref, qt_ref, k_ref, vt_ref,
                 grow_ref, slope2_ref, base2_ref, newmask_ref, q2_ref, kn_ref, vn_ref, ck_hbm, cv_hbm,
                 op_ref, os_ref,
                 qq_ref, m_ref, acc_ref, t0_ref, t1_ref, kbuf, vbuf, sem, s_ref,
                 *, tq, lam_init, n_pages, page_rows, n_heads, chunk):
    seq = (pl.program_id(0) * pl.num_programs(1) + pl.program_id(1)) * pl.num_programs(2) + pl.program_id(2)
    n_seq = pl.num_programs(0) * pl.num_programs(1) * pl.num_programs(2)
    _attn_sample_prefetch(pt_ref, ck_hbm, cv_hbm, kbuf, vbuf, sem, seq, n_seq,
                          n_pages=n_pages, page_rows=page_rows)
    _attn_prompt_body(sl_ref, lamv_ref, gcol_ref, pos_ref, qt_ref, k_ref, vt_ref, op_ref,
                      qq_ref, m_ref, acc_ref, t0_ref, t1_ref, tq=tq, lam_init=lam_init)
    _attn_sample_compute(lamv_ref, grow_ref, slope2_ref, base2_ref, newmask_ref, q2_ref, kn_ref, vn_ref,
                         ck_hbm, cv_hbm, os_ref, kbuf, vbuf, sem, s_ref, seq,
                         n_pages=n_pages, page_rows=page_rows, n_heads=n_heads, lam_init=lam_init, chunk=chunk)


def _attn(qt, kb, vt, slope_parts, lamv, g_col, page_table, q2, knf, vnf, ckf, cvf, g_row, slope2, base2,
          newmask, lam_init, tq):
    b, aw, t = qt.shape
    nh = aw // HEAD_W
    nq = t // tq
    db, nrow2, _ = q2.shape
    nrow = nrow2 // 2
    n_pages = page_table.shape[1]
    chunk = base2.shape[1]
    page_rows = PAGE * nh
    past_rows = n_pages * page_rows
    assert past_rows % (2 * chunk) == 0 and knf.shape[1] <= chunk
    assert db == b * nh * nq, "one decode sequence per prompt grid step"
    i = jnp.arange(tq)
    lane = jnp.arange(HEAD_W)
    feat = jnp.where(lane[None, :] < 2, (i % 256)[:, None], jnp.where(lane[None, :] < 4, (i // 256 * 256)[:, None], 0))
    pos = feat.astype(BF16)
    seq_of = lambda bi, hi, qi: (bi * nh + hi) * nq + qi
    per_seq = lambda a: pl.BlockSpec((None,) + a.shape[1:], lambda bi, hi, qi, pt: (seq_of(bi, hi, qi), 0, 0))
    const = lambda a: pl.BlockSpec(a.shape, lambda bi, hi, qi, pt: (0,) * a.ndim, pipeline_mode=pl.Buffered(1))
    grid_spec = pltpu.PrefetchScalarGridSpec(
        num_scalar_prefetch=1,
        grid=(b, nh, nq),
        in_specs=[
            pl.BlockSpec(memory_space=pltpu.SMEM),
            const(lamv), const(g_col), const(pos),
            pl.BlockSpec((None, HEAD_W, t), lambda bi, hi, qi, pt: (bi, hi, 0)),
            pl.BlockSpec((None, t, HEAD_W), lambda bi, hi, qi, pt: (bi, 0, hi)),
            pl.BlockSpec((None, HEAD_W, t), lambda bi, hi, qi, pt: (bi, hi, 0)),
            const(g_row), const(slope2), const(base2), const(newmask),
            per_seq(q2), per_seq(knf), per_seq(vnf),
            pl.BlockSpec(memory_space=pl.ANY),
            pl.BlockSpec(memory_space=pl.ANY),
        ],
        out_specs=[
            pl.BlockSpec((None, tq, HEAD_W), lambda bi, hi, qi, pt: (bi, qi, hi)),
            pl.BlockSpec((None, nrow // 2, HEAD_W), lambda bi, hi, qi, pt: (seq_of(bi, hi, qi), 0, 0)),
        ],
        scratch_shapes=[
            pltpu.VMEM((1, 2 * HEAD_W, 2 * tq), BF16),
            pltpu.VMEM((1, 1, 2 * tq), F32),
            pltpu.VMEM((1, HEAD_W + 16, 2 * tq), F32),
            pltpu.VMEM((1, tq, 2 * tq), F32),
            pltpu.VMEM((1, tq, 2 * tq), F32),
            pltpu.VMEM((2, past_rows + chunk, HEAD_W), F32),
            pltpu.VMEM((2, past_rows + chunk, HEAD_W), F32),
            pltpu.SemaphoreType.DMA((2, 2)),
            pltpu.VMEM((nrow2, past_rows // 2), F32),
        ],
    )
    return pl.pallas_call(
        functools.partial(_attn_kernel, tq=tq, lam_init=lam_init, n_pages=n_pages, page_rows=page_rows,
                          n_heads=nh, chunk=chunk),
        grid_spec=grid_spec,
        out_shape=[jax.ShapeDtypeStruct((b, t, aw), BF16),
                   jax.ShapeDtypeStruct((db, nrow // 2, HEAD_W), F32)],
        compiler_params=_cparams(3),
        name="attn",
    )(page_table.reshape(-1), slope_parts, lamv, g_col, pos, qt, kb, vt,
      g_row, slope2, base2, newmask, q2, knf, vnf, ckf, cvf)


def _s5_prep_kernel(lre_ref, lim_ref, ldt_ref, bre_ref, bim_ref, cre_ref, cim_ref, dsk_ref,
                    wst_ref, cat_ref, kt_ref, sct_ref):
    lr = lre_ref[...]
    li = lim_ref[...]
    dt = jnp.exp(ldt_ref[...])

    def lam_bar_pow(n):
        mag = jnp.exp(n * (lr * dt))
        ang = n * (li * dt)
        return mag * jnp.cos(ang), mag * jnp.sin(ang)

    a_r, a_i = lam_bar_pow(1.0)
    den = lr * lr + li * li
    nr, ni = a_r - 1.0, a_i
    f_r = (nr * lr + ni * li) / den
    f_i = (ni * lr - nr * li) / den
    b_r, b_i = bre_ref[...], bim_ref[...]
    bb_r = f_r * b_r - f_i * b_i
    bb_i = f_r * b_i + f_i * b_r
    c_r, c_i = cre_ref[...], cim_ref[...]
    gw = b_r.shape[0]
    ns = b_r.shape[1]

    ktau = []
    for tau in range(S5_BLOCK):
        p_r, p_i = lam_bar_pow(float(tau))
        w_r = bb_r * p_r - bb_i * p_i
        w_i = bb_r * p_i + bb_i * p_r
        j = S5_BLOCK - 1 - tau
        wst_ref[j * gw:(j + 1) * gw, 0:ns] = w_r.astype(BF16)
        wst_ref[j * gw:(j + 1) * gw, ns:2 * ns] = w_i.astype(BF16)
        ktau.append(_dot_nt(w_r, c_r, lax.Precision.HIGHEST) - _dot_nt(w_i, c_i, lax.Precision.HIGHEST))
    rr = lax.broadcasted_iota(jnp.int32, (gw, gw), 0)
    cc = lax.broadcasted_iota(jnp.int32, (gw, gw), 1)
    ktau[0] = ktau[0] + jnp.where(rr == cc, dsk_ref[...], 0.0)

    for j in range(S5_BLOCK):
        p_r, p_i = lam_bar_pow(float(j + 1))
        cat_ref[j * gw:(j + 1) * gw, 0:ns] = (c_r * p_r - c_i * p_i).astype(BF16)
        cat_ref[j * gw:(j + 1) * gw, ns:2 * ns] = (-(c_r * p_i + c_i * p_r)).astype(BF16)

    zeros = jnp.zeros((gw, gw), BF16)
    for i in range(S5_BLOCK):
        for j in range(S5_BLOCK):
            blk = ktau[j - i].astype(BF16) if j >= i else zeros
            kt_ref[i * gw:(i + 1) * gw, j * gw:(j + 1) * gw] = blk

    row = lax.broadcasted_iota(jnp.int32, (8, ns), 0)
    for idx, k in enumerate((1, 2, 4)):
        p_r, p_i = lam_bar_pow(float(S5_BLOCK * k))
        sct_ref[2 * idx] = jnp.where(row >= k, p_r, 0.0)
        sct_ref[2 * idx + 1] = jnp.where(row >= k, p_i, 0.0)
    n = (S5_BLOCK * (row + 1)).astype(F32)
    mag = jnp.exp(n * (lr * dt))
    ang = n * (li * dt)
    sct_ref[6] = mag * jnp.cos(ang)
    sct_ref[7] = mag * jnp.sin(ang)


def _s5_prep(lre, lim, ldt, bre, bim, cre, cim, dsk):
    nq, gw, ns = bre.shape
    kw = S5_BLOCK * gw
    per_q = lambda *tail: pl.BlockSpec((None,) + tail, lambda q: (q,) + (0,) * len(tail))
    return pl.pallas_call(
        _s5_prep_kernel,
        grid=(nq,),
        in_specs=[per_q(1, ns), per_q(1, ns), per_q(1, ns), per_q(gw, ns), per_q(gw, ns),
                  per_q(gw, ns), per_q(gw, ns), per_q(1, gw)],
        out_specs=[per_q(kw, 2 * ns), per_q(kw, 2 * ns), per_q(kw, kw), per_q(8, 8, ns)],
        out_shape=[
            jax.ShapeDtypeStruct((nq, kw, 2 * ns), BF16),
            jax.ShapeDtypeStruct((nq, kw, 2 * ns), BF16),
            jax.ShapeDtypeStruct((nq, kw, kw), BF16),
            jax.ShapeDtypeStruct((nq, 8, 8, ns), F32),
        ],
        compiler_params=_cparams(1),
        name="s5_prep",
    )(lre, lim, ldt, bre, bim, cre, cim, dsk)


def _quarter_lanes(u_ref, q, gw, sw):
    return jnp.concatenate(
        [u_ref[:, j * sw + q * gw: j * sw + (q + 1) * gw] for j in range(S5_BLOCK)], axis=1)


def _store_quarter(y_ref, y, q, gw, sw):
    for j in range(S5_BLOCK):
        y_ref[:, j * sw + q * gw: j * sw + (q + 1) * gw] = y[:, j * gw:(j + 1) * gw].astype(y_ref.dtype)


def _s5_prompt_kernel(u_ref, wst_ref, cat_ref, kt_ref, sct_ref, y_ref, st_ref,
                      carry_ref, s_scr, hp_scr, *, rb, gw, sw, ns):
    blk = pl.program_id(1)

    @pl.when(blk == 0)
    def _():
        carry_ref[...] = jnp.zeros(carry_ref.shape, F32)

    row0 = lax.broadcasted_iota(jnp.int32, (8, ns), 0) == 0
    for q in range(S5_QUARTERS):
        uq = _quarter_lanes(u_ref, q, gw, sw)
        s_scr[...] = _dot(uq, wst_ref[q])
        tabs = [sct_ref[q, i] for i in range(8)]

        def tile(t, carry, tabs=tabs):
            cr, ci = carry
            r0 = pl.multiple_of(t * 8, 8)
            xr = s_scr[pl.ds(r0, 8), 0:ns]
            xi = s_scr[pl.ds(r0, 8), ns:2 * ns]
            for idx, k in enumerate((1, 2, 4)):
                mr, mi = tabs[2 * idx], tabs[2 * idx + 1]
                sr = pltpu.roll(xr, k, 0)
                si = pltpu.roll(xi, k, 0)
                xr, xi = xr + (mr * sr - mi * si), xi + (mr * si + mi * sr)
            crb = jnp.broadcast_to(cr, (8, ns))
            cib = jnp.broadcast_to(ci, (8, ns))
            hr = xr + (tabs[6] * crb - tabs[7] * cib)
            hi = xi + (tabs[6] * cib + tabs[7] * crb)
            hp_scr[pl.ds(r0, 8), 0:ns] = jnp.where(row0, crb, pltpu.roll(hr, 1, 0))
            hp_scr[pl.ds(r0, 8), ns:2 * ns] = jnp.where(row0, cib, pltpu.roll(hi, 1, 0))
            return hr[7:8, :], hi[7:8, :]

        cr, ci = lax.fori_loop(0, rb // 8, tile,
                               (carry_ref[q, :, 0:ns], carry_ref[q, :, ns:2 * ns]))
        carry_ref[q, :, 0:ns] = cr
        carry_ref[q, :, ns:2 * ns] = ci
        st_ref[q, :, 0:ns] = cr
        st_ref[q, :, ns:2 * ns] = ci
        y = _dot_nt(hp_scr[...].astype(BF16), cat_ref[q]) + _dot(uq, kt_ref[q])
        _store_quarter(y_ref, y, q, gw, sw)


def _s5_prompt(u4, wst, cat, kt, sct, rb):
    b, rows, lanes = u4.shape
    nq, kw, ns2 = wst.shape
    ns, gw, sw = ns2 // 2, kw // S5_BLOCK, lanes // S5_BLOCK
    return pl.pallas_call(
        functools.partial(_s5_prompt_kernel, rb=rb, gw=gw, sw=sw, ns=ns),
        grid=(b, rows // rb),
        in_specs=[
            pl.BlockSpec((None, rb, lanes), lambda bi, ri: (bi, ri, 0)),
            _const_spec(wst.shape), _const_spec(cat.shape), _const_spec(kt.shape), _const_spec(sct.shape),
        ],
        out_specs=[
            pl.BlockSpec((None, rb, lanes), lambda bi, ri: (bi, ri, 0)),
            pl.BlockSpec((None, nq, 1, ns2), lambda bi, ri: (bi, 0, 0, 0)),
        ],
        out_shape=[
            jax.ShapeDtypeStruct((b, rows, lanes), BF16),
            jax.ShapeDtypeStruct((b, nq, 1, ns2), F32),
        ],
        scratch_shapes=[
            pltpu.VMEM((nq, 1, ns2), F32),
            pltpu.VMEM((rb, ns2), F32),
            pltpu.VMEM((rb, ns2), F32),
        ],
        compiler_params=_cparams(2),
        name="s5_prompt",
    )(u4, wst, cat, kt, sct)


def _s5_sample_kernel(u_ref, h0_ref, wst_ref, cat_ref, kt_ref, sct_ref, y_ref, hout_ref, *, gw, sw, ns):
    for q in range(S5_QUARTERS):
        uq = _quarter_lanes(u_ref, q, gw, sw)
        s = _dot(uq, wst_ref[q])
        h0 = h0_ref[q]
        h0r, h0i = h0[:, 0:ns], h0[:, ns:2 * ns]
        a4r = sct_ref[q, 6][0:1, :]
        a4i = sct_ref[q, 7][0:1, :]
        hout_ref[q, :, 0:ns] = s[:, 0:ns] + (a4r * h0r - a4i * h0i)
        hout_ref[q, :, ns:2 * ns] = s[:, ns:2 * ns] + (a4r * h0i + a4i * h0r)
        y = _dot_nt(h0.astype(BF16), cat_ref[q]) + _dot(uq, kt_ref[q])
        _store_quarter(y_ref, y, q, gw, sw)


def _s5_sample(u4, h0q, wst, cat, kt, sct):
    rows, lanes = u4.shape
    nq, kw, ns2 = wst.shape
    ns, gw, sw = ns2 // 2, kw // S5_BLOCK, lanes // S5_BLOCK
    return pl.pallas_call(
        functools.partial(_s5_sample_kernel, gw=gw, sw=sw, ns=ns),
        grid=(1,),
        in_specs=[_const_spec(u4.shape), _const_spec(h0q.shape), _const_spec(wst.shape),
                  _const_spec(cat.shape), _const_spec(kt.shape), _const_spec(sct.shape)],
        out_specs=[pl.BlockSpec(u4.shape, lambda i: (0, 0)),
                   pl.BlockSpec(h0q.shape, lambda i: (0, 0, 0))],
        out_shape=[jax.ShapeDtypeStruct(u4.shape, BF16), jax.ShapeDtypeStruct(h0q.shape, F32)],
        compiler_params=_cparams(1),
        name="s5_sample",
    )(u4, h0q, wst, cat, kt, sct)


def _mem_kv_kernel(x_ref, g_ref, wk_ref, wv_ref, k_ref, v_ref, kb_ref, vb_ref):
    hn = _rms(x_ref[...], g_ref[...]).astype(BF16)
    k = _dot(hn, wk_ref[...])
    k_ref[...] = k
    kb_ref[...] = k.astype(BF16)
    v = _dot(hn, wv_ref[...])
    v_ref[...] = v
    vb_ref[...] = v.astype(BF16)


def _mem_kv(mem, g, wk_bf, wv_bf, tm):
    n, d = mem.shape
    row = pl.BlockSpec((tm, d), lambda i: (i, 0))
    return pl.pallas_call(
        _mem_kv_kernel,
        grid=(n // tm,),
        in_specs=[row, _const_spec((1, d)), _const_spec(wk_bf.shape), _const_spec(wv_bf.shape)],
        out_specs=[row, row, row, row],
        out_shape=[jax.ShapeDtypeStruct((n, d), F32), jax.ShapeDtypeStruct((n, d), F32),
                   jax.ShapeDtypeStruct((n, d), BF16), jax.ShapeDtypeStruct((n, d), BF16)],
        compiler_params=_cparams(1),
        name="mem_kv",
    )(mem, g, wk_bf, wv_bf)


def _merge_and_query(x, att, sy4_ref, sy_scr, wglu_ref, gso_ref, wout_ref, gpost_ref, gmpre_ref, wmq_ref,
                     mem_scale):
    aw = att.shape[1]
    rows, n_lb = sy4_ref.shape[0], sy_scr.shape[0]
    sw = n_lb * LANES
    for j in range(S5_BLOCK):
        for c in range(n_lb):
            blk = sy4_ref[:, j * sw + c * LANES:j * sw + (c + 1) * LANES].astype(F32)
            sy_scr[c, pl.ds(j, rows, stride=S5_BLOCK), :] = blk
    y = jax.nn.gelu(jnp.concatenate([sy_scr[c] for c in range(n_lb)], axis=1))
    y = y * jax.nn.sigmoid(_dot(y.astype(BF16), wglu_ref[...]))
    y = _rms(y, gso_ref[...])
    mix = _dot(att, wout_ref[0:aw, :]) + _dot(y.astype(BF16), wout_ref[aw:, :])
    x1 = x + _rms(mix, gpost_ref[...])
    hq = _rms(x1, gmpre_ref[...]).astype(BF16)
    qm = (_dot(hq, wmq_ref[...]) * mem_scale).astype(BF16)
    return x1, qm


def _mix_mem_body(x_ref, att_ref, sy_ref, mk_ref, mv_ref, wglu_ref, gso_ref, wout_ref, gpost_ref,
                  gmpre_ref, wmq_ref, wmo_ref, gmpost_ref, o_ref, oh_ref, sy_scr, *, n_heads, mem_scale):
    x1, qm = _merge_and_query(x_ref[...], att_ref[...], sy_ref, sy_scr, wglu_ref, gso_ref, wout_ref,
                              gpost_ref, gmpre_ref, wmq_ref, mem_scale)
    hd = qm.shape[1] // n_heads
    for h in range(n_heads):
        sl = slice(h * hd, (h + 1) * hd)
        s = _dot_nt(qm[:, sl], mk_ref[:, sl])
        p = jnp.exp(s - jnp.max(s, axis=-1, keepdims=True))
        l = jnp.sum(p, axis=-1, keepdims=True)
        oh_ref[:, sl] = (_dot(p.astype(BF16), mv_ref[:, sl]) / l).astype(BF16)
    mo = _dot(oh_ref[...], wmo_ref[...])
    o_ref[...] = x1 + _rms(mo, gmpost_ref[...])


def _mix_q_kernel(x_ref, att_ref, sy_ref, wglu_ref, gso_ref, wout_ref, gpost_ref, gmpre_ref, wmq_ref,
                  x1_ref, qm_ref, sy_scr, *, mem_scale):
    x1, qm = _merge_and_query(x_ref[...], att_ref[...], sy_ref, sy_scr, wglu_ref, gso_ref, wout_ref,
                              gpost_ref, gmpre_ref, wmq_ref, mem_scale)
    x1_ref[...] = x1
    qm_ref[...] = qm


def _mix_q(x, att, sy4, w, n_heads):
    n, d = x.shape
    weights = [w["w_glu"], w["g_ssm_out"], w["w_out"], w["g_mix_post"], w["g_mem_pre"], w["w_mq"]]
    ins = [x, att, sy4] + weights
    return pl.pallas_call(
        functools.partial(_mix_q_kernel, mem_scale=(d // n_heads) ** -0.5),
        grid=(1,),
        in_specs=[_const_spec(a.shape) for a in ins],
        out_specs=[pl.BlockSpec((n, d), lambda i: (0, 0)), pl.BlockSpec((n, d), lambda i: (0, 0))],
        out_shape=[jax.ShapeDtypeStruct((n, d), F32), jax.ShapeDtypeStruct((n, d), BF16)],
        scratch_shapes=[pltpu.VMEM((sy4.shape[1] // S5_BLOCK // LANES, n, LANES), F32)],
        compiler_params=_cparams(1),
        name="mix_q",
    )(*ins)


def _mem_sample_copies(ck_hbm, cv_hbm, kbuf, vbuf, sem, step, n_per, n_mem, n_heads):
    sl = step % 2
    out = []
    for j in range(n_per):
        for kind, (hbm, buf) in enumerate(((ck_hbm, kbuf), (cv_hbm, vbuf))):
            for lb in range(kbuf.shape[2]):
                src = hbm.at[step * n_per + j, :, :, pl.ds(lb * LANES, LANES)]
                dst = buf.at[sl, j, lb].reshape(n_mem, n_heads, LANES)
                out.append(pltpu.make_async_copy(src, dst, sem.at[kind, sl]))
    return out


def _mem_sample_compute(mask_ref, q_ref, o_ref, kbuf, vbuf, slot, j):
    n_halves = kbuf.shape[2]
    kcat = jnp.concatenate([kbuf[slot, j, lb] for lb in range(n_halves)], axis=1).astype(BF16)
    vcat = jnp.concatenate([vbuf[slot, j, lb] for lb in range(n_halves)], axis=1).astype(BF16)
    s = _dot_nt(q_ref[j], kcat) + mask_ref[...]
    p = jnp.exp(s - jnp.max(s, axis=-1, keepdims=True))
    l = jnp.sum(p, axis=-1, keepdims=True)
    o_ref[j] = _dot(p.astype(BF16), vcat) / l


def _mix_mem_kernel(x_ref, att_ref, sy_ref, mk_ref, mv_ref, wglu_ref, gso_ref, wout_ref, gpost_ref,
                    gmpre_ref, wmq_ref, wmo_ref, gmpost_ref, mask_ref, qs_ref, ck_hbm, cv_hbm,
                    o_ref, os_ref, oh_ref, sy_scr, kbuf, vbuf, sem, *, n_heads, mem_scale, n_mem):
    step = pl.program_id(0) * pl.num_programs(1) + pl.program_id(1)
    n_steps = pl.num_programs(0) * pl.num_programs(1)
    n_per = qs_ref.shape[0]
    copies = lambda st: _mem_sample_copies(ck_hbm, cv_hbm, kbuf, vbuf, sem, st, n_per, n_mem, n_heads)

    @pl.when(step == 0)
    def _():
        for cp in copies(0):
            cp.start()

    @pl.when(step + 1 < n_steps)
    def _():
        for cp in copies(step + 1):
            cp.start()

    _mix_mem_body(x_ref, att_ref, sy_ref, mk_ref, mv_ref, wglu_ref, gso_ref, wout_ref, gpost_ref,
                  gmpre_ref, wmq_ref, wmo_ref, gmpost_ref, o_ref, oh_ref, sy_scr,
                  n_heads=n_heads, mem_scale=mem_scale)

    for cp in copies(step):
        cp.wait()
    for j in range(n_per):
        _mem_sample_compute(mask_ref, qs_ref, os_ref, kbuf, vbuf, step % 2, j)


def _mix_mem(x, att, sy4, mkb, mvb, w, qh, ck, cv, mask, tm, n_heads):
    b, t, d = x.shape
    aw, sw = att.shape[2], sy4.shape[2] // S5_BLOCK
    nm = mkb.shape[1]
    nt = t // tm
    db, nrow, hd = qh.shape
    n_ckeys = ck.shape[1]
    n_per = db // (b * nt)
    assert db == n_per * b * nt, "decode sequences must divide evenly over the prompt grid steps"
    tok = lambda width: pl.BlockSpec((None, tm, width), lambda bi, ti: (bi, ti, 0))
    tok4 = pl.BlockSpec((None, tm // S5_BLOCK, S5_BLOCK * sw), lambda bi, ti: (bi, ti, 0))
    per_b = pl.BlockSpec((None, nm, d), lambda bi, ti: (bi, 0, 0))
    per_step = pl.BlockSpec((n_per, nrow, hd), lambda bi, ti: (bi * nt + ti, 0, 0))
    any_spec = pl.BlockSpec(memory_space=pl.ANY)
    weights = [w["w_glu"], w["g_ssm_out"], w["w_out"], w["g_mix_post"], w["g_mem_pre"], w["w_mq"],
               w["w_mo"], w["g_mem_post"], mask]
    return pl.pallas_call(
        functools.partial(_mix_mem_kernel, n_heads=n_heads, mem_scale=(d // n_heads) ** -0.5, n_mem=n_ckeys),
        grid=(b, nt),
        in_specs=[tok(d), tok(aw), tok4, per_b, per_b] + [_const_spec(a.shape) for a in weights]
                 + [per_step, any_spec, any_spec],
        out_specs=[tok(d), per_step],
        out_shape=[jax.ShapeDtypeStruct((b, t, d), F32), jax.ShapeDtypeStruct((db, nrow, hd), F32)],
        scratch_shapes=[
            pltpu.VMEM((tm, d), BF16),
            pltpu.VMEM((sw // LANES, tm, LANES), F32),
            pltpu.VMEM((2, n_per, hd // LANES, n_ckeys * n_heads, LANES), F32),
            pltpu.VMEM((2, n_per, hd // LANES, n_ckeys * n_heads, LANES), F32),
            pltpu.SemaphoreType.DMA((2, 2)),
        ],
        compiler_params=_cparams(2),
        name="mix_mem",
    )(x, att, sy4, mkb, mvb, *weights, qh, ck, cv)


def _mem_out_kernel(x1_ref, o_ref, wmo_ref, g_ref, y_ref):
    mo = _dot(o_ref[...], wmo_ref[...])
    y_ref[...] = x1_ref[...] + _rms(mo, g_ref[...])


def _mem_out(x1, o, wmo, g):
    n, d = x1.shape
    ins = [x1, o, wmo, g]
    return pl.pallas_call(
        _mem_out_kernel,
        grid=(1,),
        in_specs=[_const_spec(a.shape) for a in ins],
        out_specs=pl.BlockSpec((n, d), lambda i: (0, 0)),
        out_shape=jax.ShapeDtypeStruct((n, d), F32),
        compiler_params=_cparams(1),
        name="mem_out",
    )(*ins)


def _ffn_kernel(x_ref, gpre_ref, wg_ref, wu_ref, wd_ref, gpost_ref, o_ref, *, n_chunks):
    x = x_ref[...]
    hn = _rms(x, gpre_ref[...]).astype(BF16)
    ff = wg_ref.shape[1]
    cw = ff // n_chunks
    acc = jnp.zeros(x.shape, F32)
    for c in range(n_chunks):
        sl = slice(c * cw, (c + 1) * cw)
        a = jax.nn.silu(_dot(hn, wg_ref[:, sl])) * _dot(hn, wu_ref[:, sl])
        acc = acc + _dot(a.astype(BF16), wd_ref[sl, :])
    o_ref[...] = x + _rms(acc, gpost_ref[...])


def _ffn(x, gpre, wg, wu, wd, gpost, tm, n_chunks):
    n, d = x.shape
    row = pl.BlockSpec((tm, d), lambda i: (i, 0))
    weights = [gpre, wg, wu, wd, gpost]
    return pl.pallas_call(
        functools.partial(_ffn_kernel, n_chunks=n_chunks),
        grid=(n // tm,),
        in_specs=[row] + [_const_spec(a.shape) for a in weights],
        out_specs=row,
        out_shape=jax.ShapeDtypeStruct((n, d), F32),
        compiler_params=_cparams(1),
        name="ffn",
    )(x, *weights)


def _s5_param_layout(lam_re, lam_im, log_dt, b_re, b_im, c_re, c_im, d_skip):
    g, p, hc = b_re.shape
    nq = S5_QUARTERS
    gq = g // nq
    rowv = lambda a: a.reshape(nq, 1, gq * p)
    ldt = jnp.broadcast_to(log_dt[:, None], (g, p))
    eye = jnp.eye(gq, dtype=F32)

    def embed_b(b):
        bq = b.reshape(nq, gq, p, hc)
        return jnp.einsum("qgph,gk->qghkp", bq, eye).reshape(nq, gq * hc, gq * p)

    def embed_c(c):
        cq = c.reshape(nq, gq, hc, p)
        return jnp.einsum("qghp,gk->qghkp", cq, eye).reshape(nq, gq * hc, gq * p)

    return (rowv(lam_re), rowv(lam_im), rowv(ldt), embed_b(b_re), embed_b(b_im),
            embed_c(c_re), embed_c(c_im), d_skip.reshape(nq, 1, gq * hc))


def _state_to_quarters(re, im):
    n, g, p = re.shape
    gq = g // S5_QUARTERS
    f = lambda a: a.reshape(n, S5_QUARTERS, gq * p).transpose(1, 0, 2)
    return jnp.concatenate([f(re), f(im)], axis=-1)


def _state_from_quarters(st, g, p):
    n = st.shape[0]
    ns = st.shape[-1] // 2
    return st[..., :ns].reshape(n, g, p), st[..., ns:].reshape(n, g, p)


def kernel(x_prompt, x_sample, mem_prompt, cache_k, cache_v, page_table, state_ssm_re, state_ssm_im,
           cache_mem_k, cache_mem_v, g_mix_pre, g_mix_post, w_in, lam_q1, lam_k1, lam_q2, lam_k2, g_subln,
           ssm_lam_re, ssm_lam_im, ssm_log_dt, ssm_b_re, ssm_b_im, ssm_c_re, ssm_c_im, ssm_d, w_glu, g_ssm_out,
           w_out, g_mem_pre, g_mem_post, g_mem_kv, w_mq, w_mk, w_mv, w_mo, g_ffn_pre, g_ffn_post,
           w_gate, w_up, w_down):
    depth = w_in.shape[0]
    b, t, d = x_prompt.shape
    db, dt_, _ = x_sample.shape
    n_mem = mem_prompt.shape[1]
    n_mem_heads = cache_mem_k.shape[3]
    n_heads = cache_k.shape[3]
    aw = n_heads * HEAD_W
    g, p = ssm_lam_re.shape[1], ssm_lam_re.shape[2]
    sw = g * SSM_GROUP
    n_phys = cache_k.shape[1]
    assert dt_ == S5_BLOCK and t % (S5_BLOCK * 256) == 0

    slopes = LOG2E * 2.0 ** (-8.0 * jnp.arange(1, n_heads + 1, dtype=F32) / n_heads)
    slope_hi = slopes.astype(BF16).astype(F32)
    slope_parts = jnp.stack([slopes, slope_hi, (slopes - slope_hi).astype(BF16).astype(F32)])
    row = lambda a: a.reshape(1, -1).astype(F32)
    bf = lambda a: a.astype(BF16)

    branch_lane = jnp.arange(HEAD_W) // HEAD_DIM
    nrow = 2 * n_heads * dt_
    chunk = 256
    r = jnp.arange(nrow)
    r_head, r_tok = (r // dt_) % n_heads, r % dt_
    r_slope = slopes[r_head][:, None]
    c = jnp.arange(chunk)
    c_tok, c_head = c // n_heads, c % n_heads
    own_head = c_head[None, :] == r_head[:, None]
    base = r_slope * c_tok[None, :].astype(F32) + jnp.where(own_head, 0.0, NEG)
    base2 = jnp.concatenate([base, base + r_slope * float(chunk // n_heads)], axis=0)
    slope2 = jnp.concatenate([r_slope, r_slope], axis=0)
    new_ok = own_head & (c[None, :] < dt_ * n_heads) & (c_tok[None, :] <= r_tok[:, None])
    newmask = jnp.where(new_ok, r_slope * c_tok[None, :].astype(F32), NEG)
    mem_row_head = jnp.arange(n_mem_heads * dt_) // dt_
    mem_col_head = jnp.arange(n_mem * n_mem_heads) % n_mem_heads
    mem_mask = jnp.where(mem_col_head[None, :] == mem_row_head[:, None], 0.0, NEG).astype(F32)

    y_p = x_prompt
    y_s = x_sample.reshape(db * dt_, d)
    outs = {k: [] for k in ("kp", "vp", "ks", "vs", "rp", "ip", "rs", "is", "mkp", "mvp")}
    for l in range(depth):
        lam_init = 0.8 - 0.6 * math.exp(-0.3 * l)
        lamv = jnp.stack([lam_q1[l], lam_k1[l], lam_q2[l], lam_k2[l]]).astype(F32)
        gsub = row(g_subln[l])
        w_in_bf = bf(w_in[l])
        w = {
            "w_glu": bf(w_glu[l]), "g_ssm_out": row(g_ssm_out[l]), "w_out": bf(w_out[l]),
            "g_mix_post": row(g_mix_post[l]), "g_mem_pre": row(g_mem_pre[l]), "w_mq": bf(w_mq[l]),
            "w_mo": bf(w_mo[l]), "g_mem_post": row(g_mem_post[l]),
        }

        qp, kp, vp, kpb, vpb, up = _proj(y_p.reshape(b * t, d), row(g_mix_pre[l]), w_in_bf, aw, 512)
        qs, ks, vs, _, _, us = _proj(y_s, row(g_mix_pre[l]), w_in_bf, aw, 512)

        qh = qs.reshape(db, dt_, n_heads, HEAD_W).transpose(0, 2, 1, 3)
        zq = jnp.zeros((), BF16)
        q32 = jnp.stack([jnp.where(branch_lane == 0, qh, zq), jnp.where(branch_lane == 1, qh, zq)], axis=1)
        q32 = q32.reshape(db, nrow, HEAD_W)
        zq32 = jnp.zeros_like(q32)
        q2 = jnp.concatenate([jnp.concatenate([q32, zq32], axis=2), jnp.concatenate([zq32, q32], axis=2)], axis=1)
        att_p, att_s16 = _attn(jnp.swapaxes(qp.reshape(b, t, aw), 1, 2), kpb.reshape(b, t, aw),
                               jnp.swapaxes(vpb.reshape(b, t, aw), 1, 2),
                               slope_parts, lamv, g_subln[l].reshape(HEAD_W, 1).astype(F32),
                               page_table, q2, ks.reshape(db, dt_ * n_heads, HEAD_W),
                               vs.reshape(db, dt_ * n_heads, HEAD_W),
                               cache_k[l].reshape(n_phys * PAGE * n_heads, HEAD_W),
                               cache_v[l].reshape(n_phys * PAGE * n_heads, HEAD_W),
                               gsub, slope2, base2, newmask, lam_init, 512)
        att_s = bf(att_s16.reshape(db, n_heads, dt_, HEAD_W).transpose(0, 2, 1, 3).reshape(db * dt_, aw))

        tabs = _s5_prep(*_s5_param_layout(ssm_lam_re[l], ssm_lam_im[l], ssm_log_dt[l], ssm_b_re[l],
                                          ssm_b_im[l], ssm_c_re[l], ssm_c_im[l], ssm_d[l]))
        syp, stp = _s5_prompt(up.reshape(b, t // S5_BLOCK, S5_BLOCK * sw), *tabs, 256)
        rp, ip = _state_from_quarters(stp.reshape(b, S5_QUARTERS, -1), g, p)
        sys_, sts = _s5_sample(us, _state_to_quarters(state_ssm_re[l], state_ssm_im[l]), *tabs)
        rs, is_ = _state_from_quarters(sts.transpose(1, 0, 2), g, p)

        mkp, mvp, mkb, mvb = _mem_kv(mem_prompt.reshape(b * n_mem, d), row(g_mem_kv[l]),
                                     bf(w_mk[l]), bf(w_mv[l]), 512)
        x1s, qms = _mix_q(y_s, att_s, sys_, w, n_mem_heads)
        mhd = d // n_mem_heads
        qmh = qms.reshape(db, dt_, n_mem_heads, mhd).transpose(0, 2, 1, 3).reshape(db, n_mem_heads * dt_, mhd)
        y_p, omh = _mix_mem(y_p, att_p, syp, mkb.reshape(b, n_mem, d), mvb.reshape(b, n_mem, d), w,
                            qmh, cache_mem_k[l], cache_mem_v[l], mem_mask, 512, n_mem_heads)
        om = bf(omh.reshape(db, n_mem_heads, dt_, mhd).transpose(0, 2, 1, 3).reshape(db * dt_, d))
        y_s = _mem_out(x1s, om, w["w_mo"], w["g_mem_post"])

        ffw = (row(g_ffn_pre[l]), bf(w_gate[l]), bf(w_up[l]), bf(w_down[l]), row(g_ffn_post[l]))
        y_p = _ffn(y_p.reshape(b * t, d), *ffw, 1024, 11).reshape(b, t, d)
        y_s = _ffn(y_s, *ffw, 512, 2)

        outs["kp"].append(kp.reshape(b, t, n_heads, HEAD_W))
        outs["vp"].append(vp.reshape(b, t, n_heads, HEAD_W))
        outs["ks"].append(ks.reshape(db, dt_, n_heads, HEAD_W))
        outs["vs"].append(vs.reshape(db, dt_, n_heads, HEAD_W))
        outs["rp"].append(rp); outs["ip"].append(ip); outs["rs"].append(rs); outs["is"].append(is_)
        outs["mkp"].append(mkp.reshape(b, n_mem, n_mem_heads, d // n_mem_heads))
        outs["mvp"].append(mvp.reshape(b, n_mem, n_mem_heads, d // n_mem_heads))

    st = lambda k: jnp.stack(outs[k], 0)
    return (y_p, y_s.reshape(db, dt_, d), st("kp"), st("vp"), st("ks"), st("vs"),
            st("rp"), st("ip"), st("rs"), st("is"), st("mkp"), st("mvp"))
```

```python
import functools
import math

import jax
import jax.numpy as jnp
from jax import lax
from jax.experimental import pallas as pl
from jax.experimental.pallas import tpu as pltpu

F32 = jnp.float32
BF16 = jnp.bfloat16

RMS_EPS = 1e-6
HEAD_DIM = 64
HEAD_W = 2 * HEAD_DIM
SSM_GROUP = 16
SSM_STATE = 64
S5_BLOCK = 4
S5_QUARTERS = 4
PAGE = 128
LANES = 128
NEG = -1e30
LOG2E = 1.4426950408889634
VMEM_LIMIT_BYTES = 56 * 1024 * 1024


def _cparams(n_grid_dims):
    return pltpu.CompilerParams(
        dimension_semantics=("arbitrary",) * n_grid_dims,
        vmem_limit_bytes=VMEM_LIMIT_BYTES)


def _const_spec(shape):
    nd = len(shape)
    return pl.BlockSpec(shape, lambda *_: (0,) * nd, pipeline_mode=pl.Buffered(1))


def _rms(x, g):
    ms = jnp.mean(x * x, axis=-1, keepdims=True)
    return x * lax.rsqrt(ms + RMS_EPS) * g


def _dot(a, b):
    return jnp.dot(a, b, preferred_element_type=F32)


def _dot_nt(a, b, precision=None):
    return lax.dot_general(a, b, (((1,), (1,)), ((), ())),
                           preferred_element_type=F32, precision=precision)


def _proj_kernel(x_ref, g_ref, w_ref, q_ref, k_ref, v_ref, kb_ref, vb_ref, u_ref, u_scr, *, aw, tm):
    hn = _rms(x_ref[...], g_ref[...]).astype(BF16)
    q = _dot(hn, w_ref[:, 0:aw])
    q_ref[...] = (q * (HEAD_DIM ** -0.5 * LOG2E)).astype(BF16)
    nh = aw // HEAD_W
    k = _dot(hn, w_ref[:, aw:2 * aw])
    kb_ref[...] = k.astype(BF16)
    v = _dot(hn, w_ref[:, 2 * aw:3 * aw])
    vb_ref[...] = v.astype(BF16)
    for h in range(nh):
        k_ref[pl.ds(h, tm, stride=nh), :] = k[:, h * HEAD_W:(h + 1) * HEAD_W]
        v_ref[pl.ds(h, tm, stride=nh), :] = v[:, h * HEAD_W:(h + 1) * HEAD_W]
    u = _dot(hn, w_ref[:, 3 * aw:])
    n_lb = u_scr.shape[0]
    sw = n_lb * LANES
    for c in range(n_lb):
        u_scr[c] = u[:, c * LANES:(c + 1) * LANES]
    for j in range(S5_BLOCK):
        for c in range(n_lb):
            rows = u_scr[c, pl.ds(j, tm // S5_BLOCK, stride=S5_BLOCK), :]
            u_ref[:, j * sw + c * LANES:j * sw + (c + 1) * LANES] = rows.astype(BF16)


def _proj(x, g, w_bf, aw, tm):
    n, d = x.shape
    sw = w_bf.shape[1] - 3 * aw
    nh = aw // HEAD_W
    row = lambda width: pl.BlockSpec((tm, width), lambda i: (i, 0))
    flat = pl.BlockSpec((tm * nh, HEAD_W), lambda i: (i, 0))
    return pl.pallas_call(
        functools.partial(_proj_kernel, aw=aw, tm=tm),
        grid=(n // tm,),
        in_specs=[row(d), _const_spec((1, d)), _const_spec(w_bf.shape)],
        out_specs=[row(aw), flat, flat, row(aw), row(aw),
                   pl.BlockSpec((tm // S5_BLOCK, S5_BLOCK * sw), lambda i: (i, 0))],
        out_shape=[
            jax.ShapeDtypeStruct((n, aw), BF16),
            jax.ShapeDtypeStruct((n * nh, HEAD_W), F32),
            jax.ShapeDtypeStruct((n * nh, HEAD_W), F32),
            jax.ShapeDtypeStruct((n, aw), BF16),
            jax.ShapeDtypeStruct((n, aw), BF16),
            jax.ShapeDtypeStruct((n // S5_BLOCK, S5_BLOCK * sw), BF16),
        ],
        scratch_shapes=[pltpu.VMEM((sw // LANES, tm, LANES), F32)],
        compiler_params=_cparams(1),
        name="proj",
    )(x, g, w_bf)


def _diff_lambda(lamv, lam_init):
    d1 = jnp.sum(lamv[0:1] * lamv[1:2], axis=-1, keepdims=True)
    d2 = jnp.sum(lamv[2:3] * lamv[3:4], axis=-1, keepdims=True)
    return jnp.exp(d1) - jnp.exp(d2) + lam_init


def _attn_prompt_body(sl_ref, lamv_ref, g_ref, pos_ref, qt_ref, k_ref, vt_ref, o_ref,
                        qq_ref, m_ref, acc_ref, t0_ref, t1_ref, *, tq, lam_init):
    hp = pl.program_id(1)
    qi = pl.program_id(2)
    tk = tq
    n_hh = qq_ref.shape[0]
    heads = [hp * n_hh + hh for hh in range(n_hh)]
    hrows = [slice(hh * HEAD_W, (hh + 1) * HEAD_W) for hh in range(n_hh)]

    def prep_queries(qblk):
        q0 = pl.multiple_of(qblk * tq, tq)
        arow = lax.broadcasted_iota(jnp.int32, (HEAD_W, 2 * tq), 0)
        for hh in range(n_hh):
            qt = qt_ref[hrows[hh], pl.ds(q0, tq)].astype(F32)
            frow = lax.broadcasted_iota(jnp.int32, qt.shape, 0)
            qq_ref[hh, 0:HEAD_W, 0:tq] = jnp.where(frow < HEAD_DIM, qt, 0.0).astype(BF16)
            qq_ref[hh, 0:HEAD_W, tq:2 * tq] = jnp.where(frow >= HEAD_DIM, qt, 0.0).astype(BF16)
            slope_parts = jnp.where((arow & 1) == 0, sl_ref[1, heads[hh]], sl_ref[2, heads[hh]])
            qq_ref[hh, HEAD_W:2 * HEAD_W, :] = jnp.where(arow < 4, slope_parts, 0.0).astype(BF16)

    m_ref[...] = jnp.full(m_ref.shape, NEG, F32)
    acc_ref[...] = jnp.zeros(acc_ref.shape, F32)
    ones = jnp.ones((acc_ref.shape[1] - HEAD_W, tk), BF16)

    def scores(hh, kj):
        k0 = pl.multiple_of(kj * tk, tk)
        ka = jnp.concatenate([k_ref[pl.ds(k0, tk), hrows[hh]], pos_ref[...]], axis=1)
        return _dot(ka, qq_ref[hh])

    def consume(hh, kj, t):
        k0 = pl.multiple_of(kj * tk, tk)
        vta = jnp.concatenate([vt_ref[hrows[hh], pl.ds(k0, tk)], ones], axis=0)
        off = sl_ref[0, heads[hh]] * jnp.full((1, 2 * tq), (kj - qi) * tk, jnp.int32).astype(F32)
        m_old = m_ref[hh]
        m_new = jnp.maximum(m_old, jnp.max(t, axis=0, keepdims=True) + off)
        alpha = jnp.exp2(m_old - m_new)
        p = jnp.exp2(t - (m_new - off)).astype(BF16)
        acc_ref[hh] = alpha * acc_ref[hh] + _dot(vta, p)
        m_ref[hh] = m_new

    tbuf = (t0_ref, t1_ref)

    def step(kj, cur):
        for hh in range(n_hh):
            tbuf[1 - cur][hh] = scores(hh, kj + 1)
        for hh in range(n_hh):
            consume(hh, kj, tbuf[cur][hh])

    @pl.when(qi == 0)
    def _():
        prep_queries(0)
        for hh in range(n_hh):
            t0_ref[hh] = scores(hh, 0)

    def body(i, carry):
        step(2 * i, 0)
        step(2 * i + 1, 1)
        return carry

    lax.fori_loop(0, qi // 2, body, 0)
    odd = qi % 2 == 1

    @pl.when(odd)
    def _():
        step(qi - 1, 0)

    def finish(cur):
        krow = lax.broadcasted_iota(jnp.int32, (tk, 2 * tq), 0)
        qcol = lax.broadcasted_iota(jnp.int32, (tk, 2 * tq), 1)
        qcol = jnp.where(qcol >= tq, qcol - tq, qcol)
        lam = _diff_lambda(lamv_ref[...], lam_init)
        for hh in range(n_hh):
            consume(hh, qi, jnp.where(krow <= qcol, tbuf[cur][hh], NEG))
        prep_queries(jnp.minimum(qi + 1, pl.num_programs(2) - 1))
        for hh in range(n_hh):
            t0_ref[hh] = scores(hh, 0)
        for hh in range(n_hh):
            acc = acc_ref[hh]
            ot = acc[0:HEAD_W] / acc[HEAD_W:HEAD_W + 1]
            d = ot[:, 0:tq] - lam * ot[:, tq:2 * tq]
            ms = jnp.mean(d * d, axis=0, keepdims=True)
            dn = d * lax.rsqrt(ms + RMS_EPS) * g_ref[...] * (1.0 - lam_init)
            o_ref[:, hrows[hh]] = dn.T.astype(BF16)

    @pl.when(odd)
    def _():
        finish(1)

    @pl.when(jnp.logical_not(odd))
    def _():
        finish(0)


def _page_copy(hbm, buf, sem, kind, sl, p, page, page_rows):
    src = hbm.at[pl.ds(pl.multiple_of(page * page_rows, page_rows), page_rows)]
    return pltpu.make_async_copy(src, buf.at[sl, pl.ds(p * page_rows, page_rows)], sem.at[kind, sl])


def _attn_sample_prefetch(pt_ref, ck_hbm, cv_hbm, kbuf, vbuf, sem, seq, n_seq, *, n_pages, page_rows):
    def start_fetch(s, sl):
        for p in range(n_pages):
            page = pt_ref[s * n_pages + p]
            _page_copy(ck_hbm, kbuf, sem, 0, sl, p, page, page_rows).start()
            _page_copy(cv_hbm, vbuf, sem, 1, sl, p, page, page_rows).start()

    @pl.when(seq == 0)
    def _():
        start_fetch(0, 0)

    @pl.when(seq + 1 < n_seq)
    def _():
        start_fetch(seq + 1, 1 - seq % 2)


def _attn_sample_compute(lamv_ref, g_ref, slope2_ref, base2_ref, newmask_ref, q2_ref, kn_ref, vn_ref,
                         ck_hbm, cv_hbm, o_ref, kbuf, vbuf, sem, s_ref, seq,
                         *, n_pages, page_rows, n_heads, lam_init, chunk):
    slot = seq % 2
    past_rows = n_pages * page_rows
    nrow = q2_ref.shape[0] // 2

    for p in range(n_pages):
        _page_copy(ck_hbm, kbuf, sem, 0, slot, p, 0, page_rows).wait()
        _page_copy(cv_hbm, vbuf, sem, 1, slot, p, 0, page_rows).wait()

    pad = jnp.zeros((chunk - kn_ref.shape[0], HEAD_W), F32)
    kbuf[slot, past_rows:past_rows + chunk, :] = jnp.concatenate([kn_ref[...], pad], axis=0)
    vbuf[slot, past_rows:past_rows + chunk, :] = jnp.concatenate([vn_ref[...], pad], axis=0)

    q2 = q2_ref[...]
    slope2 = slope2_ref[...]
    n_pairs = past_rows // (2 * chunk)

    for a in range(n_pairs):
        r0 = 2 * a * chunk
        ke = kbuf[slot, r0:r0 + chunk, :].astype(BF16)
        ko = kbuf[slot, r0 + chunk:r0 + 2 * chunk, :].astype(BF16)
        s2 = _dot_nt(q2, jnp.concatenate([ke, ko], axis=1))
        tok0 = float(2 * a * (chunk // n_heads) - past_rows // n_heads)
        s_ref[:, a * chunk:(a + 1) * chunk] = s2 + base2_ref[...] + slope2 * tok0
    q = q2[0:nrow, 0:HEAD_W]
    kn = kbuf[slot, past_rows:past_rows + chunk, :].astype(BF16)
    s_new = _dot_nt(q, kn) + newmask_ref[...]

    s2 = s_ref[...]
    m2 = jnp.max(s2, axis=-1, keepdims=True)
    m = jnp.maximum(jnp.maximum(m2[0:nrow], m2[nrow:2 * nrow]), jnp.max(s_new, axis=-1, keepdims=True))
    p2 = jnp.exp2(s2 - jnp.concatenate([m, m], axis=0))
    p_new = jnp.exp2(s_new - m)
    l2 = jnp.sum(p2, axis=-1, keepdims=True)
    l = l2[0:nrow] + l2[nrow:2 * nrow] + jnp.sum(p_new, axis=-1, keepdims=True)
    p2b = p2.astype(BF16)
    acc2 = jnp.zeros((2 * nrow, 2 * HEAD_W), F32)
    for a in range(n_pairs):
        r0 = 2 * a * chunk
        ve = vbuf[slot, r0:r0 + chunk, :].astype(BF16)
        vo = vbuf[slot, r0 + chunk:r0 + 2 * chunk, :].astype(BF16)
        acc2 = acc2 + _dot(p2b[:, a * chunk:(a + 1) * chunk], jnp.concatenate([ve, vo], axis=1))
    vn = vbuf[slot, past_rows:past_rows + chunk, :].astype(BF16)
    acc = acc2[0:nrow, 0:HEAD_W] + acc2[nrow:2 * nrow, HEAD_W:2 * HEAD_W] + _dot(p_new.astype(BF16), vn)

    o = acc / l
    lam = _diff_lambda(lamv_ref[...], lam_init)
    half = nrow // 2
    d = o[0:half] - lam * o[half:nrow]
    o_ref[...] = _rms(d, g_ref[...]) * (1.0 - lam_init)


def _attn_kernel(pt_ref, sl_ref, lamv_ref, gcol_ref, pos_ref, qt_ref, k_ref, vt_ref,
                 grow_ref, slope2_ref, base2_ref, newmask_ref, q2_ref, kn_ref, vn_ref, ck_hbm, cv_hbm,
                 op_ref, os_ref,
                 qq_ref, m_ref, acc_ref, t0_ref, t1_ref, kbuf, vbuf, sem, s_ref,
                 *, tq, lam_init, n_pages, page_rows, n_heads, chunk):
    seq = (pl.program_id(0) * pl.num_programs(1) + pl.program_id(1)) * pl.num_programs(2) + pl.program_id(2)
    n_seq = pl.num_programs(0) * pl.num_programs(1) * pl.num_programs(2)
    _attn_sample_prefetch(pt_ref, ck_hbm, cv_hbm, kbuf, vbuf, sem, seq, n_seq,
                          n_pages=n_pages, page_rows=page_rows)
    _attn_prompt_body(sl_ref, lamv_ref, gcol_ref, pos_ref, qt_ref, k_ref, vt_ref, op_ref,
                      qq_ref, m_ref, acc_ref, t0_ref, t1_ref, tq=tq, lam_init=lam_init)
    _attn_sample_compute(lamv_ref, grow_ref, slope2_ref, base2_ref, newmask_ref, q2_ref, kn_ref, vn_ref,
                         ck_hbm, cv_hbm, os_ref, kbuf, vbuf, sem, s_ref, seq,
                         n_pages=n_pages, page_rows=page_rows, n_heads=n_heads, lam_init=lam_init, chunk=chunk)


def _attn(qt, kb, vt, slope_parts, lamv, g_col, page_table, q2, knf, vnf, ckf, cvf, g_row, slope2, base2,
          newmask, lam_init, tq):
    b, aw, t = qt.shape
    nh = aw // HEAD_W
    nq = t // tq
    db, nrow2, _ = q2.shape
    nrow = nrow2 // 2
    n_pages = page_table.shape[1]
    chunk = base2.shape[1]
    page_rows = PAGE * nh
    past_rows = n_pages * page_rows
    assert past_rows % (2 * chunk) == 0 and knf.shape[1] <= chunk
    assert db == b * nh * nq, "one decode sequence per prompt grid step"
    i = jnp.arange(tq)
    lane = jnp.arange(HEAD_W)
    feat = jnp.where(lane[None, :] < 2, (i % 256)[:, None], jnp.where(lane[None, :] < 4, (i // 256 * 256)[:, None], 0))
    pos = feat.astype(BF16)
    seq_of = lambda bi, hi, qi: (bi * nh + hi) * nq + qi
    per_seq = lambda a: pl.BlockSpec((None,) + a.shape[1:], lambda bi, hi, qi, pt: (seq_of(bi, hi, qi), 0, 0))
    const = lambda a: pl.BlockSpec(a.shape, lambda bi, hi, qi, pt: (0,) * a.ndim, pipeline_mode=pl.Buffered(1))
    grid_spec = pltpu.PrefetchScalarGridSpec(
        num_scalar_prefetch=1,
        grid=(b, nh, nq),
        in_specs=[
            pl.BlockSpec(memory_space=pltpu.SMEM),
            const(lamv), const(g_col), const(pos),
            pl.BlockSpec((None, HEAD_W, t), lambda bi, hi, qi, pt: (bi, hi, 0)),
            pl.BlockSpec((None, t, HEAD_W), lambda bi, hi, qi, pt: (bi, 0, hi)),
            pl.BlockSpec((None, HEAD_W, t), lambda bi, hi, qi, pt: (bi, hi, 0)),
            const(g_row), const(slope2), const(base2), const(newmask),
            per_seq(q2), per_seq(knf), per_seq(vnf),
            pl.BlockSpec(memory_space=pl.ANY),
            pl.BlockSpec(memory_space=pl.ANY),
        ],
        out_specs=[
            pl.BlockSpec((None, tq, HEAD_W), lambda bi, hi, qi, pt: (bi, qi, hi)),
            pl.BlockSpec((None, nrow // 2, HEAD_W), lambda bi, hi, qi, pt: (seq_of(bi, hi, qi), 0, 0)),
        ],
        scratch_shapes=[
            pltpu.VMEM((1, 2 * HEAD_W, 2 * tq), BF16),
            pltpu.VMEM((1, 1, 2 * tq), F32),
            pltpu.VMEM((1, HEAD_W + 16, 2 * tq), F32),
            pltpu.VMEM((1, tq, 2 * tq), F32),
            pltpu.VMEM((1, tq, 2 * tq), F32),
            pltpu.VMEM((2, past_rows + chunk, HEAD_W), F32),
            pltpu.VMEM((2, past_rows + chunk, HEAD_W), F32),
            pltpu.SemaphoreType.DMA((2, 2)),
            pltpu.VMEM((nrow2, past_rows // 2), F32),
        ],
    )
    return pl.pallas_call(
        functools.partial(_attn_kernel, tq=tq, lam_init=lam_init, n_pages=n_pages, page_rows=page_rows,
                          n_heads=nh, chunk=chunk),
        grid_spec=grid_spec,
        out_shape=[jax.ShapeDtypeStruct((b, t, aw), BF16),
                   jax.ShapeDtypeStruct((db, nrow // 2, HEAD_W), F32)],
        compiler_params=_cparams(3),
        name="attn",
    )(page_table.reshape(-1), slope_parts, lamv, g_col, pos, qt, kb, vt,
      g_row, slope2, base2, newmask, q2, knf, vnf, ckf, cvf)


def _s5_prep_kernel(lre_ref, lim_ref, ldt_ref, bre_ref, bim_ref, cre_ref, cim_ref, dsk_ref,
                    wst_ref, cat_ref, kt_ref, sct_ref):
    lr = lre_ref[...]
    li = lim_ref[...]
    dt = jnp.exp(ldt_ref[...])

    def lam_bar_pow(n):
        mag = jnp.exp(n * (lr * dt))
        ang = n * (li * dt)
        return mag * jnp.cos(ang), mag * jnp.sin(ang)

    a_r, a_i = lam_bar_pow(1.0)
    den = lr * lr + li * li
    nr, ni = a_r - 1.0, a_i
    f_r = (nr * lr + ni * li) / den
    f_i = (ni * lr - nr * li) / den
    b_r, b_i = bre_ref[...], bim_ref[...]
    bb_r = f_r * b_r - f_i * b_i
    bb_i = f_r * b_i + f_i * b_r
    c_r, c_i = cre_ref[...], cim_ref[...]
    gw = b_r.shape[0]
    ns = b_r.shape[1]

    ktau = []
    for tau in range(S5_BLOCK):
        p_r, p_i = lam_bar_pow(float(tau))
        w_r = bb_r * p_r - bb_i * p_i
        w_i = bb_r * p_i + bb_i * p_r
        j = S5_BLOCK - 1 - tau
        wst_ref[j * gw:(j + 1) * gw, 0:ns] = w_r.astype(BF16)
        wst_ref[j * gw:(j + 1) * gw, ns:2 * ns] = w_i.astype(BF16)
        ktau.append(_dot_nt(w_r, c_r, lax.Precision.HIGHEST) - _dot_nt(w_i, c_i, lax.Precision.HIGHEST))
    rr = lax.broadcasted_iota(jnp.int32, (gw, gw), 0)
    cc = lax.broadcasted_iota(jnp.int32, (gw, gw), 1)
    ktau[0] = ktau[0] + jnp.where(rr == cc, dsk_ref[...], 0.0)

    for j in range(S5_BLOCK):
        p_r, p_i = lam_bar_pow(float(j + 1))
        cat_ref[j * gw:(j + 1) * gw, 0:ns] = (c_r * p_r - c_i * p_i).astype(BF16)
        cat_ref[j * gw:(j + 1) * gw, ns:2 * ns] = (-(c_r * p_i + c_i * p_r)).astype(BF16)

    zeros = jnp.zeros((gw, gw), BF16)
    for i in range(S5_BLOCK):
        for j in range(S5_BLOCK):
            blk = ktau[j - i].astype(BF16) if j >= i else zeros
            kt_ref[i * gw:(i + 1) * gw, j * gw:(j + 1) * gw] = blk

    row = lax.broadcasted_iota(jnp.int32, (8, ns), 0)
    for idx, k in enumerate((1, 2, 4)):
        p_r, p_i = lam_bar_pow(float(S5_BLOCK * k))
        sct_ref[2 * idx] = jnp.where(row >= k, p_r, 0.0)
        sct_ref[2 * idx + 1] = jnp.where(row >= k, p_i, 0.0)
    n = (S5_BLOCK * (row + 1)).astype(F32)
    mag = jnp.exp(n * (lr * dt))
    ang = n * (li * dt)
    sct_ref[6] = mag * jnp.cos(ang)
    sct_ref[7] = mag * jnp.sin(ang)


def _s5_prep(lre, lim, ldt, bre, bim, cre, cim, dsk):
    nq, gw, ns = bre.shape
    kw = S5_BLOCK * gw
    per_q = lambda *tail: pl.BlockSpec((None,) + tail, lambda q: (q,) + (0,) * len(tail))
    return pl.pallas_call(
        _s5_prep_kernel,
        grid=(nq,),
        in_specs=[per_q(1, ns), per_q(1, ns), per_q(1, ns), per_q(gw, ns), per_q(gw, ns),
                  per_q(gw, ns), per_q(gw, ns), per_q(1, gw)],
        out_specs=[per_q(kw, 2 * ns), per_q(kw, 2 * ns), per_q(kw, kw), per_q(8, 8, ns)],
        out_shape=[
            jax.ShapeDtypeStruct((nq, kw, 2 * ns), BF16),
            jax.ShapeDtypeStruct((nq, kw, 2 * ns), BF16),
            jax.ShapeDtypeStruct((nq, kw, kw), BF16),
            jax.ShapeDtypeStruct((nq, 8, 8, ns), F32),
        ],
        compiler_params=_cparams(1),
        name="s5_prep",
    )(lre, lim, ldt, bre, bim, cre, cim, dsk)


def _quarter_lanes(u_ref, q, gw, sw):
    return jnp.concatenate(
        [u_ref[:, j * sw + q * gw: j * sw + (q + 1) * gw] for j in range(S5_BLOCK)], axis=1)


def _store_quarter(y_ref, y, q, gw, sw):
    for j in range(S5_BLOCK):
        y_ref[:, j * sw + q * gw: j * sw + (q + 1) * gw] = y[:, j * gw:(j + 1) * gw].astype(y_ref.dtype)


def _s5_prompt_kernel(u_ref, wst_ref, cat_ref, kt_ref, sct_ref, y_ref, st_ref,
                      carry_ref, s_scr, hp_scr, *, rb, gw, sw, ns):
    blk = pl.program_id(1)

    @pl.when(blk == 0)
    def _():
        carry_ref[...] = jnp.zeros(carry_ref.shape, F32)

    row0 = lax.broadcasted_iota(jnp.int32, (8, ns), 0) == 0
    for q in range(S5_QUARTERS):
        uq = _quarter_lanes(u_ref, q, gw, sw)
        s_scr[...] = _dot(uq, wst_ref[q])
        tabs = [sct_ref[q, i] for i in range(8)]

        def tile(t, carry, tabs=tabs):
            cr, ci = carry
            r0 = pl.multiple_of(t * 8, 8)
            xr = s_scr[pl.ds(r0, 8), 0:ns]
            xi = s_scr[pl.ds(r0, 8), ns:2 * ns]
            for idx, k in enumerate((1, 2, 4)):
                mr, mi = tabs[2 * idx], tabs[2 * idx + 1]
                sr = pltpu.roll(xr, k, 0)
                si = pltpu.roll(xi, k, 0)
                xr, xi = xr + (mr * sr - mi * si), xi + (mr * si + mi * sr)
            crb = jnp.broadcast_to(cr, (8, ns))
            cib = jnp.broadcast_to(ci, (8, ns))
            hr = xr + (tabs[6] * crb - tabs[7] * cib)
            hi = xi + (tabs[6] * cib + tabs[7] * crb)
            hp_scr[pl.ds(r0, 8), 0:ns] = jnp.where(row0, crb, pltpu.roll(hr, 1, 0))
            hp_scr[pl.ds(r0, 8), ns:2 * ns] = jnp.where(row0, cib, pltpu.roll(hi, 1, 0))
            return hr[7:8, :], hi[7:8, :]

        cr, ci = lax.fori_loop(0, rb // 8, tile,
                               (carry_ref[q, :, 0:ns], carry_ref[q, :, ns:2 * ns]))
        carry_ref[q, :, 0:ns] = cr
        carry_ref[q, :, ns:2 * ns] = ci
        st_ref[q, :, 0:ns] = cr
        st_ref[q, :, ns:2 * ns] = ci
        y = _dot_nt(hp_scr[...].astype(BF16), cat_ref[q]) + _dot(uq, kt_ref[q])
        _store_quarter(y_ref, y, q, gw, sw)


def _s5_prompt(u4, wst, cat, kt, sct, rb):
    b, rows, lanes = u4.shape
    nq, kw, ns2 = wst.shape
    ns, gw, sw = ns2 // 2, kw // S5_BLOCK, lanes // S5_BLOCK
    return pl.pallas_call(
        functools.partial(_s5_prompt_kernel, rb=rb, gw=gw, sw=sw, ns=ns),
        grid=(b, rows // rb),
        in_specs=[
            pl.BlockSpec((None, rb, lanes), lambda bi, ri: (bi, ri, 0)),
            _const_spec(wst.shape), _const_spec(cat.shape), _const_spec(kt.shape), _const_spec(sct.shape),
        ],
        out_specs=[
            pl.BlockSpec((None, rb, lanes), lambda bi, ri: (bi, ri, 0)),
            pl.BlockSpec((None, nq, 1, ns2), lambda bi, ri: (bi, 0, 0, 0)),
        ],
        out_shape=[
            jax.ShapeDtypeStruct((b, rows, lanes), BF16),
            jax.ShapeDtypeStruct((b, nq, 1, ns2), F32),
        ],
        scratch_shapes=[
            pltpu.VMEM((nq, 1, ns2), F32),
            pltpu.VMEM((rb, ns2), F32),
            pltpu.VMEM((rb, ns2), F32),
        ],
        compiler_params=_cparams(2),
        name="s5_prompt",
    )(u4, wst, cat, kt, sct)


def _s5_sample_kernel(u_ref, h0_ref, wst_ref, cat_ref, kt_ref, sct_ref, y_ref, hout_ref, *, gw, sw, ns):
    for q in range(S5_QUARTERS):
        uq = _quarter_lanes(u_ref, q, gw, sw)
        s = _dot(uq, wst_ref[q])
        h0 = h0_ref[q]
        h0r, h0i = h0[:, 0:ns], h0[:, ns:2 * ns]
        a4r = sct_ref[q, 6][0:1, :]
        a4i = sct_ref[q, 7][0:1, :]
        hout_ref[q, :, 0:ns] = s[:, 0:ns] + (a4r * h0r - a4i * h0i)
        hout_ref[q, :, ns:2 * ns] = s[:, ns:2 * ns] + (a4r * h0i + a4i * h0r)
        y = _dot_nt(h0.astype(BF16), cat_ref[q]) + _dot(uq, kt_ref[q])
        _store_quarter(y_ref, y, q, gw, sw)


def _s5_sample(u4, h0q, wst, cat, kt, sct):
    rows, lanes = u4.shape
    nq, kw, ns2 = wst.shape
    ns, gw, sw = ns2 // 2, kw // S5_BLOCK, lanes // S5_BLOCK
    return pl.pallas_call(
        functools.partial(_s5_sample_kernel, gw=gw, sw=sw, ns=ns),
        grid=(1,),
        in_specs=[_const_spec(u4.shape), _const_spec(h0q.shape), _const_spec(wst.shape),
                  _const_spec(cat.shape), _const_spec(kt.shape), _const_spec(sct.shape)],
        out_specs=[pl.BlockSpec(u4.shape, lambda i: (0, 0)),
                   pl.BlockSpec(h0q.shape, lambda i: (0, 0, 0))],
        out_shape=[jax.ShapeDtypeStruct(u4.shape, BF16), jax.ShapeDtypeStruct(h0q.shape, F32)],
        compiler_params=_cparams(1),
        name="s5_sample",
    )(u4, h0q, wst, cat, kt, sct)


def _mem_kv_kernel(x_ref, g_ref, wk_ref, wv_ref, k_ref, v_ref, kb_ref, vb_ref):
    hn = _rms(x_ref[...], g_ref[...]).astype(BF16)
    k = _dot(hn, wk_ref[...])
    k_ref[...] = k
    kb_ref[...] = k.astype(BF16)
    v = _dot(hn, wv_ref[...])
    v_ref[...] = v
    vb_ref[...] = v.astype(BF16)


def _mem_kv(mem, g, wk_bf, wv_bf, tm):
    n, d = mem.shape
    row = pl.BlockSpec((tm, d), lambda i: (i, 0))
    return pl.pallas_call(
        _mem_kv_kernel,
        grid=(n // tm,),
        in_specs=[row, _const_spec((1, d)), _const_spec(wk_bf.shape), _const_spec(wv_bf.shape)],
        out_specs=[row, row, row, row],
        out_shape=[jax.ShapeDtypeStruct((n, d), F32), jax.ShapeDtypeStruct((n, d), F32),
                   jax.ShapeDtypeStruct((n, d), BF16), jax.ShapeDtypeStruct((n, d), BF16)],
        compiler_params=_cparams(1),
        name="mem_kv",
    )(mem, g, wk_bf, wv_bf)


def _merge_and_query(x, att, sy4_ref, sy_scr, wglu_ref, gso_ref, wout_ref, gpost_ref, gmpre_ref, wmq_ref,
                     mem_scale):
    aw = att.shape[1]
    rows, n_lb = sy4_ref.shape[0], sy_scr.shape[0]
    sw = n_lb * LANES
    for j in range(S5_BLOCK):
        for c in range(n_lb):
            blk = sy4_ref[:, j * sw + c * LANES:j * sw + (c + 1) * LANES].astype(F32)
            sy_scr[c, pl.ds(j, rows, stride=S5_BLOCK), :] = blk
    y = jax.nn.gelu(jnp.concatenate([sy_scr[c] for c in range(n_lb)], axis=1))
    y = y * jax.nn.sigmoid(_dot(y.astype(BF16), wglu_ref[...]))
    y = _rms(y, gso_ref[...])
    mix = _dot(att, wout_ref[0:aw, :]) + _dot(y.astype(BF16), wout_ref[aw:, :])
    x1 = x + _rms(mix, gpost_ref[...])
    hq = _rms(x1, gmpre_ref[...]).astype(BF16)
    qm = (_dot(hq, wmq_ref[...]) * mem_scale).astype(BF16)
    return x1, qm


def _mix_mem_body(x_ref, att_ref, sy_ref, mk_ref, mv_ref, wglu_ref, gso_ref, wout_ref, gpost_ref,
                  gmpre_ref, wmq_ref, wmo_ref, gmpost_ref, o_ref, oh_ref, sy_scr, *, n_heads, mem_scale):
    x1, qm = _merge_and_query(x_ref[...], att_ref[...], sy_ref, sy_scr, wglu_ref, gso_ref, wout_ref,
                              gpost_ref, gmpre_ref, wmq_ref, mem_scale)
    hd = qm.shape[1] // n_heads
    for h in range(n_heads):
        sl = slice(h * hd, (h + 1) * hd)
        s = _dot_nt(qm[:, sl], mk_ref[:, sl])
        p = jnp.exp(s - jnp.max(s, axis=-1, keepdims=True))
        l = jnp.sum(p, axis=-1, keepdims=True)
        oh_ref[:, sl] = (_dot(p.astype(BF16), mv_ref[:, sl]) / l).astype(BF16)
    mo = _dot(oh_ref[...], wmo_ref[...])
    o_ref[...] = x1 + _rms(mo, gmpost_ref[...])


def _mix_q_kernel(x_ref, att_ref, sy_ref, wglu_ref, gso_ref, wout_ref, gpost_ref, gmpre_ref, wmq_ref,
                  x1_ref, qm_ref, sy_scr, *, mem_scale):
    x1, qm = _merge_and_query(x_ref[...], att_ref[...], sy_ref, sy_scr, wglu_ref, gso_ref, wout_ref,
                              gpost_ref, gmpre_ref, wmq_ref, mem_scale)
    x1_ref[...] = x1
    qm_ref[...] = qm


def _mix_q(x, att, sy4, w, n_heads):
    n, d = x.shape
    weights = [w["w_glu"], w["g_ssm_out"], w["w_out"], w["g_mix_post"], w["g_mem_pre"], w["w_mq"]]
    ins = [x, att, sy4] + weights
    return pl.pallas_call(
        functools.partial(_mix_q_kernel, mem_scale=(d // n_heads) ** -0.5),
        grid=(1,),
        in_specs=[_const_spec(a.shape) for a in ins],
        out_specs=[pl.BlockSpec((n, d), lambda i: (0, 0)), pl.BlockSpec((n, d), lambda i: (0, 0))],
        out_shape=[jax.ShapeDtypeStruct((n, d), F32), jax.ShapeDtypeStruct((n, d), BF16)],
        scratch_shapes=[pltpu.VMEM((sy4.shape[1] // S5_BLOCK // LANES, n, LANES), F32)],
        compiler_params=_cparams(1),
        name="mix_q",
    )(*ins)


def _mem_sample_copies(ck_hbm, cv_hbm, kbuf, vbuf, sem, step, n_per, n_mem, n_heads):
    sl = step % 2
    out = []
    for j in range(n_per):
        for kind, (hbm, buf) in enumerate(((ck_hbm, kbuf), (cv_hbm, vbuf))):
            for lb in range(kbuf.shape[2]):
                src = hbm.at[step * n_per + j, :, :, pl.ds(lb * LANES, LANES)]
                dst = buf.at[sl, j, lb].reshape(n_mem, n_heads, LANES)
                out.append(pltpu.make_async_copy(src, dst, sem.at[kind, sl]))
    return out


def _mem_sample_compute(mask_ref, q_ref, o_ref, kbuf, vbuf, slot, j):
    n_halves = kbuf.shape[2]
    kcat = jnp.concatenate([kbuf[slot, j, lb] for lb in range(n_halves)], axis=1).astype(BF16)
    vcat = jnp.concatenate([vbuf[slot, j, lb] for lb in range(n_halves)], axis=1).astype(BF16)
    s = _dot_nt(q_ref[j], kcat) + mask_ref[...]
    p = jnp.exp(s - jnp.max(s, axis=-1, keepdims=True))
    l = jnp.sum(p, axis=-1, keepdims=True)
    o_ref[j] = _dot(p.astype(BF16), vcat) / l


def _mix_mem_kernel(x_ref, att_ref, sy_ref, mk_ref, mv_ref, wglu_ref, gso_ref, wout_ref, gpost_ref,
                    gmpre_ref, wmq_ref, wmo_ref, gmpost_ref, mask_ref, qs_ref, ck_hbm, cv_hbm,
                    o_ref, os_ref, oh_ref, sy_scr, kbuf, vbuf, sem, *, n_heads, mem_scale, n_mem):
    step = pl.program_id(0) * pl.num_programs(1) + pl.program_id(1)
    n_steps = pl.num_programs(0) * pl.num_programs(1)
    n_per = qs_ref.shape[0]
    copies = lambda st: _mem_sample_copies(ck_hbm, cv_hbm, kbuf, vbuf, sem, st, n_per, n_mem, n_heads)

    @pl.when(step == 0)
    def _():
        for cp in copies(0):
            cp.start()

    @pl.when(step + 1 < n_steps)
    def _():
        for cp in copies(step + 1):
            cp.start()

    _mix_mem_body(x_ref, att_ref, sy_ref, mk_ref, mv_ref, wglu_ref, gso_ref, wout_ref, gpost_ref,
                  gmpre_ref, wmq_ref, wmo_ref, gmpost_ref, o_ref, oh_ref, sy_scr,
                  n_heads=n_heads, mem_scale=mem_scale)

    for cp in copies(step):
        cp.wait()
    for j in range(n_per):
        _mem_sample_compute(mask_ref, qs_ref, os_ref, kbuf, vbuf, step % 2, j)


def _mix_mem(x, att, sy4, mkb, mvb, w, qh, ck, cv, mask, tm, n_heads):
    b, t, d = x.shape
    aw, sw = att.shape[2], sy4.shape[2] // S5_BLOCK
    nm = mkb.shape[1]
    nt = t // tm
    db, nrow, hd = qh.shape
    n_ckeys = ck.shape[1]
    n_per = db // (b * nt)
    assert db == n_per * b * nt, "decode sequences must divide evenly over the prompt grid steps"
    tok = lambda width: pl.BlockSpec((None, tm, width), lambda bi, ti: (bi, ti, 0))
    tok4 = pl.BlockSpec((None, tm // S5_BLOCK, S5_BLOCK * sw), lambda bi, ti: (bi, ti, 0))
    per_b = pl.BlockSpec((None, nm, d), lambda bi, ti: (bi, 0, 0))
    per_step = pl.BlockSpec((n_per, nrow, hd), lambda bi, ti: (bi * nt + ti, 0, 0))
    any_spec = pl.BlockSpec(memory_space=pl.ANY)
    weights = [w["w_glu"], w["g_ssm_out"], w["w_out"], w["g_mix_post"], w["g_mem_pre"], w["w_mq"],
               w["w_mo"], w["g_mem_post"], mask]
    return pl.pallas_call(
        functools.partial(_mix_mem_kernel, n_heads=n_heads, mem_scale=(d // n_heads) ** -0.5, n_mem=n_ckeys),
        grid=(b, nt),
        in_specs=[tok(d), tok(aw), tok4, per_b, per_b] + [_const_spec(a.shape) for a in weights]
                 + [per_step, any_spec, any_spec],
        out_specs=[tok(d), per_step],
        out_shape=[jax.ShapeDtypeStruct((b, t, d), F32), jax.ShapeDtypeStruct((db, nrow, hd), F32)],
        scratch_shapes=[
            pltpu.VMEM((tm, d), BF16),
            pltpu.VMEM((sw // LANES, tm, LANES), F32),
            pltpu.VMEM((2, n_per, hd // LANES, n_ckeys * n_heads, LANES), F32),
            pltpu.VMEM((2, n_per, hd // LANES, n_ckeys * n_heads, LANES), F32),
            pltpu.SemaphoreType.DMA((2, 2)),
        ],
        compiler_params=_cparams(2),
        name="mix_mem",
    )(x, att, sy4, mkb, mvb, *weights, qh, ck, cv)


def _mem_out_kernel(x1_ref, o_ref, wmo_ref, g_ref, y_ref):
    mo = _dot(o_ref[...], wmo_ref[...])
    y_ref[...] = x1_ref[...] + _rms(mo, g_ref[...])


def _mem_out(x1, o, wmo, g):
    n, d = x1.shape
    ins = [x1, o, wmo, g]
    return pl.pallas_call(
        _mem_out_kernel,
        grid=(1,),
        in_specs=[_const_spec(a.shape) for a in ins],
        out_specs=pl.BlockSpec((n, d), lambda i: (0, 0)),
        out_shape=jax.ShapeDtypeStruct((n, d), F32),
        compiler_params=_cparams(1),
        name="mem_out",
    )(*ins)


def _ffn_kernel(x_ref, gpre_ref, wg_ref, wu_ref, wd_ref, gpost_ref, o_ref, *, n_chunks):
    x = x_ref[...]
    hn = _rms(x, gpre_ref[...]).astype(BF16)
    ff = wg_ref.shape[1]
    cw = ff // n_chunks
    acc = jnp.zeros(x.shape, F32)
    for c in range(n_chunks):
        sl = slice(c * cw, (c + 1) * cw)
        a = jax.nn.silu(_dot(hn, wg_ref[:, sl])) * _dot(hn, wu_ref[:, sl])
        acc = acc + _dot(a.astype(BF16), wd_ref[sl, :])
    o_ref[...] = x + _rms(acc, gpost_ref[...])


def _ffn(x, gpre, wg, wu, wd, gpost, tm, n_chunks):
    n, d = x.shape
    row = pl.BlockSpec((tm, d), lambda i: (i, 0))
    weights = [gpre, wg, wu, wd, gpost]
    return pl.pallas_call(
        functools.partial(_ffn_kernel, n_chunks=n_chunks),
        grid=(n // tm,),
        in_specs=[row] + [_const_spec(a.shape) for a in weights],
        out_specs=row,
        out_shape=jax.ShapeDtypeStruct((n, d), F32),
        compiler_params=_cparams(1),
        name="ffn",
    )(x, *weights)


def _s5_param_layout(lam_re, lam_im, log_dt, b_re, b_im, c_re, c_im, d_skip):
    g, p, hc = b_re.shape
    nq = S5_QUARTERS
    gq = g // nq
    rowv = lambda a: a.reshape(nq, 1, gq * p)
    ldt = jnp.broadcast_to(log_dt[:, None], (g, p))
    eye = jnp.eye(gq, dtype=F32)

    def embed_b(b):
        bq = b.reshape(nq, gq, p, hc)
        return jnp.einsum("qgph,gk->qghkp", bq, eye).reshape(nq, gq * hc, gq * p)

    def embed_c(c):
        cq = c.reshape(nq, gq, hc, p)
        return jnp.einsum("qghp,gk->qghkp", cq, eye).reshape(nq, gq * hc, gq * p)

    return (rowv(lam_re), rowv(lam_im), rowv(ldt), embed_b(b_re), embed_b(b_im),
            embed_c(c_re), embed_c(c_im), d_skip.reshape(nq, 1, gq * hc))


def _state_to_quarters(re, im):
    n, g, p = re.shape
    gq = g // S5_QUARTERS
    f = lambda a: a.reshape(n, S5_QUARTERS, gq * p).transpose(1, 0, 2)
    return jnp.concatenate([f(re), f(im)], axis=-1)


def _state_from_quarters(st, g, p):
    n = st.shape[0]
    ns = st.shape[-1] // 2
    return st[..., :ns].reshape(n, g, p), st[..., ns:].reshape(n, g, p)


def kernel(x_prompt, x_sample, mem_prompt, cache_k, cache_v, page_table, state_ssm_re, state_ssm_im,
           cache_mem_k, cache_mem_v, g_mix_pre, g_mix_post, w_in, lam_q1, lam_k1, lam_q2, lam_k2, g_subln,
           ssm_lam_re, ssm_lam_im, ssm_log_dt, ssm_b_re, ssm_b_im, ssm_c_re, ssm_c_im, ssm_d, w_glu, g_ssm_out,
           w_out, g_mem_pre, g_mem_post, g_mem_kv, w_mq, w_mk, w_mv, w_mo, g_ffn_pre, g_ffn_post,
           w_gate, w_up, w_down):
    depth = w_in.shape[0]
    b, t, d = x_prompt.shape
    db, dt_, _ = x_sample.shape
    n_mem = mem_prompt.shape[1]
    n_mem_heads = cache_mem_k.shape[3]
    n_heads = cache_k.shape[3]
    aw = n_heads * HEAD_W
    g, p = ssm_lam_re.shape[1], ssm_lam_re.shape[2]
    sw = g * SSM_GROUP
    n_phys = cache_k.shape[1]
    assert dt_ == S5_BLOCK and t % (S5_BLOCK * 256) == 0

    slopes = LOG2E * 2.0 ** (-8.0 * jnp.arange(1, n_heads + 1, dtype=F32) / n_heads)
    slope_hi = slopes.astype(BF16).astype(F32)
    slope_parts = jnp.stack([slopes, slope_hi, (slopes - slope_hi).astype(BF16).astype(F32)])
    row = lambda a: a.reshape(1, -1).astype(F32)
    bf = lambda a: a.astype(BF16)

    branch_lane = jnp.arange(HEAD_W) // HEAD_DIM
    nrow = 2 * n_heads * dt_
    chunk = 256
    r = jnp.arange(nrow)
    r_head, r_tok = (r // dt_) % n_heads, r % dt_
    r_slope = slopes[r_head][:, None]
    c = jnp.arange(chunk)
    c_tok, c_head = c // n_heads, c % n_heads
    own_head = c_head[None, :] == r_head[:, None]
    base = r_slope * c_tok[None, :].astype(F32) + jnp.where(own_head, 0.0, NEG)
    base2 = jnp.concatenate([base, base + r_slope * float(chunk // n_heads)], axis=0)
    slope2 = jnp.concatenate([r_slope, r_slope], axis=0)
    new_ok = own_head & (c[None, :] < dt_ * n_heads) & (c_tok[None, :] <= r_tok[:, None])
    newmask = jnp.where(new_ok, r_slope * c_tok[None, :].astype(F32), NEG)
    mem_row_head = jnp.arange(n_mem_heads * dt_) // dt_
    mem_col_head = jnp.arange(n_mem * n_mem_heads) % n_mem_heads
    mem_mask = jnp.where(mem_col_head[None, :] == mem_row_head[:, None], 0.0, NEG).astype(F32)

    y_p = x_prompt
    y_s = x_sample.reshape(db * dt_, d)
    outs = {k: [] for k in ("kp", "vp", "ks", "vs", "rp", "ip", "rs", "is", "mkp", "mvp")}
    for l in range(depth):
        lam_init = 0.8 - 0.6 * math.exp(-0.3 * l)
        lamv = jnp.stack([lam_q1[l], lam_k1[l], lam_q2[l], lam_k2[l]]).astype(F32)
        gsub = row(g_subln[l])
        w_in_bf = bf(w_in[l])
        w = {
            "w_glu": bf(w_glu[l]), "g_ssm_out": row(g_ssm_out[l]), "w_out": bf(w_out[l]),
            "g_mix_post": row(g_mix_post[l]), "g_mem_pre": row(g_mem_pre[l]), "w_mq": bf(w_mq[l]),
            "w_mo": bf(w_mo[l]), "g_mem_post": row(g_mem_post[l]),
        }

        qp, kp, vp, kpb, vpb, up = _proj(y_p.reshape(b * t, d), row(g_mix_pre[l]), w_in_bf, aw, 1024)
        qs, ks, vs, _, _, us = _proj(y_s, row(g_mix_pre[l]), w_in_bf, aw, 512)

        qh = qs.reshape(db, dt_, n_heads, HEAD_W).transpose(0, 2, 1, 3)
        zq = jnp.zeros((), BF16)
        q32 = jnp.stack([jnp.where(branch_lane == 0, qh, zq), jnp.where(branch_lane == 1, qh, zq)], axis=1)
        q32 = q32.reshape(db, nrow, HEAD_W)
        zq32 = jnp.zeros_like(q32)
        q2 = jnp.concatenate([jnp.concatenate([q32, zq32], axis=2), jnp.concatenate([zq32, q32], axis=2)], axis=1)
        att_p, att_s16 = _attn(jnp.swapaxes(qp.reshape(b, t, aw), 1, 2), kpb.reshape(b, t, aw),
                               jnp.swapaxes(vpb.reshape(b, t, aw), 1, 2),
                               slope_parts, lamv, g_subln[l].reshape(HEAD_W, 1).astype(F32),
                               page_table, q2, ks.reshape(db, dt_ * n_heads, HEAD_W),
                               vs.reshape(db, dt_ * n_heads, HEAD_W),
                               cache_k[l].reshape(n_phys * PAGE * n_heads, HEAD_W),
                               cache_v[l].reshape(n_phys * PAGE * n_heads, HEAD_W),
                               gsub, slope2, base2, newmask, lam_init, 512)
        att_s = bf(att_s16.reshape(db, n_heads, dt_, HEAD_W).transpose(0, 2, 1, 3).reshape(db * dt_, aw))

        tabs = _s5_prep(*_s5_param_layout(ssm_lam_re[l], ssm_lam_im[l], ssm_log_dt[l], ssm_b_re[l],
                                          ssm_b_im[l], ssm_c_re[l], ssm_c_im[l], ssm_d[l]))
        syp, stp = _s5_prompt(up.reshape(b, t // S5_BLOCK, S5_BLOCK * sw), *tabs, 512)
        rp, ip = _state_from_quarters(stp.reshape(b, S5_QUARTERS, -1), g, p)
        sys_, sts = _s5_sample(us, _state_to_quarters(state_ssm_re[l], state_ssm_im[l]), *tabs)
        rs, is_ = _state_from_quarters(sts.transpose(1, 0, 2), g, p)

        mkp, mvp, mkb, mvb = _mem_kv(mem_prompt.reshape(b * n_mem, d), row(g_mem_kv[l]),
                                     bf(w_mk[l]), bf(w_mv[l]), 512)
        x1s, qms = _mix_q(y_s, att_s, sys_, w, n_mem_heads)
        mhd = d // n_mem_heads
        qmh = qms.reshape(db, dt_, n_mem_heads, mhd).transpose(0, 2, 1, 3).reshape(db, n_mem_heads * dt_, mhd)
        y_p, omh = _mix_mem(y_p, att_p, syp, mkb.reshape(b, n_mem, d), mvb.reshape(b, n_mem, d), w,
                            qmh, cache_mem_k[l], cache_mem_v[l], mem_mask, 512, n_mem_heads)
        om = bf(omh.reshape(db, n_mem_heads, dt_, mhd).transpose(0, 2, 1, 3).reshape(db * dt_, d))
        y_s = _mem_out(x1s, om, w["w_mo"], w["g_mem_post"])

        ffw = (row(g_ffn_pre[l]), bf(w_gate[l]), bf(w_up[l]), bf(w_down[l]), row(g_ffn_post[l]))
        y_p = _ffn(y_p.reshape(b * t, d), *ffw, 1024, 11).reshape(b, t, d)
        y_s = _ffn(y_s, *ffw, 512, 2)

        outs["kp"].append(kp.reshape(b, t, n_heads, HEAD_W))
        outs["vp"].append(vp.reshape(b, t, n_heads, HEAD_W))
        outs["ks"].append(ks.reshape(db, dt_, n_heads, HEAD_W))
        outs["vs"].append(vs.reshape(db, dt_, n_heads, HEAD_W))
        outs["rp"].append(rp); outs["ip"].append(ip); outs["rs"].append(rs); outs["is"].append(is_)
        outs["mkp"].append(mkp.reshape(b, n_mem, n_mem_heads, d // n_mem_heads))
        outs["mvp"].append(mvp.reshape(b, n_mem, n_mem_heads, d // n_mem_heads))

    st = lambda k: jnp.stack(outs[k], 0)
    return (y_p, y_s.reshape(db, dt_, d), st("kp"), st("vp"), st("ks"), st("vs"),
            st("rp"), st("ip"), st("rs"), st("is"), st("mkp"), st("mvp"))
```

```python
import functools
import math

import jax
import jax.numpy as jnp
from jax import lax
from jax.experimental import pallas as pl
from jax.experimental.pallas import tpu as pltpu

F32 = jnp.float32
BF16 = jnp.bfloat16

RMS_EPS = 1e-6
HEAD_DIM = 64
HEAD_W = 2 * HEAD_DIM
SSM_GROUP = 16
SSM_STATE = 64
S5_BLOCK = 4
S5_QUARTERS = 4
PAGE = 128
LANES = 128
NEG = -1e30
LOG2E = 1.4426950408889634
VMEM_LIMIT_BYTES = 56 * 1024 * 1024


def _cparams(n_grid_dims):
    return pltpu.CompilerParams(
        dimension_semantics=("arbitrary",) * n_grid_dims,
        vmem_limit_bytes=VMEM_LIMIT_BYTES)


def _const_spec(shape):
    nd = len(shape)
    return pl.BlockSpec(shape, lambda *_: (0,) * nd, pipeline_mode=pl.Buffered(1))


def _rms(x, g):
    ms = jnp.mean(x * x, axis=-1, keepdims=True)
    return x * lax.rsqrt(ms + RMS_EPS) * g


def _dot(a, b):
    return jnp.dot(a, b, preferred_element_type=F32)


def _dot_nt(a, b, precision=None):
    return lax.dot_general(a, b, (((1,), (1,)), ((), ())),
                           preferred_element_type=F32, precision=precision)


def _proj_kernel(x_ref, g_ref, w_ref, q_ref, k_ref, v_ref, kb_ref, vb_ref, u_ref, u_scr, *, aw, tm):
    hn = _rms(x_ref[...], g_ref[...]).astype(BF16)
    q = _dot(hn, w_ref[:, 0:aw])
    q_ref[...] = (q * (HEAD_DIM ** -0.5 * LOG2E)).astype(BF16)
    nh = aw // HEAD_W
    k = _dot(hn, w_ref[:, aw:2 * aw])
    kb_ref[...] = k.astype(BF16)
    v = _dot(hn, w_ref[:, 2 * aw:3 * aw])
    vb_ref[...] = v.astype(BF16)
    for h in range(nh):
        k_ref[pl.ds(h, tm, stride=nh), :] = k[:, h * HEAD_W:(h + 1) * HEAD_W]
        v_ref[pl.ds(h, tm, stride=nh), :] = v[:, h * HEAD_W:(h + 1) * HEAD_W]
    u = _dot(hn, w_ref[:, 3 * aw:])
    n_lb = u_scr.shape[0]
    sw = n_lb * LANES
    for c in range(n_lb):
        u_scr[c] = u[:, c * LANES:(c + 1) * LANES]
    for j in range(S5_BLOCK):
        for c in range(n_lb):
            rows = u_scr[c, pl.ds(j, tm // S5_BLOCK, stride=S5_BLOCK), :]
            u_ref[:, j * sw + c * LANES:j * sw + (c + 1) * LANES] = rows.astype(BF16)


def _proj(x, g, w_bf, aw, tm):
    n, d = x.shape
    sw = w_bf.shape[1] - 3 * aw
    nh = aw // HEAD_W
    row = lambda width: pl.BlockSpec((tm, width), lambda i: (i, 0))
    flat = pl.BlockSpec((tm * nh, HEAD_W), lambda i: (i, 0))
    return pl.pallas_call(
        functools.partial(_proj_kernel, aw=aw, tm=tm),
        grid=(n // tm,),
        in_specs=[row(d), _const_spec((1, d)), _const_spec(w_bf.shape)],
        out_specs=[row(aw), flat, flat, row(aw), row(aw),
                   pl.BlockSpec((tm // S5_BLOCK, S5_BLOCK * sw), lambda i: (i, 0))],
        out_shape=[
            jax.ShapeDtypeStruct((n, aw), BF16),
            jax.ShapeDtypeStruct((n * nh, HEAD_W), F32),
            jax.ShapeDtypeStruct((n * nh, HEAD_W), F32),
            jax.ShapeDtypeStruct((n, aw), BF16),
            jax.ShapeDtypeStruct((n, aw), BF16),
            jax.ShapeDtypeStruct((n // S5_BLOCK, S5_BLOCK * sw), BF16),
        ],
        scratch_shapes=[pltpu.VMEM((sw // LANES, tm, LANES), F32)],
        compiler_params=_cparams(1),
        name="proj",
    )(x, g, w_bf)


def _diff_lambda(lamv, lam_init):
    d1 = jnp.sum(lamv[0:1] * lamv[1:2], axis=-1, keepdims=True)
    d2 = jnp.sum(lamv[2:3] * lamv[3:4], axis=-1, keepdims=True)
    return jnp.exp(d1) - jnp.exp(d2) + lam_init


def _attn_prompt_body(sl_ref, lamv_ref, g_ref, pos_ref, qt_ref, k_ref, vt_ref, o_ref,
                        qq_ref, m_ref, acc_ref, t0_ref, t1_ref, *, tq, lam_init):
    hp = pl.program_id(1)
    qi = pl.program_id(2)
    tk = tq
    n_hh = qq_ref.shape[0]
    heads = [hp * n_hh + hh for hh in range(n_hh)]
    hrows = [slice(hh * HEAD_W, (hh + 1) * HEAD_W) for hh in range(n_hh)]

    def prep_queries(qblk):
        q0 = pl.multiple_of(qblk * tq, tq)
        arow = lax.broadcasted_iota(jnp.int32, (HEAD_W, 2 * tq), 0)
        for hh in range(n_hh):
            qt = qt_ref[hrows[hh], pl.ds(q0, tq)].astype(F32)
            frow = lax.broadcasted_iota(jnp.int32, qt.shape, 0)
            qq_ref[hh, 0:HEAD_W, 0:tq] = jnp.where(frow < HEAD_DIM, qt, 0.0).astype(BF16)
            qq_ref[hh, 0:HEAD_W, tq:2 * tq] = jnp.where(frow >= HEAD_DIM, qt, 0.0).astype(BF16)
            slope_parts = jnp.where((arow & 1) == 0, sl_ref[1, heads[hh]], sl_ref[2, heads[hh]])
            qq_ref[hh, HEAD_W:2 * HEAD_W, :] = jnp.where(arow < 4, slope_parts, 0.0).astype(BF16)

    m_ref[...] = jnp.full(m_ref.shape, NEG, F32)
    acc_ref[...] = jnp.zeros(acc_ref.shape, F32)
    ones = jnp.ones((acc_ref.shape[1] - HEAD_W, tk), BF16)

    def scores(hh, kj):
        k0 = pl.multiple_of(kj * tk, tk)
        ka = jnp.concatenate([k_ref[pl.ds(k0, tk), hrows[hh]], pos_ref[...]], axis=1)
        return _dot(ka, qq_ref[hh])

    def consume(hh, kj, t):
        k0 = pl.multiple_of(kj * tk, tk)
        vta = jnp.concatenate([vt_ref[hrows[hh], pl.ds(k0, tk)], ones], axis=0)
        off = sl_ref[0, heads[hh]] * jnp.full((1, 2 * tq), (kj - qi) * tk, jnp.int32).astype(F32)
        m_old = m_ref[hh]
        m_new = jnp.maximum(m_old, jnp.max(t, axis=0, keepdims=True) + off)
        alpha = jnp.exp2(m_old - m_new)
        p = jnp.exp2(t - (m_new - off)).astype(BF16)
        acc_ref[hh] = alpha * acc_ref[hh] + _dot(vta, p)
        m_ref[hh] = m_new

    tbuf = (t0_ref, t1_ref)

    def step(kj, cur):
        for hh in range(n_hh):
            tbuf[1 - cur][hh] = scores(hh, kj + 1)
        for hh in range(n_hh):
            consume(hh, kj, tbuf[cur][hh])

    @pl.when(qi == 0)
    def _():
        prep_queries(0)
        for hh in range(n_hh):
            t0_ref[hh] = scores(hh, 0)

    def body(i, carry):
        step(2 * i, 0)
        step(2 * i + 1, 1)
        return carry

    lax.fori_loop(0, qi // 2, body, 0)
    odd = qi % 2 == 1

    @pl.when(odd)
    def _():
        step(qi - 1, 0)

    def finish(cur):
        krow = lax.broadcasted_iota(jnp.int32, (tk, 2 * tq), 0)
        qcol = lax.broadcasted_iota(jnp.int32, (tk, 2 * tq), 1)
        qcol = jnp.where(qcol >= tq, qcol - tq, qcol)
        lam = _diff_lambda(lamv_ref[...], lam_init)
        for hh in range(n_hh):
            consume(hh, qi, jnp.where(krow <= qcol, tbuf[cur][hh], NEG))
        prep_queries(jnp.minimum(qi + 1, pl.num_programs(2) - 1))
        for hh in range(n_hh):
            t0_ref[hh] = scores(hh, 0)
        for hh in range(n_hh):
            acc = acc_ref[hh]
            ot = acc[0:HEAD_W] / acc[HEAD_W:HEAD_W + 1]
            d = ot[:, 0:tq] - lam * ot[:, tq:2 * tq]
            ms = jnp.mean(d * d, axis=0, keepdims=True)
            dn = d * lax.rsqrt(ms + RMS_EPS) * g_ref[...] * (1.0 - lam_init)
            o_ref[:, hrows[hh]] = dn.T.astype(BF16)

    @pl.when(odd)
    def _():
        finish(1)

    @pl.when(jnp.logical_not(odd))
    def _():
        finish(0)


def _page_copy(hbm, buf, sem, kind, sl, p, page, page_rows):
    src = hbm.at[pl.ds(pl.multiple_of(page * page_rows, page_rows), page_rows)]
    return pltpu.make_async_copy(src, buf.at[sl, pl.ds(p * page_rows, page_rows)], sem.at[kind, sl])


def _attn_sample_prefetch(pt_ref, ck_hbm, cv_hbm, kbuf, vbuf, sem, seq, n_seq, *, n_pages, page_rows):
    def start_fetch(s, sl):
        for p in range(n_pages):
            page = pt_ref[s * n_pages + p]
            _page_copy(ck_hbm, kbuf, sem, 0, sl, p, page, page_rows).start()
            _page_copy(cv_hbm, vbuf, sem, 1, sl, p, page, page_rows).start()

    @pl.when(seq == 0)
    def _():
        start_fetch(0, 0)

    @pl.when(seq + 1 < n_seq)
    def _():
        start_fetch(seq + 1, 1 - seq % 2)


def _attn_sample_compute(lamv_ref, g_ref, slope2_ref, base2_ref, newmask_ref, q2_ref, kn_ref, vn_ref,
                         ck_hbm, cv_hbm, o_ref, kbuf, vbuf, sem, s_ref, seq,
                         *, n_pages, page_rows, n_heads, lam_init, chunk):
    slot = seq % 2
    past_rows = n_pages * page_rows
    nrow = q2_ref.shape[0] // 2

    for p in range(n_pages):
        _page_copy(ck_hbm, kbuf, sem, 0, slot, p, 0, page_rows).wait()
        _page_copy(cv_hbm, vbuf, sem, 1, slot, p, 0, page_rows).wait()

    pad = jnp.zeros((chunk - kn_ref.shape[0], HEAD_W), F32)
    kbuf[slot, past_rows:past_rows + chunk, :] = jnp.concatenate([kn_ref[...], pad], axis=0)
    vbuf[slot, past_rows:past_rows + chunk, :] = jnp.concatenate([vn_ref[...], pad], axis=0)

    q2 = q2_ref[...]
    slope2 = slope2_ref[...]
    n_pairs = past_rows // (2 * chunk)

    for a in range(n_pairs):
        r0 = 2 * a * chunk
        ke = kbuf[slot, r0:r0 + chunk, :].astype(BF16)
        ko = kbuf[slot, r0 + chunk:r0 + 2 * chunk, :].astype(BF16)
        s2 = _dot_nt(q2, jnp.concatenate([ke, ko], axis=1))
        tok0 = float(2 * a * (chunk // n_heads) - past_rows // n_heads)
        s_ref[:, a * chunk:(a + 1) * chunk] = s2 + base2_ref[...] + slope2 * tok0
    q = q2[0:nrow, 0:HEAD_W]
    kn = kbuf[slot, past_rows:past_rows + chunk, :].astype(BF16)
    s_new = _dot_nt(q, kn) + newmask_ref[...]

    s2 = s_ref[...]
    m2 = jnp.max(s2, axis=-1, keepdims=True)
    m = jnp.maximum(jnp.maximum(m2[0:nrow], m2[nrow:2 * nrow]), jnp.max(s_new, axis=-1, keepdims=True))
    p2 = jnp.exp2(s2 - jnp.concatenate([m, m], axis=0))
    p_new = jnp.exp2(s_new - m)
    l2 = jnp.sum(p2, axis=-1, keepdims=True)
    l = l2[0:nrow] + l2[nrow:2 * nrow] + jnp.sum(p_new, axis=-1, keepdims=True)
    p2b = p2.astype(BF16)
    acc2 = jnp.zeros((2 * nrow, 2 * HEAD_W), F32)
    for a in range(n_pairs):
        r0 = 2 * a * chunk
        ve = vbuf[slot, r0:r0 + chunk, :].astype(BF16)
        vo = vbuf[slot, r0 + chunk:r0 + 2 * chunk, :].astype(BF16)
        acc2 = acc2 + _dot(p2b[:, a * chunk:(a + 1) * chunk], jnp.concatenate([ve, vo], axis=1))
    vn = vbuf[slot, past_rows:past_rows + chunk, :].astype(BF16)
    acc = acc2[0:nrow, 0:HEAD_W] + acc2[nrow:2 * nrow, HEAD_W:2 * HEAD_W] + _dot(p_new.astype(BF16), vn)

    o = acc / l
    lam = _diff_lambda(lamv_ref[...], lam_init)
    half = nrow // 2
    d = o[0:half] - lam * o[half:nrow]
    o_ref[...] = _rms(d, g_ref[...]) * (1.0 - lam_init)


def _attn_kernel(pt_ref, sl_ref, lamv_ref, gcol_ref, pos_ref, qt_ref, k_ref, vt_ref,
                 grow_ref, slope2_ref, base2_ref, newmask_ref, q2_ref, kn_ref, vn_ref, ck_hbm, cv_hbm,
                 op_ref, os_ref,
                 qq_ref, m_ref, acc_ref, t0_ref, t1_ref, kbuf, vbuf, sem, s_ref,
                 *, tq, lam_init, n_pages, page_rows, n_heads, chunk):
    seq = (pl.program_id(0) * pl.num_programs(1) + pl.program_id(1)) * pl.num_programs(2) + pl.program_id(2)
    n_seq = pl.num_programs(0) * pl.num_programs(1) * pl.num_programs(2)
    _attn_sample_prefetch(pt_ref, ck_hbm, cv_hbm, kbuf, vbuf, sem, seq, n_seq,
                          n_pages=n_pages, page_rows=page_rows)
    _attn_prompt_body(sl_ref, lamv_ref, gcol_ref, pos_ref, qt_ref, k_ref, vt_ref, op_ref,
                      qq_ref, m_ref, acc_ref, t0_ref, t1_ref, tq=tq, lam_init=lam_init)
    _attn_sample_compute(lamv_ref, grow_ref, slope2_ref, base2_ref, newmask_ref, q2_ref, kn_ref, vn_ref,
                         ck_hbm, cv_hbm, os_ref, kbuf, vbuf, sem, s_ref, seq,
                         n_pages=n_pages, page_rows=page_rows, n_heads=n_heads, lam_init=lam_init, chunk=chunk)


def _attn(qt, kb, vt, slope_parts, lamv, g_col, page_table, q2, knf, vnf, ckf, cvf, g_row, slope2, base2,
          newmask, lam_init, tq):
    b, aw, t = qt.shape
    nh = aw // HEAD_W
    nq = t // tq
    db, nrow2, _ = q2.shape
    nrow = nrow2 // 2
    n_pages = page_table.shape[1]
    chunk = base2.shape[1]
    page_rows = PAGE * nh
    past_rows = n_pages * page_rows
    assert past_rows % (2 * chunk) == 0 and knf.shape[1] <= chunk
    assert db == b * nh * nq, "one decode sequence per prompt grid step"
    i = jnp.arange(tq)
    lane = jnp.arange(HEAD_W)
    feat = jnp.where(lane[None, :] < 2, (i % 256)[:, None], jnp.where(lane[None, :] < 4, (i // 256 * 256)[:, None], 0))
    pos = feat.astype(BF16)
    seq_of = lambda bi, hi, qi: (bi * nh + hi) * nq + qi
    per_seq = lambda a: pl.BlockSpec((None,) + a.shape[1:], lambda bi, hi, qi, pt: (seq_of(bi, hi, qi), 0, 0))
    const = lambda a: pl.BlockSpec(a.shape, lambda bi, hi, qi, pt: (0,) * a.ndim, pipeline_mode=pl.Buffered(1))
    grid_spec = pltpu.PrefetchScalarGridSpec(
        num_scalar_prefetch=1,
        grid=(b, nh, nq),
        in_specs=[
            pl.BlockSpec(memory_space=pltpu.SMEM),
            const(lamv), const(g_col), const(pos),
            pl.BlockSpec((None, HEAD_W, t), lambda bi, hi, qi, pt: (bi, hi, 0)),
            pl.BlockSpec((None, t, HEAD_W), lambda bi, hi, qi, pt: (bi, 0, hi)),
            pl.BlockSpec((None, HEAD_W, t), lambda bi, hi, qi, pt: (bi, hi, 0)),
            const(g_row), const(slope2), const(base2), const(newmask),
            per_seq(q2), per_seq(knf), per_seq(vnf),
            pl.BlockSpec(memory_space=pl.ANY),
            pl.BlockSpec(memory_space=pl.ANY),
        ],
        out_specs=[
            pl.BlockSpec((None, tq, HEAD_W), lambda bi, hi, qi, pt: (bi, qi, hi)),
            pl.BlockSpec((None, nrow // 2, HEAD_W), lambda bi, hi, qi, pt: (seq_of(bi, hi, qi), 0, 0)),
        ],
        scratch_shapes=[
            pltpu.VMEM((1, 2 * HEAD_W, 2 * tq), BF16),
            pltpu.VMEM((1, 1, 2 * tq), F32),
            pltpu.VMEM((1, HEAD_W + 16, 2 * tq), F32),
            pltpu.VMEM((1, tq, 2 * tq), F32),
            pltpu.VMEM((1, tq, 2 * tq), F32),
            pltpu.VMEM((2, past_rows + chunk, HEAD_W), F32),
            pltpu.VMEM((2, past_rows + chunk, HEAD_W), F32),
            pltpu.SemaphoreType.DMA((2, 2)),
            pltpu.VMEM((nrow2, past_rows // 2), F32),
        ],
    )
    return pl.pallas_call(
        functools.partial(_attn_kernel, tq=tq, lam_init=lam_init, n_pages=n_pages, page_rows=page_rows,
                          n_heads=nh, chunk=chunk),
        grid_spec=grid_spec,
        out_shape=[jax.ShapeDtypeStruct((b, t, aw), BF16),
                   jax.ShapeDtypeStruct((db, nrow // 2, HEAD_W), F32)],
        compiler_params=_cparams(3),
        name="attn",
    )(page_table.reshape(-1), slope_parts, lamv, g_col, pos, qt, kb, vt,
      g_row, slope2, base2, newmask, q2, knf, vnf, ckf, cvf)


def _s5_prep_kernel(lre_ref, lim_ref, ldt_ref, bre_ref, bim_ref, cre_ref, cim_ref, dsk_ref,
                    wst_ref, cat_ref, kt_ref, sct_ref):
    lr = lre_ref[...]
    li = lim_ref[...]
    dt = jnp.exp(ldt_ref[...])

    def lam_bar_pow(n):
        mag = jnp.exp(n * (lr * dt))
        ang = n * (li * dt)
        return mag * jnp.cos(ang), mag * jnp.sin(ang)

    a_r, a_i = lam_bar_pow(1.0)
    den = lr * lr + li * li
    nr, ni = a_r - 1.0, a_i
    f_r = (nr * lr + ni * li) / den
    f_i = (ni * lr - nr * li) / den
    b_r, b_i = bre_ref[...], bim_ref[...]
    bb_r = f_r * b_r - f_i * b_i
    bb_i = f_r * b_i + f_i * b_r
    c_r, c_i = cre_ref[...], cim_ref[...]
    gw = b_r.shape[0]
    ns = b_r.shape[1]

    ktau = []
    for tau in range(S5_BLOCK):
        p_r, p_i = lam_bar_pow(float(tau))
        w_r = bb_r * p_r - bb_i * p_i
        w_i = bb_r * p_i + bb_i * p_r
        j = S5_BLOCK - 1 - tau
        wst_ref[j * gw:(j + 1) * gw, 0:ns] = w_r.astype(BF16)
        wst_ref[j * gw:(j + 1) * gw, ns:2 * ns] = w_i.astype(BF16)
        ktau.append(_dot_nt(w_r, c_r, lax.Precision.HIGHEST) - _dot_nt(w_i, c_i, lax.Precision.HIGHEST))
    rr = lax.broadcasted_iota(jnp.int32, (gw, gw), 0)
    cc = lax.broadcasted_iota(jnp.int32, (gw, gw), 1)
    ktau[0] = ktau[0] + jnp.where(rr == cc, dsk_ref[...], 0.0)

    for j in range(S5_BLOCK):
        p_r, p_i = lam_bar_pow(float(j + 1))
        cat_ref[j * gw:(j + 1) * gw, 0:ns] = (c_r * p_r - c_i * p_i).astype(BF16)
        cat_ref[j * gw:(j + 1) * gw, ns:2 * ns] = (-(c_r * p_i + c_i * p_r)).astype(BF16)

    zeros = jnp.zeros((gw, gw), BF16)
    for i in range(S5_BLOCK):
        for j in range(S5_BLOCK):
            blk = ktau[j - i].astype(BF16) if j >= i else zeros
            kt_ref[i * gw:(i + 1) * gw, j * gw:(j + 1) * gw] = blk

    row = lax.broadcasted_iota(jnp.int32, (8, ns), 0)
    for idx, k in enumerate((1, 2, 4)):
        p_r, p_i = lam_bar_pow(float(S5_BLOCK * k))
        sct_ref[2 * idx] = jnp.where(row >= k, p_r, 0.0)
        sct_ref[2 * idx + 1] = jnp.where(row >= k, p_i, 0.0)
    n = (S5_BLOCK * (row + 1)).astype(F32)
    mag = jnp.exp(n * (lr * dt))
    ang = n * (li * dt)
    sct_ref[6] = mag * jnp.cos(ang)
    sct_ref[7] = mag * jnp.sin(ang)


def _s5_prep(lre, lim, ldt, bre, bim, cre, cim, dsk):
    nq, gw, ns = bre.shape
    kw = S5_BLOCK * gw
    per_q = lambda *tail: pl.BlockSpec((None,) + tail, lambda q: (q,) + (0,) * len(tail))
    return pl.pallas_call(
        _s5_prep_kernel,
        grid=(nq,),
        in_specs=[per_q(1, ns), per_q(1, ns), per_q(1, ns), per_q(gw, ns), per_q(gw, ns),
                  per_q(gw, ns), per_q(gw, ns), per_q(1, gw)],
        out_specs=[per_q(kw, 2 * ns), per_q(kw, 2 * ns), per_q(kw, kw), per_q(8, 8, ns)],
        out_shape=[
            jax.ShapeDtypeStruct((nq, kw, 2 * ns), BF16),
            jax.ShapeDtypeStruct((nq, kw, 2 * ns), BF16),
            jax.ShapeDtypeStruct((nq, kw, kw), BF16),
            jax.ShapeDtypeStruct((nq, 8, 8, ns), F32),
        ],
        compiler_params=_cparams(1),
        name="s5_prep",
    )(lre, lim, ldt, bre, bim, cre, cim, dsk)


def _quarter_lanes(u_ref, q, gw, sw):
    return jnp.concatenate(
        [u_ref[:, j * sw + q * gw: j * sw + (q + 1) * gw] for j in range(S5_BLOCK)], axis=1)


def _store_quarter(y_ref, y, q, gw, sw):
    for j in range(S5_BLOCK):
        y_ref[:, j * sw + q * gw: j * sw + (q + 1) * gw] = y[:, j * gw:(j + 1) * gw].astype(y_ref.dtype)


def _s5_prompt_kernel(u_ref, wst_ref, cat_ref, kt_ref, sct_ref, y_ref, st_ref,
                      carry_ref, s_scr, hp_scr, *, rb, gw, sw, ns):
    blk = pl.program_id(1)

    @pl.when(blk == 0)
    def _():
        carry_ref[...] = jnp.zeros(carry_ref.shape, F32)

    row0 = lax.broadcasted_iota(jnp.int32, (8, ns), 0) == 0
    for q in range(S5_QUARTERS):
        uq = _quarter_lanes(u_ref, q, gw, sw)
        s_scr[...] = _dot(uq, wst_ref[q])
        tabs = [sct_ref[q, i] for i in range(8)]

        def tile(t, carry, tabs=tabs):
            cr, ci = carry
            r0 = pl.multiple_of(t * 8, 8)
            xr = s_scr[pl.ds(r0, 8), 0:ns]
            xi = s_scr[pl.ds(r0, 8), ns:2 * ns]
            for idx, k in enumerate((1, 2, 4)):
                mr, mi = tabs[2 * idx], tabs[2 * idx + 1]
                sr = pltpu.roll(xr, k, 0)
                si = pltpu.roll(xi, k, 0)
                xr, xi = xr + (mr * sr - mi * si), xi + (mr * si + mi * sr)
            crb = jnp.broadcast_to(cr, (8, ns))
            cib = jnp.broadcast_to(ci, (8, ns))
            hr = xr + (tabs[6] * crb - tabs[7] * cib)
            hi = xi + (tabs[6] * cib + tabs[7] * crb)
            hp_scr[pl.ds(r0, 8), 0:ns] = jnp.where(row0, crb, pltpu.roll(hr, 1, 0))
            hp_scr[pl.ds(r0, 8), ns:2 * ns] = jnp.where(row0, cib, pltpu.roll(hi, 1, 0))
            return hr[7:8, :], hi[7:8, :]

        cr, ci = lax.fori_loop(0, rb // 8, tile,
                               (carry_ref[q, :, 0:ns], carry_ref[q, :, ns:2 * ns]))
        carry_ref[q, :, 0:ns] = cr
        carry_ref[q, :, ns:2 * ns] = ci
        st_ref[q, :, 0:ns] = cr
        st_ref[q, :, ns:2 * ns] = ci
        y = _dot_nt(hp_scr[...].astype(BF16), cat_ref[q]) + _dot(uq, kt_ref[q])
        _store_quarter(y_ref, y, q, gw, sw)


def _s5_prompt(u4, wst, cat, kt, sct, rb):
    b, rows, lanes = u4.shape
    nq, kw, ns2 = wst.shape
    ns, gw, sw = ns2 // 2, kw // S5_BLOCK, lanes // S5_BLOCK
    return pl.pallas_call(
        functools.partial(_s5_prompt_kernel, rb=rb, gw=gw, sw=sw, ns=ns),
        grid=(b, rows // rb),
        in_specs=[
            pl.BlockSpec((None, rb, lanes), lambda bi, ri: (bi, ri, 0)),
            _const_spec(wst.shape), _const_spec(cat.shape), _const_spec(kt.shape), _const_spec(sct.shape),
        ],
        out_specs=[
            pl.BlockSpec((None, rb, lanes), lambda bi, ri: (bi, ri, 0)),
            pl.BlockSpec((None, nq, 1, ns2), lambda bi, ri: (bi, 0, 0, 0)),
        ],
        out_shape=[
            jax.ShapeDtypeStruct((b, rows, lanes), BF16),
            jax.ShapeDtypeStruct((b, nq, 1, ns2), F32),
        ],
        scratch_shapes=[
            pltpu.VMEM((nq, 1, ns2), F32),
            pltpu.VMEM((rb, ns2), F32),
            pltpu.VMEM((rb, ns2), F32),
        ],
        compiler_params=_cparams(2),
        name="s5_prompt",
    )(u4, wst, cat, kt, sct)


def _s5_sample_kernel(u_ref, h0_ref, wst_ref, cat_ref, kt_ref, sct_ref, y_ref, hout_ref, *, gw, sw, ns):
    for q in range(S5_QUARTERS):
        uq = _quarter_lanes(u_ref, q, gw, sw)
        s = _dot(uq, wst_ref[q])
        h0 = h0_ref[q]
        h0r, h0i = h0[:, 0:ns], h0[:, ns:2 * ns]
        a4r = sct_ref[q, 6][0:1, :]
        a4i = sct_ref[q, 7][0:1, :]
        hout_ref[q, :, 0:ns] = s[:, 0:ns] + (a4r * h0r - a4i * h0i)
        hout_ref[q, :, ns:2 * ns] = s[:, ns:2 * ns] + (a4r * h0i + a4i * h0r)
        y = _dot_nt(h0.astype(BF16), cat_ref[q]) + _dot(uq, kt_ref[q])
        _store_quarter(y_ref, y, q, gw, sw)


def _s5_sample(u4, h0q, wst, cat, kt, sct):
    rows, lanes = u4.shape
    nq, kw, ns2 = wst.shape
    ns, gw, sw = ns2 // 2, kw // S5_BLOCK, lanes // S5_BLOCK
    return pl.pallas_call(
        functools.partial(_s5_sample_kernel, gw=gw, sw=sw, ns=ns),
        grid=(1,),
        in_specs=[_const_spec(u4.shape), _const_spec(h0q.shape), _const_spec(wst.shape),
                  _const_spec(cat.shape), _const_spec(kt.shape), _const_spec(sct.shape)],
        out_specs=[pl.BlockSpec(u4.shape, lambda i: (0, 0)),
                   pl.BlockSpec(h0q.shape, lambda i: (0, 0, 0))],
        out_shape=[jax.ShapeDtypeStruct(u4.shape, BF16), jax.ShapeDtypeStruct(h0q.shape, F32)],
        compiler_params=_cparams(1),
        name="s5_sample",
    )(u4, h0q, wst, cat, kt, sct)


def _mem_kv_kernel(x_ref, g_ref, wk_ref, wv_ref, k_ref, v_ref, kb_ref, vb_ref):
    hn = _rms(x_ref[...], g_ref[...]).astype(BF16)
    k = _dot(hn, wk_ref[...])
    k_ref[...] = k
    kb_ref[...] = k.astype(BF16)
    v = _dot(hn, wv_ref[...])
    v_ref[...] = v
    vb_ref[...] = v.astype(BF16)


def _mem_kv(mem, g, wk_bf, wv_bf, tm):
    n, d = mem.shape
    row = pl.BlockSpec((tm, d), lambda i: (i, 0))
    return pl.pallas_call(
        _mem_kv_kernel,
        grid=(n // tm,),
        in_specs=[row, _const_spec((1, d)), _const_spec(wk_bf.shape), _const_spec(wv_bf.shape)],
        out_specs=[row, row, row, row],
        out_shape=[jax.ShapeDtypeStruct((n, d), F32), jax.ShapeDtypeStruct((n, d), F32),
                   jax.ShapeDtypeStruct((n, d), BF16), jax.ShapeDtypeStruct((n, d), BF16)],
        compiler_params=_cparams(1),
        name="mem_kv",
    )(mem, g, wk_bf, wv_bf)


def _merge_and_query(x, att, sy4_ref, sy_scr, wglu_ref, gso_ref, wout_ref, gpost_ref, gmpre_ref, wmq_ref,
                     mem_scale):
    aw = att.shape[1]
    rows, n_lb = sy4_ref.shape[0], sy_scr.shape[0]
    sw = n_lb * LANES
    for j in range(S5_BLOCK):
        for c in range(n_lb):
            blk = sy4_ref[:, j * sw + c * LANES:j * sw + (c + 1) * LANES].astype(F32)
            sy_scr[c, pl.ds(j, rows, stride=S5_BLOCK), :] = blk
    y = jax.nn.gelu(jnp.concatenate([sy_scr[c] for c in range(n_lb)], axis=1))
    y = y * jax.nn.sigmoid(_dot(y.astype(BF16), wglu_ref[...]))
    y = _rms(y, gso_ref[...])
    mix = _dot(att, wout_ref[0:aw, :]) + _dot(y.astype(BF16), wout_ref[aw:, :])
    x1 = x + _rms(mix, gpost_ref[...])
    hq = _rms(x1, gmpre_ref[...]).astype(BF16)
    qm = (_dot(hq, wmq_ref[...]) * mem_scale).astype(BF16)
    return x1, qm


def _mix_mem_body(x_ref, att_ref, sy_ref, mk_ref, mv_ref, wglu_ref, gso_ref, wout_ref, gpost_ref,
                  gmpre_ref, wmq_ref, wmo_ref, gmpost_ref, o_ref, oh_ref, sy_scr, *, n_heads, mem_scale):
    x1, qm = _merge_and_query(x_ref[...], att_ref[...], sy_ref, sy_scr, wglu_ref, gso_ref, wout_ref,
                              gpost_ref, gmpre_ref, wmq_ref, mem_scale)
    hd = qm.shape[1] // n_heads
    sls = [slice(h * hd, (h + 1) * hd) for h in range(n_heads)]
    ss = [_dot_nt(qm[:, sl], mk_ref[:, sl]) for sl in sls]
    ps = [jnp.exp(s - jnp.max(s, axis=-1, keepdims=True)) for s in ss]
    for sl, p in zip(sls, ps):
        l = jnp.sum(p, axis=-1, keepdims=True)
        oh_ref[:, sl] = (_dot(p.astype(BF16), mv_ref[:, sl]) / l).astype(BF16)
    mo = _dot(oh_ref[...], wmo_ref[...])
    o_ref[...] = x1 + _rms(mo, gmpost_ref[...])


def _mix_q_kernel(x_ref, att_ref, sy_ref, wglu_ref, gso_ref, wout_ref, gpost_ref, gmpre_ref, wmq_ref,
                  x1_ref, qm_ref, sy_scr, *, mem_scale):
    x1, qm = _merge_and_query(x_ref[...], att_ref[...], sy_ref, sy_scr, wglu_ref, gso_ref, wout_ref,
                              gpost_ref, gmpre_ref, wmq_ref, mem_scale)
    x1_ref[...] = x1
    qm_ref[...] = qm


def _mix_q(x, att, sy4, w, n_heads):
    n, d = x.shape
    weights = [w["w_glu"], w["g_ssm_out"], w["w_out"], w["g_mix_post"], w["g_mem_pre"], w["w_mq"]]
    ins = [x, att, sy4] + weights
    return pl.pallas_call(
        functools.partial(_mix_q_kernel, mem_scale=(d // n_heads) ** -0.5),
        grid=(1,),
        in_specs=[_const_spec(a.shape) for a in ins],
        out_specs=[pl.BlockSpec((n, d), lambda i: (0, 0)), pl.BlockSpec((n, d), lambda i: (0, 0))],
        out_shape=[jax.ShapeDtypeStruct((n, d), F32), jax.ShapeDtypeStruct((n, d), BF16)],
        scratch_shapes=[pltpu.VMEM((sy4.shape[1] // S5_BLOCK // LANES, n, LANES), F32)],
        compiler_params=_cparams(1),
        name="mix_q",
    )(*ins)


def _mem_sample_copies(ck_hbm, cv_hbm, kbuf, vbuf, sem, step, n_per, n_mem, n_heads):
    sl = step % 2
    out = []
    for j in range(n_per):
        for kind, (hbm, buf) in enumerate(((ck_hbm, kbuf), (cv_hbm, vbuf))):
            for lb in range(kbuf.shape[2]):
                src = hbm.at[step * n_per + j, :, :, pl.ds(lb * LANES, LANES)]
                dst = buf.at[sl, j, lb].reshape(n_mem, n_heads, LANES)
                out.append(pltpu.make_async_copy(src, dst, sem.at[kind, sl]))
    return out


def _mem_sample_compute(mask_ref, q_ref, o_ref, kbuf, vbuf, slot, js):
    n_halves = kbuf.shape[2]
    cat = lambda buf, j: jnp.concatenate([buf[slot, j, lb] for lb in range(n_halves)], axis=1).astype(BF16)
    ss = [_dot_nt(q_ref[j], cat(kbuf, j)) + mask_ref[...] for j in js]
    ps = [jnp.exp(s - jnp.max(s, axis=-1, keepdims=True)) for s in ss]
    ls = [jnp.sum(p, axis=-1, keepdims=True) for p in ps]
    for j, p, l in zip(js, ps, ls):
        o_ref[j] = _dot(p.astype(BF16), cat(vbuf, j)) / l


def _mix_mem_kernel(x_ref, att_ref, sy_ref, mk_ref, mv_ref, wglu_ref, gso_ref, wout_ref, gpost_ref,
                    gmpre_ref, wmq_ref, wmo_ref, gmpost_ref, mask_ref, qs_ref, ck_hbm, cv_hbm,
                    o_ref, os_ref, oh_ref, sy_scr, kbuf, vbuf, sem, *, n_heads, mem_scale, n_mem):
    step = pl.program_id(0) * pl.num_programs(1) + pl.program_id(1)
    n_steps = pl.num_programs(0) * pl.num_programs(1)
    n_per = qs_ref.shape[0]
    copies = lambda st: _mem_sample_copies(ck_hbm, cv_hbm, kbuf, vbuf, sem, st, n_per, n_mem, n_heads)

    @pl.when(step == 0)
    def _():
        for cp in copies(0):
            cp.start()

    @pl.when(step + 1 < n_steps)
    def _():
        for cp in copies(step + 1):
            cp.start()

    _mix_mem_body(x_ref, att_ref, sy_ref, mk_ref, mv_ref, wglu_ref, gso_ref, wout_ref, gpost_ref,
                  gmpre_ref, wmq_ref, wmo_ref, gmpost_ref, o_ref, oh_ref, sy_scr,
                  n_heads=n_heads, mem_scale=mem_scale)

    for cp in copies(step):
        cp.wait()
    _mem_sample_compute(mask_ref, qs_ref, os_ref, kbuf, vbuf, step % 2, range(n_per))


def _mix_mem(x, att, sy4, mkb, mvb, w, qh, ck, cv, mask, tm, n_heads):
    b, t, d = x.shape
    aw, sw = att.shape[2], sy4.shape[2] // S5_BLOCK
    nm = mkb.shape[1]
    nt = t // tm
    db, nrow, hd = qh.shape
    n_ckeys = ck.shape[1]
    n_per = db // (b * nt)
    assert db == n_per * b * nt, "decode sequences must divide evenly over the prompt grid steps"
    tok = lambda width: pl.BlockSpec((None, tm, width), lambda bi, ti: (bi, ti, 0))
    tok4 = pl.BlockSpec((None, tm // S5_BLOCK, S5_BLOCK * sw), lambda bi, ti: (bi, ti, 0))
    per_b = pl.BlockSpec((None, nm, d), lambda bi, ti: (bi, 0, 0))
    per_step = pl.BlockSpec((n_per, nrow, hd), lambda bi, ti: (bi * nt + ti, 0, 0))
    any_spec = pl.BlockSpec(memory_space=pl.ANY)
    weights = [w["w_glu"], w["g_ssm_out"], w["w_out"], w["g_mix_post"], w["g_mem_pre"], w["w_mq"],
               w["w_mo"], w["g_mem_post"], mask]
    return pl.pallas_call(
        functools.partial(_mix_mem_kernel, n_heads=n_heads, mem_scale=(d // n_heads) ** -0.5, n_mem=n_ckeys),
        grid=(b, nt),
        in_specs=[tok(d), tok(aw), tok4, per_b, per_b] + [_const_spec(a.shape) for a in weights]
                 + [per_step, any_spec, any_spec],
        out_specs=[tok(d), per_step],
        out_shape=[jax.ShapeDtypeStruct((b, t, d), F32), jax.ShapeDtypeStruct((db, nrow, hd), F32)],
        scratch_shapes=[
            pltpu.VMEM((tm, d), BF16),
            pltpu.VMEM((sw // LANES, tm, LANES), F32),
            pltpu.VMEM((2, n_per, hd // LANES, n_ckeys * n_heads, LANES), F32),
            pltpu.VMEM((2, n_per, hd // LANES, n_ckeys * n_heads, LANES), F32),
            pltpu.SemaphoreType.DMA((2, 2)),
        ],
        compiler_params=_cparams(2),
        name="mix_mem",
    )(x, att, sy4, mkb, mvb, *weights, qh, ck, cv)


def _mem_out_kernel(x1_ref, o_ref, wmo_ref, g_ref, y_ref):
    mo = _dot(o_ref[...], wmo_ref[...])
    y_ref[...] = x1_ref[...] + _rms(mo, g_ref[...])


def _mem_out(x1, o, wmo, g):
    n, d = x1.shape
    ins = [x1, o, wmo, g]
    return pl.pallas_call(
        _mem_out_kernel,
        grid=(1,),
        in_specs=[_const_spec(a.shape) for a in ins],
        out_specs=pl.BlockSpec((n, d), lambda i: (0, 0)),
        out_shape=jax.ShapeDtypeStruct((n, d), F32),
        compiler_params=_cparams(1),
        name="mem_out",
    )(*ins)


def _ffn_kernel(x_ref, gpre_ref, wg_ref, wu_ref, wd_ref, gpost_ref, o_ref, *, n_chunks):
    x = x_ref[...]
    hn = _rms(x, gpre_ref[...]).astype(BF16)
    ff = wg_ref.shape[1]
    cw = ff // n_chunks
    acc = jnp.zeros(x.shape, F32)
    for c in range(n_chunks):
        sl = slice(c * cw, (c + 1) * cw)
        a = jax.nn.silu(_dot(hn, wg_ref[:, sl])) * _dot(hn, wu_ref[:, sl])
        acc = acc + _dot(a.astype(BF16), wd_ref[sl, :])
    o_ref[...] = x + _rms(acc, gpost_ref[...])


def _ffn(x, gpre, wg, wu, wd, gpost, tm, n_chunks):
    n, d = x.shape
    row = pl.BlockSpec((tm, d), lambda i: (i, 0))
    weights = [gpre, wg, wu, wd, gpost]
    return pl.pallas_call(
        functools.partial(_ffn_kernel, n_chunks=n_chunks),
        grid=(n // tm,),
        in_specs=[row] + [_const_spec(a.shape) for a in weights],
        out_specs=row,
        out_shape=jax.ShapeDtypeStruct((n, d), F32),
        compiler_params=_cparams(1),
        name="ffn",
    )(x, *weights)


def _s5_param_layout(lam_re, lam_im, log_dt, b_re, b_im, c_re, c_im, d_skip):
    g, p, hc = b_re.shape
    nq = S5_QUARTERS
    gq = g // nq
    rowv = lambda a: a.reshape(nq, 1, gq * p)
    ldt = jnp.broadcast_to(log_dt[:, None], (g, p))
    eye = jnp.eye(gq, dtype=F32)

    def embed_b(b):
        bq = b.reshape(nq, gq, p, hc)
        return jnp.einsum("qgph,gk->qghkp", bq, eye).reshape(nq, gq * hc, gq * p)

    def embed_c(c):
        cq = c.reshape(nq, gq, hc, p)
        return jnp.einsum("qghp,gk->qghkp", cq, eye).reshape(nq, gq * hc, gq * p)

    return (rowv(lam_re), rowv(lam_im), rowv(ldt), embed_b(b_re), embed_b(b_im),
            embed_c(c_re), embed_c(c_im), d_skip.reshape(nq, 1, gq * hc))


def _state_to_quarters(re, im):
    n, g, p = re.shape
    gq = g // S5_QUARTERS
    f = lambda a: a.reshape(n, S5_QUARTERS, gq * p).transpose(1, 0, 2)
    return jnp.concatenate([f(re), f(im)], axis=-1)


def _state_from_quarters(st, g, p):
    n = st.shape[0]
    ns = st.shape[-1] // 2
    return st[..., :ns].reshape(n, g, p), st[..., ns:].reshape(n, g, p)


def kernel(x_prompt, x_sample, mem_prompt, cache_k, cache_v, page_table, state_ssm_re, state_ssm_im,
           cache_mem_k, cache_mem_v, g_mix_pre, g_mix_post, w_in, lam_q1, lam_k1, lam_q2, lam_k2, g_subln,
           ssm_lam_re, ssm_lam_im, ssm_log_dt, ssm_b_re, ssm_b_im, ssm_c_re, ssm_c_im, ssm_d, w_glu, g_ssm_out,
           w_out, g_mem_pre, g_mem_post, g_mem_kv, w_mq, w_mk, w_mv, w_mo, g_ffn_pre, g_ffn_post,
           w_gate, w_up, w_down):
    depth = w_in.shape[0]
    b, t, d = x_prompt.shape
    db, dt_, _ = x_sample.shape
    n_mem = mem_prompt.shape[1]
    n_mem_heads = cache_mem_k.shape[3]
    n_heads = cache_k.shape[3]
    aw = n_heads * HEAD_W
    g, p = ssm_lam_re.shape[1], ssm_lam_re.shape[2]
    sw = g * SSM_GROUP
    n_phys = cache_k.shape[1]
    assert dt_ == S5_BLOCK and t % (S5_BLOCK * 256) == 0

    slopes = LOG2E * 2.0 ** (-8.0 * jnp.arange(1, n_heads + 1, dtype=F32) / n_heads)
    slope_hi = slopes.astype(BF16).astype(F32)
    slope_parts = jnp.stack([slopes, slope_hi, (slopes - slope_hi).astype(BF16).astype(F32)])
    row = lambda a: a.reshape(1, -1).astype(F32)
    bf = lambda a: a.astype(BF16)

    branch_lane = jnp.arange(HEAD_W) // HEAD_DIM
    nrow = 2 * n_heads * dt_
    chunk = 256
    r = jnp.arange(nrow)
    r_head, r_tok = (r // dt_) % n_heads, r % dt_
    r_slope = slopes[r_head][:, None]
    c = jnp.arange(chunk)
    c_tok, c_head = c // n_heads, c % n_heads
    own_head = c_head[None, :] == r_head[:, None]
    base = r_slope * c_tok[None, :].astype(F32) + jnp.where(own_head, 0.0, NEG)
    base2 = jnp.concatenate([base, base + r_slope * float(chunk // n_heads)], axis=0)
    slope2 = jnp.concatenate([r_slope, r_slope], axis=0)
    new_ok = own_head & (c[None, :] < dt_ * n_heads) & (c_tok[None, :] <= r_tok[:, None])
    newmask = jnp.where(new_ok, r_slope * c_tok[None, :].astype(F32), NEG)
    mem_row_head = jnp.arange(n_mem_heads * dt_) // dt_
    mem_col_head = jnp.arange(n_mem * n_mem_heads) % n_mem_heads
    mem_mask = jnp.where(mem_col_head[None, :] == mem_row_head[:, None], 0.0, NEG).astype(F32)

    y_p = x_prompt
    y_s = x_sample.reshape(db * dt_, d)
    outs = {k: [] for k in ("kp", "vp", "ks", "vs", "rp", "ip", "rs", "is", "mkp", "mvp")}
    for l in range(depth):
        lam_init = 0.8 - 0.6 * math.exp(-0.3 * l)
        lamv = jnp.stack([lam_q1[l], lam_k1[l], lam_q2[l], lam_k2[l]]).astype(F32)
        gsub = row(g_subln[l])
        w_in_bf = bf(w_in[l])
        w = {
            "w_glu": bf(w_glu[l]), "g_ssm_out": row(g_ssm_out[l]), "w_out": bf(w_out[l]),
            "g_mix_post": row(g_mix_post[l]), "g_mem_pre": row(g_mem_pre[l]), "w_mq": bf(w_mq[l]),
            "w_mo": bf(w_mo[l]), "g_mem_post": row(g_mem_post[l]),
        }

        qp, kp, vp, kpb, vpb, up = _proj(y_p.reshape(b * t, d), row(g_mix_pre[l]), w_in_bf, aw, 1024)
        qs, ks, vs, _, _, us = _proj(y_s, row(g_mix_pre[l]), w_in_bf, aw, 512)

        qh = qs.reshape(db, dt_, n_heads, HEAD_W).transpose(0, 2, 1, 3)
        zq = jnp.zeros((), BF16)
        q32 = jnp.stack([jnp.where(branch_lane == 0, qh, zq), jnp.where(branch_lane == 1, qh, zq)], axis=1)
        q32 = q32.reshape(db, nrow, HEAD_W)
        zq32 = jnp.zeros_like(q32)
        q2 = jnp.concatenate([jnp.concatenate([q32, zq32], axis=2), jnp.concatenate([zq32, q32], axis=2)], axis=1)
        att_p, att_s16 = _attn(jnp.swapaxes(qp.reshape(b, t, aw), 1, 2), kpb.reshape(b, t, aw),
                               jnp.swapaxes(vpb.reshape(b, t, aw), 1, 2),
                               slope_parts, lamv, g_subln[l].reshape(HEAD_W, 1).astype(F32),
                               page_table, q2, ks.reshape(db, dt_ * n_heads, HEAD_W),
                               vs.reshape(db, dt_ * n_heads, HEAD_W),
                               cache_k[l].reshape(n_phys * PAGE * n_heads, HEAD_W),
                               cache_v[l].reshape(n_phys * PAGE * n_heads, HEAD_W),
                               gsub, slope2, base2, newmask, lam_init, 512)
        att_s = bf(att_s16.reshape(db, n_heads, dt_, HEAD_W).transpose(0, 2, 1, 3).reshape(db * dt_, aw))

        tabs = _s5_prep(*_s5_param_layout(ssm_lam_re[l], ssm_lam_im[l], ssm_log_dt[l], ssm_b_re[l],
                                          ssm_b_im[l], ssm_c_re[l], ssm_c_im[l], ssm_d[l]))
        syp, stp = _s5_prompt(up.reshape(b, t // S5_BLOCK, S5_BLOCK * sw), *tabs, 512)
        rp, ip = _state_from_quarters(stp.reshape(b, S5_QUARTERS, -1), g, p)
        sys_, sts = _s5_sample(us, _state_to_quarters(state_ssm_re[l], state_ssm_im[l]), *tabs)
        rs, is_ = _state_from_quarters(sts.transpose(1, 0, 2), g, p)

        mkp, mvp, mkb, mvb = _mem_kv(mem_prompt.reshape(b * n_mem, d), row(g_mem_kv[l]),
                                     bf(w_mk[l]), bf(w_mv[l]), 512)
        x1s, qms = _mix_q(y_s, att_s, sys_, w, n_mem_heads)
        mhd = d // n_mem_heads
        qmh = qms.reshape(db, dt_, n_mem_heads, mhd).transpose(0, 2, 1, 3).reshape(db, n_mem_heads * dt_, mhd)
        y_p, omh = _mix_mem(y_p, att_p, syp, mkb.reshape(b, n_mem, d), mvb.reshape(b, n_mem, d), w,
                            qmh, cache_mem_k[l], cache_mem_v[l], mem_mask, 512, n_mem_heads)
        om = bf(omh.reshape(db, n_mem_heads, dt_, mhd).transpose(0, 2, 1, 3).reshape(db * dt_, d))
        y_s = _mem_out(x1s, om, w["w_mo"], w["g_mem_post"])

        ffw = (row(g_ffn_pre[l]), bf(w_gate[l]), bf(w_up[l]), bf(w_down[l]), row(g_ffn_post[l]))
        y_p = _ffn(y_p.reshape(b * t, d), *ffw, 1024, 11).reshape(b, t, d)
        y_s = _ffn(y_s, *ffw, 512, 2)

        outs["kp"].append(kp.reshape(b, t, n_heads, HEAD_W))
        outs["vp"].append(vp.reshape(b, t, n_heads, HEAD_W))
        outs["ks"].append(ks.reshape(db, dt_, n_heads, HEAD_W))
        outs["vs"].append(vs.reshape(db, dt_, n_heads, HEAD_W))
        outs["rp"].append(rp); outs["ip"].append(ip); outs["rs"].append(rs); outs["is"].append(is_)
        outs["mkp"].append(mkp.reshape(b, n_mem, n_mem_heads, d // n_mem_heads))
        outs["mvp"].append(mvp.reshape(b, n_mem, n_mem_heads, d // n_mem_heads))

    st = lambda k: jnp.stack(outs[k], 0)
    return (y_p, y_s.reshape(db, dt_, d), st("kp"), st("vp"), st("ks"), st("vs"),
            st("rp"), st("ip"), st("rs"), st("is"), st("mkp"), st("mvp"))
```

```python
import functools
import math

import jax
import jax.numpy as jnp
from jax import lax
from jax.experimental import pallas as pl
from jax.experimental.pallas import tpu as pltpu

F32 = jnp.float32
BF16 = jnp.bfloat16

RMS_EPS = 1e-6
HEAD_DIM = 64
HEAD_W = 2 * HEAD_DIM
SSM_GROUP = 16
SSM_STATE = 64
S5_BLOCK = 4
S5_QUARTERS = 4
PAGE = 128
LANES = 128
DECODE_GROUPS = 4
NEG = -1e30
LOG2E = 1.4426950408889634
VMEM_LIMIT_BYTES = 56 * 1024 * 1024


def _cparams(n_grid_dims):
    return pltpu.CompilerParams(
        dimension_semantics=("arbitrary",) * n_grid_dims,
        vmem_limit_bytes=VMEM_LIMIT_BYTES)


def _const_spec(shape):
    nd = len(shape)
    return pl.BlockSpec(shape, lambda *_: (0,) * nd, pipeline_mode=pl.Buffered(1))


def _rms(x, g):
    ms = jnp.mean(x * x, axis=-1, keepdims=True)
    return x * lax.rsqrt(ms + RMS_EPS) * g


def _dot(a, b):
    return jnp.dot(a, b, preferred_element_type=F32)


def _dot_nt(a, b, precision=None):
    return lax.dot_general(a, b, (((1,), (1,)), ((), ())),
                           preferred_element_type=F32, precision=precision)


def _proj_kernel(x_ref, g_ref, w_ref, q_ref, k_ref, v_ref, kb_ref, vb_ref, u_ref, u_scr, *, aw, tm):
    hn = _rms(x_ref[...], g_ref[...]).astype(BF16)
    q = _dot(hn, w_ref[:, 0:aw])
    q_ref[...] = (q * (HEAD_DIM ** -0.5 * LOG2E)).astype(BF16)
    nh = aw // HEAD_W
    k = _dot(hn, w_ref[:, aw:2 * aw])
    kb_ref[...] = k.astype(BF16)
    v = _dot(hn, w_ref[:, 2 * aw:3 * aw])
    vb_ref[...] = v.astype(BF16)
    for h in range(nh):
        k_ref[pl.ds(h, tm, stride=nh), :] = k[:, h * HEAD_W:(h + 1) * HEAD_W]
        v_ref[pl.ds(h, tm, stride=nh), :] = v[:, h * HEAD_W:(h + 1) * HEAD_W]
    u = _dot(hn, w_ref[:, 3 * aw:])
    n_lb = u_scr.shape[0]
    sw = n_lb * LANES
    for c in range(n_lb):
        u_scr[c] = u[:, c * LANES:(c + 1) * LANES]
    for j in range(S5_BLOCK):
        for c in range(n_lb):
            rows = u_scr[c, pl.ds(j, tm // S5_BLOCK, stride=S5_BLOCK), :]
            u_ref[:, j * sw + c * LANES:j * sw + (c + 1) * LANES] = rows.astype(BF16)


def _proj(x, g, w_bf, aw, tm):
    n, d = x.shape
    sw = w_bf.shape[1] - 3 * aw
    nh = aw // HEAD_W
    row = lambda width: pl.BlockSpec((tm, width), lambda i: (i, 0))
    flat = pl.BlockSpec((tm * nh, HEAD_W), lambda i: (i, 0))
    return pl.pallas_call(
        functools.partial(_proj_kernel, aw=aw, tm=tm),
        grid=(n // tm,),
        in_specs=[row(d), _const_spec((1, d)), _const_spec(w_bf.shape)],
        out_specs=[row(aw), flat, flat, row(aw), row(aw),
                   pl.BlockSpec((tm // S5_BLOCK, S5_BLOCK * sw), lambda i: (i, 0))],
        out_shape=[
            jax.ShapeDtypeStruct((n, aw), BF16),
            jax.ShapeDtypeStruct((n * nh, HEAD_W), F32),
            jax.ShapeDtypeStruct((n * nh, HEAD_W), F32),
            jax.ShapeDtypeStruct((n, aw), BF16),
            jax.ShapeDtypeStruct((n, aw), BF16),
            jax.ShapeDtypeStruct((n // S5_BLOCK, S5_BLOCK * sw), BF16),
        ],
        scratch_shapes=[pltpu.VMEM((sw // LANES, tm, LANES), F32)],
        compiler_params=_cparams(1),
        name="proj",
    )(x, g, w_bf)


def _diff_lambda(lamv, lam_init):
    d1 = jnp.sum(lamv[0:1] * lamv[1:2], axis=-1, keepdims=True)
    d2 = jnp.sum(lamv[2:3] * lamv[3:4], axis=-1, keepdims=True)
    return jnp.exp(d1) - jnp.exp(d2) + lam_init


def _attn_prompt_body(sl_ref, lamv_ref, g_ref, pos_ref, qt_ref, k_ref, vt_ref, o_ref,
                        qq_ref, m_ref, acc_ref, t0_ref, t1_ref, *, tq, lam_init):
    hp = pl.program_id(1)
    qi = pl.program_id(2)
    tk = tq
    n_hh = qq_ref.shape[0]
    heads = [hp * n_hh + hh for hh in range(n_hh)]
    hrows = [slice(hh * HEAD_W, (hh + 1) * HEAD_W) for hh in range(n_hh)]

    def prep_queries(qblk):
        q0 = pl.multiple_of(qblk * tq, tq)
        arow = lax.broadcasted_iota(jnp.int32, (HEAD_W, 2 * tq), 0)
        for hh in range(n_hh):
            qt = qt_ref[hrows[hh], pl.ds(q0, tq)].astype(F32)
            frow = lax.broadcasted_iota(jnp.int32, qt.shape, 0)
            qq_ref[hh, 0:HEAD_W, 0:tq] = jnp.where(frow < HEAD_DIM, qt, 0.0).astype(BF16)
            qq_ref[hh, 0:HEAD_W, tq:2 * tq] = jnp.where(frow >= HEAD_DIM, qt, 0.0).astype(BF16)
            slope_parts = jnp.where((arow & 1) == 0, sl_ref[1, heads[hh]], sl_ref[2, heads[hh]])
            qq_ref[hh, HEAD_W:2 * HEAD_W, :] = jnp.where(arow < 4, slope_parts, 0.0).astype(BF16)

    m_ref[...] = jnp.full(m_ref.shape, NEG, F32)
    acc_ref[...] = jnp.zeros(acc_ref.shape, F32)
    ones = jnp.ones((acc_ref.shape[1] - HEAD_W, tk), BF16)

    def scores(hh, kj):
        k0 = pl.multiple_of(kj * tk, tk)
        ka = jnp.concatenate([k_ref[pl.ds(k0, tk), hrows[hh]], pos_ref[...]], axis=1)
        return _dot(ka, qq_ref[hh])

    def consume(hh, kj, t):
        k0 = pl.multiple_of(kj * tk, tk)
        vta = jnp.concatenate([vt_ref[hrows[hh], pl.ds(k0, tk)], ones], axis=0)
        off = sl_ref[0, heads[hh]] * jnp.full((1, 2 * tq), (kj - qi) * tk, jnp.int32).astype(F32)
        m_old = m_ref[hh]
        m_new = jnp.maximum(m_old, jnp.max(t, axis=0, keepdims=True) + off)
        alpha = jnp.exp2(m_old - m_new)
        p = jnp.exp2(t - (m_new - off)).astype(BF16)
        acc_ref[hh] = alpha * acc_ref[hh] + _dot(vta, p)
        m_ref[hh] = m_new

    tbuf = (t0_ref, t1_ref)

    def step(kj, cur):
        for hh in range(n_hh):
            tbuf[1 - cur][hh] = scores(hh, kj + 1)
        for hh in range(n_hh):
            consume(hh, kj, tbuf[cur][hh])

    @pl.when(qi == 0)
    def _():
        prep_queries(0)
        for hh in range(n_hh):
            t0_ref[hh] = scores(hh, 0)

    def body(i, carry):
        step(2 * i, 0)
        step(2 * i + 1, 1)
        return carry

    lax.fori_loop(0, qi // 2, body, 0)
    odd = qi % 2 == 1

    @pl.when(odd)
    def _():
        step(qi - 1, 0)

    def finish(cur):
        krow = lax.broadcasted_iota(jnp.int32, (tk, 2 * tq), 0)
        qcol = lax.broadcasted_iota(jnp.int32, (tk, 2 * tq), 1)
        qcol = jnp.where(qcol >= tq, qcol - tq, qcol)
        lam = _diff_lambda(lamv_ref[...], lam_init)
        for hh in range(n_hh):
            consume(hh, qi, jnp.where(krow <= qcol, tbuf[cur][hh], NEG))
        prep_queries(jnp.minimum(qi + 1, pl.num_programs(2) - 1))
        for hh in range(n_hh):
            t0_ref[hh] = scores(hh, 0)
        for hh in range(n_hh):
            acc = acc_ref[hh]
            ot = acc[0:HEAD_W] / acc[HEAD_W:HEAD_W + 1]
            d = ot[:, 0:tq] - lam * ot[:, tq:2 * tq]
            ms = jnp.mean(d * d, axis=0, keepdims=True)
            dn = d * lax.rsqrt(ms + RMS_EPS) * g_ref[...] * (1.0 - lam_init)
            o_ref[:, hrows[hh]] = dn.T.astype(BF16)

    @pl.when(odd)
    def _():
        finish(1)

    @pl.when(jnp.logical_not(odd))
    def _():
        finish(0)


def _page_copy(hbm, buf, sem, kind, sl, p, page, page_rows):
    src = hbm.at[pl.ds(pl.multiple_of(page * page_rows, page_rows), page_rows)]
    return pltpu.make_async_copy(src, buf.at[sl, pl.ds(p * page_rows, page_rows)], sem.at[kind, sl])


def _attn_sample_prefetch(pt_ref, ck_hbm, cv_hbm, kbuf, vbuf, sem, seq, n_seq, *, n_pages, page_rows):
    def start_fetch(s, sl):
        for p in range(n_pages):
            page = pt_ref[s * n_pages + p]
            _page_copy(ck_hbm, kbuf, sem, 0, sl, p, page, page_rows).start()
            _page_copy(cv_hbm, vbuf, sem, 1, sl, p, page, page_rows).start()

    @pl.when(seq == 0)
    def _():
        start_fetch(0, 0)

    @pl.when(seq + 1 < n_seq)
    def _():
        start_fetch(seq + 1, 1 - seq % 2)


def _attn_sample_compute(lamv_ref, g_ref, slope2_ref, base2_ref, newmask_ref, q2_ref, kn_ref, vn_ref,
                         ck_hbm, cv_hbm, o_ref, kbuf, vbuf, sem, s_ref, seq,
                         *, n_pages, page_rows, n_heads, lam_init, chunk):
    slot = seq % 2
    past_rows = n_pages * page_rows
    nrow = q2_ref.shape[0] // 2

    for p in range(n_pages):
        _page_copy(ck_hbm, kbuf, sem, 0, slot, p, 0, page_rows).wait()
        _page_copy(cv_hbm, vbuf, sem, 1, slot, p, 0, page_rows).wait()

    pad = jnp.zeros((chunk - kn_ref.shape[0], HEAD_W), F32)
    kbuf[slot, past_rows:past_rows + chunk, :] = jnp.concatenate([kn_ref[...], pad], axis=0)
    vbuf[slot, past_rows:past_rows + chunk, :] = jnp.concatenate([vn_ref[...], pad], axis=0)

    q2 = q2_ref[...]
    slope2 = slope2_ref[...]
    n_pairs = past_rows // (2 * chunk)

    def pair_tile(buf, a):
        r0 = 2 * a * chunk
        return jnp.concatenate([buf[slot, r0:r0 + chunk, :].astype(BF16),
                                buf[slot, r0 + chunk:r0 + 2 * chunk, :].astype(BF16)], axis=1)

    def score_pair(a):
        tok0 = float(2 * a * (chunk // n_heads) - past_rows // n_heads)
        s_ref[:, a * chunk:(a + 1) * chunk] = _dot_nt(q2, pair_tile(kbuf, a)) + base2_ref[...] + slope2 * tok0

    def fold(acc):
        return acc[0:nrow, 0:HEAD_W] + acc[nrow:2 * nrow, HEAD_W:2 * HEAD_W]

    def softmax_half(lo, hi, extra=None):
        s = s_ref[:, lo * chunk:hi * chunk]
        mr = jnp.max(s, axis=-1, keepdims=True)
        m = jnp.maximum(mr[0:nrow], mr[nrow:2 * nrow])
        if extra is not None:
            m = jnp.maximum(m, jnp.max(extra, axis=-1, keepdims=True))
        p = jnp.exp2(s - jnp.concatenate([m, m], axis=0))
        lr = jnp.sum(p, axis=-1, keepdims=True)
        return m, lr[0:nrow] + lr[nrow:2 * nrow], p.astype(BF16)

    gp = n_pairs // DECODE_GROUPS
    q = q2[0:nrow, 0:HEAD_W]
    kn = kbuf[slot, past_rows:past_rows + chunk, :].astype(BF16)
    vn = vbuf[slot, past_rows:past_rows + chunk, :].astype(BF16)
    for a in range(gp):
        score_pair(a)
    parts = []
    for g in range(DECODE_GROUPS):
        last = g == DECODE_GROUPS - 1
        s_new = _dot_nt(q, kn) + newmask_ref[...] if last else None
        m_g, l_g, p_g = softmax_half(g * gp, (g + 1) * gp, extra=s_new)
        acc = jnp.zeros((2 * nrow, 2 * HEAD_W), F32)
        for a in range(gp):
            if not last:
                score_pair((g + 1) * gp + a)
            acc = acc + _dot(p_g[:, a * chunk:(a + 1) * chunk], pair_tile(vbuf, g * gp + a))
        o_g = fold(acc)
        if last:
            p_new = jnp.exp2(s_new - m_g)
            l_g = l_g + jnp.sum(p_new, axis=-1, keepdims=True)
            o_g = o_g + _dot(p_new.astype(BF16), vn)
        parts.append((m_g, l_g, o_g))
    m = functools.reduce(jnp.maximum, [pt[0] for pt in parts])
    ws = [jnp.exp2(pt[0] - m) for pt in parts]
    o = sum(pt[2] * w for pt, w in zip(parts, ws)) / sum(pt[1] * w for pt, w in zip(parts, ws))
    lam = _diff_lambda(lamv_ref[...], lam_init)
    half = nrow // 2
    d = o[0:half] - lam * o[half:nrow]
    o_ref[...] = _rms(d, g_ref[...]) * (1.0 - lam_init)


def _attn_kernel(pt_ref, sl_ref, lamv_ref, gcol_ref, pos_ref, qt_ref, k_ref, vt_ref,
                 grow_ref, slope2_ref, base2_ref, newmask_ref, q2_ref, kn_ref, vn_ref, ck_hbm, cv_hbm,
                 op_ref, os_ref,
                 qq_ref, m_ref, acc_ref, t0_ref, t1_ref, kbuf, vbuf, sem, s_ref,
                 *, tq, lam_init, n_pages, page_rows, n_heads, chunk):
    seq = (pl.program_id(0) * pl.num_programs(1) + pl.program_id(1)) * pl.num_programs(2) + pl.program_id(2)
    n_seq = pl.num_programs(0) * pl.num_programs(1) * pl.num_programs(2)
    _attn_sample_prefetch(pt_ref, ck_hbm, cv_hbm, kbuf, vbuf, sem, seq, n_seq,
                          n_pages=n_pages, page_rows=page_rows)
    _attn_prompt_body(sl_ref, lamv_ref, gcol_ref, pos_ref, qt_ref, k_ref, vt_ref, op_ref,
                      qq_ref, m_ref, acc_ref, t0_ref, t1_ref, tq=tq, lam_init=lam_init)
    _attn_sample_compute(lamv_ref, grow_ref, slope2_ref, base2_ref, newmask_ref, q2_ref, kn_ref, vn_ref,
                         ck_hbm, cv_hbm, os_ref, kbuf, vbuf, sem, s_ref, seq,
                         n_pages=n_pages, page_rows=page_rows, n_heads=n_heads, lam_init=lam_init, chunk=chunk)


def _attn(qt, kb, vt, slope_parts, lamv, g_col, page_table, q2, knf, vnf, ckf, cvf, g_row, slope2, base2,
          newmask, lam_init, tq):
    b, aw, t = qt.shape
    nh = aw // HEAD_W
    nq = t // tq
    db, nrow2, _ = q2.shape
    nrow = nrow2 // 2
    n_pages = page_table.shape[1]
    chunk = base2.shape[1]
    page_rows = PAGE * nh
    past_rows = n_pages * page_rows
    assert past_rows % (2 * chunk) == 0 and knf.shape[1] <= chunk
    assert db == b * nh * nq, "one decode sequence per prompt grid step"
    i = jnp.arange(tq)
    lane = jnp.arange(HEAD_W)
    feat = jnp.where(lane[None, :] < 2, (i % 256)[:, None], jnp.where(lane[None, :] < 4, (i // 256 * 256)[:, None], 0))
    pos = feat.astype(BF16)
    seq_of = lambda bi, hi, qi: (bi * nh + hi) * nq + qi
    per_seq = lambda a: pl.BlockSpec((None,) + a.shape[1:], lambda bi, hi, qi, pt: (seq_of(bi, hi, qi), 0, 0))
    const = lambda a: pl.BlockSpec(a.shape, lambda bi, hi, qi, pt: (0,) * a.ndim, pipeline_mode=pl.Buffered(1))
    grid_spec = pltpu.PrefetchScalarGridSpec(
        num_scalar_prefetch=1,
        grid=(b, nh, nq),
        in_specs=[
            pl.BlockSpec(memory_space=pltpu.SMEM),
            const(lamv), const(g_col), const(pos),
            pl.BlockSpec((None, HEAD_W, t), lambda bi, hi, qi, pt: (bi, hi, 0)),
            pl.BlockSpec((None, t, HEAD_W), lambda bi, hi, qi, pt: (bi, 0, hi)),
            pl.BlockSpec((None, HEAD_W, t), lambda bi, hi, qi, pt: (bi, hi, 0)),
            const(g_row), const(slope2), const(base2), const(newmask),
            per_seq(q2), per_seq(knf), per_seq(vnf),
            pl.BlockSpec(memory_space=pl.ANY),
            pl.BlockSpec(memory_space=pl.ANY),
        ],
        out_specs=[
            pl.BlockSpec((None, tq, HEAD_W), lambda bi, hi, qi, pt: (bi, qi, hi)),
            pl.BlockSpec((None, nrow // 2, HEAD_W), lambda bi, hi, qi, pt: (seq_of(bi, hi, qi), 0, 0)),
        ],
        scratch_shapes=[
            pltpu.VMEM((1, 2 * HEAD_W, 2 * tq), BF16),
            pltpu.VMEM((1, 1, 2 * tq), F32),
            pltpu.VMEM((1, HEAD_W + 16, 2 * tq), F32),
            pltpu.VMEM((1, tq, 2 * tq), F32),
            pltpu.VMEM((1, tq, 2 * tq), F32),
            pltpu.VMEM((2, past_rows + chunk, HEAD_W), F32),
            pltpu.VMEM((2, past_rows + chunk, HEAD_W), F32),
            pltpu.SemaphoreType.DMA((2, 2)),
            pltpu.VMEM((nrow2, past_rows // 2), F32),
        ],
    )
    return pl.pallas_call(
        functools.partial(_attn_kernel, tq=tq, lam_init=lam_init, n_pages=n_pages, page_rows=page_rows,
                          n_heads=nh, chunk=chunk),
        grid_spec=grid_spec,
        out_shape=[jax.ShapeDtypeStruct((b, t, aw), BF16),
                   jax.ShapeDtypeStruct((db, nrow // 2, HEAD_W), F32)],
        compiler_params=_cparams(3),
        name="attn",
    )(page_table.reshape(-1), slope_parts, lamv, g_col, pos, qt, kb, vt,
      g_row, slope2, base2, newmask, q2, knf, vnf, ckf, cvf)


def _s5_prep_kernel(lre_ref, lim_ref, ldt_ref, bre_ref, bim_ref, cre_ref, cim_ref, dsk_ref,
                    wst_ref, cat_ref, kt_ref, sct_ref):
    lr = lre_ref[...]
    li = lim_ref[...]
    dt = jnp.exp(ldt_ref[...])

    def lam_bar_pow(n):
        mag = jnp.exp(n * (lr * dt))
        ang = n * (li * dt)
        return mag * jnp.cos(ang), mag * jnp.sin(ang)

    a_r, a_i = lam_bar_pow(1.0)
    den = lr * lr + li * li
    nr, ni = a_r - 1.0, a_i
    f_r = (nr * lr + ni * li) / den
    f_i = (ni * lr - nr * li) / den
    b_r, b_i = bre_ref[...], bim_ref[...]
    bb_r = f_r * b_r - f_i * b_i
    bb_i = f_r * b_i + f_i * b_r
    c_r, c_i = cre_ref[...], cim_ref[...]
    gw = b_r.shape[0]
    ns = b_r.shape[1]

    ktau = []
    for tau in range(S5_BLOCK):
        p_r, p_i = lam_bar_pow(float(tau))
        w_r = bb_r * p_r - bb_i * p_i
        w_i = bb_r * p_i + bb_i * p_r
        j = S5_BLOCK - 1 - tau
        wst_ref[j * gw:(j + 1) * gw, 0:ns] = w_r.astype(BF16)
        wst_ref[j * gw:(j + 1) * gw, ns:2 * ns] = w_i.astype(BF16)
        ktau.append(_dot_nt(w_r, c_r, lax.Precision.HIGHEST) - _dot_nt(w_i, c_i, lax.Precision.HIGHEST))
    rr = lax.broadcasted_iota(jnp.int32, (gw, gw), 0)
    cc = lax.broadcasted_iota(jnp.int32, (gw, gw), 1)
    ktau[0] = ktau[0] + jnp.where(rr == cc, dsk_ref[...], 0.0)

    for j in range(S5_BLOCK):
        p_r, p_i = lam_bar_pow(float(j + 1))
        cat_ref[j * gw:(j + 1) * gw, 0:ns] = (c_r * p_r - c_i * p_i).astype(BF16)
        cat_ref[j * gw:(j + 1) * gw, ns:2 * ns] = (-(c_r * p_i + c_i * p_r)).astype(BF16)

    zeros = jnp.zeros((gw, gw), BF16)
    for i in range(S5_BLOCK):
        for j in range(S5_BLOCK):
            blk = ktau[j - i].astype(BF16) if j >= i else zeros
            kt_ref[i * gw:(i + 1) * gw, j * gw:(j + 1) * gw] = blk

    row = lax.broadcasted_iota(jnp.int32, (8, ns), 0)
    for idx, k in enumerate((1, 2, 4)):
        p_r, p_i = lam_bar_pow(float(S5_BLOCK * k))
        sct_ref[2 * idx] = jnp.where(row >= k, p_r, 0.0)
        sct_ref[2 * idx + 1] = jnp.where(row >= k, p_i, 0.0)
    n = (S5_BLOCK * (row + 1)).astype(F32)
    mag = jnp.exp(n * (lr * dt))
    ang = n * (li * dt)
    sct_ref[6] = mag * jnp.cos(ang)
    sct_ref[7] = mag * jnp.sin(ang)


def _s5_prep(lre, lim, ldt, bre, bim, cre, cim, dsk):
    nq, gw, ns = bre.shape
    kw = S5_BLOCK * gw
    per_q = lambda *tail: pl.BlockSpec((None,) + tail, lambda q: (q,) + (0,) * len(tail))
    return pl.pallas_call(
        _s5_prep_kernel,
        grid=(nq,),
        in_specs=[per_q(1, ns), per_q(1, ns), per_q(1, ns), per_q(gw, ns), per_q(gw, ns),
                  per_q(gw, ns), per_q(gw, ns), per_q(1, gw)],
        out_specs=[per_q(kw, 2 * ns), per_q(kw, 2 * ns), per_q(kw, kw), per_q(8, 8, ns)],
        out_shape=[
            jax.ShapeDtypeStruct((nq, kw, 2 * ns), BF16),
            jax.ShapeDtypeStruct((nq, kw, 2 * ns), BF16),
            jax.ShapeDtypeStruct((nq, kw, kw), BF16),
            jax.ShapeDtypeStruct((nq, 8, 8, ns), F32),
        ],
        compiler_params=_cparams(1),
        name="s5_prep",
    )(lre, lim, ldt, bre, bim, cre, cim, dsk)


def _quarter_lanes(u_ref, q, gw, sw):
    return jnp.concatenate(
        [u_ref[:, j * sw + q * gw: j * sw + (q + 1) * gw] for j in range(S5_BLOCK)], axis=1)


def _store_quarter(y_ref, y, q, gw, sw):
    for j in range(S5_BLOCK):
        y_ref[:, j * sw + q * gw: j * sw + (q + 1) * gw] = y[:, j * gw:(j + 1) * gw].astype(y_ref.dtype)


def _s5_prompt_kernel(u_ref, wst_ref, cat_ref, kt_ref, sct_ref, y_ref, st_ref,
                      carry_ref, s_scr, hp_scr, *, rb, gw, sw, ns):
    blk = pl.program_id(1)

    @pl.when(blk == 0)
    def _():
        carry_ref[...] = jnp.zeros(carry_ref.shape, F32)

    row0 = lax.broadcasted_iota(jnp.int32, (8, ns), 0) == 0
    for q in range(S5_QUARTERS):
        uq = _quarter_lanes(u_ref, q, gw, sw)
        s_scr[...] = _dot(uq, wst_ref[q])
        tabs = [sct_ref[q, i] for i in range(8)]

        def tile(t, carry, tabs=tabs):
            cr, ci = carry
            r0 = pl.multiple_of(t * 8, 8)
            xr = s_scr[pl.ds(r0, 8), 0:ns]
            xi = s_scr[pl.ds(r0, 8), ns:2 * ns]
            for idx, k in enumerate((1, 2, 4)):
                mr, mi = tabs[2 * idx], tabs[2 * idx + 1]
                sr = pltpu.roll(xr, k, 0)
                si = pltpu.roll(xi, k, 0)
                xr, xi = xr + (mr * sr - mi * si), xi + (mr * si + mi * sr)
            crb = jnp.broadcast_to(cr, (8, ns))
            cib = jnp.broadcast_to(ci, (8, ns))
            hr = xr + (tabs[6] * crb - tabs[7] * cib)
            hi = xi + (tabs[6] * cib + tabs[7] * crb)
            hp_scr[pl.ds(r0, 8), 0:ns] = jnp.where(row0, crb, pltpu.roll(hr, 1, 0))
            hp_scr[pl.ds(r0, 8), ns:2 * ns] = jnp.where(row0, cib, pltpu.roll(hi, 1, 0))
            return hr[7:8, :], hi[7:8, :]

        cr, ci = lax.fori_loop(0, rb // 8, tile,
                               (carry_ref[q, :, 0:ns], carry_ref[q, :, ns:2 * ns]))
        carry_ref[q, :, 0:ns] = cr
        carry_ref[q, :, ns:2 * ns] = ci
        st_ref[q, :, 0:ns] = cr
        st_ref[q, :, ns:2 * ns] = ci
        y = _dot_nt(hp_scr[...].astype(BF16), cat_ref[q]) + _dot(uq, kt_ref[q])
        _store_quarter(y_ref, y, q, gw, sw)


def _s5_prompt(u4, wst, cat, kt, sct, rb):
    b, rows, lanes = u4.shape
    nq, kw, ns2 = wst.shape
    ns, gw, sw = ns2 // 2, kw // S5_BLOCK, lanes // S5_BLOCK
    return pl.pallas_call(
        functools.partial(_s5_prompt_kernel, rb=rb, gw=gw, sw=sw, ns=ns),
        grid=(b, rows // rb),
        in_specs=[
            pl.BlockSpec((None, rb, lanes), lambda bi, ri: (bi, ri, 0)),
            _const_spec(wst.shape), _const_spec(cat.shape), _const_spec(kt.shape), _const_spec(sct.shape),
        ],
        out_specs=[
            pl.BlockSpec((None, rb, lanes), lambda bi, ri: (bi, ri, 0)),
            pl.BlockSpec((None, nq, 1, ns2), lambda bi, ri: (bi, 0, 0, 0)),
        ],
        out_shape=[
            jax.ShapeDtypeStruct((b, rows, lanes), BF16),
            jax.ShapeDtypeStruct((b, nq, 1, ns2), F32),
        ],
        scratch_shapes=[
            pltpu.VMEM((nq, 1, ns2), F32),
            pltpu.VMEM((rb, ns2), F32),
            pltpu.VMEM((rb, ns2), F32),
        ],
        compiler_params=_cparams(2),
        name="s5_prompt",
    )(u4, wst, cat, kt, sct)


def _s5_sample_kernel(u_ref, h0_ref, wst_ref, cat_ref, kt_ref, sct_ref, y_ref, hout_ref, *, gw, sw, ns):
    for q in range(S5_QUARTERS):
        uq = _quarter_lanes(u_ref, q, gw, sw)
        s = _dot(uq, wst_ref[q])
        h0 = h0_ref[q]
        h0r, h0i = h0[:, 0:ns], h0[:, ns:2 * ns]
        a4r = sct_ref[q, 6][0:1, :]
        a4i = sct_ref[q, 7][0:1, :]
        hout_ref[q, :, 0:ns] = s[:, 0:ns] + (a4r * h0r - a4i * h0i)
        hout_ref[q, :, ns:2 * ns] = s[:, ns:2 * ns] + (a4r * h0i + a4i * h0r)
        y = _dot_nt(h0.astype(BF16), cat_ref[q]) + _dot(uq, kt_ref[q])
        _store_quarter(y_ref, y, q, gw, sw)


def _s5_sample(u4, h0q, wst, cat, kt, sct):
    rows, lanes = u4.shape
    nq, kw, ns2 = wst.shape
    ns, gw, sw = ns2 // 2, kw // S5_BLOCK, lanes // S5_BLOCK
    return pl.pallas_call(
        functools.partial(_s5_sample_kernel, gw=gw, sw=sw, ns=ns),
        grid=(1,),
        in_specs=[_const_spec(u4.shape), _const_spec(h0q.shape), _const_spec(wst.shape),
                  _const_spec(cat.shape), _const_spec(kt.shape), _const_spec(sct.shape)],
        out_specs=[pl.BlockSpec(u4.shape, lambda i: (0, 0)),
                   pl.BlockSpec(h0q.shape, lambda i: (0, 0, 0))],
        out_shape=[jax.ShapeDtypeStruct(u4.shape, BF16), jax.ShapeDtypeStruct(h0q.shape, F32)],
        compiler_params=_cparams(1),
        name="s5_sample",
    )(u4, h0q, wst, cat, kt, sct)


def _mem_kv_kernel(x_ref, g_ref, wk_ref, wv_ref, k_ref, v_ref, kb_ref, vb_ref):
    hn = _rms(x_ref[...], g_ref[...]).astype(BF16)
    k = _dot(hn, wk_ref[...])
    k_ref[...] = k
    kb_ref[...] = k.astype(BF16)
    v = _dot(hn, wv_ref[...])
    v_ref[...] = v
    vb_ref[...] = v.astype(BF16)


def _mem_kv(mem, g, wk_bf, wv_bf, tm):
    n, d = mem.shape
    row = pl.BlockSpec((tm, d), lambda i: (i, 0))
    return pl.pallas_call(
        _mem_kv_kernel,
        grid=(n // tm,),
        in_specs=[row, _const_spec((1, d)), _const_spec(wk_bf.shape), _const_spec(wv_bf.shape)],
        out_specs=[row, row, row, row],
        out_shape=[jax.ShapeDtypeStruct((n, d), F32), jax.ShapeDtypeStruct((n, d), F32),
                   jax.ShapeDtypeStruct((n, d), BF16), jax.ShapeDtypeStruct((n, d), BF16)],
        compiler_params=_cparams(1),
        name="mem_kv",
    )(mem, g, wk_bf, wv_bf)


def _merge_and_query(x, att, sy4_ref, sy_scr, wglu_ref, gso_ref, wout_ref, gpost_ref, gmpre_ref, wmq_ref,
                     mem_scale):
    aw = att.shape[1]
    rows, n_lb = sy4_ref.shape[0], sy_scr.shape[0]
    sw = n_lb * LANES
    for j in range(S5_BLOCK):
        for c in range(n_lb):
            blk = sy4_ref[:, j * sw + c * LANES:j * sw + (c + 1) * LANES].astype(F32)
            sy_scr[c, pl.ds(j, rows, stride=S5_BLOCK), :] = blk
    y = jax.nn.gelu(jnp.concatenate([sy_scr[c] for c in range(n_lb)], axis=1))
    y = y * jax.nn.sigmoid(_dot(y.astype(BF16), wglu_ref[...]))
    y = _rms(y, gso_ref[...])
    mix = _dot(att, wout_ref[0:aw, :]) + _dot(y.astype(BF16), wout_ref[aw:, :])
    x1 = x + _rms(mix, gpost_ref[...])
    hq = _rms(x1, gmpre_ref[...]).astype(BF16)
    qm = (_dot(hq, wmq_ref[...]) * mem_scale).astype(BF16)
    return x1, qm


def _mix_mem_body(x_ref, att_ref, sy_ref, mk_ref, mv_ref, wglu_ref, gso_ref, wout_ref, gpost_ref,
                  gmpre_ref, wmq_ref, wmo_ref, gmpost_ref, o_ref, oh_ref, sy_scr, *, n_heads, mem_scale):
    x1, qm = _merge_and_query(x_ref[...], att_ref[...], sy_ref, sy_scr, wglu_ref, gso_ref, wout_ref,
                              gpost_ref, gmpre_ref, wmq_ref, mem_scale)
    hd = qm.shape[1] // n_heads
    sls = [slice(h * hd, (h + 1) * hd) for h in range(n_heads)]
    ss = [_dot_nt(qm[:, sl], mk_ref[:, sl]) for sl in sls]
    ps = [jnp.exp(s - jnp.max(s, axis=-1, keepdims=True)) for s in ss]
    for sl, p in zip(sls, ps):
        l = jnp.sum(p, axis=-1, keepdims=True)
        oh_ref[:, sl] = (_dot(p.astype(BF16), mv_ref[:, sl]) / l).astype(BF16)
    mo = _dot(oh_ref[...], wmo_ref[...])
    o_ref[...] = x1 + _rms(mo, gmpost_ref[...])


def _mix_q_kernel(x_ref, att_ref, sy_ref, wglu_ref, gso_ref, wout_ref, gpost_ref, gmpre_ref, wmq_ref,
                  x1_ref, qm_ref, sy_scr, *, mem_scale):
    x1, qm = _merge_and_query(x_ref[...], att_ref[...], sy_ref, sy_scr, wglu_ref, gso_ref, wout_ref,
                              gpost_ref, gmpre_ref, wmq_ref, mem_scale)
    x1_ref[...] = x1
    qm_ref[...] = qm


def _mix_q(x, att, sy4, w, n_heads):
    n, d = x.shape
    weights = [w["w_glu"], w["g_ssm_out"], w["w_out"], w["g_mix_post"], w["g_mem_pre"], w["w_mq"]]
    ins = [x, att, sy4] + weights
    return pl.pallas_call(
        functools.partial(_mix_q_kernel, mem_scale=(d // n_heads) ** -0.5),
        grid=(1,),
        in_specs=[_const_spec(a.shape) for a in ins],
        out_specs=[pl.BlockSpec((n, d), lambda i: (0, 0)), pl.BlockSpec((n, d), lambda i: (0, 0))],
        out_shape=[jax.ShapeDtypeStruct((n, d), F32), jax.ShapeDtypeStruct((n, d), BF16)],
        scratch_shapes=[pltpu.VMEM((sy4.shape[1] // S5_BLOCK // LANES, n, LANES), F32)],
        compiler_params=_cparams(1),
        name="mix_q",
    )(*ins)


def _mem_sample_copies(ck_hbm, cv_hbm, kbuf, vbuf, sem, step, n_per, n_mem, n_heads):
    sl = step % 2
    out = []
    for j in range(n_per):
        for kind, (hbm, buf) in enumerate(((ck_hbm, kbuf), (cv_hbm, vbuf))):
            for lb in range(kbuf.shape[2]):
                src = hbm.at[step * n_per + j, :, :, pl.ds(lb * LANES, LANES)]
                dst = buf.at[sl, j, lb].reshape(n_mem, n_heads, LANES)
                out.append(pltpu.make_async_copy(src, dst, sem.at[kind, sl]))
    return out


def _mem_sample_compute(mask_ref, q_ref, o_ref, kbuf, vbuf, slot, js):
    n_halves = kbuf.shape[2]
    cat = lambda buf, j: jnp.concatenate([buf[slot, j, lb] for lb in range(n_halves)], axis=1).astype(BF16)
    ss = [_dot_nt(q_ref[j], cat(kbuf, j)) + mask_ref[...] for j in js]
    ps = [jnp.exp(s - jnp.max(s, axis=-1, keepdims=True)) for s in ss]
    ls = [jnp.sum(p, axis=-1, keepdims=True) for p in ps]
    for j, p, l in zip(js, ps, ls):
        o_ref[j] = _dot(p.astype(BF16), cat(vbuf, j)) / l


def _mix_mem_kernel(x_ref, att_ref, sy_ref, mk_ref, mv_ref, wglu_ref, gso_ref, wout_ref, gpost_ref,
                    gmpre_ref, wmq_ref, wmo_ref, gmpost_ref, mask_ref, qs_ref, ck_hbm, cv_hbm,
                    o_ref, os_ref, oh_ref, sy_scr, kbuf, vbuf, sem, *, n_heads, mem_scale, n_mem):
    step = pl.program_id(0) * pl.num_programs(1) + pl.program_id(1)
    n_steps = pl.num_programs(0) * pl.num_programs(1)
    n_per = qs_ref.shape[0]
    copies = lambda st: _mem_sample_copies(ck_hbm, cv_hbm, kbuf, vbuf, sem, st, n_per, n_mem, n_heads)

    @pl.when(step == 0)
    def _():
        for cp in copies(0):
            cp.start()

    @pl.when(step + 1 < n_steps)
    def _():
        for cp in copies(step + 1):
            cp.start()

    _mix_mem_body(x_ref, att_ref, sy_ref, mk_ref, mv_ref, wglu_ref, gso_ref, wout_ref, gpost_ref,
                  gmpre_ref, wmq_ref, wmo_ref, gmpost_ref, o_ref, oh_ref, sy_scr,
                  n_heads=n_heads, mem_scale=mem_scale)

    for cp in copies(step):
        cp.wait()
    _mem_sample_compute(mask_ref, qs_ref, os_ref, kbuf, vbuf, step % 2, range(n_per))


def _mix_mem(x, att, sy4, mkb, mvb, w, qh, ck, cv, mask, tm, n_heads):
    b, t, d = x.shape
    aw, sw = att.shape[2], sy4.shape[2] // S5_BLOCK
    nm = mkb.shape[1]
    nt = t // tm
    db, nrow, hd = qh.shape
    n_ckeys = ck.shape[1]
    n_per = db // (b * nt)
    assert db == n_per * b * nt, "decode sequences must divide evenly over the prompt grid steps"
    tok = lambda width: pl.BlockSpec((None, tm, width), lambda bi, ti: (bi, ti, 0))
    tok4 = pl.BlockSpec((None, tm // S5_BLOCK, S5_BLOCK * sw), lambda bi, ti: (bi, ti, 0))
    per_b = pl.BlockSpec((None, nm, d), lambda bi, ti: (bi, 0, 0))
    per_step = pl.BlockSpec((n_per, nrow, hd), lambda bi, ti: (bi * nt + ti, 0, 0))
    any_spec = pl.BlockSpec(memory_space=pl.ANY)
    weights = [w["w_glu"], w["g_ssm_out"], w["w_out"], w["g_mix_post"], w["g_mem_pre"], w["w_mq"],
               w["w_mo"], w["g_mem_post"], mask]
    return pl.pallas_call(
        functools.partial(_mix_mem_kernel, n_heads=n_heads, mem_scale=(d // n_heads) ** -0.5, n_mem=n_ckeys),
        grid=(b, nt),
        in_specs=[tok(d), tok(aw), tok4, per_b, per_b] + [_const_spec(a.shape) for a in weights]
                 + [per_step, any_spec, any_spec],
        out_specs=[tok(d), per_step],
        out_shape=[jax.ShapeDtypeStruct((b, t, d), F32), jax.ShapeDtypeStruct((db, nrow, hd), F32)],
        scratch_shapes=[
            pltpu.VMEM((tm, d), BF16),
            pltpu.VMEM((sw // LANES, tm, LANES), F32),
            pltpu.VMEM((2, n_per, hd // LANES, n_ckeys * n_heads, LANES), F32),
            pltpu.VMEM((2, n_per, hd // LANES, n_ckeys * n_heads, LANES), F32),
            pltpu.SemaphoreType.DMA((2, 2)),
        ],
        compiler_params=_cparams(2),
        name="mix_mem",
    )(x, att, sy4, mkb, mvb, *weights, qh, ck, cv)


def _mem_out_kernel(x1_ref, o_ref, wmo_ref, g_ref, y_ref):
    mo = _dot(o_ref[...], wmo_ref[...])
    y_ref[...] = x1_ref[...] + _rms(mo, g_ref[...])


def _mem_out(x1, o, wmo, g):
    n, d = x1.shape
    ins = [x1, o, wmo, g]
    return pl.pallas_call(
        _mem_out_kernel,
        grid=(1,),
        in_specs=[_const_spec(a.shape) for a in ins],
        out_specs=pl.BlockSpec((n, d), lambda i: (0, 0)),
        out_shape=jax.ShapeDtypeStruct((n, d), F32),
        compiler_params=_cparams(1),
        name="mem_out",
    )(*ins)


def _ffn_kernel(x_ref, gpre_ref, wg_ref, wu_ref, wd_ref, gpost_ref, o_ref, *, n_chunks):
    x = x_ref[...]
    hn = _rms(x, gpre_ref[...]).astype(BF16)
    ff = wg_ref.shape[1]
    cw = ff // n_chunks
    acc = jnp.zeros(x.shape, F32)
    for c in range(n_chunks):
        sl = slice(c * cw, (c + 1) * cw)
        a = jax.nn.silu(_dot(hn, wg_ref[:, sl])) * _dot(hn, wu_ref[:, sl])
        acc = acc + _dot(a.astype(BF16), wd_ref[sl, :])
    o_ref[...] = x + _rms(acc, gpost_ref[...])


def _ffn(x, gpre, wg, wu, wd, gpost, tm, n_chunks):
    n, d = x.shape
    row = pl.BlockSpec((tm, d), lambda i: (i, 0))
    weights = [gpre, wg, wu, wd, gpost]
    return pl.pallas_call(
        functools.partial(_ffn_kernel, n_chunks=n_chunks),
        grid=(n // tm,),
        in_specs=[row] + [_const_spec(a.shape) for a in weights],
        out_specs=row,
        out_shape=jax.ShapeDtypeStruct((n, d), F32),
        compiler_params=_cparams(1),
        name="ffn",
    )(x, *weights)


def _s5_param_layout(lam_re, lam_im, log_dt, b_re, b_im, c_re, c_im, d_skip):
    g, p, hc = b_re.shape
    nq = S5_QUARTERS
    gq = g // nq
    rowv = lambda a: a.reshape(nq, 1, gq * p)
    ldt = jnp.broadcast_to(log_dt[:, None], (g, p))
    eye = jnp.eye(gq, dtype=F32)

    def embed_b(b):
        bq = b.reshape(nq, gq, p, hc)
        return jnp.einsum("qgph,gk->qghkp", bq, eye).reshape(nq, gq * hc, gq * p)

    def embed_c(c):
        cq = c.reshape(nq, gq, hc, p)
        return jnp.einsum("qghp,gk->qghkp", cq, eye).reshape(nq, gq * hc, gq * p)

    return (rowv(lam_re), rowv(lam_im), rowv(ldt), embed_b(b_re), embed_b(b_im),
            embed_c(c_re), embed_c(c_im), d_skip.reshape(nq, 1, gq * hc))


def _state_to_quarters(re, im):
    n, g, p = re.shape
    gq = g // S5_QUARTERS
    f = lambda a: a.reshape(n, S5_QUARTERS, gq * p).transpose(1, 0, 2)
    return jnp.concatenate([f(re), f(im)], axis=-1)


def _state_from_quarters(st, g, p):
    n = st.shape[0]
    ns = st.shape[-1] // 2
    return st[..., :ns].reshape(n, g, p), st[..., ns:].reshape(n, g, p)


def kernel(x_prompt, x_sample, mem_prompt, cache_k, cache_v, page_table, state_ssm_re, state_ssm_im,
           cache_mem_k, cache_mem_v, g_mix_pre, g_mix_post, w_in, lam_q1, lam_k1, lam_q2, lam_k2, g_subln,
           ssm_lam_re, ssm_lam_im, ssm_log_dt, ssm_b_re, ssm_b_im, ssm_c_re, ssm_c_im, ssm_d, w_glu, g_ssm_out,
           w_out, g_mem_pre, g_mem_post, g_mem_kv, w_mq, w_mk, w_mv, w_mo, g_ffn_pre, g_ffn_post,
           w_gate, w_up, w_down):
    depth = w_in.shape[0]
    b, t, d = x_prompt.shape
    db, dt_, _ = x_sample.shape
    n_mem = mem_prompt.shape[1]
    n_mem_heads = cache_mem_k.shape[3]
    n_heads = cache_k.shape[3]
    aw = n_heads * HEAD_W
    g, p = ssm_lam_re.shape[1], ssm_lam_re.shape[2]
    sw = g * SSM_GROUP
    n_phys = cache_k.shape[1]
    assert dt_ == S5_BLOCK and t % (S5_BLOCK * 256) == 0

    slopes = LOG2E * 2.0 ** (-8.0 * jnp.arange(1, n_heads + 1, dtype=F32) / n_heads)
    slope_hi = slopes.astype(BF16).astype(F32)
    slope_parts = jnp.stack([slopes, slope_hi, (slopes - slope_hi).astype(BF16).astype(F32)])
    row = lambda a: a.reshape(1, -1).astype(F32)
    bf = lambda a: a.astype(BF16)

    branch_lane = jnp.arange(HEAD_W) // HEAD_DIM
    nrow = 2 * n_heads * dt_
    chunk = 256
    r = jnp.arange(nrow)
    r_head, r_tok = (r // dt_) % n_heads, r % dt_
    r_slope = slopes[r_head][:, None]
    c = jnp.arange(chunk)
    c_tok, c_head = c // n_heads, c % n_heads
    own_head = c_head[None, :] == r_head[:, None]
    base = r_slope * c_tok[None, :].astype(F32) + jnp.where(own_head, 0.0, NEG)
    base2 = jnp.concatenate([base, base + r_slope * float(chunk // n_heads)], axis=0)
    slope2 = jnp.concatenate([r_slope, r_slope], axis=0)
    new_ok = own_head & (c[None, :] < dt_ * n_heads) & (c_tok[None, :] <= r_tok[:, None])
    newmask = jnp.where(new_ok, r_slope * c_tok[None, :].astype(F32), NEG)
    mem_row_head = jnp.arange(n_mem_heads * dt_) // dt_
    mem_col_head = jnp.arange(n_mem * n_mem_heads) % n_mem_heads
    mem_mask = jnp.where(mem_col_head[None, :] == mem_row_head[:, None], 0.0, NEG).astype(F32)

    y_p = x_prompt
    y_s = x_sample.reshape(db * dt_, d)
    outs = {k: [] for k in ("kp", "vp", "ks", "vs", "rp", "ip", "rs", "is", "mkp", "mvp")}
    for l in range(depth):
        lam_init = 0.8 - 0.6 * math.exp(-0.3 * l)
        lamv = jnp.stack([lam_q1[l], lam_k1[l], lam_q2[l], lam_k2[l]]).astype(F32)
        gsub = row(g_subln[l])
        w_in_bf = bf(w_in[l])
        w = {
            "w_glu": bf(w_glu[l]), "g_ssm_out": row(g_ssm_out[l]), "w_out": bf(w_out[l]),
            "g_mix_post": row(g_mix_post[l]), "g_mem_pre": row(g_mem_pre[l]), "w_mq": bf(w_mq[l]),
            "w_mo": bf(w_mo[l]), "g_mem_post": row(g_mem_post[l]),
        }

        qp, kp, vp, kpb, vpb, up = _proj(y_p.reshape(b * t, d), row(g_mix_pre[l]), w_in_bf, aw, 1024)
        qs, ks, vs, _, _, us = _proj(y_s, row(g_mix_pre[l]), w_in_bf, aw, 512)

        qh = qs.reshape(db, dt_, n_heads, HEAD_W).transpose(0, 2, 1, 3)
        zq = jnp.zeros((), BF16)
        q32 = jnp.stack([jnp.where(branch_lane == 0, qh, zq), jnp.where(branch_lane == 1, qh, zq)], axis=1)
        q32 = q32.reshape(db, nrow, HEAD_W)
        zq32 = jnp.zeros_like(q32)
        q2 = jnp.concatenate([jnp.concatenate([q32, zq32], axis=2), jnp.concatenate([zq32, q32], axis=2)], axis=1)
        att_p, att_s16 = _attn(jnp.swapaxes(qp.reshape(b, t, aw), 1, 2), kpb.reshape(b, t, aw),
                               jnp.swapaxes(vpb.reshape(b, t, aw), 1, 2),
                               slope_parts, lamv, g_subln[l].reshape(HEAD_W, 1).astype(F32),
                               page_table, q2, ks.reshape(db, dt_ * n_heads, HEAD_W),
                               vs.reshape(db, dt_ * n_heads, HEAD_W),
                               cache_k[l].reshape(n_phys * PAGE * n_heads, HEAD_W),
                               cache_v[l].reshape(n_phys * PAGE * n_heads, HEAD_W),
                               gsub, slope2, base2, newmask, lam_init, 512)
        att_s = bf(att_s16.reshape(db, n_heads, dt_, HEAD_W).transpose(0, 2, 1, 3).reshape(db * dt_, aw))

        tabs = _s5_prep(*_s5_param_layout(ssm_lam_re[l], ssm_lam_im[l], ssm_log_dt[l], ssm_b_re[l],
                                          ssm_b_im[l], ssm_c_re[l], ssm_c_im[l], ssm_d[l]))
        syp, stp = _s5_prompt(up.reshape(b, t // S5_BLOCK, S5_BLOCK * sw), *tabs, 512)
        rp, ip = _state_from_quarters(stp.reshape(b, S5_QUARTERS, -1), g, p)
        sys_, sts = _s5_sample(us, _state_to_quarters(state_ssm_re[l], state_ssm_im[l]), *tabs)
        rs, is_ = _state_from_quarters(sts.transpose(1, 0, 2), g, p)

        mkp, mvp, mkb, mvb = _mem_kv(mem_prompt.reshape(b * n_mem, d), row(g_mem_kv[l]),
                                     bf(w_mk[l]), bf(w_mv[l]), 512)
        x1s, qms = _mix_q(y_s, att_s, sys_, w, n_mem_heads)
        mhd = d // n_mem_heads
        qmh = qms.reshape(db, dt_, n_mem_heads, mhd).transpose(0, 2, 1, 3).reshape(db, n_mem_heads * dt_, mhd)
        y_p, omh = _mix_mem(y_p, att_p, syp, mkb.reshape(b, n_mem, d), mvb.reshape(b, n_mem, d), w,
                            qmh, cache_mem_k[l], cache_mem_v[l], mem_mask, 512, n_mem_heads)
        om = bf(omh.reshape(db, n_mem_heads, dt_, mhd).transpose(0, 2, 1, 3).reshape(db * dt_, d))
        y_s = _mem_out(x1s, om, w["w_mo"], w["g_mem_post"])

        ffw = (row(g_ffn_pre[l]), bf(w_gate[l]), bf(w_up[l]), bf(w_down[l]), row(g_ffn_post[l]))
        y_p = _ffn(y_p.reshape(b * t, d), *ffw, 1024, 11).reshape(b, t, d)
        y_s = _ffn(y_s, *ffw, 512, 2)

        outs["kp"].append(kp.reshape(b, t, n_heads, HEAD_W))
        outs["vp"].append(vp.reshape(b, t, n_heads, HEAD_W))
        outs["ks"].append(ks.reshape(db, dt_, n_heads, HEAD_W))
        outs["vs"].append(vs.reshape(db, dt_, n_heads, HEAD_W))
        outs["rp"].append(rp); outs["ip"].append(ip); outs["rs"].append(rs); outs["is"].append(is_)
        outs["mkp"].append(mkp.reshape(b, n_mem, n_mem_heads, d // n_mem_heads))
        outs["mvp"].append(mvp.reshape(b, n_mem, n_mem_heads, d // n_mem_heads))

    st = lambda k: jnp.stack(outs[k], 0)
    return (y_p, y_s.reshape(db, dt_, d), st("kp"), st("vp"), st("ks"), st("vs"),
            st("rp"), st("ip"), st("rs"), st("is"), st("mkp"), st("mvp"))
```

```python
import functools
import math

import jax
import jax.numpy as jnp
from jax import lax
from jax.experimental import pallas as pl
from jax.experimental.pallas import tpu as pltpu

F32 = jnp.float32
BF16 = jnp.bfloat16

RMS_EPS = 1e-6
HEAD_DIM = 64
HEAD_W = 2 * HEAD_DIM
SSM_GROUP = 16
SSM_STATE = 64
S5_BLOCK = 4
S5_QUARTERS = 4
PAGE = 128
LANES = 128
BF16_ROWS = 16
DECODE_GROUPS = 4
NEG = -1e30
LOG2E = 1.4426950408889634
VMEM_LIMIT_BYTES = 56 * 1024 * 1024


def _cparams(n_grid_dims):
    return pltpu.CompilerParams(
        dimension_semantics=("arbitrary",) * n_grid_dims,
        vmem_limit_bytes=VMEM_LIMIT_BYTES)


def _const_spec(shape):
    nd = len(shape)
    return pl.BlockSpec(shape, lambda *_: (0,) * nd, pipeline_mode=pl.Buffered(1))


def _rms(x, g):
    ms = jnp.mean(x * x, axis=-1, keepdims=True)
    return x * lax.rsqrt(ms + RMS_EPS) * g


def _dot(a, b):
    return jnp.dot(a, b, preferred_element_type=F32)


def _dot_nt(a, b, precision=None):
    return lax.dot_general(a, b, (((1,), (1,)), ((), ())),
                           preferred_element_type=F32, precision=precision)


def _proj_kernel(x_ref, g_ref, w_ref, q_ref, k_ref, v_ref, kb_ref, vb_ref, u_ref, u_scr, *, aw, tm):
    hn = _rms(x_ref[...], g_ref[...]).astype(BF16)
    q = _dot(hn, w_ref[:, 0:aw])
    q_ref[...] = (q * (HEAD_DIM ** -0.5 * LOG2E)).astype(BF16)
    nh = aw // HEAD_W
    k = _dot(hn, w_ref[:, aw:2 * aw])
    kb_ref[...] = k.astype(BF16)
    v = _dot(hn, w_ref[:, 2 * aw:3 * aw])
    vb_ref[...] = v.astype(BF16)
    for h in range(nh):
        k_ref[pl.ds(h, tm, stride=nh), :] = k[:, h * HEAD_W:(h + 1) * HEAD_W]
        v_ref[pl.ds(h, tm, stride=nh), :] = v[:, h * HEAD_W:(h + 1) * HEAD_W]
    u = _dot(hn, w_ref[:, 3 * aw:])
    n_lb = u_scr.shape[0]
    sw = n_lb * LANES
    for c in range(n_lb):
        u_scr[c] = u[:, c * LANES:(c + 1) * LANES]
    for j in range(S5_BLOCK):
        for c in range(n_lb):
            rows = u_scr[c, pl.ds(j, tm // S5_BLOCK, stride=S5_BLOCK), :]
            u_ref[:, j * sw + c * LANES:j * sw + (c + 1) * LANES] = rows.astype(BF16)


def _proj(x, g, w_bf, aw, tm):
    n, d = x.shape
    sw = w_bf.shape[1] - 3 * aw
    nh = aw // HEAD_W
    row = lambda width: pl.BlockSpec((tm, width), lambda i: (i, 0))
    flat = pl.BlockSpec((tm * nh, HEAD_W), lambda i: (i, 0))
    return pl.pallas_call(
        functools.partial(_proj_kernel, aw=aw, tm=tm),
        grid=(n // tm,),
        in_specs=[row(d), _const_spec((1, d)), _const_spec(w_bf.shape)],
        out_specs=[row(aw), flat, flat, row(aw), row(aw),
                   pl.BlockSpec((tm // S5_BLOCK, S5_BLOCK * sw), lambda i: (i, 0))],
        out_shape=[
            jax.ShapeDtypeStruct((n, aw), BF16),
            jax.ShapeDtypeStruct((n * nh, HEAD_W), F32),
            jax.ShapeDtypeStruct((n * nh, HEAD_W), F32),
            jax.ShapeDtypeStruct((n, aw), BF16),
            jax.ShapeDtypeStruct((n, aw), BF16),
            jax.ShapeDtypeStruct((n // S5_BLOCK, S5_BLOCK * sw), BF16),
        ],
        scratch_shapes=[pltpu.VMEM((sw // LANES, tm, LANES), F32)],
        compiler_params=_cparams(1),
        name="proj",
    )(x, g, w_bf)


def _diff_lambda(lamv, lam_init):
    d1 = jnp.sum(lamv[0:1] * lamv[1:2], axis=-1, keepdims=True)
    d2 = jnp.sum(lamv[2:3] * lamv[3:4], axis=-1, keepdims=True)
    return jnp.exp(d1) - jnp.exp(d2) + lam_init


def _attn_prompt_body(sl_ref, lamv_ref, g_ref, pos_ref, qt_ref, k_ref, vt_ref, o_ref,
                        qq_ref, m_ref, acc_ref, t0_ref, t1_ref, *, tq, lam_init):
    hp = pl.program_id(1)
    qi = pl.program_id(2)
    tk = tq
    n_hh = qq_ref.shape[0]
    heads = [hp * n_hh + hh for hh in range(n_hh)]
    hrows = [slice(hh * HEAD_W, (hh + 1) * HEAD_W) for hh in range(n_hh)]

    def prep_queries(qblk):
        q0 = pl.multiple_of(qblk * tq, tq)
        arow = lax.broadcasted_iota(jnp.int32, (HEAD_W, 2 * tq), 0)
        for hh in range(n_hh):
            qt = qt_ref[hrows[hh], pl.ds(q0, tq)].astype(F32)
            frow = lax.broadcasted_iota(jnp.int32, qt.shape, 0)
            qq_ref[hh, 0:HEAD_W, 0:tq] = jnp.where(frow < HEAD_DIM, qt, 0.0).astype(BF16)
            qq_ref[hh, 0:HEAD_W, tq:2 * tq] = jnp.where(frow >= HEAD_DIM, qt, 0.0).astype(BF16)
            slope_parts = jnp.where((arow & 1) == 0, sl_ref[1, heads[hh]], sl_ref[2, heads[hh]])
            qq_ref[hh, HEAD_W:2 * HEAD_W, :] = jnp.where(arow < 4, slope_parts, 0.0).astype(BF16)

    m_ref[...] = jnp.full(m_ref.shape, NEG, F32)
    acc_ref[...] = jnp.zeros(acc_ref.shape, F32)
    ones = jnp.ones((acc_ref.shape[1] - HEAD_W, tk), BF16)

    def scores(hh, kj):
        k0 = pl.multiple_of(kj * tk, tk)
        ka = jnp.concatenate([k_ref[pl.ds(k0, tk), hrows[hh]], pos_ref[...]], axis=1)
        return _dot(ka, qq_ref[hh])

    def consume(hh, kj, t):
        k0 = pl.multiple_of(kj * tk, tk)
        vta = jnp.concatenate([vt_ref[hrows[hh], pl.ds(k0, tk)], ones], axis=0)
        off = sl_ref[0, heads[hh]] * jnp.full((1, 2 * tq), (kj - qi) * tk, jnp.int32).astype(F32)
        m_old = m_ref[hh]
        m_new = jnp.maximum(m_old, jnp.max(t, axis=0, keepdims=True) + off)
        alpha = jnp.exp2(m_old - m_new)
        p = jnp.exp2(t - (m_new - off)).astype(BF16)
        acc_ref[hh] = alpha * acc_ref[hh] + _dot(vta, p)
        m_ref[hh] = m_new

    tbuf = (t0_ref, t1_ref)

    def step(kj, cur):
        for hh in range(n_hh):
            tbuf[1 - cur][hh] = scores(hh, kj + 1)
        for hh in range(n_hh):
            consume(hh, kj, tbuf[cur][hh])

    @pl.when(qi == 0)
    def _():
        prep_queries(0)
        for hh in range(n_hh):
            t0_ref[hh] = scores(hh, 0)

    def body(i, carry):
        step(2 * i, 0)
        step(2 * i + 1, 1)
        return carry

    lax.fori_loop(0, qi // 2, body, 0)
    odd = qi % 2 == 1

    @pl.when(odd)
    def _():
        step(qi - 1, 0)

    def finish(cur):
        krow = lax.broadcasted_iota(jnp.int32, (tk, 2 * tq), 0)
        qcol = lax.broadcasted_iota(jnp.int32, (tk, 2 * tq), 1)
        qcol = jnp.where(qcol >= tq, qcol - tq, qcol)
        lam = _diff_lambda(lamv_ref[...], lam_init)
        for hh in range(n_hh):
            consume(hh, qi, jnp.where(krow <= qcol, tbuf[cur][hh], NEG))
        prep_queries(jnp.minimum(qi + 1, pl.num_programs(2) - 1))
        for hh in range(n_hh):
            t0_ref[hh] = scores(hh, 0)
        for hh in range(n_hh):
            acc = acc_ref[hh]
            ot = acc[0:HEAD_W] / acc[HEAD_W:HEAD_W + 1]
            d = ot[:, 0:tq] - lam * ot[:, tq:2 * tq]
            ms = jnp.mean(d * d, axis=0, keepdims=True)
            dn = d * lax.rsqrt(ms + RMS_EPS) * g_ref[...] * (1.0 - lam_init)
            o_ref[:, hrows[hh]] = dn.T.astype(BF16)

    @pl.when(odd)
    def _():
        finish(1)

    @pl.when(jnp.logical_not(odd))
    def _():
        finish(0)


def _page_copy(hbm, buf, sem, kind, sl, p, page, page_rows):
    src = hbm.at[pl.ds(pl.multiple_of(page * page_rows, page_rows), page_rows)]
    return pltpu.make_async_copy(src, buf.at[sl, pl.ds(p * page_rows, page_rows)], sem.at[kind, sl])


def _attn_sample_prefetch(pt_ref, ck_hbm, cv_hbm, kbuf, vbuf, sem, seq, n_seq, *, n_pages, page_rows):
    def start_fetch(s, sl):
        for p in range(n_pages):
            page = pt_ref[s * n_pages + p]
            _page_copy(ck_hbm, kbuf, sem, 0, sl, p, page, page_rows).start()
            _page_copy(cv_hbm, vbuf, sem, 1, sl, p, page, page_rows).start()

    @pl.when(seq == 0)
    def _():
        start_fetch(0, 0)

    @pl.when(seq + 1 < n_seq)
    def _():
        start_fetch(seq + 1, 1 - seq % 2)


def _attn_sample_compute(lamv_ref, g_ref, slope2_ref, base2_ref, newmask_ref, q2_ref, kn_ref, vn_ref,
                         ck_hbm, cv_hbm, o_ref, kbuf, vbuf, sem, s_ref, seq,
                         *, n_pages, page_rows, n_heads, lam_init, chunk):
    slot = seq % 2
    past_rows = n_pages * page_rows
    nrow = q2_ref.shape[0] // 2

    for p in range(n_pages):
        _page_copy(ck_hbm, kbuf, sem, 0, slot, p, 0, page_rows).wait()
        _page_copy(cv_hbm, vbuf, sem, 1, slot, p, 0, page_rows).wait()

    pad = jnp.zeros((chunk - kn_ref.shape[0], HEAD_W), F32)
    kbuf[slot, past_rows:past_rows + chunk, :] = jnp.concatenate([kn_ref[...], pad], axis=0)
    vbuf[slot, past_rows:past_rows + chunk, :] = jnp.concatenate([vn_ref[...], pad], axis=0)

    q2 = q2_ref[...]
    slope2 = slope2_ref[...]
    n_pairs = past_rows // (2 * chunk)

    def pair_tile(buf, a):
        r0 = 2 * a * chunk
        return jnp.concatenate([buf[slot, r0:r0 + chunk, :].astype(BF16),
                                buf[slot, r0 + chunk:r0 + 2 * chunk, :].astype(BF16)], axis=1)

    def score_pair(a):
        tok0 = float(2 * a * (chunk // n_heads) - past_rows // n_heads)
        s_ref[:, a * chunk:(a + 1) * chunk] = _dot_nt(q2, pair_tile(kbuf, a)) + base2_ref[...] + slope2 * tok0

    def fold(acc):
        return acc[0:nrow, 0:HEAD_W] + acc[nrow:2 * nrow, HEAD_W:2 * HEAD_W]

    def softmax_half(lo, hi, extra=None):
        s = s_ref[:, lo * chunk:hi * chunk]
        mr = jnp.max(s, axis=-1, keepdims=True)
        m = jnp.maximum(mr[0:nrow], mr[nrow:2 * nrow])
        if extra is not None:
            m = jnp.maximum(m, jnp.max(extra, axis=-1, keepdims=True))
        p = jnp.exp2(s - jnp.concatenate([m, m], axis=0))
        lr = jnp.sum(p, axis=-1, keepdims=True)
        return m, lr[0:nrow] + lr[nrow:2 * nrow], p.astype(BF16)

    gp = n_pairs // DECODE_GROUPS
    q = q2[0:nrow, 0:HEAD_W]
    kn = kbuf[slot, past_rows:past_rows + chunk, :].astype(BF16)
    vn = vbuf[slot, past_rows:past_rows + chunk, :].astype(BF16)
    for a in range(gp):
        score_pair(a)
    parts = []
    for g in range(DECODE_GROUPS):
        last = g == DECODE_GROUPS - 1
        s_new = _dot_nt(q, kn) + newmask_ref[...] if last else None
        m_g, l_g, p_g = softmax_half(g * gp, (g + 1) * gp, extra=s_new)
        acc = jnp.zeros((2 * nrow, 2 * HEAD_W), F32)
        for a in range(gp):
            if not last:
                score_pair((g + 1) * gp + a)
            acc = acc + _dot(p_g[:, a * chunk:(a + 1) * chunk], pair_tile(vbuf, g * gp + a))
        o_g = fold(acc)
        if last:
            p_new = jnp.exp2(s_new - m_g)
            l_g = l_g + jnp.sum(p_new, axis=-1, keepdims=True)
            o_g = o_g + _dot(p_new.astype(BF16), vn)
        parts.append((m_g, l_g, o_g))
    m = functools.reduce(jnp.maximum, [pt[0] for pt in parts])
    ws = [jnp.exp2(pt[0] - m) for pt in parts]
    o = sum(pt[2] * w for pt, w in zip(parts, ws)) / sum(pt[1] * w for pt, w in zip(parts, ws))
    lam = _diff_lambda(lamv_ref[...], lam_init)
    half = nrow // 2
    d = o[0:half] - lam * o[half:nrow]
    o_ref[...] = _rms(d, g_ref[...]) * (1.0 - lam_init)


def _attn_kernel(pt_ref, sl_ref, lamv_ref, gcol_ref, pos_ref, qt_ref, k_ref, vt_ref,
                 grow_ref, slope2_ref, base2_ref, newmask_ref, q2_ref, kn_ref, vn_ref, ck_hbm, cv_hbm,
                 op_ref, os_ref,
                 qq_ref, m_ref, acc_ref, t0_ref, t1_ref, kbuf, vbuf, sem, s_ref,
                 *, tq, lam_init, n_pages, page_rows, n_heads, chunk):
    seq = (pl.program_id(0) * pl.num_programs(1) + pl.program_id(1)) * pl.num_programs(2) + pl.program_id(2)
    n_seq = pl.num_programs(0) * pl.num_programs(1) * pl.num_programs(2)
    _attn_sample_prefetch(pt_ref, ck_hbm, cv_hbm, kbuf, vbuf, sem, seq, n_seq,
                          n_pages=n_pages, page_rows=page_rows)
    _attn_prompt_body(sl_ref, lamv_ref, gcol_ref, pos_ref, qt_ref, k_ref, vt_ref, op_ref,
                      qq_ref, m_ref, acc_ref, t0_ref, t1_ref, tq=tq, lam_init=lam_init)
    _attn_sample_compute(lamv_ref, grow_ref, slope2_ref, base2_ref, newmask_ref, q2_ref, kn_ref, vn_ref,
                         ck_hbm, cv_hbm, os_ref, kbuf, vbuf, sem, s_ref, seq,
                         n_pages=n_pages, page_rows=page_rows, n_heads=n_heads, lam_init=lam_init, chunk=chunk)


def _attn(qt, kb, vt, slope_parts, lamv, g_col, page_table, q2, knf, vnf, ckf, cvf, g_row, slope2, base2,
          newmask, lam_init, tq):
    b, aw, t = qt.shape
    nh = aw // HEAD_W
    nq = t // tq
    db, nrow2, _ = q2.shape
    nrow = nrow2 // 2
    n_pages = page_table.shape[1]
    chunk = base2.shape[1]
    page_rows = PAGE * nh
    past_rows = n_pages * page_rows
    assert past_rows % (2 * chunk) == 0 and knf.shape[1] <= chunk
    assert db == b * nh * nq, "one decode sequence per prompt grid step"
    i = jnp.arange(tq)
    lane = jnp.arange(HEAD_W)
    feat = jnp.where(lane[None, :] < 2, (i % 256)[:, None], jnp.where(lane[None, :] < 4, (i // 256 * 256)[:, None], 0))
    pos = feat.astype(BF16)
    seq_of = lambda bi, hi, qi: (bi * nh + hi) * nq + qi
    per_seq = lambda a: pl.BlockSpec((None,) + a.shape[1:], lambda bi, hi, qi, pt: (seq_of(bi, hi, qi), 0, 0))
    const = lambda a: pl.BlockSpec(a.shape, lambda bi, hi, qi, pt: (0,) * a.ndim, pipeline_mode=pl.Buffered(1))
    grid_spec = pltpu.PrefetchScalarGridSpec(
        num_scalar_prefetch=1,
        grid=(b, nh, nq),
        in_specs=[
            pl.BlockSpec(memory_space=pltpu.SMEM),
            const(lamv), const(g_col), const(pos),
            pl.BlockSpec((None, HEAD_W, t), lambda bi, hi, qi, pt: (bi, hi, 0)),
            pl.BlockSpec((None, t, HEAD_W), lambda bi, hi, qi, pt: (bi, 0, hi)),
            pl.BlockSpec((None, HEAD_W, t), lambda bi, hi, qi, pt: (bi, hi, 0)),
            const(g_row), const(slope2), const(base2), const(newmask),
            per_seq(q2), per_seq(knf), per_seq(vnf),
            pl.BlockSpec(memory_space=pl.ANY),
            pl.BlockSpec(memory_space=pl.ANY),
        ],
        out_specs=[
            pl.BlockSpec((None, tq, HEAD_W), lambda bi, hi, qi, pt: (bi, qi, hi)),
            pl.BlockSpec((None, nrow // 2, HEAD_W), lambda bi, hi, qi, pt: (seq_of(bi, hi, qi), 0, 0)),
        ],
        scratch_shapes=[
            pltpu.VMEM((1, 2 * HEAD_W, 2 * tq), BF16),
            pltpu.VMEM((1, 1, 2 * tq), F32),
            pltpu.VMEM((1, HEAD_W + BF16_ROWS, 2 * tq), F32),
            pltpu.VMEM((1, tq, 2 * tq), F32),
            pltpu.VMEM((1, tq, 2 * tq), F32),
            pltpu.VMEM((2, past_rows + chunk, HEAD_W), F32),
            pltpu.VMEM((2, past_rows + chunk, HEAD_W), F32),
            pltpu.SemaphoreType.DMA((2, 2)),
            pltpu.VMEM((nrow2, past_rows // 2), F32),
        ],
    )
    return pl.pallas_call(
        functools.partial(_attn_kernel, tq=tq, lam_init=lam_init, n_pages=n_pages, page_rows=page_rows,
                          n_heads=nh, chunk=chunk),
        grid_spec=grid_spec,
        out_shape=[jax.ShapeDtypeStruct((b, t, aw), BF16),
                   jax.ShapeDtypeStruct((db, nrow // 2, HEAD_W), F32)],
        compiler_params=_cparams(3),
        name="attn",
    )(page_table.reshape(-1), slope_parts, lamv, g_col, pos, qt, kb, vt,
      g_row, slope2, base2, newmask, q2, knf, vnf, ckf, cvf)


def _s5_prep_kernel(lre_ref, lim_ref, ldt_ref, bre_ref, bim_ref, cre_ref, cim_ref, dsk_ref,
                    wst_ref, cat_ref, kt_ref, sct_ref):
    lr = lre_ref[...]
    li = lim_ref[...]
    dt = jnp.exp(ldt_ref[...])

    def lam_bar_pow(n):
        mag = jnp.exp(n * (lr * dt))
        ang = n * (li * dt)
        return mag * jnp.cos(ang), mag * jnp.sin(ang)

    a_r, a_i = lam_bar_pow(1.0)
    den = lr * lr + li * li
    nr, ni = a_r - 1.0, a_i
    f_r = (nr * lr + ni * li) / den
    f_i = (ni * lr - nr * li) / den
    b_r, b_i = bre_ref[...], bim_ref[...]
    bb_r = f_r * b_r - f_i * b_i
    bb_i = f_r * b_i + f_i * b_r
    c_r, c_i = cre_ref[...], cim_ref[...]
    gw = b_r.shape[0]
    ns = b_r.shape[1]

    ktau = []
    for tau in range(S5_BLOCK):
        p_r, p_i = lam_bar_pow(float(tau))
        w_r = bb_r * p_r - bb_i * p_i
        w_i = bb_r * p_i + bb_i * p_r
        j = S5_BLOCK - 1 - tau
        wst_ref[j * gw:(j + 1) * gw, 0:ns] = w_r.astype(BF16)
        wst_ref[j * gw:(j + 1) * gw, ns:2 * ns] = w_i.astype(BF16)
        ktau.append(_dot_nt(w_r, c_r, lax.Precision.HIGHEST) - _dot_nt(w_i, c_i, lax.Precision.HIGHEST))
    rr = lax.broadcasted_iota(jnp.int32, (gw, gw), 0)
    cc = lax.broadcasted_iota(jnp.int32, (gw, gw), 1)
    ktau[0] = ktau[0] + jnp.where(rr == cc, dsk_ref[...], 0.0)

    for j in range(S5_BLOCK):
        p_r, p_i = lam_bar_pow(float(j + 1))
        cat_ref[j * gw:(j + 1) * gw, 0:ns] = (c_r * p_r - c_i * p_i).astype(BF16)
        cat_ref[j * gw:(j + 1) * gw, ns:2 * ns] = (-(c_r * p_i + c_i * p_r)).astype(BF16)

    zeros = jnp.zeros((gw, gw), BF16)
    for i in range(S5_BLOCK):
        for j in range(S5_BLOCK):
            blk = ktau[j - i].astype(BF16) if j >= i else zeros
            kt_ref[i * gw:(i + 1) * gw, j * gw:(j + 1) * gw] = blk

    row = lax.broadcasted_iota(jnp.int32, (8, ns), 0)
    for idx, k in enumerate((1, 2, 4)):
        p_r, p_i = lam_bar_pow(float(S5_BLOCK * k))
        sct_ref[2 * idx] = jnp.where(row >= k, p_r, 0.0)
        sct_ref[2 * idx + 1] = jnp.where(row >= k, p_i, 0.0)
    n = (S5_BLOCK * (row + 1)).astype(F32)
    mag = jnp.exp(n * (lr * dt))
    ang = n * (li * dt)
    sct_ref[6] = mag * jnp.cos(ang)
    sct_ref[7] = mag * jnp.sin(ang)


def _s5_prep(lre, lim, ldt, bre, bim, cre, cim, dsk):
    nq, gw, ns = bre.shape
    kw = S5_BLOCK * gw
    per_q = lambda *tail: pl.BlockSpec((None,) + tail, lambda q: (q,) + (0,) * len(tail))
    return pl.pallas_call(
        _s5_prep_kernel,
        grid=(nq,),
        in_specs=[per_q(1, ns), per_q(1, ns), per_q(1, ns), per_q(gw, ns), per_q(gw, ns),
                  per_q(gw, ns), per_q(gw, ns), per_q(1, gw)],
        out_specs=[per_q(kw, 2 * ns), per_q(kw, 2 * ns), per_q(kw, kw), per_q(8, 8, ns)],
        out_shape=[
            jax.ShapeDtypeStruct((nq, kw, 2 * ns), BF16),
            jax.ShapeDtypeStruct((nq, kw, 2 * ns), BF16),
            jax.ShapeDtypeStruct((nq, kw, kw), BF16),
            jax.ShapeDtypeStruct((nq, 8, 8, ns), F32),
        ],
        compiler_params=_cparams(1),
        name="s5_prep",
    )(lre, lim, ldt, bre, bim, cre, cim, dsk)


def _quarter_lanes(u_ref, q, gw, sw):
    return jnp.concatenate(
        [u_ref[:, j * sw + q * gw: j * sw + (q + 1) * gw] for j in range(S5_BLOCK)], axis=1)


def _store_quarter(y_ref, y, q, gw, sw):
    for j in range(S5_BLOCK):
        y_ref[:, j * sw + q * gw: j * sw + (q + 1) * gw] = y[:, j * gw:(j + 1) * gw].astype(y_ref.dtype)


def _s5_prompt_kernel(u_ref, wst_ref, cat_ref, kt_ref, sct_ref, y_ref, st_ref,
                      carry_ref, s_scr, hp_scr, *, rb, gw, sw, ns):
    blk = pl.program_id(1)

    @pl.when(blk == 0)
    def _():
        carry_ref[...] = jnp.zeros(carry_ref.shape, F32)

    row0 = lax.broadcasted_iota(jnp.int32, (8, ns), 0) == 0
    for q in range(S5_QUARTERS):
        uq = _quarter_lanes(u_ref, q, gw, sw)
        s_scr[...] = _dot(uq, wst_ref[q])
        tabs = [sct_ref[q, i] for i in range(8)]

        def tile(t, carry, tabs=tabs):
            cr, ci = carry
            r0 = pl.multiple_of(t * 8, 8)
            xr = s_scr[pl.ds(r0, 8), 0:ns]
            xi = s_scr[pl.ds(r0, 8), ns:2 * ns]
            for idx, k in enumerate((1, 2, 4)):
                mr, mi = tabs[2 * idx], tabs[2 * idx + 1]
                sr = pltpu.roll(xr, k, 0)
                si = pltpu.roll(xi, k, 0)
                xr, xi = xr + (mr * sr - mi * si), xi + (mr * si + mi * sr)
            crb = jnp.broadcast_to(cr, (8, ns))
            cib = jnp.broadcast_to(ci, (8, ns))
            hr = xr + (tabs[6] * crb - tabs[7] * cib)
            hi = xi + (tabs[6] * cib + tabs[7] * crb)
            hp_scr[pl.ds(r0, 8), 0:ns] = jnp.where(row0, crb, pltpu.roll(hr, 1, 0))
            hp_scr[pl.ds(r0, 8), ns:2 * ns] = jnp.where(row0, cib, pltpu.roll(hi, 1, 0))
            return hr[7:8, :], hi[7:8, :]

        cr, ci = lax.fori_loop(0, rb // 8, tile,
                               (carry_ref[q, :, 0:ns], carry_ref[q, :, ns:2 * ns]))
        carry_ref[q, :, 0:ns] = cr
        carry_ref[q, :, ns:2 * ns] = ci
        st_ref[q, :, 0:ns] = cr
        st_ref[q, :, ns:2 * ns] = ci
        y = _dot_nt(hp_scr[...].astype(BF16), cat_ref[q]) + _dot(uq, kt_ref[q])
        _store_quarter(y_ref, y, q, gw, sw)


def _s5_prompt(u4, wst, cat, kt, sct, rb):
    b, rows, lanes = u4.shape
    nq, kw, ns2 = wst.shape
    ns, gw, sw = ns2 // 2, kw // S5_BLOCK, lanes // S5_BLOCK
    return pl.pallas_call(
        functools.partial(_s5_prompt_kernel, rb=rb, gw=gw, sw=sw, ns=ns),
        grid=(b, rows // rb),
        in_specs=[
            pl.BlockSpec((None, rb, lanes), lambda bi, ri: (bi, ri, 0)),
            _const_spec(wst.shape), _const_spec(cat.shape), _const_spec(kt.shape), _const_spec(sct.shape),
        ],
        out_specs=[
            pl.BlockSpec((None, rb, lanes), lambda bi, ri: (bi, ri, 0)),
            pl.BlockSpec((None, nq, 1, ns2), lambda bi, ri: (bi, 0, 0, 0)),
        ],
        out_shape=[
            jax.ShapeDtypeStruct((b, rows, lanes), BF16),
            jax.ShapeDtypeStruct((b, nq, 1, ns2), F32),
        ],
        scratch_shapes=[
            pltpu.VMEM((nq, 1, ns2), F32),
            pltpu.VMEM((rb, ns2), F32),
            pltpu.VMEM((rb, ns2), F32),
        ],
        compiler_params=_cparams(2),
        name="s5_prompt",
    )(u4, wst, cat, kt, sct)


def _s5_sample_kernel(u_ref, h0_ref, wst_ref, cat_ref, kt_ref, sct_ref, y_ref, hout_ref, *, gw, sw, ns):
    for q in range(S5_QUARTERS):
        uq = _quarter_lanes(u_ref, q, gw, sw)
        s = _dot(uq, wst_ref[q])
        h0 = h0_ref[q]
        h0r, h0i = h0[:, 0:ns], h0[:, ns:2 * ns]
        a4r = sct_ref[q, 6][0:1, :]
        a4i = sct_ref[q, 7][0:1, :]
        hout_ref[q, :, 0:ns] = s[:, 0:ns] + (a4r * h0r - a4i * h0i)
        hout_ref[q, :, ns:2 * ns] = s[:, ns:2 * ns] + (a4r * h0i + a4i * h0r)
        y = _dot_nt(h0.astype(BF16), cat_ref[q]) + _dot(uq, kt_ref[q])
        _store_quarter(y_ref, y, q, gw, sw)


def _s5_sample(u4, h0q, wst, cat, kt, sct):
    rows, lanes = u4.shape
    nq, kw, ns2 = wst.shape
    ns, gw, sw = ns2 // 2, kw // S5_BLOCK, lanes // S5_BLOCK
    return pl.pallas_call(
        functools.partial(_s5_sample_kernel, gw=gw, sw=sw, ns=ns),
        grid=(1,),
        in_specs=[_const_spec(u4.shape), _const_spec(h0q.shape), _const_spec(wst.shape),
                  _const_spec(cat.shape), _const_spec(kt.shape), _const_spec(sct.shape)],
        out_specs=[pl.BlockSpec(u4.shape, lambda i: (0, 0)),
                   pl.BlockSpec(h0q.shape, lambda i: (0, 0, 0))],
        out_shape=[jax.ShapeDtypeStruct(u4.shape, BF16), jax.ShapeDtypeStruct(h0q.shape, F32)],
        compiler_params=_cparams(1),
        name="s5_sample",
    )(u4, h0q, wst, cat, kt, sct)


def _mem_kv_kernel(x_ref, g_ref, wk_ref, wv_ref, k_ref, v_ref, kb_ref, vb_ref):
    hn = _rms(x_ref[...], g_ref[...]).astype(BF16)
    k = _dot(hn, wk_ref[...])
    k_ref[...] = k
    kb_ref[...] = k.astype(BF16)
    v = _dot(hn, wv_ref[...])
    v_ref[...] = v
    vb_ref[...] = v.astype(BF16)


def _mem_kv(mem, g, wk_bf, wv_bf, tm):
    n, d = mem.shape
    row = pl.BlockSpec((tm, d), lambda i: (i, 0))
    return pl.pallas_call(
        _mem_kv_kernel,
        grid=(n // tm,),
        in_specs=[row, _const_spec((1, d)), _const_spec(wk_bf.shape), _const_spec(wv_bf.shape)],
        out_specs=[row, row, row, row],
        out_shape=[jax.ShapeDtypeStruct((n, d), F32), jax.ShapeDtypeStruct((n, d), F32),
                   jax.ShapeDtypeStruct((n, d), BF16), jax.ShapeDtypeStruct((n, d), BF16)],
        compiler_params=_cparams(1),
        name="mem_kv",
    )(mem, g, wk_bf, wv_bf)


def _merge_and_query(x, att, sy4_ref, sy_scr, wglu_ref, gso_ref, wout_ref, gpost_ref, gmpre_ref, wmq_ref,
                     mem_scale):
    aw = att.shape[1]
    rows, n_lb = sy4_ref.shape[0], sy_scr.shape[0]
    sw = n_lb * LANES
    for j in range(S5_BLOCK):
        for c in range(n_lb):
            blk = sy4_ref[:, j * sw + c * LANES:j * sw + (c + 1) * LANES].astype(F32)
            sy_scr[c, pl.ds(j, rows, stride=S5_BLOCK), :] = blk
    y = jax.nn.gelu(jnp.concatenate([sy_scr[c] for c in range(n_lb)], axis=1))
    y = y * jax.nn.sigmoid(_dot(y.astype(BF16), wglu_ref[...]))
    y = _rms(y, gso_ref[...])
    mix = _dot(att, wout_ref[0:aw, :]) + _dot(y.astype(BF16), wout_ref[aw:, :])
    x1 = x + _rms(mix, gpost_ref[...])
    hq = _rms(x1, gmpre_ref[...]).astype(BF16)
    qm = (_dot(hq, wmq_ref[...]) * mem_scale).astype(BF16)
    return x1, qm


def _mix_mem_body(x_ref, att_ref, sy_ref, mk_ref, mv_ref, wglu_ref, gso_ref, wout_ref, gpost_ref,
                  gmpre_ref, wmq_ref, wmo_ref, gmpost_ref, o_ref, oh_ref, sy_scr, *, n_heads, mem_scale):
    x1, qm = _merge_and_query(x_ref[...], att_ref[...], sy_ref, sy_scr, wglu_ref, gso_ref, wout_ref,
                              gpost_ref, gmpre_ref, wmq_ref, mem_scale)
    hd = qm.shape[1] // n_heads
    sls = [slice(h * hd, (h + 1) * hd) for h in range(n_heads)]
    ss = [_dot_nt(qm[:, sl], mk_ref[:, sl]) for sl in sls]
    ps = [jnp.exp(s - jnp.max(s, axis=-1, keepdims=True)) for s in ss]
    for sl, p in zip(sls, ps):
        l = jnp.sum(p, axis=-1, keepdims=True)
        oh_ref[:, sl] = (_dot(p.astype(BF16), mv_ref[:, sl]) / l).astype(BF16)
    mo = _dot(oh_ref[...], wmo_ref[...])
    o_ref[...] = x1 + _rms(mo, gmpost_ref[...])


def _mix_q_kernel(x_ref, att_ref, sy_ref, wglu_ref, gso_ref, wout_ref, gpost_ref, gmpre_ref, wmq_ref,
                  x1_ref, qm_ref, sy_scr, *, mem_scale):
    x1, qm = _merge_and_query(x_ref[...], att_ref[...], sy_ref, sy_scr, wglu_ref, gso_ref, wout_ref,
                              gpost_ref, gmpre_ref, wmq_ref, mem_scale)
    x1_ref[...] = x1
    qm_ref[...] = qm


def _mix_q(x, att, sy4, w, n_heads):
    n, d = x.shape
    weights = [w["w_glu"], w["g_ssm_out"], w["w_out"], w["g_mix_post"], w["g_mem_pre"], w["w_mq"]]
    ins = [x, att, sy4] + weights
    return pl.pallas_call(
        functools.partial(_mix_q_kernel, mem_scale=(d // n_heads) ** -0.5),
        grid=(1,),
        in_specs=[_const_spec(a.shape) for a in ins],
        out_specs=[pl.BlockSpec((n, d), lambda i: (0, 0)), pl.BlockSpec((n, d), lambda i: (0, 0))],
        out_shape=[jax.ShapeDtypeStruct((n, d), F32), jax.ShapeDtypeStruct((n, d), BF16)],
        scratch_shapes=[pltpu.VMEM((sy4.shape[1] // S5_BLOCK // LANES, n, LANES), F32)],
        compiler_params=_cparams(1),
        name="mix_q",
    )(*ins)


def _mem_sample_copies(ck_hbm, cv_hbm, kbuf, vbuf, sem, step, n_per, n_mem, n_heads):
    sl = step % 2
    out = []
    for j in range(n_per):
        for kind, (hbm, buf) in enumerate(((ck_hbm, kbuf), (cv_hbm, vbuf))):
            for lb in range(kbuf.shape[2]):
                src = hbm.at[step * n_per + j, :, :, pl.ds(lb * LANES, LANES)]
                dst = buf.at[sl, j, lb].reshape(n_mem, n_heads, LANES)
                out.append(pltpu.make_async_copy(src, dst, sem.at[kind, sl]))
    return out


def _mem_sample_compute(mask_ref, q_ref, o_ref, kbuf, vbuf, slot, js):
    n_halves = kbuf.shape[2]
    cat = lambda buf, j: jnp.concatenate([buf[slot, j, lb] for lb in range(n_halves)], axis=1).astype(BF16)
    ss = [_dot_nt(q_ref[j], cat(kbuf, j)) + mask_ref[...] for j in js]
    ps = [jnp.exp(s - jnp.max(s, axis=-1, keepdims=True)) for s in ss]
    ls = [jnp.sum(p, axis=-1, keepdims=True) for p in ps]
    for j, p, l in zip(js, ps, ls):
        o_ref[j] = _dot(p.astype(BF16), cat(vbuf, j)) / l


def _mix_mem_kernel(x_ref, att_ref, sy_ref, mk_ref, mv_ref, wglu_ref, gso_ref, wout_ref, gpost_ref,
                    gmpre_ref, wmq_ref, wmo_ref, gmpost_ref, mask_ref, qs_ref, ck_hbm, cv_hbm,
                    o_ref, os_ref, oh_ref, sy_scr, kbuf, vbuf, sem, *, n_heads, mem_scale, n_mem):
    step = pl.program_id(0) * pl.num_programs(1) + pl.program_id(1)
    n_steps = pl.num_programs(0) * pl.num_programs(1)
    n_per = qs_ref.shape[0]
    copies = lambda st: _mem_sample_copies(ck_hbm, cv_hbm, kbuf, vbuf, sem, st, n_per, n_mem, n_heads)

    @pl.when(step == 0)
    def _():
        for cp in copies(0):
            cp.start()

    @pl.when(step + 1 < n_steps)
    def _():
        for cp in copies(step + 1):
            cp.start()

    _mix_mem_body(x_ref, att_ref, sy_ref, mk_ref, mv_ref, wglu_ref, gso_ref, wout_ref, gpost_ref,
                  gmpre_ref, wmq_ref, wmo_ref, gmpost_ref, o_ref, oh_ref, sy_scr,
                  n_heads=n_heads, mem_scale=mem_scale)

    for cp in copies(step):
        cp.wait()
    _mem_sample_compute(mask_ref, qs_ref, os_ref, kbuf, vbuf, step % 2, range(n_per))


def _mix_mem(x, att, sy4, mkb, mvb, w, qh, ck, cv, mask, tm, n_heads):
    b, t, d = x.shape
    aw, sw = att.shape[2], sy4.shape[2] // S5_BLOCK
    nm = mkb.shape[1]
    nt = t // tm
    db, nrow, hd = qh.shape
    n_ckeys = ck.shape[1]
    n_per = db // (b * nt)
    assert db == n_per * b * nt, "decode sequences must divide evenly over the prompt grid steps"
    tok = lambda width: pl.BlockSpec((None, tm, width), lambda bi, ti: (bi, ti, 0))
    tok4 = pl.BlockSpec((None, tm // S5_BLOCK, S5_BLOCK * sw), lambda bi, ti: (bi, ti, 0))
    per_b = pl.BlockSpec((None, nm, d), lambda bi, ti: (bi, 0, 0))
    per_step = pl.BlockSpec((n_per, nrow, hd), lambda bi, ti: (bi * nt + ti, 0, 0))
    any_spec = pl.BlockSpec(memory_space=pl.ANY)
    weights = [w["w_glu"], w["g_ssm_out"], w["w_out"], w["g_mix_post"], w["g_mem_pre"], w["w_mq"],
               w["w_mo"], w["g_mem_post"], mask]
    return pl.pallas_call(
        functools.partial(_mix_mem_kernel, n_heads=n_heads, mem_scale=(d // n_heads) ** -0.5, n_mem=n_ckeys),
        grid=(b, nt),
        in_specs=[tok(d), tok(aw), tok4, per_b, per_b] + [_const_spec(a.shape) for a in weights]
                 + [per_step, any_spec, any_spec],
        out_specs=[tok(d), per_step],
        out_shape=[jax.ShapeDtypeStruct((b, t, d), F32), jax.ShapeDtypeStruct((db, nrow, hd), F32)],
        scratch_shapes=[
            pltpu.VMEM((tm, d), BF16),
            pltpu.VMEM((sw // LANES, tm, LANES), F32),
            pltpu.VMEM((2, n_per, hd // LANES, n_ckeys * n_heads, LANES), F32),
            pltpu.VMEM((2, n_per, hd // LANES, n_ckeys * n_heads, LANES), F32),
            pltpu.SemaphoreType.DMA((2, 2)),
        ],
        compiler_params=_cparams(2),
        name="mix_mem",
    )(x, att, sy4, mkb, mvb, *weights, qh, ck, cv)


def _mem_out_kernel(x1_ref, o_ref, wmo_ref, g_ref, y_ref):
    mo = _dot(o_ref[...], wmo_ref[...])
    y_ref[...] = x1_ref[...] + _rms(mo, g_ref[...])


def _mem_out(x1, o, wmo, g):
    n, d = x1.shape
    ins = [x1, o, wmo, g]
    return pl.pallas_call(
        _mem_out_kernel,
        grid=(1,),
        in_specs=[_const_spec(a.shape) for a in ins],
        out_specs=pl.BlockSpec((n, d), lambda i: (0, 0)),
        out_shape=jax.ShapeDtypeStruct((n, d), F32),
        compiler_params=_cparams(1),
        name="mem_out",
    )(*ins)


def _ffn_kernel(x_ref, gpre_ref, wg_ref, wu_ref, wd_ref, gpost_ref, o_ref, *, n_chunks):
    x = x_ref[...]
    hn = _rms(x, gpre_ref[...]).astype(BF16)
    ff = wg_ref.shape[1]
    cw = ff // n_chunks
    acc = jnp.zeros(x.shape, F32)
    for c in range(n_chunks):
        sl = slice(c * cw, (c + 1) * cw)
        a = jax.nn.silu(_dot(hn, wg_ref[:, sl])) * _dot(hn, wu_ref[:, sl])
        acc = acc + _dot(a.astype(BF16), wd_ref[sl, :])
    o_ref[...] = x + _rms(acc, gpost_ref[...])


def _ffn(x, gpre, wg, wu, wd, gpost, tm, n_chunks):
    n, d = x.shape
    row = pl.BlockSpec((tm, d), lambda i: (i, 0))
    weights = [gpre, wg, wu, wd, gpost]
    return pl.pallas_call(
        functools.partial(_ffn_kernel, n_chunks=n_chunks),
        grid=(n // tm,),
        in_specs=[row] + [_const_spec(a.shape) for a in weights],
        out_specs=row,
        out_shape=jax.ShapeDtypeStruct((n, d), F32),
        compiler_params=_cparams(1),
        name="ffn",
    )(x, *weights)


def _s5_param_layout(lam_re, lam_im, log_dt, b_re, b_im, c_re, c_im, d_skip):
    g, p, hc = b_re.shape
    nq = S5_QUARTERS
    gq = g // nq
    rowv = lambda a: a.reshape(nq, 1, gq * p)
    ldt = jnp.broadcast_to(log_dt[:, None], (g, p))
    eye = jnp.eye(gq, dtype=F32)

    def embed_b(b):
        bq = b.reshape(nq, gq, p, hc)
        return jnp.einsum("qgph,gk->qghkp", bq, eye).reshape(nq, gq * hc, gq * p)

    def embed_c(c):
        cq = c.reshape(nq, gq, hc, p)
        return jnp.einsum("qghp,gk->qghkp", cq, eye).reshape(nq, gq * hc, gq * p)

    return (rowv(lam_re), rowv(lam_im), rowv(ldt), embed_b(b_re), embed_b(b_im),
            embed_c(c_re), embed_c(c_im), d_skip.reshape(nq, 1, gq * hc))


def _state_to_quarters(re, im):
    n, g, p = re.shape
    gq = g // S5_QUARTERS
    f = lambda a: a.reshape(n, S5_QUARTERS, gq * p).transpose(1, 0, 2)
    return jnp.concatenate([f(re), f(im)], axis=-1)


def _state_from_quarters(st, g, p):
    n = st.shape[0]
    ns = st.shape[-1] // 2
    return st[..., :ns].reshape(n, g, p), st[..., ns:].reshape(n, g, p)


def kernel(x_prompt, x_sample, mem_prompt, cache_k, cache_v, page_table, state_ssm_re, state_ssm_im,
           cache_mem_k, cache_mem_v, g_mix_pre, g_mix_post, w_in, lam_q1, lam_k1, lam_q2, lam_k2, g_subln,
           ssm_lam_re, ssm_lam_im, ssm_log_dt, ssm_b_re, ssm_b_im, ssm_c_re, ssm_c_im, ssm_d, w_glu, g_ssm_out,
           w_out, g_mem_pre, g_mem_post, g_mem_kv, w_mq, w_mk, w_mv, w_mo, g_ffn_pre, g_ffn_post,
           w_gate, w_up, w_down):
    depth = w_in.shape[0]
    b, t, d = x_prompt.shape
    db, dt_, _ = x_sample.shape
    n_mem = mem_prompt.shape[1]
    n_mem_heads = cache_mem_k.shape[3]
    n_heads = cache_k.shape[3]
    aw = n_heads * HEAD_W
    g, p = ssm_lam_re.shape[1], ssm_lam_re.shape[2]
    sw = g * SSM_GROUP
    n_phys = cache_k.shape[1]
    assert dt_ == S5_BLOCK and t % (S5_BLOCK * 256) == 0

    slopes = LOG2E * 2.0 ** (-8.0 * jnp.arange(1, n_heads + 1, dtype=F32) / n_heads)
    slope_hi = slopes.astype(BF16).astype(F32)
    slope_parts = jnp.stack([slopes, slope_hi, (slopes - slope_hi).astype(BF16).astype(F32)])
    row = lambda a: a.reshape(1, -1).astype(F32)
    bf = lambda a: a.astype(BF16)

    branch_lane = jnp.arange(HEAD_W) // HEAD_DIM
    nrow = 2 * n_heads * dt_
    chunk = 256
    r = jnp.arange(nrow)
    r_head, r_tok = (r // dt_) % n_heads, r % dt_
    r_slope = slopes[r_head][:, None]
    c = jnp.arange(chunk)
    c_tok, c_head = c // n_heads, c % n_heads
    own_head = c_head[None, :] == r_head[:, None]
    base = r_slope * c_tok[None, :].astype(F32) + jnp.where(own_head, 0.0, NEG)
    base2 = jnp.concatenate([base, base + r_slope * float(chunk // n_heads)], axis=0)
    slope2 = jnp.concatenate([r_slope, r_slope], axis=0)
    new_ok = own_head & (c[None, :] < dt_ * n_heads) & (c_tok[None, :] <= r_tok[:, None])
    newmask = jnp.where(new_ok, r_slope * c_tok[None, :].astype(F32), NEG)
    mem_row_head = jnp.arange(n_mem_heads * dt_) // dt_
    mem_col_head = jnp.arange(n_mem * n_mem_heads) % n_mem_heads
    mem_mask = jnp.where(mem_col_head[None, :] == mem_row_head[:, None], 0.0, NEG).astype(F32)

    y_p = x_prompt
    y_s = x_sample.reshape(db * dt_, d)
    outs = {k: [] for k in ("kp", "vp", "ks", "vs", "rp", "ip", "rs", "is", "mkp", "mvp")}
    for l in range(depth):
        lam_init = 0.8 - 0.6 * math.exp(-0.3 * l)
        lamv = jnp.stack([lam_q1[l], lam_k1[l], lam_q2[l], lam_k2[l]]).astype(F32)
        gsub = row(g_subln[l])
        w_in_bf = bf(w_in[l])
        w = {
            "w_glu": bf(w_glu[l]), "g_ssm_out": row(g_ssm_out[l]), "w_out": bf(w_out[l]),
            "g_mix_post": row(g_mix_post[l]), "g_mem_pre": row(g_mem_pre[l]), "w_mq": bf(w_mq[l]),
            "w_mo": bf(w_mo[l]), "g_mem_post": row(g_mem_post[l]),
        }

        qp, kp, vp, kpb, vpb, up = _proj(y_p.reshape(b * t, d), row(g_mix_pre[l]), w_in_bf, aw, 1024)
        qs, ks, vs, _, _, us = _proj(y_s, row(g_mix_pre[l]), w_in_bf, aw, 512)

        qh = qs.reshape(db, dt_, n_heads, HEAD_W).transpose(0, 2, 1, 3)
        zq = jnp.zeros((), BF16)
        q32 = jnp.stack([jnp.where(branch_lane == 0, qh, zq), jnp.where(branch_lane == 1, qh, zq)], axis=1)
        q32 = q32.reshape(db, nrow, HEAD_W)
        zq32 = jnp.zeros_like(q32)
        q2 = jnp.concatenate([jnp.concatenate([q32, zq32], axis=2), jnp.concatenate([zq32, q32], axis=2)], axis=1)
        att_p, att_s16 = _attn(jnp.swapaxes(qp.reshape(b, t, aw), 1, 2), kpb.reshape(b, t, aw),
                               jnp.swapaxes(vpb.reshape(b, t, aw), 1, 2),
                               slope_parts, lamv, g_subln[l].reshape(HEAD_W, 1).astype(F32),
                               page_table, q2, ks.reshape(db, dt_ * n_heads, HEAD_W),
                               vs.reshape(db, dt_ * n_heads, HEAD_W),
                               cache_k[l].reshape(n_phys * PAGE * n_heads, HEAD_W),
                               cache_v[l].reshape(n_phys * PAGE * n_heads, HEAD_W),
                               gsub, slope2, base2, newmask, lam_init, 512)
        att_s = bf(att_s16.reshape(db, n_heads, dt_, HEAD_W).transpose(0, 2, 1, 3).reshape(db * dt_, aw))

        tabs = _s5_prep(*_s5_param_layout(ssm_lam_re[l], ssm_lam_im[l], ssm_log_dt[l], ssm_b_re[l],
                                          ssm_b_im[l], ssm_c_re[l], ssm_c_im[l], ssm_d[l]))
        syp, stp = _s5_prompt(up.reshape(b, t // S5_BLOCK, S5_BLOCK * sw), *tabs, 512)
        rp, ip = _state_from_quarters(stp.reshape(b, S5_QUARTERS, -1), g, p)
        sys_, sts = _s5_sample(us, _state_to_quarters(state_ssm_re[l], state_ssm_im[l]), *tabs)
        rs, is_ = _state_from_quarters(sts.transpose(1, 0, 2), g, p)

        mkp, mvp, mkb, mvb = _mem_kv(mem_prompt.reshape(b * n_mem, d), row(g_mem_kv[l]),
                                     bf(w_mk[l]), bf(w_mv[l]), 512)
        x1s, qms = _mix_q(y_s, att_s, sys_, w, n_mem_heads)
        mhd = d // n_mem_heads
        qmh = qms.reshape(db, dt_, n_mem_heads, mhd).transpose(0, 2, 1, 3).reshape(db, n_mem_heads * dt_, mhd)
        y_p, omh = _mix_mem(y_p, att_p, syp, mkb.reshape(b, n_mem, d), mvb.reshape(b, n_mem, d), w,
                            qmh, cache_mem_k[l], cache_mem_v[l], mem_mask, 512, n_mem_heads)
        om = bf(omh.reshape(db, n_mem_heads, dt_, mhd).transpose(0, 2, 1, 3).reshape(db * dt_, d))
        y_s = _mem_out(x1s, om, w["w_mo"], w["g_mem_post"])

        ffw = (row(g_ffn_pre[l]), bf(w_gate[l]), bf(w_up[l]), bf(w_down[l]), row(g_ffn_post[l]))
        y_p = _ffn(y_p.reshape(b * t, d), *ffw, 1024, 11).reshape(b, t, d)
        y_s = _ffn(y_s, *ffw, 512, 2)

        outs["kp"].append(kp.reshape(b, t, n_heads, HEAD_W))
        outs["vp"].append(vp.reshape(b, t, n_heads, HEAD_W))
        outs["ks"].append(ks.reshape(db, dt_, n_heads, HEAD_W))
        outs["vs"].append(vs.reshape(db, dt_, n_heads, HEAD_W))
        outs["rp"].append(rp); outs["ip"].append(ip); outs["rs"].append(rs); outs["is"].append(is_)
        outs["mkp"].append(mkp.reshape(b, n_mem, n_mem_heads, d // n_mem_heads))
        outs["mvp"].append(mvp.reshape(b, n_mem, n_mem_heads, d // n_mem_heads))

    st = lambda k: jnp.stack(outs[k], 0)
    return (y_p, y_s.reshape(db, dt_, d), st("kp"), st("vp"), st("ks"), st("vs"),
            st("rp"), st("ip"), st("rs"), st("is"), st("mkp"), st("mvp"))
```

```python
import functools
import math

import jax
import jax.numpy as jnp
from jax import lax
from jax.experimental import pallas as pl
from jax.experimental.pallas import tpu as pltpu

F32 = jnp.float32
BF16 = jnp.bfloat16

RMS_EPS = 1e-6
HEAD_DIM = 64
HEAD_W = 2 * HEAD_DIM
SSM_GROUP = 16
SSM_STATE = 64
S5_BLOCK = 4
S5_QUARTERS = 4
PAGE = 128
LANES = 128
BF16_ROWS = 16
DECODE_GROUPS = 4
NEG = -1e30
LOG2E = 1.4426950408889634
VMEM_LIMIT_BYTES = 56 * 1024 * 1024


def _cparams(n_grid_dims):
    return pltpu.CompilerParams(
        dimension_semantics=("arbitrary",) * n_grid_dims,
        vmem_limit_bytes=VMEM_LIMIT_BYTES)


def _const_spec(shape):
    nd = len(shape)
    return pl.BlockSpec(shape, lambda *_: (0,) * nd, pipeline_mode=pl.Buffered(1))


def _rms(x, g):
    ms = jnp.mean(x * x, axis=-1, keepdims=True)
    return x * lax.rsqrt(ms + RMS_EPS) * g


def _dot(a, b):
    return jnp.dot(a, b, preferred_element_type=F32)


def _dot_nt(a, b, precision=None):
    return lax.dot_general(a, b, (((1,), (1,)), ((), ())),
                           preferred_element_type=F32, precision=precision)


def _proj_kernel(x_ref, g_ref, w_ref, q_ref, k_ref, v_ref, kb_ref, vb_ref, u_ref, u_scr, *, aw, tm):
    hn = _rms(x_ref[...], g_ref[...]).astype(BF16)
    q = _dot(hn, w_ref[:, 0:aw])
    q_ref[...] = (q * (HEAD_DIM ** -0.5 * LOG2E)).astype(BF16)
    nh = aw // HEAD_W
    k = _dot(hn, w_ref[:, aw:2 * aw])
    kb_ref[...] = k.astype(BF16)
    v = _dot(hn, w_ref[:, 2 * aw:3 * aw])
    vb_ref[...] = v.astype(BF16)
    for h in range(nh):
        k_ref[pl.ds(h, tm, stride=nh), :] = k[:, h * HEAD_W:(h + 1) * HEAD_W]
        v_ref[pl.ds(h, tm, stride=nh), :] = v[:, h * HEAD_W:(h + 1) * HEAD_W]
    u = _dot(hn, w_ref[:, 3 * aw:])
    n_lb = u_scr.shape[0]
    sw = n_lb * LANES
    for c in range(n_lb):
        u_scr[c] = u[:, c * LANES:(c + 1) * LANES]
    for j in range(S5_BLOCK):
        for c in range(n_lb):
            rows = u_scr[c, pl.ds(j, tm // S5_BLOCK, stride=S5_BLOCK), :]
            u_ref[:, j * sw + c * LANES:j * sw + (c + 1) * LANES] = rows.astype(BF16)


def _proj(x, g, w_bf, aw, tm):
    n, d = x.shape
    sw = w_bf.shape[1] - 3 * aw
    nh = aw // HEAD_W
    row = lambda width: pl.BlockSpec((tm, width), lambda i: (i, 0))
    flat = pl.BlockSpec((tm * nh, HEAD_W), lambda i: (i, 0))
    return pl.pallas_call(
        functools.partial(_proj_kernel, aw=aw, tm=tm),
        grid=(n // tm,),
        in_specs=[row(d), _const_spec((1, d)), _const_spec(w_bf.shape)],
        out_specs=[row(aw), flat, flat, row(aw), row(aw),
                   pl.BlockSpec((tm // S5_BLOCK, S5_BLOCK * sw), lambda i: (i, 0))],
        out_shape=[
            jax.ShapeDtypeStruct((n, aw), BF16),
            jax.ShapeDtypeStruct((n * nh, HEAD_W), F32),
            jax.ShapeDtypeStruct((n * nh, HEAD_W), F32),
            jax.ShapeDtypeStruct((n, aw), BF16),
            jax.ShapeDtypeStruct((n, aw), BF16),
            jax.ShapeDtypeStruct((n // S5_BLOCK, S5_BLOCK * sw), BF16),
        ],
        scratch_shapes=[pltpu.VMEM((sw // LANES, tm, LANES), F32)],
        compiler_params=_cparams(1),
        name="proj",
    )(x, g, w_bf)


def _diff_lambda(lamv, lam_init):
    d1 = jnp.sum(lamv[0:1] * lamv[1:2], axis=-1, keepdims=True)
    d2 = jnp.sum(lamv[2:3] * lamv[3:4], axis=-1, keepdims=True)
    return jnp.exp(d1) - jnp.exp(d2) + lam_init


def _attn_prompt_body(sl_ref, lamv_ref, g_ref, pos_ref, qt_ref, k_ref, vt_ref, o_ref,
                        qq_ref, m_ref, acc_ref, t0_ref, t1_ref, *, tq, lam_init):
    hp = pl.program_id(1)
    qi = pl.program_id(2)
    tk = tq
    n_hh = qq_ref.shape[0]
    heads = [hp * n_hh + hh for hh in range(n_hh)]
    hrows = [slice(hh * HEAD_W, (hh + 1) * HEAD_W) for hh in range(n_hh)]

    def prep_queries(qblk):
        q0 = pl.multiple_of(qblk * tq, tq)
        arow = lax.broadcasted_iota(jnp.int32, (HEAD_W, 2 * tq), 0)
        for hh in range(n_hh):
            qt = qt_ref[hrows[hh], pl.ds(q0, tq)].astype(F32)
            frow = lax.broadcasted_iota(jnp.int32, qt.shape, 0)
            qq_ref[hh, 0:HEAD_W, 0:tq] = jnp.where(frow < HEAD_DIM, qt, 0.0).astype(BF16)
            qq_ref[hh, 0:HEAD_W, tq:2 * tq] = jnp.where(frow >= HEAD_DIM, qt, 0.0).astype(BF16)
            slope_parts = jnp.where((arow & 1) == 0, sl_ref[1, heads[hh]], sl_ref[2, heads[hh]])
            qq_ref[hh, HEAD_W:2 * HEAD_W, :] = jnp.where(arow < 4, slope_parts, 0.0).astype(BF16)

    m_ref[...] = jnp.full(m_ref.shape, NEG, F32)
    acc_ref[...] = jnp.zeros(acc_ref.shape, F32)
    ones = jnp.ones((acc_ref.shape[1] - HEAD_W, tk), BF16)

    def scores(hh, kj):
        k0 = pl.multiple_of(kj * tk, tk)
        ka = jnp.concatenate([k_ref[pl.ds(k0, tk), hrows[hh]], pos_ref[...]], axis=1)
        return _dot(ka, qq_ref[hh])

    def consume(hh, kj, t):
        k0 = pl.multiple_of(kj * tk, tk)
        vta = jnp.concatenate([vt_ref[hrows[hh], pl.ds(k0, tk)], ones], axis=0)
        off = sl_ref[0, heads[hh]] * jnp.full((1, 2 * tq), (kj - qi) * tk, jnp.int32).astype(F32)
        m_old = m_ref[hh]
        m_new = jnp.maximum(m_old, jnp.max(t, axis=0, keepdims=True) + off)
        alpha = jnp.exp2(m_old - m_new)
        p = jnp.exp2(t - (m_new - off)).astype(BF16)
        acc_ref[hh] = alpha * acc_ref[hh] + _dot(vta, p)
        m_ref[hh] = m_new

    tbuf = (t0_ref, t1_ref)

    def step(kj, cur):
        for hh in range(n_hh):
            tbuf[1 - cur][hh] = scores(hh, kj + 1)
        for hh in range(n_hh):
            consume(hh, kj, tbuf[cur][hh])

    @pl.when(qi == 0)
    def _():
        prep_queries(0)
        for hh in range(n_hh):
            t0_ref[hh] = scores(hh, 0)

    def body(i, carry):
        step(2 * i, 0)
        step(2 * i + 1, 1)
        return carry

    lax.fori_loop(0, qi // 2, body, 0)
    odd = qi % 2 == 1

    @pl.when(odd)
    def _():
        step(qi - 1, 0)

    def finish(cur):
        krow = lax.broadcasted_iota(jnp.int32, (tk, 2 * tq), 0)
        qcol = lax.broadcasted_iota(jnp.int32, (tk, 2 * tq), 1)
        qcol = jnp.where(qcol >= tq, qcol - tq, qcol)
        lam = _diff_lambda(lamv_ref[...], lam_init)
        for hh in range(n_hh):
            consume(hh, qi, jnp.where(krow <= qcol, tbuf[cur][hh], NEG))
        prep_queries(jnp.minimum(qi + 1, pl.num_programs(2) - 1))
        for hh in range(n_hh):
            t0_ref[hh] = scores(hh, 0)
        for hh in range(n_hh):
            acc = acc_ref[hh]
            ot = acc[0:HEAD_W] / acc[HEAD_W:HEAD_W + 1]
            d = ot[:, 0:tq] - lam * ot[:, tq:2 * tq]
            ms = jnp.mean(d * d, axis=0, keepdims=True)
            dn = d * lax.rsqrt(ms + RMS_EPS) * g_ref[...] * (1.0 - lam_init)
            o_ref[:, hrows[hh]] = dn.T.astype(BF16)

    @pl.when(odd)
    def _():
        finish(1)

    @pl.when(jnp.logical_not(odd))
    def _():
        finish(0)


def _page_copy(hbm, buf, sem, kind, sl, p, page, page_rows):
    src = hbm.at[pl.ds(pl.multiple_of(page * page_rows, page_rows), page_rows)]
    return pltpu.make_async_copy(src, buf.at[sl, pl.ds(p * page_rows, page_rows)], sem.at[kind, sl])


def _attn_sample_prefetch(pt_ref, ck_hbm, cv_hbm, kbuf, vbuf, sem, seq, n_seq, *, n_pages, page_rows):
    def start_fetch(s, sl):
        for p in range(n_pages):
            page = pt_ref[s * n_pages + p]
            _page_copy(ck_hbm, kbuf, sem, 0, sl, p, page, page_rows).start(priority=0)
            _page_copy(cv_hbm, vbuf, sem, 1, sl, p, page, page_rows).start(priority=1)

    @pl.when(seq == 0)
    def _():
        start_fetch(0, 0)

    @pl.when(seq + 1 < n_seq)
    def _():
        start_fetch(seq + 1, 1 - seq % 2)


def _attn_sample_compute(lamv_ref, g_ref, slope2_ref, base2_ref, newmask_ref, q2_ref, kn_ref, vn_ref,
                         ck_hbm, cv_hbm, o_ref, kbuf, vbuf, sem, s_ref, seq,
                         *, n_pages, page_rows, n_heads, lam_init, chunk):
    slot = seq % 2
    past_rows = n_pages * page_rows
    nrow = q2_ref.shape[0] // 2

    for p in range(n_pages):
        _page_copy(ck_hbm, kbuf, sem, 0, slot, p, 0, page_rows).wait()
        _page_copy(cv_hbm, vbuf, sem, 1, slot, p, 0, page_rows).wait()

    pad = jnp.zeros((chunk - kn_ref.shape[0], HEAD_W), F32)
    kbuf[slot, past_rows:past_rows + chunk, :] = jnp.concatenate([kn_ref[...], pad], axis=0)
    vbuf[slot, past_rows:past_rows + chunk, :] = jnp.concatenate([vn_ref[...], pad], axis=0)

    q2 = q2_ref[...]
    slope2 = slope2_ref[...]
    n_pairs = past_rows // (2 * chunk)

    def pair_tile(buf, a):
        r0 = 2 * a * chunk
        return jnp.concatenate([buf[slot, r0:r0 + chunk, :].astype(BF16),
                                buf[slot, r0 + chunk:r0 + 2 * chunk, :].astype(BF16)], axis=1)

    def score_pair(a):
        tok0 = float(2 * a * (chunk // n_heads) - past_rows // n_heads)
        s_ref[:, a * chunk:(a + 1) * chunk] = _dot_nt(q2, pair_tile(kbuf, a)) + base2_ref[...] + slope2 * tok0

    def fold(acc):
        return acc[0:nrow, 0:HEAD_W] + acc[nrow:2 * nrow, HEAD_W:2 * HEAD_W]

    def softmax_half(lo, hi, extra=None):
        s = s_ref[:, lo * chunk:hi * chunk]
        mr = jnp.max(s, axis=-1, keepdims=True)
        m = jnp.maximum(mr[0:nrow], mr[nrow:2 * nrow])
        if extra is not None:
            m = jnp.maximum(m, jnp.max(extra, axis=-1, keepdims=True))
        p = jnp.exp2(s - jnp.concatenate([m, m], axis=0))
        lr = jnp.sum(p, axis=-1, keepdims=True)
        return m, lr[0:nrow] + lr[nrow:2 * nrow], p.astype(BF16)

    gp = n_pairs // DECODE_GROUPS
    q = q2[0:nrow, 0:HEAD_W]
    kn = kbuf[slot, past_rows:past_rows + chunk, :].astype(BF16)
    vn = vbuf[slot, past_rows:past_rows + chunk, :].astype(BF16)
    for a in range(gp):
        score_pair(a)
    parts = []
    for g in range(DECODE_GROUPS):
        last = g == DECODE_GROUPS - 1
        s_new = _dot_nt(q, kn) + newmask_ref[...] if last else None
        m_g, l_g, p_g = softmax_half(g * gp, (g + 1) * gp, extra=s_new)
        acc = jnp.zeros((2 * nrow, 2 * HEAD_W), F32)
        for a in range(gp):
            if not last:
                score_pair((g + 1) * gp + a)
            acc = acc + _dot(p_g[:, a * chunk:(a + 1) * chunk], pair_tile(vbuf, g * gp + a))
        o_g = fold(acc)
        if last:
            p_new = jnp.exp2(s_new - m_g)
            l_g = l_g + jnp.sum(p_new, axis=-1, keepdims=True)
            o_g = o_g + _dot(p_new.astype(BF16), vn)
        parts.append((m_g, l_g, o_g))
    m = functools.reduce(jnp.maximum, [pt[0] for pt in parts])
    ws = [jnp.exp2(pt[0] - m) for pt in parts]
    o = sum(pt[2] * w for pt, w in zip(parts, ws)) / sum(pt[1] * w for pt, w in zip(parts, ws))
    lam = _diff_lambda(lamv_ref[...], lam_init)
    half = nrow // 2
    d = o[0:half] - lam * o[half:nrow]
    o_ref[...] = _rms(d, g_ref[...]) * (1.0 - lam_init)


def _attn_kernel(pt_ref, sl_ref, lamv_ref, gcol_ref, pos_ref, qt_ref, k_ref, vt_ref,
                 grow_ref, slope2_ref, base2_ref, newmask_ref, q2_ref, kn_ref, vn_ref, ck_hbm, cv_hbm,
                 op_ref, os_ref,
                 qq_ref, m_ref, acc_ref, t0_ref, t1_ref, kbuf, vbuf, sem, s_ref,
                 *, tq, lam_init, n_pages, page_rows, n_heads, chunk):
    seq = (pl.program_id(0) * pl.num_programs(1) + pl.program_id(1)) * pl.num_programs(2) + pl.program_id(2)
    n_seq = pl.num_programs(0) * pl.num_programs(1) * pl.num_programs(2)
    _attn_sample_prefetch(pt_ref, ck_hbm, cv_hbm, kbuf, vbuf, sem, seq, n_seq,
                          n_pages=n_pages, page_rows=page_rows)
    _attn_prompt_body(sl_ref, lamv_ref, gcol_ref, pos_ref, qt_ref, k_ref, vt_ref, op_ref,
                      qq_ref, m_ref, acc_ref, t0_ref, t1_ref, tq=tq, lam_init=lam_init)
    _attn_sample_compute(lamv_ref, grow_ref, slope2_ref, base2_ref, newmask_ref, q2_ref, kn_ref, vn_ref,
                         ck_hbm, cv_hbm, os_ref, kbuf, vbuf, sem, s_ref, seq,
                         n_pages=n_pages, page_rows=page_rows, n_heads=n_heads, lam_init=lam_init, chunk=chunk)


def _attn(qt, kb, vt, slope_parts, lamv, g_col, page_table, q2, knf, vnf, ckf, cvf, g_row, slope2, base2,
          newmask, lam_init, tq):
    b, aw, t = qt.shape
    nh = aw // HEAD_W
    nq = t // tq
    db, nrow2, _ = q2.shape
    nrow = nrow2 // 2
    n_pages = page_table.shape[1]
    chunk = base2.shape[1]
    page_rows = PAGE * nh
    past_rows = n_pages * page_rows
    assert past_rows % (2 * chunk) == 0 and knf.shape[1] <= chunk
    assert db == b * nh * nq, "one decode sequence per prompt grid step"
    i = jnp.arange(tq)
    lane = jnp.arange(HEAD_W)
    feat = jnp.where(lane[None, :] < 2, (i % 256)[:, None], jnp.where(lane[None, :] < 4, (i // 256 * 256)[:, None], 0))
    pos = feat.astype(BF16)
    seq_of = lambda bi, hi, qi: (bi * nh + hi) * nq + qi
    per_seq = lambda a: pl.BlockSpec((None,) + a.shape[1:], lambda bi, hi, qi, pt: (seq_of(bi, hi, qi), 0, 0))
    const = lambda a: pl.BlockSpec(a.shape, lambda bi, hi, qi, pt: (0,) * a.ndim, pipeline_mode=pl.Buffered(1))
    grid_spec = pltpu.PrefetchScalarGridSpec(
        num_scalar_prefetch=1,
        grid=(b, nh, nq),
        in_specs=[
            pl.BlockSpec(memory_space=pltpu.SMEM),
            const(lamv), const(g_col), const(pos),
            pl.BlockSpec((None, HEAD_W, t), lambda bi, hi, qi, pt: (bi, hi, 0)),
            pl.BlockSpec((None, t, HEAD_W), lambda bi, hi, qi, pt: (bi, 0, hi)),
            pl.BlockSpec((None, HEAD_W, t), lambda bi, hi, qi, pt: (bi, hi, 0)),
            const(g_row), const(slope2), const(base2), const(newmask),
            per_seq(q2), per_seq(knf), per_seq(vnf),
            pl.BlockSpec(memory_space=pl.ANY),
            pl.BlockSpec(memory_space=pl.ANY),
        ],
        out_specs=[
            pl.BlockSpec((None, tq, HEAD_W), lambda bi, hi, qi, pt: (bi, qi, hi)),
            pl.BlockSpec((None, nrow // 2, HEAD_W), lambda bi, hi, qi, pt: (seq_of(bi, hi, qi), 0, 0)),
        ],
        scratch_shapes=[
            pltpu.VMEM((1, 2 * HEAD_W, 2 * tq), BF16),
            pltpu.VMEM((1, 1, 2 * tq), F32),
            pltpu.VMEM((1, HEAD_W + BF16_ROWS, 2 * tq), F32),
            pltpu.VMEM((1, tq, 2 * tq), F32),
            pltpu.VMEM((1, tq, 2 * tq), F32),
            pltpu.VMEM((2, past_rows + chunk, HEAD_W), F32),
            pltpu.VMEM((2, past_rows + chunk, HEAD_W), F32),
            pltpu.SemaphoreType.DMA((2, 2)),
            pltpu.VMEM((nrow2, past_rows // 2), F32),
        ],
    )
    return pl.pallas_call(
        functools.partial(_attn_kernel, tq=tq, lam_init=lam_init, n_pages=n_pages, page_rows=page_rows,
                          n_heads=nh, chunk=chunk),
        grid_spec=grid_spec,
        out_shape=[jax.ShapeDtypeStruct((b, t, aw), BF16),
                   jax.ShapeDtypeStruct((db, nrow // 2, HEAD_W), F32)],
        compiler_params=_cparams(3),
        name="attn",
    )(page_table.reshape(-1), slope_parts, lamv, g_col, pos, qt, kb, vt,
      g_row, slope2, base2, newmask, q2, knf, vnf, ckf, cvf)


def _s5_prep_kernel(lre_ref, lim_ref, ldt_ref, bre_ref, bim_ref, cre_ref, cim_ref, dsk_ref,
                    wst_ref, cat_ref, kt_ref, sct_ref):
    lr = lre_ref[...]
    li = lim_ref[...]
    dt = jnp.exp(ldt_ref[...])

    def lam_bar_pow(n):
        mag = jnp.exp(n * (lr * dt))
        ang = n * (li * dt)
        return mag * jnp.cos(ang), mag * jnp.sin(ang)

    a_r, a_i = lam_bar_pow(1.0)
    den = lr * lr + li * li
    nr, ni = a_r - 1.0, a_i
    f_r = (nr * lr + ni * li) / den
    f_i = (ni * lr - nr * li) / den
    b_r, b_i = bre_ref[...], bim_ref[...]
    bb_r = f_r * b_r - f_i * b_i
    bb_i = f_r * b_i + f_i * b_r
    c_r, c_i = cre_ref[...], cim_ref[...]
    gw = b_r.shape[0]
    ns = b_r.shape[1]

    ktau = []
    for tau in range(S5_BLOCK):
        p_r, p_i = lam_bar_pow(float(tau))
        w_r = bb_r * p_r - bb_i * p_i
        w_i = bb_r * p_i + bb_i * p_r
        j = S5_BLOCK - 1 - tau
        wst_ref[j * gw:(j + 1) * gw, 0:ns] = w_r.astype(BF16)
        wst_ref[j * gw:(j + 1) * gw, ns:2 * ns] = w_i.astype(BF16)
        ktau.append(_dot_nt(w_r, c_r, lax.Precision.HIGHEST) - _dot_nt(w_i, c_i, lax.Precision.HIGHEST))
    rr = lax.broadcasted_iota(jnp.int32, (gw, gw), 0)
    cc = lax.broadcasted_iota(jnp.int32, (gw, gw), 1)
    ktau[0] = ktau[0] + jnp.where(rr == cc, dsk_ref[...], 0.0)

    for j in range(S5_BLOCK):
        p_r, p_i = lam_bar_pow(float(j + 1))
        cat_ref[j * gw:(j + 1) * gw, 0:ns] = (c_r * p_r - c_i * p_i).astype(BF16)
        cat_ref[j * gw:(j + 1) * gw, ns:2 * ns] = (-(c_r * p_i + c_i * p_r)).astype(BF16)

    zeros = jnp.zeros((gw, gw), BF16)
    for i in range(S5_BLOCK):
        for j in range(S5_BLOCK):
            blk = ktau[j - i].astype(BF16) if j >= i else zeros
            kt_ref[i * gw:(i + 1) * gw, j * gw:(j + 1) * gw] = blk

    row = lax.broadcasted_iota(jnp.int32, (8, ns), 0)
    for idx, k in enumerate((1, 2, 4)):
        p_r, p_i = lam_bar_pow(float(S5_BLOCK * k))
        sct_ref[2 * idx] = jnp.where(row >= k, p_r, 0.0)
        sct_ref[2 * idx + 1] = jnp.where(row >= k, p_i, 0.0)
    n = (S5_BLOCK * (row + 1)).astype(F32)
    mag = jnp.exp(n * (lr * dt))
    ang = n * (li * dt)
    sct_ref[6] = mag * jnp.cos(ang)
    sct_ref[7] = mag * jnp.sin(ang)


def _s5_prep(lre, lim, ldt, bre, bim, cre, cim, dsk):
    nq, gw, ns = bre.shape
    kw = S5_BLOCK * gw
    per_q = lambda *tail: pl.BlockSpec((None,) + tail, lambda q: (q,) + (0,) * len(tail))
    return pl.pallas_call(
        _s5_prep_kernel,
        grid=(nq,),
        in_specs=[per_q(1, ns), per_q(1, ns), per_q(1, ns), per_q(gw, ns), per_q(gw, ns),
                  per_q(gw, ns), per_q(gw, ns), per_q(1, gw)],
        out_specs=[per_q(kw, 2 * ns), per_q(kw, 2 * ns), per_q(kw, kw), per_q(8, 8, ns)],
        out_shape=[
            jax.ShapeDtypeStruct((nq, kw, 2 * ns), BF16),
            jax.ShapeDtypeStruct((nq, kw, 2 * ns), BF16),
            jax.ShapeDtypeStruct((nq, kw, kw), BF16),
            jax.ShapeDtypeStruct((nq, 8, 8, ns), F32),
        ],
        compiler_params=_cparams(1),
        name="s5_prep",
    )(lre, lim, ldt, bre, bim, cre, cim, dsk)


def _quarter_lanes(u_ref, q, gw, sw):
    return jnp.concatenate(
        [u_ref[:, j * sw + q * gw: j * sw + (q + 1) * gw] for j in range(S5_BLOCK)], axis=1)


def _store_quarter(y_ref, y, q, gw, sw):
    for j in range(S5_BLOCK):
        y_ref[:, j * sw + q * gw: j * sw + (q + 1) * gw] = y[:, j * gw:(j + 1) * gw].astype(y_ref.dtype)


def _s5_prompt_kernel(u_ref, wst_ref, cat_ref, kt_ref, sct_ref, y_ref, st_ref,
                      carry_ref, s_scr, hp_scr, *, rb, gw, sw, ns):
    blk = pl.program_id(1)

    @pl.when(blk == 0)
    def _():
        carry_ref[...] = jnp.zeros(carry_ref.shape, F32)

    row0 = lax.broadcasted_iota(jnp.int32, (8, ns), 0) == 0
    for q in range(S5_QUARTERS):
        uq = _quarter_lanes(u_ref, q, gw, sw)
        s_scr[...] = _dot(uq, wst_ref[q])
        tabs = [sct_ref[q, i] for i in range(8)]

        def tile(t, carry, tabs=tabs):
            cr, ci = carry
            r0 = pl.multiple_of(t * 8, 8)
            xr = s_scr[pl.ds(r0, 8), 0:ns]
            xi = s_scr[pl.ds(r0, 8), ns:2 * ns]
            for idx, k in enumerate((1, 2, 4)):
                mr, mi = tabs[2 * idx], tabs[2 * idx + 1]
                sr = pltpu.roll(xr, k, 0)
                si = pltpu.roll(xi, k, 0)
                xr, xi = xr + (mr * sr - mi * si), xi + (mr * si + mi * sr)
            crb = jnp.broadcast_to(cr, (8, ns))
            cib = jnp.broadcast_to(ci, (8, ns))
            hr = xr + (tabs[6] * crb - tabs[7] * cib)
            hi = xi + (tabs[6] * cib + tabs[7] * crb)
            hp_scr[pl.ds(r0, 8), 0:ns] = jnp.where(row0, crb, pltpu.roll(hr, 1, 0))
            hp_scr[pl.ds(r0, 8), ns:2 * ns] = jnp.where(row0, cib, pltpu.roll(hi, 1, 0))
            return hr[7:8, :], hi[7:8, :]

        cr, ci = lax.fori_loop(0, rb // 8, tile,
                               (carry_ref[q, :, 0:ns], carry_ref[q, :, ns:2 * ns]), unroll=2)
        carry_ref[q, :, 0:ns] = cr
        carry_ref[q, :, ns:2 * ns] = ci
        st_ref[q, :, 0:ns] = cr
        st_ref[q, :, ns:2 * ns] = ci
        y = _dot_nt(hp_scr[...].astype(BF16), cat_ref[q]) + _dot(uq, kt_ref[q])
        _store_quarter(y_ref, y, q, gw, sw)


def _s5_prompt(u4, wst, cat, kt, sct, rb):
    b, rows, lanes = u4.shape
    nq, kw, ns2 = wst.shape
    ns, gw, sw = ns2 // 2, kw // S5_BLOCK, lanes // S5_BLOCK
    return pl.pallas_call(
        functools.partial(_s5_prompt_kernel, rb=rb, gw=gw, sw=sw, ns=ns),
        grid=(b, rows // rb),
        in_specs=[
            pl.BlockSpec((None, rb, lanes), lambda bi, ri: (bi, ri, 0)),
            _const_spec(wst.shape), _const_spec(cat.shape), _const_spec(kt.shape), _const_spec(sct.shape),
        ],
        out_specs=[
            pl.BlockSpec((None, rb, lanes), lambda bi, ri: (bi, ri, 0)),
            pl.BlockSpec((None, nq, 1, ns2), lambda bi, ri: (bi, 0, 0, 0)),
        ],
        out_shape=[
            jax.ShapeDtypeStruct((b, rows, lanes), BF16),
            jax.ShapeDtypeStruct((b, nq, 1, ns2), F32),
        ],
        scratch_shapes=[
            pltpu.VMEM((nq, 1, ns2), F32),
            pltpu.VMEM((rb, ns2), F32),
            pltpu.VMEM((rb, ns2), F32),
        ],
        compiler_params=_cparams(2),
        name="s5_prompt",
    )(u4, wst, cat, kt, sct)


def _s5_sample_kernel(u_ref, h0_ref, wst_ref, cat_ref, kt_ref, sct_ref, y_ref, hout_ref, *, gw, sw, ns):
    for q in range(S5_QUARTERS):
        uq = _quarter_lanes(u_ref, q, gw, sw)
        s = _dot(uq, wst_ref[q])
        h0 = h0_ref[q]
        h0r, h0i = h0[:, 0:ns], h0[:, ns:2 * ns]
        a4r = sct_ref[q, 6][0:1, :]
        a4i = sct_ref[q, 7][0:1, :]
        hout_ref[q, :, 0:ns] = s[:, 0:ns] + (a4r * h0r - a4i * h0i)
        hout_ref[q, :, ns:2 * ns] = s[:, ns:2 * ns] + (a4r * h0i + a4i * h0r)
        y = _dot_nt(h0.astype(BF16), cat_ref[q]) + _dot(uq, kt_ref[q])
        _store_quarter(y_ref, y, q, gw, sw)


def _s5_sample(u4, h0q, wst, cat, kt, sct):
    rows, lanes = u4.shape
    nq, kw, ns2 = wst.shape
    ns, gw, sw = ns2 // 2, kw // S5_BLOCK, lanes // S5_BLOCK
    return pl.pallas_call(
        functools.partial(_s5_sample_kernel, gw=gw, sw=sw, ns=ns),
        grid=(1,),
        in_specs=[_const_spec(u4.shape), _const_spec(h0q.shape), _const_spec(wst.shape),
                  _const_spec(cat.shape), _const_spec(kt.shape), _const_spec(sct.shape)],
        out_specs=[pl.BlockSpec(u4.shape, lambda i: (0, 0)),
                   pl.BlockSpec(h0q.shape, lambda i: (0, 0, 0))],
        out_shape=[jax.ShapeDtypeStruct(u4.shape, BF16), jax.ShapeDtypeStruct(h0q.shape, F32)],
        compiler_params=_cparams(1),
        name="s5_sample",
    )(u4, h0q, wst, cat, kt, sct)


def _mem_kv_kernel(x_ref, g_ref, wk_ref, wv_ref, k_ref, v_ref, kb_ref, vb_ref):
    hn = _rms(x_ref[...], g_ref[...]).astype(BF16)
    k = _dot(hn, wk_ref[...])
    k_ref[...] = k
    kb_ref[...] = k.astype(BF16)
    v = _dot(hn, wv_ref[...])
    v_ref[...] = v
    vb_ref[...] = v.astype(BF16)


def _mem_kv(mem, g, wk_bf, wv_bf, tm):
    n, d = mem.shape
    row = pl.BlockSpec((tm, d), lambda i: (i, 0))
    return pl.pallas_call(
        _mem_kv_kernel,
        grid=(n // tm,),
        in_specs=[row, _const_spec((1, d)), _const_spec(wk_bf.shape), _const_spec(wv_bf.shape)],
        out_specs=[row, row, row, row],
        out_shape=[jax.ShapeDtypeStruct((n, d), F32), jax.ShapeDtypeStruct((n, d), F32),
                   jax.ShapeDtypeStruct((n, d), BF16), jax.ShapeDtypeStruct((n, d), BF16)],
        compiler_params=_cparams(1),
        name="mem_kv",
    )(mem, g, wk_bf, wv_bf)


def _merge_and_query(x, att, sy4_ref, sy_scr, wglu_ref, gso_ref, wout_ref, gpost_ref, gmpre_ref, wmq_ref,
                     mem_scale):
    aw = att.shape[1]
    rows, n_lb = sy4_ref.shape[0], sy_scr.shape[0]
    sw = n_lb * LANES
    for j in range(S5_BLOCK):
        for c in range(n_lb):
            blk = sy4_ref[:, j * sw + c * LANES:j * sw + (c + 1) * LANES].astype(F32)
            sy_scr[c, pl.ds(j, rows, stride=S5_BLOCK), :] = blk
    y = jax.nn.gelu(jnp.concatenate([sy_scr[c] for c in range(n_lb)], axis=1))
    y = y * jax.nn.sigmoid(_dot(y.astype(BF16), wglu_ref[...]))
    y = _rms(y, gso_ref[...])
    mix = _dot(att, wout_ref[0:aw, :]) + _dot(y.astype(BF16), wout_ref[aw:, :])
    x1 = x + _rms(mix, gpost_ref[...])
    hq = _rms(x1, gmpre_ref[...]).astype(BF16)
    qm = (_dot(hq, wmq_ref[...]) * mem_scale).astype(BF16)
    return x1, qm


def _mix_mem_body(x_ref, att_ref, sy_ref, mk_ref, mv_ref, wglu_ref, gso_ref, wout_ref, gpost_ref,
                  gmpre_ref, wmq_ref, wmo_ref, gmpost_ref, o_ref, oh_ref, sy_scr, *, n_heads, mem_scale):
    x1, qm = _merge_and_query(x_ref[...], att_ref[...], sy_ref, sy_scr, wglu_ref, gso_ref, wout_ref,
                              gpost_ref, gmpre_ref, wmq_ref, mem_scale)
    hd = qm.shape[1] // n_heads
    sls = [slice(h * hd, (h + 1) * hd) for h in range(n_heads)]
    ss = [_dot_nt(qm[:, sl], mk_ref[:, sl]) for sl in sls]
    ps = [jnp.exp(s - jnp.max(s, axis=-1, keepdims=True)) for s in ss]
    for sl, p in zip(sls, ps):
        l = jnp.sum(p, axis=-1, keepdims=True)
        oh_ref[:, sl] = (_dot(p.astype(BF16), mv_ref[:, sl]) / l).astype(BF16)
    mo = _dot(oh_ref[...], wmo_ref[...])
    o_ref[...] = x1 + _rms(mo, gmpost_ref[...])


def _mix_q_kernel(x_ref, att_ref, sy_ref, wglu_ref, gso_ref, wout_ref, gpost_ref, gmpre_ref, wmq_ref,
                  x1_ref, qm_ref, sy_scr, *, mem_scale):
    x1, qm = _merge_and_query(x_ref[...], att_ref[...], sy_ref, sy_scr, wglu_ref, gso_ref, wout_ref,
                              gpost_ref, gmpre_ref, wmq_ref, mem_scale)
    x1_ref[...] = x1
    qm_ref[...] = qm


def _mix_q(x, att, sy4, w, n_heads):
    n, d = x.shape
    weights = [w["w_glu"], w["g_ssm_out"], w["w_out"], w["g_mix_post"], w["g_mem_pre"], w["w_mq"]]
    ins = [x, att, sy4] + weights
    return pl.pallas_call(
        functools.partial(_mix_q_kernel, mem_scale=(d // n_heads) ** -0.5),
        grid=(1,),
        in_specs=[_const_spec(a.shape) for a in ins],
        out_specs=[pl.BlockSpec((n, d), lambda i: (0, 0)), pl.BlockSpec((n, d), lambda i: (0, 0))],
        out_shape=[jax.ShapeDtypeStruct((n, d), F32), jax.ShapeDtypeStruct((n, d), BF16)],
        scratch_shapes=[pltpu.VMEM((sy4.shape[1] // S5_BLOCK // LANES, n, LANES), F32)],
        compiler_params=_cparams(1),
        name="mix_q",
    )(*ins)


def _mem_sample_copies(ck_hbm, cv_hbm, kbuf, vbuf, sem, step, n_per, n_mem, n_heads):
    sl = step % 2
    out = []
    for j in range(n_per):
        for kind, (hbm, buf) in enumerate(((ck_hbm, kbuf), (cv_hbm, vbuf))):
            for lb in range(kbuf.shape[2]):
                src = hbm.at[step * n_per + j, :, :, pl.ds(lb * LANES, LANES)]
                dst = buf.at[sl, j, lb].reshape(n_mem, n_heads, LANES)
                out.append(pltpu.make_async_copy(src, dst, sem.at[kind, sl]))
    return out


def _mem_sample_compute(mask_ref, q_ref, o_ref, kbuf, vbuf, slot, js):
    n_halves = kbuf.shape[2]
    cat = lambda buf, j: jnp.concatenate([buf[slot, j, lb] for lb in range(n_halves)], axis=1).astype(BF16)
    ss = [_dot_nt(q_ref[j], cat(kbuf, j)) + mask_ref[...] for j in js]
    ps = [jnp.exp(s - jnp.max(s, axis=-1, keepdims=True)) for s in ss]
    ls = [jnp.sum(p, axis=-1, keepdims=True) for p in ps]
    for j, p, l in zip(js, ps, ls):
        o_ref[j] = _dot(p.astype(BF16), cat(vbuf, j)) / l


def _mix_mem_kernel(x_ref, att_ref, sy_ref, mk_ref, mv_ref, wglu_ref, gso_ref, wout_ref, gpost_ref,
                    gmpre_ref, wmq_ref, wmo_ref, gmpost_ref, mask_ref, qs_ref, ck_hbm, cv_hbm,
                    o_ref, os_ref, oh_ref, sy_scr, kbuf, vbuf, sem, *, n_heads, mem_scale, n_mem):
    step = pl.program_id(0) * pl.num_programs(1) + pl.program_id(1)
    n_steps = pl.num_programs(0) * pl.num_programs(1)
    n_per = qs_ref.shape[0]
    copies = lambda st: _mem_sample_copies(ck_hbm, cv_hbm, kbuf, vbuf, sem, st, n_per, n_mem, n_heads)

    @pl.when(step == 0)
    def _():
        for i, cp in enumerate(copies(0)):
            cp.start(priority=i % 2)

    @pl.when(step + 1 < n_steps)
    def _():
        for i, cp in enumerate(copies(step + 1)):
            cp.start(priority=i % 2)

    _mix_mem_body(x_ref, att_ref, sy_ref, mk_ref, mv_ref, wglu_ref, gso_ref, wout_ref, gpost_ref,
                  gmpre_ref, wmq_ref, wmo_ref, gmpost_ref, o_ref, oh_ref, sy_scr,
                  n_heads=n_heads, mem_scale=mem_scale)

    for cp in copies(step):
        cp.wait()
    _mem_sample_compute(mask_ref, qs_ref, os_ref, kbuf, vbuf, step % 2, range(n_per))


def _mix_mem(x, att, sy4, mkb, mvb, w, qh, ck, cv, mask, tm, n_heads):
    b, t, d = x.shape
    aw, sw = att.shape[2], sy4.shape[2] // S5_BLOCK
    nm = mkb.shape[1]
    nt = t // tm
    db, nrow, hd = qh.shape
    n_ckeys = ck.shape[1]
    n_per = db // (b * nt)
    assert db == n_per * b * nt, "decode sequences must divide evenly over the prompt grid steps"
    tok = lambda width: pl.BlockSpec((None, tm, width), lambda bi, ti: (bi, ti, 0))
    tok4 = pl.BlockSpec((None, tm // S5_BLOCK, S5_BLOCK * sw), lambda bi, ti: (bi, ti, 0))
    per_b = pl.BlockSpec((None, nm, d), lambda bi, ti: (bi, 0, 0))
    per_step = pl.BlockSpec((n_per, nrow, hd), lambda bi, ti: (bi * nt + ti, 0, 0))
    any_spec = pl.BlockSpec(memory_space=pl.ANY)
    weights = [w["w_glu"], w["g_ssm_out"], w["w_out"], w["g_mix_post"], w["g_mem_pre"], w["w_mq"],
               w["w_mo"], w["g_mem_post"], mask]
    return pl.pallas_call(
        functools.partial(_mix_mem_kernel, n_heads=n_heads, mem_scale=(d // n_heads) ** -0.5, n_mem=n_ckeys),
        grid=(b, nt),
        in_specs=[tok(d), tok(aw), tok4, per_b, per_b] + [_const_spec(a.shape) for a in weights]
                 + [per_step, any_spec, any_spec],
        out_specs=[tok(d), per_step],
        out_shape=[jax.ShapeDtypeStruct((b, t, d), F32), jax.ShapeDtypeStruct((db, nrow, hd), F32)],
        scratch_shapes=[
            pltpu.VMEM((tm, d), BF16),
            pltpu.VMEM((sw // LANES, tm, LANES), F32),
            pltpu.VMEM((2, n_per, hd // LANES, n_ckeys * n_heads, LANES), F32),
            pltpu.VMEM((2, n_per, hd // LANES, n_ckeys * n_heads, LANES), F32),
            pltpu.SemaphoreType.DMA((2, 2)),
        ],
        compiler_params=_cparams(2),
        name="mix_mem",
    )(x, att, sy4, mkb, mvb, *weights, qh, ck, cv)


def _mem_out_kernel(x1_ref, o_ref, wmo_ref, g_ref, y_ref):
    mo = _dot(o_ref[...], wmo_ref[...])
    y_ref[...] = x1_ref[...] + _rms(mo, g_ref[...])


def _mem_out(x1, o, wmo, g):
    n, d = x1.shape
    ins = [x1, o, wmo, g]
    return pl.pallas_call(
        _mem_out_kernel,
        grid=(1,),
        in_specs=[_const_spec(a.shape) for a in ins],
        out_specs=pl.BlockSpec((n, d), lambda i: (0, 0)),
        out_shape=jax.ShapeDtypeStruct((n, d), F32),
        compiler_params=_cparams(1),
        name="mem_out",
    )(*ins)


def _ffn_kernel(x_ref, gpre_ref, wg_ref, wu_ref, wd_ref, gpost_ref, o_ref, *, n_chunks):
    x = x_ref[...]
    hn = _rms(x, gpre_ref[...]).astype(BF16)
    ff = wg_ref.shape[1]
    cw = ff // n_chunks
    acc = jnp.zeros(x.shape, F32)
    for c in range(n_chunks):
        sl = slice(c * cw, (c + 1) * cw)
        a = jax.nn.silu(_dot(hn, wg_ref[:, sl])) * _dot(hn, wu_ref[:, sl])
        acc = acc + _dot(a.astype(BF16), wd_ref[sl, :])
    o_ref[...] = x + _rms(acc, gpost_ref[...])


def _ffn(x, gpre, wg, wu, wd, gpost, tm, n_chunks):
    n, d = x.shape
    row = pl.BlockSpec((tm, d), lambda i: (i, 0))
    weights = [gpre, wg, wu, wd, gpost]
    return pl.pallas_call(
        functools.partial(_ffn_kernel, n_chunks=n_chunks),
        grid=(n // tm,),
        in_specs=[row] + [_const_spec(a.shape) for a in weights],
        out_specs=row,
        out_shape=jax.ShapeDtypeStruct((n, d), F32),
        compiler_params=_cparams(1),
        name="ffn",
    )(x, *weights)


def _s5_param_layout(lam_re, lam_im, log_dt, b_re, b_im, c_re, c_im, d_skip):
    g, p, hc = b_re.shape
    nq = S5_QUARTERS
    gq = g // nq
    rowv = lambda a: a.reshape(nq, 1, gq * p)
    ldt = jnp.broadcast_to(log_dt[:, None], (g, p))
    eye = jnp.eye(gq, dtype=F32)

    def embed_b(b):
        bq = b.reshape(nq, gq, p, hc)
        return jnp.einsum("qgph,gk->qghkp", bq, eye).reshape(nq, gq * hc, gq * p)

    def embed_c(c):
        cq = c.reshape(nq, gq, hc, p)
        return jnp.einsum("qghp,gk->qghkp", cq, eye).reshape(nq, gq * hc, gq * p)

    return (rowv(lam_re), rowv(lam_im), rowv(ldt), embed_b(b_re), embed_b(b_im),
            embed_c(c_re), embed_c(c_im), d_skip.reshape(nq, 1, gq * hc))


def _state_to_quarters(re, im):
    n, g, p = re.shape
    gq = g // S5_QUARTERS
    f = lambda a: a.reshape(n, S5_QUARTERS, gq * p).transpose(1, 0, 2)
    return jnp.concatenate([f(re), f(im)], axis=-1)


def _state_from_quarters(st, g, p):
    n = st.shape[0]
    ns = st.shape[-1] // 2
    return st[..., :ns].reshape(n, g, p), st[..., ns:].reshape(n, g, p)


def kernel(x_prompt, x_sample, mem_prompt, cache_k, cache_v, page_table, state_ssm_re, state_ssm_im,
           cache_mem_k, cache_mem_v, g_mix_pre, g_mix_post, w_in, lam_q1, lam_k1, lam_q2, lam_k2, g_subln,
           ssm_lam_re, ssm_lam_im, ssm_log_dt, ssm_b_re, ssm_b_im, ssm_c_re, ssm_c_im, ssm_d, w_glu, g_ssm_out,
           w_out, g_mem_pre, g_mem_post, g_mem_kv, w_mq, w_mk, w_mv, w_mo, g_ffn_pre, g_ffn_post,
           w_gate, w_up, w_down):
    depth = w_in.shape[0]
    b, t, d = x_prompt.shape
    db, dt_, _ = x_sample.shape
    n_mem = mem_prompt.shape[1]
    n_mem_heads = cache_mem_k.shape[3]
    n_heads = cache_k.shape[3]
    aw = n_heads * HEAD_W
    g, p = ssm_lam_re.shape[1], ssm_lam_re.shape[2]
    sw = g * SSM_GROUP
    n_phys = cache_k.shape[1]
    assert dt_ == S5_BLOCK and t % (S5_BLOCK * 256) == 0

    slopes = LOG2E * 2.0 ** (-8.0 * jnp.arange(1, n_heads + 1, dtype=F32) / n_heads)
    slope_hi = slopes.astype(BF16).astype(F32)
    slope_parts = jnp.stack([slopes, slope_hi, (slopes - slope_hi).astype(BF16).astype(F32)])
    row = lambda a: a.reshape(1, -1).astype(F32)
    bf = lambda a: a.astype(BF16)

    branch_lane = jnp.arange(HEAD_W) // HEAD_DIM
    nrow = 2 * n_heads * dt_
    chunk = 256
    r = jnp.arange(nrow)
    r_head, r_tok = (r // dt_) % n_heads, r % dt_
    r_slope = slopes[r_head][:, None]
    c = jnp.arange(chunk)
    c_tok, c_head = c // n_heads, c % n_heads
    own_head = c_head[None, :] == r_head[:, None]
    base = r_slope * c_tok[None, :].astype(F32) + jnp.where(own_head, 0.0, NEG)
    base2 = jnp.concatenate([base, base + r_slope * float(chunk // n_heads)], axis=0)
    slope2 = jnp.concatenate([r_slope, r_slope], axis=0)
    new_ok = own_head & (c[None, :] < dt_ * n_heads) & (c_tok[None, :] <= r_tok[:, None])
    newmask = jnp.where(new_ok, r_slope * c_tok[None, :].astype(F32), NEG)
    mem_row_head = jnp.arange(n_mem_heads * dt_) // dt_
    mem_col_head = jnp.arange(n_mem * n_mem_heads) % n_mem_heads
    mem_mask = jnp.where(mem_col_head[None, :] == mem_row_head[:, None], 0.0, NEG).astype(F32)

    y_p = x_prompt
    y_s = x_sample.reshape(db * dt_, d)
    outs = {k: [] for k in ("kp", "vp", "ks", "vs", "rp", "ip", "rs", "is", "mkp", "mvp")}
    for l in range(depth):
        lam_init = 0.8 - 0.6 * math.exp(-0.3 * l)
        lamv = jnp.stack([lam_q1[l], lam_k1[l], lam_q2[l], lam_k2[l]]).astype(F32)
        gsub = row(g_subln[l])
        w_in_bf = bf(w_in[l])
        w = {
            "w_glu": bf(w_glu[l]), "g_ssm_out": row(g_ssm_out[l]), "w_out": bf(w_out[l]),
            "g_mix_post": row(g_mix_post[l]), "g_mem_pre": row(g_mem_pre[l]), "w_mq": bf(w_mq[l]),
            "w_mo": bf(w_mo[l]), "g_mem_post": row(g_mem_post[l]),
        }

        qp, kp, vp, kpb, vpb, up = _proj(y_p.reshape(b * t, d), row(g_mix_pre[l]), w_in_bf, aw, 1024)
        qs, ks, vs, _, _, us = _proj(y_s, row(g_mix_pre[l]), w_in_bf, aw, 512)

        qh = qs.reshape(db, dt_, n_heads, HEAD_W).transpose(0, 2, 1, 3)
        zq = jnp.zeros((), BF16)
        q32 = jnp.stack([jnp.where(branch_lane == 0, qh, zq), jnp.where(branch_lane == 1, qh, zq)], axis=1)
        q32 = q32.reshape(db, nrow, HEAD_W)
        zq32 = jnp.zeros_like(q32)
        q2 = jnp.concatenate([jnp.concatenate([q32, zq32], axis=2), jnp.concatenate([zq32, q32], axis=2)], axis=1)
        att_p, att_s16 = _attn(jnp.swapaxes(qp.reshape(b, t, aw), 1, 2), kpb.reshape(b, t, aw),
                               jnp.swapaxes(vpb.reshape(b, t, aw), 1, 2),
                               slope_parts, lamv, g_subln[l].reshape(HEAD_W, 1).astype(F32),
                               page_table, q2, ks.reshape(db, dt_ * n_heads, HEAD_W),
                               vs.reshape(db, dt_ * n_heads, HEAD_W),
                               cache_k[l].reshape(n_phys * PAGE * n_heads, HEAD_W),
                               cache_v[l].reshape(n_phys * PAGE * n_heads, HEAD_W),
                               gsub, slope2, base2, newmask, lam_init, 512)
        att_s = bf(att_s16.reshape(db, n_heads, dt_, HEAD_W).transpose(0, 2, 1, 3).reshape(db * dt_, aw))

        tabs = _s5_prep(*_s5_param_layout(ssm_lam_re[l], ssm_lam_im[l], ssm_log_dt[l], ssm_b_re[l],
                                          ssm_b_im[l], ssm_c_re[l], ssm_c_im[l], ssm_d[l]))
        syp, stp = _s5_prompt(up.reshape(b, t // S5_BLOCK, S5_BLOCK * sw), *tabs, 512)
        rp, ip = _state_from_quarters(stp.reshape(b, S5_QUARTERS, -1), g, p)
        sys_, sts = _s5_sample(us, _state_to_quarters(state_ssm_re[l], state_ssm_im[l]), *tabs)
        rs, is_ = _state_from_quarters(sts.transpose(1, 0, 2), g, p)

        mkp, mvp, mkb, mvb = _mem_kv(mem_prompt.reshape(b * n_mem, d), row(g_mem_kv[l]),
                                     bf(w_mk[l]), bf(w_mv[l]), 512)
        x1s, qms = _mix_q(y_s, att_s, sys_, w, n_mem_heads)
        mhd = d // n_mem_heads
        qmh = qms.reshape(db, dt_, n_mem_heads, mhd).transpose(0, 2, 1, 3).reshape(db, n_mem_heads * dt_, mhd)
        y_p, omh = _mix_mem(y_p, att_p, syp, mkb.reshape(b, n_mem, d), mvb.reshape(b, n_mem, d), w,
                            qmh, cache_mem_k[l], cache_mem_v[l], mem_mask, 512, n_mem_heads)
        om = bf(omh.reshape(db, n_mem_heads, dt_, mhd).transpose(0, 2, 1, 3).reshape(db * dt_, d))
        y_s = _mem_out(x1s, om, w["w_mo"], w["g_mem_post"])

        ffw = (row(g_ffn_pre[l]), bf(w_gate[l]), bf(w_up[l]), bf(w_down[l]), row(g_ffn_post[l]))
        y_p = _ffn(y_p.reshape(b * t, d), *ffw, 1024, 11).reshape(b, t, d)
        y_s = _ffn(y_s, *ffw, 512, 2)

        outs["kp"].append(kp.reshape(b, t, n_heads, HEAD_W))
        outs["vp"].append(vp.reshape(b, t, n_heads, HEAD_W))
        outs["ks"].append(ks.reshape(db, dt_, n_heads, HEAD_W))
        outs["vs"].append(vs.reshape(db, dt_, n_heads, HEAD_W))
        outs["rp"].append(rp); outs["ip"].append(ip); outs["rs"].append(rs); outs["is"].append(is_)
        outs["mkp"].append(mkp.reshape(b, n_mem, n_mem_heads, d // n_mem_heads))
        outs["mvp"].append(mvp.reshape(b, n_mem, n_mem_heads, d // n_mem_heads))

    st = lambda k: jnp.stack(outs[k], 0)
    return (y_p, y_s.reshape(db, dt_, d), st("kp"), st("vp"), st("ks"), st("vs"),
            st("rp"), st("ip"), st("rs"), st("is"), st("mkp"), st("mvp"))
```
